```python
import jax, jax.numpy as jnp
from jax import lax
import numpy as np

D_MODEL = 2048
BATCH = 1
SEQ = 8192
DEPTH = 1
DEC_BATCH = 32
DEC_SEQ = 1
PAST_LEN = 8192
PAGE_SIZE = 128

ATT_HEADS = 8
ATT_HD = 128
ATT_W = ATT_HEADS * ATT_HD
IDX_HEADS = 16
IDX_HD = 64
TOPK_MAX = 256
Q_BLOCK = 128
ROPE_THETA = 10000.0
RWKV_HD = 64
RWKV_HEADS = 16
RWKV_W = RWKV_HEADS * RWKV_HD
DECAY_LORA = 64
ICL_LORA = 64
MIX_W = ATT_W + RWKV_W
NORM_EPS = 1e-6
GN_EPS = 64e-5

ATT_SPLITS = (ATT_W, ATT_W, ATT_W, ATT_W, IDX_HEADS * IDX_HD, IDX_HEADS, IDX_HD)
RWKV_SPLITS = (RWKV_W, RWKV_W, RWKV_W, DECAY_LORA, ICL_LORA, RWKV_W)
ATT_COLS = sum(ATT_SPLITS)
RWKV_COLS = sum(RWKV_SPLITS)
IN_COLS = ATT_COLS + RWKV_COLS

kernel_name = 'hymba_dsa_rwkv7_step'


def _split(p, sizes):
    cuts = [int(c) for c in np.cumsum(sizes)[:-1]]
    return jnp.split(p, cuts, axis=-1)


def _rms(x, g, eps=NORM_EPS):
    xf = x.astype(jnp.float32)
    y = xf * lax.rsqrt(jnp.mean(xf * xf, axis=-1, keepdims=True) + eps)
    return (y * g.astype(jnp.float32)).astype(x.dtype)


def _rope(x, pos):
    dh = x.shape[-1]
    inv = ROPE_THETA ** (-jnp.arange(0, dh, 2, dtype=jnp.float32) / dh)
    ang = pos.astype(jnp.float32)[:, None] * inv[None, :]
    cos = jnp.cos(ang)[None, :, None, :]
    sin = jnp.sin(ang)[None, :, None, :]
    xf = x.astype(jnp.float32)
    x1, x2 = xf[..., : dh // 2], xf[..., dh // 2:]
    return jnp.concatenate([x1 * cos - x2 * sin, x2 * cos + x1 * sin], axis=-1).astype(x.dtype)


def _take_rows(rows, idx):
    return jax.vmap(lambda r, i: r[i])(rows, idx)


def _modulate_project(x, c, w_ada, b_ada, norm_g, w_in):
    mod = jax.nn.silu(c.astype(jnp.float32)) @ w_ada.astype(jnp.float32) + b_ada.astype(jnp.float32)
    shift, scale, gate = jnp.split(mod, 3, axis=-1)
    h = _rms(x, norm_g).astype(jnp.float32) * (1.0 + scale[:, None, :]) + shift[:, None, :]
    p = jnp.einsum('btd,dn->btn', h.astype(x.dtype), w_in)
    return p, gate


def _attn_prep(pa, pos, q_norm_g, k_norm_g):
    B, T = pa.shape[:2]
    q, k, v, g, iq, iw, ik = _split(pa, ATT_SPLITS)
    q = _rope(_rms(q.reshape(B, T, ATT_HEADS, ATT_HD), q_norm_g), pos)
    k = _rope(_rms(k.reshape(B, T, ATT_HEADS, ATT_HD), k_norm_g), pos)
    v = v.reshape(B, T, ATT_HEADS, ATT_HD)
    iq = _rope(iq.reshape(B, T, IDX_HEADS, IDX_HD), pos)
    ik = _rope(ik[:, :, None, :], pos)[:, :, 0, :]
    return q, k, v, g, iq, iw, ik


def _index_select(iq, iw, ik, qpos, n_sel):
    dots = jnp.einsum('bthd,bsd->bths', iq, ik).astype(jnp.float32) * (IDX_HD ** -0.5)
    score = jnp.einsum('bth,bths->bts', iw.astype(jnp.float32) * (IDX_HEADS ** -0.5), jax.nn.relu(dots))
    kpos = jnp.arange(ik.shape[1])
    score = jnp.where(kpos[None, None, :] <= qpos[None, :, None], score, -jnp.inf)
    _, idx = lax.top_k(score, n_sel)
    valid = idx <= qpos[None, :, None]
    return idx, valid


def _sparse_attend(q, kg, vg, valid):
    logits = jnp.einsum('bthd,btkhd->bthk', q, kg).astype(jnp.float32) * (ATT_HD ** -0.5)
    logits = jnp.where(valid[:, :, None, :], logits, -jnp.inf)
    p = jax.nn.softmax(logits, axis=-1)
    return jnp.einsum('bthk,btkhd->bthd', p.astype(vg.dtype), vg)


def _prompt_attention(q, k, v, iq, iw, ik):
    B, S = q.shape[:2]
    n_sel = min(TOPK_MAX, S // 4)
    nb = S // Q_BLOCK

    def block(args):
        qb, iqb, iwb, start = args
        qpos = start + jnp.arange(Q_BLOCK)
        idx, valid = _index_select(iqb, iwb, ik, qpos, n_sel)
        return _sparse_attend(qb, _take_rows(k, idx), _take_rows(v, idx), valid)

    def to_blocks(a):
        return jnp.moveaxis(a.reshape((B, nb, Q_BLOCK) + a.shape[2:]), 1, 0)

    out = lax.map(block, (to_blocks(q), to_blocks(iq), to_blocks(iw), jnp.arange(nb) * Q_BLOCK))
    return jnp.moveaxis(out, 0, 1).reshape(B, S, ATT_HEADS, ATT_HD)


def _sample_attention(q, k, v, iq, iw, ik, cache_k, cache_v, cache_idx_k, page_table, layer):
    B, T = q.shape[:2]
    n_pages = PAST_LEN // PAGE_SIZE
    n_sel = min(TOPK_MAX, (PAST_LEN + T) // 4)
    ik_past = cache_idx_k[layer, page_table].reshape(B, n_pages * PAGE_SIZE, IDX_HD).astype(ik.dtype)
    qpos = PAST_LEN + jnp.arange(T)
    idx, valid = _index_select(iq, iw, jnp.concatenate([ik_past, ik], axis=1), qpos, n_sel)
    in_past = (idx < PAST_LEN)[..., None, None]
    pidx = jnp.minimum(idx, PAST_LEN - 1)
    phys = jax.vmap(lambda pt, i: pt[i])(page_table, pidx // PAGE_SIZE)
    off = pidx % PAGE_SIZE
    nidx = jnp.clip(idx - PAST_LEN, 0, T - 1)
    kg = jnp.where(in_past, cache_k[layer, phys, off].astype(k.dtype), _take_rows(k, nidx))
    vg = jnp.where(in_past, cache_v[layer, phys, off].astype(v.dtype), _take_rows(v, nidx))
    return _sparse_attend(q, kg, vg, valid)


def _rwkv_branch(pr, prev_row, wkv0, shift_mu, w0, w_b, a0, a_b, k_k, k_a, r_k, ln_x_g, ln_x_b):
    f32 = jnp.float32
    B, T = pr.shape[:2]
    prev = jnp.concatenate([prev_row[:, None, :].astype(pr.dtype), pr[:, :-1]], axis=1)
    xm = pr + shift_mu * (prev - pr)
    r, k, v, wl, al, g = _split(xm, RWKV_SPLITS)
    w_log = -jax.nn.softplus(-(w0.astype(f32) + jnp.tanh(wl.astype(f32)) @ w_b.astype(f32))) - 0.5
    decay = jnp.exp(-jnp.exp(w_log))
    a = jax.nn.sigmoid(a0.astype(f32) + al.astype(f32) @ a_b.astype(f32))

    def heads(t):
        return t.astype(f32).reshape(B, T, RWKV_HEADS, RWKV_HD)

    r, k, v, decay, a = heads(r), heads(k), heads(v), heads(decay), heads(a)
    kk = k * k_k.astype(f32).reshape(RWKV_HEADS, RWKV_HD)
    kk = kk / jnp.maximum(jnp.sqrt(jnp.sum(kk * kk, axis=-1, keepdims=True)), 1e-12)
    k = k * (1.0 + (a - 1.0) * k_a.astype(f32).reshape(RWKV_HEADS, RWKV_HD))

    def step(S, inp):
        rt, wt, kt, vt, kkt, at = inp
        sa = jnp.einsum('bhij,bhj->bhi', S, -kkt)
        S = S * wt[:, :, None, :] + sa[..., None] * (kkt * at)[:, :, None, :] + vt[..., None] * kt[:, :, None, :]
        return S, jnp.einsum('bhij,bhj->bhi', S, rt)

    tm = lambda t: jnp.moveaxis(t, 1, 0)
    S_fin, y = lax.scan(step, wkv0.astype(f32), (tm(r), tm(decay), tm(k), tm(v), tm(kk), tm(a)))
    y = jnp.moveaxis(y, 0, 1)
    mean = jnp.mean(y, axis=-1, keepdims=True)
    var = jnp.mean(jnp.square(y - mean), axis=-1, keepdims=True)
    y = ((y - mean) * lax.rsqrt(var + GN_EPS)).reshape(B, T, RWKV_W) * ln_x_g.astype(f32) + ln_x_b.astype(f32)
    bonus = jnp.sum(r * k * r_k.astype(f32), axis=-1, keepdims=True) * v
    y = (y + bonus.reshape(B, T, RWKV_W)) * jax.nn.silu(g.astype(f32))
    return y.astype(pr.dtype), S_fin.astype(wkv0.dtype), pr[:, -1, :]


def _layer(x, c, pos, shift_prev, wkv_prev, attention, w_ada, b_ada, norm_g, w_in, q_norm_g, k_norm_g,
           shift_mu, w0, w_b, a0, a_b, k_k, k_a, r_k, ln_x_g, ln_x_b, w_out):
    B, T = x.shape[:2]
    p, gate = _modulate_project(x, c, w_ada, b_ada, norm_g, w_in)
    pa, pr = p[..., :ATT_COLS], p[..., ATT_COLS:]
    q, k, v, g_att, iq, iw, ik = _attn_prep(pa, pos, q_norm_g, k_norm_g)
    att = attention(q, k, v, iq, iw, ik).reshape(B, T, ATT_W)
    att = (att.astype(jnp.float32) * jax.nn.silu(g_att.astype(jnp.float32))).astype(x.dtype)
    rw, wkv_new, shift_new = _rwkv_branch(pr, shift_prev, wkv_prev, shift_mu, w0, w_b, a0, a_b,
                                          k_k, k_a, r_k, ln_x_g, ln_x_b)
    out = jnp.einsum('btm,md->btd', jnp.concatenate([att, rw], axis=-1), w_out)
    y = x + (gate[:, None, :] * out.astype(jnp.float32)).astype(x.dtype)
    return y, k, v, ik, wkv_new, shift_new


def setup_inputs(seed: int = 0) -> dict:
    key = jax.random.key(seed)
    ks = iter(jax.random.split(key, 40))
    f32 = jnp.float32

    def nrm(shape, s):
        return jax.random.normal(next(ks), shape, f32) * s

    n_pages = PAST_LEN // PAGE_SIZE
    n_used = DEC_BATCH * n_pages
    n_pool = n_used + max(1, n_used // 4)
    page_table = jax.random.permutation(next(ks), n_pool)[:n_used].astype(jnp.int32).reshape(DEC_BATCH, n_pages)
    L = DEPTH
    return {
        'x_prompt': nrm((BATCH, SEQ, D_MODEL), 1.0),
        'x_sample': nrm((DEC_BATCH, DEC_SEQ, D_MODEL), 1.0),
        'c_prompt': nrm((BATCH, D_MODEL), 1.0),
        'c_sample': nrm((DEC_BATCH, D_MODEL), 1.0),
        'cache_k': nrm((L, n_pool, PAGE_SIZE, ATT_HEADS, ATT_HD), 1.0),
        'cache_v': nrm((L, n_pool, PAGE_SIZE, ATT_HEADS, ATT_HD), 1.0),
        'cache_idx_k': nrm((L, n_pool, PAGE_SIZE, IDX_HD), 1.0),
        'state_wkv': nrm((L, DEC_BATCH, RWKV_HEADS, RWKV_HD, RWKV_HD), 0.3),
        'state_shift': nrm((L, DEC_BATCH, RWKV_COLS), 1.0),
        'page_table': page_table,
        'w_ada': nrm((L, D_MODEL, 3 * D_MODEL), 0.5 * D_MODEL ** -0.5),
        'b_ada': nrm((L, 3 * D_MODEL), 0.02),
        'norm_g': 1.0 + nrm((L, D_MODEL), 0.02),
        'w_in': nrm((L, D_MODEL, IN_COLS), D_MODEL ** -0.5),
        'q_norm_g': 1.0 + nrm((L, ATT_HD), 0.02),
        'k_norm_g': 1.0 + nrm((L, ATT_HD), 0.02),
        'shift_mu': jax.random.uniform(next(ks), (L, RWKV_COLS), f32),
        'w0': jax.random.uniform(next(ks), (L, RWKV_W), f32, -6.0, -1.0),
        'w_b': nrm((L, DECAY_LORA, RWKV_W), DECAY_LORA ** -0.5),
        'a0': nrm((L, RWKV_W), 0.1),
        'a_b': nrm((L, ICL_LORA, RWKV_W), ICL_LORA ** -0.5),
        'k_k': 0.85 + nrm((L, RWKV_W), 0.02),
        'k_a': 1.0 + nrm((L, RWKV_W), 0.02),
        'r_k': nrm((L, RWKV_HEADS, RWKV_HD), 0.1),
        'ln_x_g': 1.0 + nrm((L, RWKV_W), 0.02),
        'ln_x_b': nrm((L, RWKV_W), 0.02),
        'w_out': nrm((L, MIX_W, D_MODEL), MIX_W ** -0.5),
    }


def reference(x_prompt, x_sample, c_prompt, c_sample, cache_k, cache_v, cache_idx_k, state_wkv, state_shift,
              page_table, w_ada, b_ada, norm_g, w_in, q_norm_g, k_norm_g, shift_mu, w0, w_b, a0, a_b,
              k_k, k_a, r_k, ln_x_g, ln_x_b, w_out):
    B, S = x_prompt.shape[:2]
    T = x_sample.shape[1]
    pos_p = jnp.arange(S)
    pos_s = PAST_LEN + jnp.arange(T)
    hp, hs = x_prompt, x_sample
    kp_l, vp_l, ikp_l, wp_l, sp_l = [], [], [], [], []
    ks_l, vs_l, iks_l, ws_l, ss_l = [], [], [], [], []
    for l in range(DEPTH):
        lw = (w_ada[l], b_ada[l], norm_g[l], w_in[l], q_norm_g[l], k_norm_g[l], shift_mu[l], w0[l], w_b[l],
              a0[l], a_b[l], k_k[l], k_a[l], r_k[l], ln_x_g[l], ln_x_b[l], w_out[l])
        hp, kp, vp, ikp, wp, sp = _layer(
            hp, c_prompt, pos_p, jnp.zeros((B, RWKV_COLS), hp.dtype),
            jnp.zeros((B, RWKV_HEADS, RWKV_HD, RWKV_HD), hp.dtype), _prompt_attention, *lw)
        attn_s = lambda q, k, v, iq, iw, ik, l=l: _sample_attention(
            q, k, v, iq, iw, ik, cache_k, cache_v, cache_idx_k, page_table, l)
        hs, ks_, vs_, iks, ws, ss = _layer(hs, c_sample, pos_s, state_shift[l], state_wkv[l], attn_s, *lw)
        kp_l.append(kp); vp_l.append(vp); ikp_l.append(ikp); wp_l.append(wp); sp_l.append(sp)
        ks_l.append(ks_); vs_l.append(vs_); iks_l.append(iks); ws_l.append(ws); ss_l.append(ss)
    k_prompt = jnp.stack(kp_l, 0)
    v_prompt = jnp.stack(vp_l, 0)
    idx_k_prompt = jnp.stack(ikp_l, 0)
    wkv_prompt = jnp.stack(wp_l, 0)
    shift_prompt = jnp.stack(sp_l, 0)
    k_sample = jnp.stack(ks_l, 0)
    v_sample = jnp.stack(vs_l, 0)
    idx_k_sample = jnp.stack(iks_l, 0)
    wkv_sample = jnp.stack(ws_l, 0)
    shift_sample = jnp.stack(ss_l, 0)
    return (hp, hs, k_prompt, v_prompt, idx_k_prompt, wkv_prompt, shift_prompt,
            k_sample, v_sample, idx_k_sample, wkv_sample, shift_sample)
```

```python
import functools
import math

import jax
import jax.numpy as jnp
import numpy as np
from jax import lax
from jax.experimental import pallas as pl
from jax.experimental.pallas import tpu as pltpu

F32 = jnp.float32
BF16 = jnp.bfloat16
I32 = jnp.int32

ATT_HEADS = 8
ATT_HD = 128
ATT_W = ATT_HEADS * ATT_HD
IDX_HEADS = 16
IDX_HD = 64
IDX_W = IDX_HEADS * IDX_HD
TOPK_MAX = 256
PAGE_SIZE = 128
ROPE_THETA = 10000.0
RWKV_HD = 64
RWKV_HEADS = 16
RWKV_W = RWKV_HEADS * RWKV_HD
DECAY_LORA = 64
ICL_LORA = 64
NORM_EPS = 1e-6
GN_EPS = 64e-5

LANES = 128
MXU_DIM = 256
VMEM_LIMIT = 56 * 1024 * 1024
NEG_BIG = -1e30
INT_MIN = -(2 ** 31)

COL_Q, COL_K, COL_V, COL_G, COL_IQ = 0, 1024, 2048, 3072, 4096
COL_RR, COL_RK, COL_RV, COL_RG = 5120, 6144, 7168, 8192
COL_SMALL_A = 9216
COL_SMALL_R = 9344
N_COLS = 9472
N_COLS_PAD = 9728

SCAN_CHUNK = 64
HEADS_PER_GROUP = MXU_DIM // RWKV_HD
N_GROUPS = RWKV_HEADS // HEADS_PER_GROUP


def _cparams(sem, vmem=VMEM_LIMIT):
    return pltpu.CompilerParams(dimension_semantics=sem, vmem_limit_bytes=vmem)


def _split2(x):
    hi = x.astype(BF16)
    lo = (x - hi.astype(F32)).astype(BF16)
    return hi, lo


def _split3(x):
    hi = x.astype(BF16)
    r = x - hi.astype(F32)
    mid = r.astype(BF16)
    lo = (r - mid.astype(F32)).astype(BF16)
    return hi, mid, lo


_NN = (((1,), (0,)), ((), ()))
_NT = (((1,), (1,)), ((), ()))
_TN = (((0,), (0,)), ((), ()))


def _dg(a, b, dims):
    return lax.dot_general(a, b, dims, preferred_element_type=F32)


def _dot1(a, b, dims=_NN):
    return _dg(a.astype(BF16), b.astype(BF16), dims)


def _dot3(a, b, dims=_NN):
    ah, al = _split2(a)
    bh, bl = _split2(b)
    return _dg(ah, bh, dims) + _dg(al, bh, dims) + _dg(ah, bl, dims)


def _dot_exact_lhs(a_bf16, b, dims=_NN):
    b0, b1, b2 = _split3(b)
    return _dg(a_bf16, b0, dims) + _dg(a_bf16, b1, dims) + _dg(a_bf16, b2, dims)


def _dot_exact_rhs(a, b_bf16, dims=_NN):
    a0, a1, a2 = _split3(a)
    return _dg(a0, b_bf16, dims) + _dg(a1, b_bf16, dims) + _dg(a2, b_bf16, dims)


def _silu(x):
    return x * jax.nn.sigmoid(x)


def _head_sum(x, ones_bd):
    parts = []
    for j in range(x.shape[1] // LANES):
        parts.append(_dot_exact_rhs(x[:, j * LANES:(j + 1) * LANES], ones_bd))
    return jnp.concatenate(parts, axis=1)


def _mod_kernel(c_ref, w_ref, b_ref, o_ref):
    o_ref[...] = _dot3(_silu(c_ref[...]), w_ref[...]) + b_ref[...]


def _modulation(c, w_ada, b_ada):
    m, d = c.shape
    n = w_ada.shape[1]
    tn = 768
    return pl.pallas_call(
        _mod_kernel,
        grid=(n // tn,),
        in_specs=[pl.BlockSpec((m, d), lambda j: (0, 0)),
                  pl.BlockSpec((d, tn), lambda j: (0, j)),
                  pl.BlockSpec((1, tn), lambda j: (0, j))],
        out_specs=pl.BlockSpec((m, tn), lambda j: (0, j)),
        out_shape=jax.ShapeDtypeStruct((m, n), F32),
        compiler_params=_cparams(("arbitrary",)),
        name="modulation",
    )(c, w_ada, b_ada.reshape(1, n))


def _inproj_kernel(x_ref, shift_ref, scale_ref, g_ref, w_ref, o_ref, h_ref, *, rows):
    @pl.when(pl.program_id(1) == 0)
    def _():
        def body(r, c):
            sl = pl.ds(pl.multiple_of(r * rows, rows), rows)
            x = x_ref[sl, :]
            ms = jnp.mean(x * x, axis=-1, keepdims=True)
            y = x * lax.rsqrt(ms + NORM_EPS) * g_ref[...]
            if shift_ref.shape[0] == 1:
                h = y * (1.0 + scale_ref[...]) + shift_ref[...]
            else:
                h = y * (1.0 + scale_ref[sl, :]) + shift_ref[sl, :]
            h_ref[sl, :] = h.astype(BF16)
            return c
        lax.fori_loop(0, x_ref.shape[0] // rows, body, 0)

    o_ref[...] = jnp.dot(h_ref[...], w_ref[...], preferred_element_type=F32)


def _in_projection(x, shift, scale, norm_g, w_bf16):
    t, d = x.shape
    n = w_bf16.shape[1]
    tm = min(1024, t)
    tn = 512
    rows = min(128, tm)
    per_row = shift.shape[0] != 1
    mod_spec = (pl.BlockSpec((tm, d), lambda i, j: (i, 0)) if per_row
                else pl.BlockSpec((1, d), lambda i, j: (0, 0)))
    return pl.pallas_call(
        functools.partial(_inproj_kernel, rows=rows),
        grid=(t // tm, n // tn),
        in_specs=[pl.BlockSpec((tm, d), lambda i, j: (i, 0)),
                  mod_spec, mod_spec,
                  pl.BlockSpec((1, d), lambda i, j: (0, 0)),
                  pl.BlockSpec((d, tn), lambda i, j: (0, j))],
        out_specs=pl.BlockSpec((tm, tn), lambda i, j: (i, j)),
        out_shape=jax.ShapeDtypeStruct((t, n), F32),
        scratch_shapes=[pltpu.VMEM((tm, d), BF16)],
        compiler_params=_cparams(("arbitrary", "arbitrary")),
        name="in_projection",
    )(x, shift, scale, norm_g.reshape(1, d), w_bf16)


def _rope128(x, cos, sin_signed):
    return x * cos + pltpu.roll(x, 64, axis=1) * sin_signed


def _rope64(x, cos, sin_signed, first_half):
    rot = jnp.where(first_half, pltpu.roll(x, 96, axis=1), pltpu.roll(x, 32, axis=1))
    return x * cos + rot * sin_signed


def _prep_kernel(q_ref, k_ref, v_ref, iq_ref, sm_ref, c128_ref, s128_ref, c64_ref, s64_ref,
                 qg_ref, kg_ref,
                 qb_ref, k32_ref, kb_ref, vb_ref, iqb_ref, ik32_ref, ikb_ref, w_ref):
    c128, s128 = c128_ref[...], s128_ref[...]
    c64, s64 = c64_ref[...], s64_ref[...]
    tm = q_ref.shape[0]
    lane = lax.broadcasted_iota(I32, (tm, LANES), 1)
    first_half = (lane % IDX_HD) < (IDX_HD // 2)
    q_scale = ATT_HD ** -0.5
    for h in range(ATT_HEADS):
        hs = slice(h * ATT_HD, (h + 1) * ATT_HD)
        q = q_ref[:, hs]
        q = q * lax.rsqrt(jnp.mean(q * q, axis=-1, keepdims=True) + NORM_EPS) * qg_ref[...]
        qb_ref[:, hs] = (_rope128(q, c128, s128) * q_scale).astype(BF16)
        k = k_ref[:, hs]
        k = k * lax.rsqrt(jnp.mean(k * k, axis=-1, keepdims=True) + NORM_EPS) * kg_ref[...]
        k = _rope128(k, c128, s128)
        k32_ref[:, hs] = k
        kb_ref[:, hs] = k.astype(BF16)
    vb_ref[...] = v_ref[...].astype(BF16)
    for j in range(IDX_W // LANES):
        js = slice(j * LANES, (j + 1) * LANES)
        iqb_ref[:, js] = _rope64(iq_ref[:, js], c64, s64, first_half).astype(BF16)
    sm = sm_ref[...]
    ik = _rope64(sm, c64, s64, first_half)[:, :IDX_HD]
    ik32_ref[...] = ik
    ikb_ref[...] = ik.astype(BF16)
    w_ref[...] = sm * ((IDX_HEADS ** -0.5) * (IDX_HD ** -0.5))


def _attention_prep(p, tables, q_norm_g, k_norm_g):
    t = p.shape[0]
    tm = min(256, t)
    c128, s128, c64, s64 = tables
    wide = lambda col: pl.BlockSpec((tm, 1024), lambda i, c=col // 1024: (i, c))
    tab = pl.BlockSpec((tm, LANES), lambda i: (i, 0))
    gain = pl.BlockSpec((1, ATT_HD), lambda i: (0, 0))
    row1024 = pl.BlockSpec((tm, 1024), lambda i: (i, 0))
    row64 = pl.BlockSpec((tm, IDX_HD), lambda i: (i, 0))
    return pl.pallas_call(
        _prep_kernel,
        grid=(t // tm,),
        in_specs=[wide(COL_Q), wide(COL_K), wide(COL_V), wide(COL_IQ),
                  pl.BlockSpec((tm, LANES), lambda i: (i, COL_SMALL_A // LANES)),
                  tab, tab, tab, tab, gain, gain],
        out_specs=[row1024, row1024, row1024, row1024, row1024, row64, row64, tab],
        out_shape=[jax.ShapeDtypeStruct((t, ATT_W), BF16),
                   jax.ShapeDtypeStruct((t, ATT_W), F32),
                   jax.ShapeDtypeStruct((t, ATT_W), BF16),
                   jax.ShapeDtypeStruct((t, ATT_W), BF16),
                   jax.ShapeDtypeStruct((t, IDX_W), BF16),
                   jax.ShapeDtypeStruct((t, IDX_HD), F32),
                   jax.ShapeDtypeStruct((t, IDX_HD), BF16),
                   jax.ShapeDtypeStruct((t, LANES), F32)],
        compiler_params=_cparams(("arbitrary",)),
        name="attention_prep",
    )(p, p, p, p, p, c128, s128, c64, s64, q_norm_g.reshape(1, ATT_HD), k_norm_g.reshape(1, ATT_HD))


def _rope_tables(pos):
    pos = pos.astype(F32)[:, None]

    def tab(dh):
        inv = ROPE_THETA ** (-jnp.arange(0, dh, 2, dtype=F32) / dh)
        ang = pos * inv[None, :]
        cos, sin = jnp.cos(ang), jnp.sin(ang)
        c = jnp.concatenate([cos, cos], axis=-1)
        s = jnp.concatenate([-sin, sin], axis=-1)
        rep = LANES // dh
        return jnp.tile(c, (1, rep)), jnp.tile(s, (1, rep))

    c128, s128 = tab(ATT_HD)
    c64, s64 = tab(IDX_HD)
    return c128, s128, c64, s64


def _ordered_to_f32(key):
    return pltpu.bitcast(key ^ ((key >> 31) & 0x7FFFFFFF), F32)


def _indexer_kernel(iq_ref, w_ref, ik_ref, bias_ref, iqs_ref, wb_ref, thr_ref, *, n_sel):
    i = pl.program_id(0)
    tq = iq_ref.shape[0]
    tk = bias_ref.shape[3]
    nk_total = bias_ref.shape[1]
    nkb = i + 1
    rchunk = min(128, tq)
    halves = tk // LANES

    for h in range(IDX_HEADS):
        iqs_ref[h * tq:(h + 1) * tq, :] = iq_ref[:, h * IDX_HD:(h + 1) * IDX_HD]
        wb_ref[h] = jnp.broadcast_to(w_ref[:, IDX_HD + h:IDX_HD + h + 1], (tq, LANES))

    row = lax.broadcasted_iota(I32, (tq, tk), 0) + i * tq
    col = lax.broadcasted_iota(I32, (tq, tk), 1)

    def tile(x):
        return jnp.concatenate([x] * halves, axis=1) if halves > 1 else x

    def score_body(kb, c):
        ikt = ik_ref[pl.ds(pl.multiple_of(kb * tk, tk), tk), :]
        d = _dg(iqs_ref[...], ikt, _NT)
        acc = jnp.zeros((tq, tk), F32)
        for h in range(IDX_HEADS):
            acc = acc + jnp.maximum(d[h * tq:(h + 1) * tq], 0.0) * tile(wb_ref[h])
        bias_ref[0, kb] = jnp.where(col + kb * tk <= row, acc, -jnp.inf)
        return c

    lax.fori_loop(0, nkb, score_body, 0)

    thr_ref[...] = jnp.full(thr_ref.shape, INT_MIN, I32)

    def bit_body(bi, c):
        bitval = jnp.left_shift(jnp.int32(1), 31 - bi)
        for rc in range(tq // rchunk):
            rows = slice(rc * rchunk, (rc + 1) * rchunk)
            ans = thr_ref[rows, :]
            cand = ans + bitval
            cand_f = tile(_ordered_to_f32(cand))

            def cnt_body(kb, cnt):
                ge = jnp.where(bias_ref[0, kb, rows, :] >= cand_f, 1.0, 0.0)
                for j in range(halves):
                    cnt = cnt + ge[:, j * LANES:(j + 1) * LANES]
                return cnt

            cnt = lax.fori_loop(0, nkb, cnt_body, jnp.zeros((rchunk, LANES), F32))
            tot = jnp.sum(cnt, axis=-1, keepdims=True)
            thr_ref[rows, :] = jnp.where(tot >= n_sel, cand, ans)
        return c

    lax.fori_loop(0, 32, bit_body, 0)

    thr = _ordered_to_f32(thr_ref[...])
    thr = tile(jnp.where(thr != thr, -jnp.inf, thr))

    def bias_body(kb, c):
        keep = (bias_ref[0, kb] >= thr) & (col + kb * tk <= row)
        bias_ref[0, kb] = jnp.where(keep, 0.0, NEG_BIG)
        return c

    lax.fori_loop(0, nkb, bias_body, 0)

    def fill_body(kb, c):
        bias_ref[0, kb] = jnp.full((tq, tk), NEG_BIG, F32)
        return c

    lax.fori_loop(nkb, nk_total, fill_body, 0)


def _prompt_indexer(iqb, wv, ikb, n_sel, tq):
    t = iqb.shape[0]
    nq = t // tq
    return pl.pallas_call(
        functools.partial(_indexer_kernel, n_sel=n_sel),
        grid=(nq,),
        in_specs=[pl.BlockSpec((tq, IDX_W), lambda i: (i, 0)),
                  pl.BlockSpec((tq, LANES), lambda i: (i, 0)),
                  pl.BlockSpec((t, IDX_HD), lambda i: (0, 0))],
        out_specs=pl.BlockSpec((1, nq, tq, tq), lambda i: (i, 0, 0, 0)),
        out_shape=jax.ShapeDtypeStruct((nq, nq, tq, tq), F32),
        scratch_shapes=[pltpu.VMEM((IDX_HEADS * tq, IDX_HD), BF16),
                        pltpu.VMEM((IDX_HEADS, tq, LANES), F32),
                        pltpu.VMEM((tq, LANES), I32)],
        compiler_params=_cparams(("arbitrary",)),
        name="prompt_indexer",
    )(iqb, wv, ikb)


def _flash_kernel(q_ref, k_ref, v_ref, bias_ref, g_ref, o_ref, m_ref, l_ref, acc_ref):
    qi, ki = pl.program_id(0), pl.program_id(1)
    tk = k_ref.shape[0]
    halves = tk // LANES

    def tile(x):
        return jnp.concatenate([x] * halves, axis=1) if halves > 1 else x

    @pl.when(ki == 0)
    def _():
        m_ref[...] = jnp.full(m_ref.shape, NEG_BIG, F32)
        l_ref[...] = jnp.zeros(l_ref.shape, F32)
        acc_ref[...] = jnp.zeros(acc_ref.shape, F32)

    @pl.when(ki <= qi)
    def _():
        bias = bias_ref[0, 0]
        for h in range(ATT_HEADS):
            hs = slice(h * ATT_HD, (h + 1) * ATT_HD)
            s = _dg(q_ref[:, hs], k_ref[:, hs], _NT) + bias
            m_prev = m_ref[h]
            m_new = jnp.maximum(m_prev, jnp.max(s, axis=-1, keepdims=True))
            alpha = jnp.exp(m_prev - m_new)
            p = jnp.exp(s - tile(m_new))
            l_ref[h] = alpha * l_ref[h] + jnp.sum(p, axis=-1, keepdims=True)
            acc_ref[h] = alpha * acc_ref[h] + jnp.dot(p.astype(BF16), v_ref[:, hs],
                                                      preferred_element_type=F32)
            m_ref[h] = m_new

    @pl.when(ki == pl.num_programs(1) - 1)
    def _():
        for h in range(ATT_HEADS):
            hs = slice(h * ATT_HD, (h + 1) * ATT_HD)
            o_ref[:, hs] = (acc_ref[h] / l_ref[h] * _silu(g_ref[:, hs])).astype(BF16)


def _prompt_attention(qb, kb, vb, bias, p, tq):
    t = qb.shape[0]
    nq = t // tq
    kv_spec = pl.BlockSpec((tq, ATT_W), lambda i, j: (jnp.minimum(i, j), 0))
    return pl.pallas_call(
        _flash_kernel,
        grid=(nq, nq),
        in_specs=[pl.BlockSpec((tq, ATT_W), lambda i, j: (i, 0)),
                  kv_spec, kv_spec,
                  pl.BlockSpec((1, 1, tq, tq), lambda i, j: (i, jnp.minimum(i, j), 0, 0)),
                  pl.BlockSpec((tq, ATT_W), lambda i, j: (i, COL_G // ATT_W))],
        out_specs=pl.BlockSpec((tq, ATT_W), lambda i, j: (i, 0)),
        out_shape=jax.ShapeDtypeStruct((t, ATT_W), BF16),
        scratch_shapes=[pltpu.VMEM((ATT_HEADS, tq, LANES), F32),
                        pltpu.VMEM((ATT_HEADS, tq, LANES), F32),
                        pltpu.VMEM((ATT_HEADS, tq, ATT_HD), F32)],
        compiler_params=_cparams(("arbitrary", "arbitrary")),
        name="prompt_attention",
    )(qb, kb, vb, bias, p)


def _sample_attn_kernel(pt_ref, q_ref, iq_ref, w_ref, ikn_ref, kn_ref, vn_ref, g_ref,
                        cik_ref, ck_ref, cv_ref, o_ref,
                        ikbuf, kbuf, vbuf, sc_ref, isem, ksem, vsem, *, n_sel):
    b = pl.program_id(0)
    n_pages = ikbuf.shape[0]

    def idx_copy(j):
        return pltpu.make_async_copy(cik_ref.at[pt_ref[b, j]], ikbuf.at[j], isem.at[j])

    def k_copy(j, slot):
        return pltpu.make_async_copy(ck_ref.at[pt_ref[b, j]], kbuf.at[slot], ksem.at[slot])

    def v_copy(j, slot):
        return pltpu.make_async_copy(cv_ref.at[pt_ref[b, j]], vbuf.at[slot], vsem.at[slot])

    def start_idx(j, c):
        idx_copy(j).start()
        return c

    lax.fori_loop(0, n_pages, start_idx, 0)
    k_copy(0, 0).start()
    v_copy(0, 0).start()

    iq = iq_ref[0]
    w = w_ref[0]

    def score_body(j, c):
        idx_copy(j).wait()
        d = _dg(iq, ikbuf[j].astype(BF16), _NT)
        sc_ref[pl.ds(j, 1), :] = jnp.sum(jnp.maximum(d, 0.0) * w, axis=0, keepdims=True)
        return c

    lax.fori_loop(0, n_pages, score_body, 0)
    d_self = jnp.sum(iq.astype(F32) * ikn_ref[0].astype(F32), axis=-1, keepdims=True)
    s_self = jnp.sum(jnp.maximum(d_self, 0.0) * w, axis=0, keepdims=True)

    sc = sc_ref[...]

    def bit_body(bi, ans):
        cand = ans + jnp.left_shift(jnp.int32(1), 31 - bi)
        cand_f = _ordered_to_f32(cand)
        cnt = jnp.sum(jnp.where(sc >= cand_f, 1.0, 0.0), axis=-1, keepdims=True)
        cnt = jnp.sum(cnt, axis=0, keepdims=True) + jnp.where(s_self >= cand_f, 1.0, 0.0)
        return jnp.where(cnt >= n_sel, cand, ans)

    ans = lax.fori_loop(0, 32, bit_body, jnp.full((1, 1), INT_MIN, I32))
    thr = _ordered_to_f32(ans)
    thr = jnp.where(thr != thr, -jnp.inf, thr)

    hrow = lax.broadcasted_iota(I32, (ATT_HEADS, ATT_W), 0)
    hlane = lax.broadcasted_iota(I32, (ATT_HEADS, ATT_W), 1) // ATT_HD
    own = hrow == hlane
    qbd32 = jnp.where(own, jnp.broadcast_to(q_ref[0].astype(F32), (ATT_HEADS, ATT_W)), 0.0)
    qbd = qbd32.astype(BF16)

    kn = kn_ref[0].astype(BF16).astype(F32)
    s0 = jnp.sum(qbd32 * kn, axis=-1, keepdims=True)
    s0 = jnp.where(s_self >= thr, s0, NEG_BIG)
    m0 = jnp.broadcast_to(s0, (ATT_HEADS, LANES))
    l0 = jnp.where(m0 > 0.5 * NEG_BIG, 1.0, 0.0)
    acc0 = l0[:, :1] * jnp.broadcast_to(vn_ref[0].astype(BF16).astype(F32), (ATT_HEADS, ATT_W))

    def att_body(j, carry):
        m_prev, l_prev, acc = carry
        slot = j % 2

        @pl.when(j + 1 < n_pages)
        def _():
            k_copy(j + 1, 1 - slot).start()
            v_copy(j + 1, 1 - slot).start()

        k_copy(j, slot).wait()
        v_copy(j, slot).wait()
        s = _dg(qbd, kbuf[slot].astype(BF16), _NT)
        s = jnp.where(sc_ref[pl.ds(j, 1), :] >= thr, s, NEG_BIG)
        m_new = jnp.maximum(m_prev, jnp.max(s, axis=-1, keepdims=True))
        alpha = jnp.exp(m_prev - m_new)
        p = jnp.exp(s - m_new)
        l_new = alpha * l_prev + jnp.sum(p, axis=-1, keepdims=True)
        pv = jnp.dot(p.astype(BF16), vbuf[slot].astype(BF16), preferred_element_type=F32)
        return m_new, l_new, alpha[:, :1] * acc + pv

    m, l, acc = lax.fori_loop(0, n_pages, att_body, (m0, l0, acc0))
    out = jnp.sum(jnp.where(own, acc / l[:, :1], 0.0), axis=0, keepdims=True)
    o_ref[0] = (out * _silu(g_ref[0])).astype(BF16)


def _sample_attention(page_table, qb, iqb, wv, ikb, k32, v32, g, cache_idx_k, cache_k, cache_v, n_sel):
    nb, n_pages = page_table.shape
    n_pool = cache_k.shape[0]
    row3 = lambda width: pl.BlockSpec((1, 1, width), lambda b, pt: (b, 0, 0))
    any_spec = pl.BlockSpec(memory_space=pl.ANY)
    grid_spec = pltpu.PrefetchScalarGridSpec(
        num_scalar_prefetch=1,
        grid=(nb,),
        in_specs=[row3(ATT_W),
                  pl.BlockSpec((1, IDX_HEADS, IDX_HD), lambda b, pt: (b, 0, 0)),
                  pl.BlockSpec((1, IDX_HEADS, 1), lambda b, pt: (b, 0, 0)),
                  row3(IDX_HD), row3(ATT_W), row3(ATT_W), row3(ATT_W),
                  any_spec, any_spec, any_spec],
        out_specs=row3(ATT_W),
        scratch_shapes=[pltpu.VMEM((n_pages, PAGE_SIZE, IDX_HD), F32),
                        pltpu.VMEM((2, PAGE_SIZE, ATT_W), F32),
                        pltpu.VMEM((2, PAGE_SIZE, ATT_W), F32),
                        pltpu.VMEM((n_pages, PAGE_SIZE), F32),
                        pltpu.SemaphoreType.DMA((n_pages,)),
                        pltpu.SemaphoreType.DMA((2,)),
                        pltpu.SemaphoreType.DMA((2,))],
    )
    out = pl.pallas_call(
        functools.partial(_sample_attn_kernel, n_sel=n_sel),
        grid_spec=grid_spec,
        out_shape=jax.ShapeDtypeStruct((nb, 1, ATT_W), BF16),
        compiler_params=_cparams(("arbitrary",)),
        name="sample_attention",
    )(page_table,
      qb.reshape(nb, 1, ATT_W),
      iqb.reshape(nb, IDX_HEADS, IDX_HD),
      wv[:, IDX_HD:IDX_HD + IDX_HEADS].reshape(nb, IDX_HEADS, 1),
      ikb.reshape(nb, 1, IDX_HD),
      k32.reshape(nb, 1, ATT_W), v32.reshape(nb, 1, ATT_W), g.reshape(nb, 1, ATT_W),
      cache_idx_k.reshape(n_pool, PAGE_SIZE, IDX_HD),
      cache_k.reshape(n_pool, PAGE_SIZE, ATT_W),
      cache_v.reshape(n_pool, PAGE_SIZE, ATT_W))
    return out.reshape(nb, ATT_W)


def _rwkv_prep_kernel(r_ref, k_ref, v_ref, g_ref, s_ref, pr_ref, pk_ref, pv_ref, pg_ref, ps_ref,
                      mu_ref, mus_ref, wl_ref, w0_ref, a0_ref, kk_ref, ka_ref, rk_ref, ones_ref,
                      ro_ref, kpo_ref, vo_ref, kko_ref, bo_ref, lwo_ref, bonus_ref, sg_ref):
    def lerp(x_ref, p_ref, mu):
        x = x_ref[...]
        return x + mu * (p_ref[...] - x)

    w = RWKV_W
    xr = lerp(r_ref, pr_ref, mu_ref[:, 0:w])
    xk = lerp(k_ref, pk_ref, mu_ref[:, w:2 * w])
    xv = lerp(v_ref, pv_ref, mu_ref[:, 2 * w:3 * w])
    xg = lerp(g_ref, pg_ref, mu_ref[:, 3 * w:4 * w])
    xs = lerp(s_ref, ps_ref, mus_ref[...])
    lane = lax.broadcasted_iota(I32, xs.shape, 1)
    lora_in = jnp.where(lane < DECAY_LORA, jnp.tanh(xs), xs)
    lora = _dot3(lora_in, wl_ref[...])
    z = w0_ref[...] + lora[:, :w]
    softplus_neg = jnp.maximum(-z, 0.0) + jnp.log(1.0 + jnp.exp(-jnp.abs(z)))
    w_log = -softplus_neg - 0.5
    lwo_ref[...] = -jnp.exp(w_log)
    a = jax.nn.sigmoid(a0_ref[...] + lora[:, w:])
    ones_bd = ones_ref[...]
    kk = xk * kk_ref[...]
    kk = kk / jnp.maximum(jnp.sqrt(_head_sum(kk * kk, ones_bd)), 1e-12)
    kp = xk * (1.0 + (a - 1.0) * ka_ref[...])
    ro_ref[...] = xr
    kpo_ref[...] = kp
    vo_ref[...] = xv
    kko_ref[...] = kk
    bo_ref[...] = kk * a
    bonus_ref[...] = _head_sum(xr * kp * rk_ref[...], ones_bd) * xv
    sg_ref[...] = _silu(xg)


def _rwkv_prep(p, prev, consts):
    t = p.shape[0]
    tm = min(256, t)
    mu_big, mu_small, w_lora, w0, a0, k_k, k_a, r_k, ones_bd = consts
    wide = lambda col: pl.BlockSpec((tm, RWKV_W), lambda i, c=col // RWKV_W: (i, c))
    small = pl.BlockSpec((tm, LANES), lambda i: (i, COL_SMALL_R // LANES))
    full = lambda a: pl.BlockSpec(a.shape, lambda i: (0, 0))
    row = pl.BlockSpec((tm, RWKV_W), lambda i: (i, 0))
    vec = lambda a: a.reshape(1, -1)
    args = [mu_big, mu_small, w_lora, vec(w0), vec(a0), vec(k_k), vec(k_a), vec(r_k), ones_bd]
    return pl.pallas_call(
        _rwkv_prep_kernel,
        grid=(t // tm,),
        in_specs=[wide(COL_RR), wide(COL_RK), wide(COL_RV), wide(COL_RG), small,
                  wide(COL_RR), wide(COL_RK), wide(COL_RV), wide(COL_RG), small]
                 + [full(a) for a in args],
        out_specs=[row] * 8,
        out_shape=[jax.ShapeDtypeStruct((t, RWKV_W), F32)] * 8,
        compiler_params=_cparams(("arbitrary",)),
        name="rwkv_prep",
    )(p, p, p, p, p, prev, prev, prev, prev, prev, *args)


def _scan_kernel(r_ref, kp_ref, v_ref, kk_ref, b_ref, lw_ref, s0_ref, y_ref, st_ref, s_ref):
    c = s_ref.shape

    @pl.when(pl.program_id(0) == 0)
    def _():
        s_ref[...] = s0_ref[...]

    ch = r_ref.shape[0]
    gw = MXU_DIM
    nh = HEADS_PER_GROUP
    lw = lw_ref[...]
    tr = lax.broadcasted_iota(I32, (ch, ch), 0)
    tc = lax.broadcasted_iota(I32, (ch, ch), 1)
    tri = jnp.where(tr >= tc, 1.0, 0.0).astype(BF16)
    cum = _dot_exact_lhs(tri, lw)
    cum_end = cum[ch - 1:ch, :]
    kk, b, kp, r, v = kk_ref[...], b_ref[...], kp_ref[...], r_ref[...], v_ref[...]
    g_inv = jnp.exp(-cum)
    g_end = jnp.exp(cum_end - cum)
    a_t = -kk * jnp.exp(cum - lw)
    b_t = b * g_inv
    k_t = kp * g_inv
    r_t = r * jnp.exp(cum)
    b_h = b * g_end
    k_h = kp * g_end
    g_c = jnp.exp(cum_end)

    srow = lax.broadcasted_iota(I32, (nh * ch, gw), 0)
    scol = lax.broadcasted_iota(I32, (nh * ch, gw), 1)
    own = (srow // ch) == (scol // RWKV_HD)
    mrow = lax.broadcasted_iota(I32, (nh * ch, nh * ch), 0)
    mcol = lax.broadcasted_iota(I32, (nh * ch, nh * ch), 1)
    same = (mrow // ch) == (mcol // ch)
    strict = same & ((mrow % ch) > (mcol % ch))
    incl = same & ((mrow % ch) >= (mcol % ch))
    eye = jnp.where(mrow == mcol, 1.0, 0.0)

    def stack(x):
        return jnp.where(own, jnp.concatenate([x] * nh, axis=0), 0.0)

    for g in range(N_GROUPS):
        ls = slice(g * gw, (g + 1) * gw)
        la, lr = stack(a_t[:, ls]), stack(r_t[:, ls])
        rb, rk = stack(b_t[:, ls]), stack(k_t[:, ls])
        v4, bh4, kh4 = stack(v[:, ls]), stack(b_h[:, ls]), stack(k_h[:, ls])
        n4 = jnp.where(strict, _dot3(la, rb, _NT), 0.0)
        bm4 = jnp.where(strict, _dot3(la, rk, _NT), 0.0)
        rb4 = jnp.where(incl, _dot3(lr, rb, _NT), 0.0)
        rk4 = jnp.where(incl, _dot3(lr, rk, _NT), 0.0)
        minv = eye + n4
        pw = n4
        for _ in range(int(math.log2(ch)) - 1):
            pw = _dot3(pw, pw)
            minv = minv + _dot3(minv, pw)
        s = s_ref[g]
        x4 = _dot3(la, s, _NT) + _dot3(bm4, v4)
        u4 = _dot3(minv, x4)
        y4 = _dot3(lr, s, _NT) + _dot3(rb4, u4) + _dot3(rk4, v4)
        y = y4[0:ch]
        for h in range(1, nh):
            y = y + y4[h * ch:(h + 1) * ch]
        y_ref[:, ls] = y
        s_ref[g] = s * g_c[:, ls] + _dot3(u4, bh4, _TN) + _dot3(v4, kh4, _TN)

    @pl.when(pl.program_id(0) == pl.num_programs(0) - 1)
    def _():
        st_ref[...] = s_ref[...]


def _rwkv_scan(r, kp, v, kk, b, lw, s0):
    t = r.shape[0]
    ch = SCAN_CHUNK
    row = pl.BlockSpec((ch, RWKV_W), lambda i: (i, 0))
    st = pl.BlockSpec((N_GROUPS, MXU_DIM, MXU_DIM), lambda i: (0, 0, 0))
    return pl.pallas_call(
        _scan_kernel,
        grid=(t // ch,),
        in_specs=[row] * 6 + [st],
        out_specs=[row, st],
        out_shape=[jax.ShapeDtypeStruct((t, RWKV_W), F32),
                   jax.ShapeDtypeStruct((N_GROUPS, MXU_DIM, MXU_DIM), F32)],
        scratch_shapes=[pltpu.VMEM((N_GROUPS, MXU_DIM, MXU_DIM), F32)],
        compiler_params=_cparams(("arbitrary",)),
        name="rwkv_scan",
    )(r, kp, v, kk, b, lw, s0)


def _rwkv_step_kernel(s_ref, r_ref, kp_ref, kk_ref, b_ref, lw_ref, v_ref, so_ref, y_ref):
    s = s_ref[0]
    kk = kk_ref[0]
    sa = -jnp.sum(s * kk, axis=-1, keepdims=True)
    s_new = s * jnp.exp(lw_ref[0]) + sa * b_ref[0] + v_ref[0] * kp_ref[0]
    so_ref[0] = s_new
    y_ref[0] = jnp.sum(s_new * r_ref[0], axis=-1, keepdims=True)


def _rwkv_step(state, r, kp, kk, b, lw, v):
    nb = state.shape[0]
    rowv = lambda a: a.reshape(nb, RWKV_HEADS, 1, RWKV_HD)
    rspec = pl.BlockSpec((1, RWKV_HEADS, 1, RWKV_HD), lambda i: (i, 0, 0, 0))
    cspec = pl.BlockSpec((1, RWKV_HEADS, RWKV_HD, 1), lambda i: (i, 0, 0, 0))
    sspec = pl.BlockSpec((1, RWKV_HEADS, RWKV_HD, RWKV_HD), lambda i: (i, 0, 0, 0))
    s_new, y = pl.pallas_call(
        _rwkv_step_kernel,
        grid=(nb,),
        in_specs=[sspec, rspec, rspec, rspec, rspec, rspec, cspec],
        out_specs=[sspec, cspec],
        out_shape=[jax.ShapeDtypeStruct(state.shape, F32),
                   jax.ShapeDtypeStruct((nb, RWKV_HEADS, RWKV_HD, 1), F32)],
        compiler_params=_cparams(("arbitrary",)),
        name="rwkv_step",
    )(state, rowv(r), rowv(kp), rowv(kk), rowv(b), rowv(lw), v.reshape(nb, RWKV_HEADS, RWKV_HD, 1))
    return s_new, y.reshape(nb, RWKV_W)


def _rwkv_post_kernel(y_ref, bonus_ref, sg_ref, lg_ref, lb_ref, ones_ref, o_ref):
    y = y_ref[...]
    ones_bd = ones_ref[...]
    mean = _head_sum(y, ones_bd) * (1.0 / RWKV_HD)
    yc = y - mean
    var = _head_sum(yc * yc, ones_bd) * (1.0 / RWKV_HD)
    yn = yc * lax.rsqrt(var + GN_EPS) * lg_ref[...] + lb_ref[...]
    o_ref[...] = ((yn + bonus_ref[...]) * sg_ref[...]).astype(BF16)


def _rwkv_post(y, bonus, sg, ln_g, ln_b, ones_bd):
    t = y.shape[0]
    tm = min(512, t)
    row = pl.BlockSpec((tm, RWKV_W), lambda i: (i, 0))
    vec = pl.BlockSpec((1, RWKV_W), lambda i: (0, 0))
    return pl.pallas_call(
        _rwkv_post_kernel,
        grid=(t // tm,),
        in_specs=[row, row, row, vec, vec, pl.BlockSpec(ones_bd.shape, lambda i: (0, 0))],
        out_specs=row,
        out_shape=jax.ShapeDtypeStruct((t, RWKV_W), BF16),
        compiler_params=_cparams(("arbitrary",)),
        name="rwkv_post",
    )(y, bonus, sg, ln_g.reshape(1, -1), ln_b.reshape(1, -1), ones_bd)


def _outproj_kernel(att_ref, rw_ref, wa_ref, wr_ref, x_ref, gate_ref, o_ref):
    out = (jnp.dot(att_ref[...], wa_ref[...], preferred_element_type=F32)
           + jnp.dot(rw_ref[...], wr_ref[...], preferred_element_type=F32))
    o_ref[...] = x_ref[...] + gate_ref[...] * out


def _out_projection(att, rw, w_out_bf16, x, gate):
    t, d = x.shape
    tm = min(512, t)
    per_row = gate.shape[0] != 1
    gspec = (pl.BlockSpec((tm, d), lambda i: (i, 0)) if per_row else pl.BlockSpec((1, d), lambda i: (0, 0)))
    return pl.pallas_call(
        _outproj_kernel,
        grid=(t // tm,),
        in_specs=[pl.BlockSpec((tm, ATT_W), lambda i: (i, 0)),
                  pl.BlockSpec((tm, RWKV_W), lambda i: (i, 0)),
                  pl.BlockSpec((ATT_W, d), lambda i: (0, 0)),
                  pl.BlockSpec((RWKV_W, d), lambda i: (1, 0)),
                  pl.BlockSpec((tm, d), lambda i: (i, 0)),
                  gspec],
        out_specs=pl.BlockSpec((tm, d), lambda i: (i, 0)),
        out_shape=jax.ShapeDtypeStruct((t, d), F32),
        compiler_params=_cparams(("arbitrary",)),
        name="out_projection",
    )(att, rw, w_out_bf16, w_out_bf16, x, gate)


def _reorder_cols(w):
    a = 0
    q, k, v, g = (w[..., a + i * ATT_W:a + (i + 1) * ATT_W] for i in range(4))
    a = 4 * ATT_W
    iq = w[..., a:a + IDX_W]
    iw = w[..., a + IDX_W:a + IDX_W + IDX_HEADS]
    ik = w[..., a + IDX_W + IDX_HEADS:a + IDX_W + IDX_HEADS + IDX_HD]
    o = a + IDX_W + IDX_HEADS + IDX_HD
    rr, rk, rv = (w[..., o + i * RWKV_W:o + (i + 1) * RWKV_W] for i in range(3))
    o2 = o + 3 * RWKV_W
    wl = w[..., o2:o2 + DECAY_LORA]
    al = w[..., o2 + DECAY_LORA:o2 + DECAY_LORA + ICL_LORA]
    rg = w[..., o2 + DECAY_LORA + ICL_LORA:o2 + DECAY_LORA + ICL_LORA + RWKV_W]
    pad_a = jnp.zeros(w.shape[:-1] + (LANES - IDX_HD - IDX_HEADS,), w.dtype)
    pad_end = jnp.zeros(w.shape[:-1] + (N_COLS_PAD - N_COLS,), w.dtype)
    return jnp.concatenate([q, k, v, g, iq, rr, rk, rv, rg, ik, iw, pad_a, wl, al, pad_end], axis=-1)


def _rwkv_cols_original_order(p):
    return jnp.concatenate([p[..., COL_RR:COL_RR + RWKV_W], p[..., COL_RK:COL_RK + RWKV_W],
                            p[..., COL_RV:COL_RV + RWKV_W],
                            p[..., COL_SMALL_R:COL_SMALL_R + DECAY_LORA + ICL_LORA],
                            p[..., COL_RG:COL_RG + RWKV_W]], axis=-1)


def _state_to_blockdiag(s):
    s = s.reshape(N_GROUPS, HEADS_PER_GROUP, RWKV_HD, RWKV_HD)
    eye = jnp.eye(HEADS_PER_GROUP, dtype=s.dtype)
    return jnp.einsum("ghvk,hj->ghvjk", s, eye).reshape(N_GROUPS, MXU_DIM, MXU_DIM)


def _blockdiag_to_state(sb):
    sb = sb.reshape(N_GROUPS, HEADS_PER_GROUP, RWKV_HD, HEADS_PER_GROUP, RWKV_HD)
    idx = jnp.arange(HEADS_PER_GROUP)
    return sb[:, idx, :, idx, :].transpose(1, 0, 2, 3).reshape(RWKV_HEADS, RWKV_HD, RWKV_HD)


def kernel(x_prompt, x_sample, c_prompt, c_sample, cache_k, cache_v, cache_idx_k, state_wkv, state_shift,
           page_table, w_ada, b_ada, norm_g, w_in, q_norm_g, k_norm_g, shift_mu, w0, w_b, a0, a_b,
           k_k, k_a, r_k, ln_x_g, ln_x_b, w_out):
    depth = w_ada.shape[0]
    assert depth == 1, "single-layer trunk"
    bsz, seq, d = x_prompt.shape
    assert bsz == 1
    nb, dec_seq, _ = x_sample.shape
    assert dec_seq == 1
    n_pages = page_table.shape[1]
    past_len = n_pages * PAGE_SIZE
    l = 0

    w_in_r = _reorder_cols(w_in[l]).astype(BF16)
    w_out_b = w_out[l].astype(BF16)
    mu = _reorder_cols(jnp.concatenate([jnp.zeros((ATT_W * 4 + IDX_W + IDX_HEADS + IDX_HD,), F32),
                                        shift_mu[l]]))
    mu_big = mu[COL_RR:COL_RR + 4 * RWKV_W].reshape(1, -1)
    mu_small = mu[COL_SMALL_R:COL_SMALL_R + LANES].reshape(1, -1)
    zl = jnp.zeros((DECAY_LORA, RWKV_W), F32)
    w_lora = jnp.concatenate([jnp.concatenate([w_b[l], zl], axis=1),
                              jnp.concatenate([zl, a_b[l]], axis=1)], axis=0)
    hid = jnp.arange(LANES) // RWKV_HD
    ones_bd = (hid[:, None] == hid[None, :]).astype(BF16)
    rwkv_consts = (mu_big, mu_small, w_lora, w0[l], a0[l], k_k[l], k_a[l], r_k[l].reshape(-1), ones_bd)

    n_mod = 1 + nb
    c_all = jnp.concatenate([c_prompt, c_sample, jnp.zeros((-n_mod % 8, d), F32)], axis=0)
    mod = _modulation(c_all, w_ada[l], b_ada[l])
    shift_p, scale_p, gate_p = mod[0:1, :d], mod[0:1, d:2 * d], mod[0:1, 2 * d:]
    shift_s, scale_s, gate_s = mod[1:n_mod, :d], mod[1:n_mod, d:2 * d], mod[1:n_mod, 2 * d:]

    xp = x_prompt.reshape(seq, d)
    xs = x_sample.reshape(nb, d)

    p = _in_projection(xp, shift_p, scale_p, norm_g[l], w_in_r)
    qb, k32, kb, vb, iqb, ik32, ikb, wv = _attention_prep(p, _rope_tables(jnp.arange(seq)),
                                                         q_norm_g[l], k_norm_g[l])
    n_sel = min(TOPK_MAX, seq // 4)
    tq = min(256, seq)
    bias = _prompt_indexer(iqb, wv, ikb, n_sel, tq)
    att = _prompt_attention(qb, kb, vb, bias, p, tq)

    prev = jnp.concatenate([jnp.zeros((1, p.shape[1]), F32), p[:-1]], axis=0)
    r_, kp_, v_, kk_, b_, lw_, bonus, sg = _rwkv_prep(p, prev, rwkv_consts)
    y_scan, s_fin = _rwkv_scan(r_, kp_, v_, kk_, b_, lw_,
                               jnp.zeros((N_GROUPS, MXU_DIM, MXU_DIM), F32))
    rw = _rwkv_post(y_scan, bonus, sg, ln_x_g[l], ln_x_b[l], ones_bd)
    y_p = _out_projection(att, rw, w_out_b, xp, gate_p)

    k_prompt = k32.reshape(1, 1, seq, ATT_HEADS, ATT_HD)
    v_prompt = p[:, COL_V:COL_V + ATT_W].reshape(1, 1, seq, ATT_HEADS, ATT_HD)
    idx_k_prompt = ik32.reshape(1, 1, seq, IDX_HD)
    wkv_prompt = _blockdiag_to_state(s_fin).reshape(1, 1, RWKV_HEADS, RWKV_HD, RWKV_HD)
    shift_prompt = _rwkv_cols_original_order(p[seq - 1:seq]).reshape(1, 1, -1)

    ps = _in_projection(xs, shift_s, scale_s, norm_g[l], w_in_r)
    pos_s = jnp.full((nb,), past_len, jnp.int32)
    qb_s, k32_s, _, _, iqb_s, ik32_s, ikb_s, wv_s = _attention_prep(ps, _rope_tables(pos_s),
                                                                   q_norm_g[l], k_norm_g[l])
    v32_s = ps[:, COL_V:COL_V + ATT_W]
    n_sel_s = min(TOPK_MAX, (past_len + 1) // 4)
    att_s = _sample_attention(page_table, qb_s, iqb_s, wv_s, ikb_s, k32_s, v32_s,
                              ps[:, COL_G:COL_G + ATT_W], cache_idx_k[l], cache_k[l], cache_v[l], n_sel_s)

    prev_s = _reorder_cols(jnp.concatenate(
        [jnp.zeros((nb, ATT_W * 4 + IDX_W + IDX_HEADS + IDX_HD), F32), state_shift[l]], axis=-1))
    r_s, kp_s, v_s, kk_s, b_s, lw_s, bonus_s, sg_s = _rwkv_prep(ps, prev_s, rwkv_consts)
    wkv_new, y_step = _rwkv_step(state_wkv[l], r_s, kp_s, kk_s, b_s, lw_s, v_s)
    rw_s = _rwkv_post(y_step, bonus_s, sg_s, ln_x_g[l], ln_x_b[l], ones_bd)
    y_s = _out_projection(att_s, rw_s, w_out_b, xs, gate_s)

    k_sample = k32_s.reshape(1, nb, 1, ATT_HEADS, ATT_HD)
    v_sample = v32_s.reshape(1, nb, 1, ATT_HEADS, ATT_HD)
    idx_k_sample = ik32_s.reshape(1, nb, 1, IDX_HD)
    wkv_sample = wkv_new.reshape(1, nb, RWKV_HEADS, RWKV_HD, RWKV_HD)
    shift_sample = _rwkv_cols_original_order(ps).reshape(1, nb, -1)

    return (y_p.reshape(1, seq, d), y_s.reshape(nb, 1, d),
            k_prompt, v_prompt, idx_k_prompt, wkv_prompt, shift_prompt,
            k_sample, v_sample, idx_k_sample, wkv_sample, shift_sample)
```

```python
import functools
import math

import jax
import jax.numpy as jnp
from jax import lax
from jax.experimental import pallas as pl
from jax.experimental.pallas import tpu as pltpu

F32 = jnp.float32
BF16 = jnp.bfloat16
I32 = jnp.int32

ATT_HEADS = 8
ATT_HD = 128
ATT_W = ATT_HEADS * ATT_HD
IDX_HEADS = 16
IDX_HD = 64
IDX_W = IDX_HEADS * IDX_HD
TOPK_MAX = 256
PAGE_SIZE = 128
ROPE_THETA = 10000.0
RWKV_HD = 64
RWKV_HEADS = 16
RWKV_W = RWKV_HEADS * RWKV_HD
DECAY_LORA = 64
ICL_LORA = 64
NORM_EPS = 1e-6
GN_EPS = 64e-5

LANES = 128
SUBLANES = 8
MXU_DIM = 256
VMEM_LIMIT = 56 * 1024 * 1024
NEG_BIG = -1e30
INT_MIN = -(2 ** 31)

COL_Q, COL_K, COL_V, COL_G, COL_IQ = 0, 1024, 2048, 3072, 4096
COL_RR, COL_RK, COL_RV, COL_RG = 5120, 6144, 7168, 8192
COL_SMALL_A = 9216
COL_SMALL_R = 9344
N_COLS = 9472
N_COLS_PAD = 9728
N_ATT_COLS_ORIG = 4 * ATT_W + IDX_W + IDX_HEADS + IDX_HD

SCAN_CHUNK = 64
SCAN_SUB = 4
HEADS_PER_GROUP = MXU_DIM // RWKV_HD
N_GROUPS = RWKV_HEADS // HEADS_PER_GROUP


def _cparams(sem, vmem=VMEM_LIMIT):
    return pltpu.CompilerParams(dimension_semantics=sem, vmem_limit_bytes=vmem)


def _split2(x):
    hi = x.astype(BF16)
    lo = (x - hi.astype(F32)).astype(BF16)
    return hi, lo


def _split3(x):
    hi = x.astype(BF16)
    r = x - hi.astype(F32)
    mid = r.astype(BF16)
    lo = (r - mid.astype(F32)).astype(BF16)
    return hi, mid, lo


_NN = (((1,), (0,)), ((), ()))
_NT = (((1,), (1,)), ((), ()))
_TN = (((0,), (0,)), ((), ()))


def _dg(a, b, dims):
    return lax.dot_general(a, b, dims, preferred_element_type=F32)


def _dot1(a, b, dims=_NN):
    return _dg(a.astype(BF16), b.astype(BF16), dims)


def _dot3(a, b, dims=_NN):
    ah, al = _split2(a)
    bh, bl = _split2(b)
    return _dg(ah, bh, dims) + _dg(al, bh, dims) + _dg(ah, bl, dims)


def _dot_exact_lhs(a_bf16, b, dims=_NN):
    b0, b1, b2 = _split3(b)
    return _dg(a_bf16, b0, dims) + _dg(a_bf16, b1, dims) + _dg(a_bf16, b2, dims)


def _dot_exact_rhs(a, b_bf16, dims=_NN):
    a0, a1, a2 = _split3(a)
    return _dg(a0, b_bf16, dims) + _dg(a1, b_bf16, dims) + _dg(a2, b_bf16, dims)


def _silu(x):
    return x * jax.nn.sigmoid(x)


def _head_sum(x, ones_bd):
    parts = []
    for j in range(x.shape[1] // LANES):
        parts.append(_dot_exact_rhs(x[:, j * LANES:(j + 1) * LANES], ones_bd))
    return jnp.concatenate(parts, axis=1)


def _ordered_to_f32(key):
    return pltpu.bitcast(key ^ ((key >> 31) & 0x7FFFFFFF), F32)


def _mod_kernel(c_ref, w_ref, b_ref, o_ref):
    o_ref[...] = _dot3(_silu(c_ref[...]), w_ref[...]) + b_ref[...]


def _modulation(c, w_ada, b_ada):
    m, d = c.shape
    n = w_ada.shape[1]
    tn = 768
    return pl.pallas_call(
        _mod_kernel,
        grid=(n // tn,),
        in_specs=[pl.BlockSpec((m, d), lambda j: (0, 0)),
                  pl.BlockSpec((d, tn), lambda j: (0, j)),
                  pl.BlockSpec((1, tn), lambda j: (0, j))],
        out_specs=pl.BlockSpec((m, tn), lambda j: (0, j)),
        out_shape=jax.ShapeDtypeStruct((m, n), F32),
        compiler_params=_cparams(("arbitrary",)),
        name="modulation",
    )(c, w_ada, b_ada.reshape(1, n))


def _inproj_kernel(x_ref, shift_ref, scale_ref, g_ref, w_ref, o_ref, h_ref, *, rows):
    @pl.when(pl.program_id(1) == 0)
    def _():
        def body(r, c):
            sl = pl.ds(pl.multiple_of(r * rows, rows), rows)
            x = x_ref[sl, :]
            ms = jnp.mean(x * x, axis=-1, keepdims=True)
            y = x * lax.rsqrt(ms + NORM_EPS) * g_ref[...]
            if shift_ref.shape[0] == 1:
                h = y * (1.0 + scale_ref[...]) + shift_ref[...]
            else:
                h = y * (1.0 + scale_ref[sl, :]) + shift_ref[sl, :]
            h_ref[sl, :] = h.astype(BF16)
            return c
        lax.fori_loop(0, x_ref.shape[0] // rows, body, 0)

    o_ref[...] = jnp.dot(h_ref[...], w_ref[...], preferred_element_type=F32)


def _in_projection(x, shift, scale, norm_g, w_bf16):
    t, d = x.shape
    n = w_bf16.shape[1]
    tm = min(1024, t)
    tn = 512
    rows = min(128, tm)
    per_row = shift.shape[0] != 1
    mod_spec = (pl.BlockSpec((tm, d), lambda i, j: (i, 0)) if per_row
                else pl.BlockSpec((1, d), lambda i, j: (0, 0)))
    return pl.pallas_call(
        functools.partial(_inproj_kernel, rows=rows),
        grid=(t // tm, n // tn),
        in_specs=[pl.BlockSpec((tm, d), lambda i, j: (i, 0)),
                  mod_spec, mod_spec,
                  pl.BlockSpec((1, d), lambda i, j: (0, 0)),
                  pl.BlockSpec((d, tn), lambda i, j: (0, j))],
        out_specs=pl.BlockSpec((tm, tn), lambda i, j: (i, j)),
        out_shape=jax.ShapeDtypeStruct((t, n), F32),
        scratch_shapes=[pltpu.VMEM((tm, d), BF16)],
        compiler_params=_cparams(("arbitrary", "arbitrary")),
        name="in_projection",
    )(x, shift, scale, norm_g.reshape(1, d), w_bf16)


def _rope128(x, cos, sin_signed):
    return x * cos + pltpu.roll(x, 64, axis=1) * sin_signed


def _rope64(x, cos, sin_signed, first_half):
    rot = jnp.where(first_half, pltpu.roll(x, 96, axis=1), pltpu.roll(x, 32, axis=1))
    return x * cos + rot * sin_signed


def _prep_kernel(q_ref, k_ref, v_ref, iq_ref, sm_ref, c128_ref, s128_ref, c64_ref, s64_ref,
                 qg_ref, kg_ref,
                 qo_ref, k32_ref, kb_ref, vo_ref, iqb_ref, ik32_ref, ikb_ref, w_ref, *, transposed):
    c128, s128 = c128_ref[...], s128_ref[...]
    c64, s64 = c64_ref[...], s64_ref[...]
    tm = q_ref.shape[0]
    lane = lax.broadcasted_iota(I32, (tm, LANES), 1)
    first_half = (lane % IDX_HD) < (IDX_HD // 2)
    q_scale = ATT_HD ** -0.5 * math.log2(math.e)
    for h in range(ATT_HEADS):
        hs = slice(h * ATT_HD, (h + 1) * ATT_HD)
        q = q_ref[:, hs]
        q = q * lax.rsqrt(jnp.mean(q * q, axis=-1, keepdims=True) + NORM_EPS) * qg_ref[...]
        q = _rope128(q, c128, s128) * q_scale
        k = k_ref[:, hs]
        k = k * lax.rsqrt(jnp.mean(k * k, axis=-1, keepdims=True) + NORM_EPS) * kg_ref[...]
        k = _rope128(k, c128, s128)
        k32_ref[:, hs] = k
        kb_ref[:, hs] = k.astype(BF16)
        if transposed:
            qo_ref[hs, :] = q.T.astype(BF16)
            vo_ref[hs, :] = v_ref[:, hs].T.astype(BF16)
        else:
            qo_ref[:, hs] = q.astype(BF16)
            vo_ref[:, hs] = v_ref[:, hs].astype(BF16)
    for j in range(IDX_W // LANES):
        js = slice(j * LANES, (j + 1) * LANES)
        iqb_ref[:, js] = _rope64(iq_ref[:, js], c64, s64, first_half).astype(BF16)
    sm = sm_ref[...]
    ik = _rope64(sm, c64, s64, first_half)[:, :IDX_HD]
    ik32_ref[...] = ik
    ikb_ref[...] = ik.astype(BF16)
    w_ref[...] = sm * ((IDX_HEADS ** -0.5) * (IDX_HD ** -0.5))


def _attention_prep(p, tables, q_norm_g, k_norm_g, transposed):
    t = p.shape[0]
    tm = min(256, t)
    c128, s128, c64, s64 = tables
    wide = lambda col: pl.BlockSpec((tm, 1024), lambda i, c=col // 1024: (i, c))
    tab = pl.BlockSpec((tm, LANES), lambda i: (i, 0))
    gain = pl.BlockSpec((1, ATT_HD), lambda i: (0, 0))
    row1024 = pl.BlockSpec((tm, 1024), lambda i: (i, 0))
    col1024 = pl.BlockSpec((1024, tm), lambda i: (0, i))
    row64 = pl.BlockSpec((tm, IDX_HD), lambda i: (i, 0))
    if transposed:
        qv_spec, qv_shape = col1024, jax.ShapeDtypeStruct((ATT_W, t), BF16)
    else:
        qv_spec, qv_shape = row1024, jax.ShapeDtypeStruct((t, ATT_W), BF16)
    return pl.pallas_call(
        functools.partial(_prep_kernel, transposed=transposed),
        grid=(t // tm,),
        in_specs=[wide(COL_Q), wide(COL_K), wide(COL_V), wide(COL_IQ),
                  pl.BlockSpec((tm, LANES), lambda i: (i, COL_SMALL_A // LANES)),
                  tab, tab, tab, tab, gain, gain],
        out_specs=[qv_spec, row1024, row1024, qv_spec, row1024, row64, row64, tab],
        out_shape=[qv_shape,
                   jax.ShapeDtypeStruct((t, ATT_W), F32),
                   jax.ShapeDtypeStruct((t, ATT_W), BF16),
                   qv_shape,
                   jax.ShapeDtypeStruct((t, IDX_W), BF16),
                   jax.ShapeDtypeStruct((t, IDX_HD), F32),
                   jax.ShapeDtypeStruct((t, IDX_HD), BF16),
                   jax.ShapeDtypeStruct((t, LANES), F32)],
        compiler_params=_cparams(("arbitrary",)),
        name="attention_prep",
    )(p, p, p, p, p, c128, s128, c64, s64, q_norm_g.reshape(1, ATT_HD), k_norm_g.reshape(1, ATT_HD))


def _rope_tables(pos):
    pos = pos.astype(F32)[:, None]

    def tab(dh):
        inv = ROPE_THETA ** (-jnp.arange(0, dh, 2, dtype=F32) / dh)
        ang = pos * inv[None, :]
        cos, sin = jnp.cos(ang), jnp.sin(ang)
        c = jnp.concatenate([cos, cos], axis=-1)
        s = jnp.concatenate([-sin, sin], axis=-1)
        rep = LANES // dh
        return jnp.tile(c, (1, rep)), jnp.tile(s, (1, rep))

    c128, s128 = tab(ATT_HD)
    c64, s64 = tab(IDX_HD)
    return c128, s128, c64, s64


def _indexer_kernel(iq_ref, wt_ref, ik_ref, bias_ref, iqs_ref, *, n_sel):
    i = pl.program_id(0)
    tq = iq_ref.shape[0]
    tk = bias_ref.shape[2]
    nk_total = bias_ref.shape[1]
    nkb = i + 1

    for h in range(IDX_HEADS):
        iqs_ref[h * tq:(h + 1) * tq, :] = iq_ref[:, h * IDX_HD:(h + 1) * IDX_HD]

    key_pos = lax.broadcasted_iota(I32, (tk, tq), 0)
    qry_pos = lax.broadcasted_iota(I32, (tk, tq), 1) + i * tq

    def score_body(kb, c):
        ikt = ik_ref[pl.ds(pl.multiple_of(kb * tk, tk), tk), :]
        d = _dg(ikt, iqs_ref[...], _NT)
        acc = jnp.zeros((tk, tq), F32)
        for h in range(IDX_HEADS):
            acc = acc + jnp.maximum(d[:, h * tq:(h + 1) * tq], 0.0) * wt_ref[h:h + 1, :]
        bias_ref[0, kb] = jnp.where(key_pos + kb * tk <= qry_pos, acc, -jnp.inf)
        return c

    lax.fori_loop(0, nkb, score_body, 0)

    def bit_body(bi, ans):
        cand = ans + jnp.left_shift(jnp.int32(1), 31 - bi)
        cand_f = _ordered_to_f32(cand)

        def cnt_body(kb, cnt):
            x = bias_ref[0, kb].reshape(tk // SUBLANES, SUBLANES, tq)
            return cnt + jnp.sum(jnp.where(x >= cand_f[None], 1.0, 0.0), axis=0)

        cnt = lax.fori_loop(0, nkb, cnt_body, jnp.zeros((SUBLANES, tq), F32))
        tot = jnp.sum(cnt, axis=0, keepdims=True)
        return jnp.where(tot >= n_sel, cand, ans)

    ans = lax.fori_loop(0, 32, bit_body, jnp.full((SUBLANES, tq), INT_MIN, I32))
    thr = _ordered_to_f32(ans)[0:1, :]
    thr = jnp.where(thr != thr, -jnp.inf, thr)

    def bias_body(kb, c):
        keep = (bias_ref[0, kb] >= thr) & (key_pos + kb * tk <= qry_pos)
        bias_ref[0, kb] = jnp.where(keep, 0.0, NEG_BIG)
        return c

    lax.fori_loop(0, nkb, bias_body, 0)

    def fill_body(kb, c):
        bias_ref[0, kb] = jnp.full((tk, tq), NEG_BIG, F32)
        return c

    lax.fori_loop(nkb, nk_total, fill_body, 0)


def _prompt_indexer(iqb, wt, ikb, n_sel, tq):
    t = iqb.shape[0]
    nq = t // tq
    return pl.pallas_call(
        functools.partial(_indexer_kernel, n_sel=n_sel),
        grid=(nq,),
        in_specs=[pl.BlockSpec((tq, IDX_W), lambda i: (i, 0)),
                  pl.BlockSpec((IDX_HEADS, tq), lambda i: (0, i)),
                  pl.BlockSpec((t, IDX_HD), lambda i: (0, 0))],
        out_specs=pl.BlockSpec((1, nq, tq, tq), lambda i: (i, 0, 0, 0)),
        out_shape=jax.ShapeDtypeStruct((nq, nq, tq, tq), F32),
        scratch_shapes=[pltpu.VMEM((IDX_HEADS * tq, IDX_HD), BF16)],
        compiler_params=_cparams(("arbitrary",)),
        name="prompt_indexer",
    )(iqb, wt, ikb)


def _flash_kernel(qi_ref, ki_ref, qt_ref, k_ref, vt_ref, bias_ref, g_ref, o_ref, m_ref, l_ref, acc_ref):
    step = pl.program_id(0)
    qi, ki = qi_ref[step], ki_ref[step]

    @pl.when(ki == 0)
    def _():
        m_ref[...] = jnp.full(m_ref.shape, NEG_BIG, F32)
        l_ref[...] = jnp.zeros(l_ref.shape, F32)
        acc_ref[...] = jnp.zeros(acc_ref.shape, F32)

    bias = bias_ref[0, 0]
    heads = [slice(h * ATT_HD, (h + 1) * ATT_HD) for h in range(ATT_HEADS)]
    s = [_dg(k_ref[:, hs], qt_ref[hs, :], _NN) + bias for hs in heads]
    m_prev = m_ref[...]
    m_new = jnp.maximum(m_prev, jnp.concatenate([jnp.max(x, axis=0, keepdims=True) for x in s], axis=0))
    alpha = jnp.exp2(m_prev - m_new)
    p = [jnp.exp2(x - m_new[h:h + 1, :]) for h, x in enumerate(s)]
    l_ref[...] = alpha * l_ref[...] + jnp.concatenate([jnp.sum(x, axis=0, keepdims=True) for x in p], axis=0)
    m_ref[...] = m_new
    for h, hs in enumerate(heads):
        acc_ref[h] = alpha[h:h + 1, :] * acc_ref[h] + _dg(vt_ref[hs, :], p[h].astype(BF16), _NN)

    @pl.when(ki == qi)
    def _():
        for h in range(ATT_HEADS):
            hs = slice(h * ATT_HD, (h + 1) * ATT_HD)
            out = (acc_ref[h] / l_ref[h:h + 1, :]).T
            o_ref[:, hs] = (out * _silu(g_ref[:, hs])).astype(BF16)


def _prompt_attention(qt, kb, vt, bias, p, tq):
    t = kb.shape[0]
    nq = t // tq
    pairs = [(i, j) for i in range(nq) for j in range(i + 1)]
    qi = jnp.asarray([a for a, _ in pairs], I32)
    ki = jnp.asarray([b for _, b in pairs], I32)
    grid_spec = pltpu.PrefetchScalarGridSpec(
        num_scalar_prefetch=2,
        grid=(len(pairs),),
        in_specs=[pl.BlockSpec((ATT_W, tq), lambda s, qi, ki: (0, qi[s])),
                  pl.BlockSpec((tq, ATT_W), lambda s, qi, ki: (ki[s], 0)),
                  pl.BlockSpec((ATT_W, tq), lambda s, qi, ki: (0, ki[s])),
                  pl.BlockSpec((1, 1, tq, tq), lambda s, qi, ki: (qi[s], ki[s], 0, 0)),
                  pl.BlockSpec((tq, ATT_W), lambda s, qi, ki: (qi[s], COL_G // ATT_W))],
        out_specs=pl.BlockSpec((tq, ATT_W), lambda s, qi, ki: (qi[s], 0)),
        scratch_shapes=[pltpu.VMEM((ATT_HEADS, tq), F32),
                        pltpu.VMEM((ATT_HEADS, tq), F32),
                        pltpu.VMEM((ATT_HEADS, ATT_HD, tq), F32)],
    )
    return pl.pallas_call(
        _flash_kernel,
        grid_spec=grid_spec,
        out_shape=jax.ShapeDtypeStruct((t, ATT_W), BF16),
        compiler_params=_cparams(("arbitrary",)),
        name="prompt_attention",
    )(qi, ki, qt, kb, vt, bias, p)


def _sample_attn_kernel(pt_ref, q_ref, iq_ref, w_ref, ikn_ref, kn_ref, vn_ref, g_ref, ex_ref,
                        cik_ref, ck_ref, cv_ref, o_ref,
                        ikbuf, kbuf, vbuf, sc_ref, sce_ref, isem, ksem, vsem, *, n_sel):
    b = pl.program_id(0)
    n_pages = ikbuf.shape[0]
    page_rows = PAGE_SIZE * ATT_HEADS

    def idx_copy(j):
        return pltpu.make_async_copy(cik_ref.at[pt_ref[b, j]], ikbuf.at[j], isem.at[j])

    def k_copy(j, slot):
        return pltpu.make_async_copy(ck_ref.at[pt_ref[b, j]], kbuf.at[slot], ksem.at[slot])

    def v_copy(j, slot):
        return pltpu.make_async_copy(cv_ref.at[pt_ref[b, j]], vbuf.at[slot], vsem.at[slot])

    def start_idx(j, c):
        idx_copy(j).start()
        return c

    lax.fori_loop(0, n_pages, start_idx, 0)
    k_copy(0, 0).start()
    v_copy(0, 0).start()

    iq = iq_ref[0]
    w = w_ref[0]

    def score_body(j, c):
        idx_copy(j).wait()
        d = _dg(iq, ikbuf[j].astype(BF16), _NT)
        sc_ref[pl.ds(j, 1), :] = jnp.sum(jnp.maximum(d, 0.0) * w, axis=0, keepdims=True)
        return c

    lax.fori_loop(0, n_pages, score_body, 0)
    d_self = jnp.sum(iq.astype(F32) * ikn_ref[0].astype(F32), axis=-1, keepdims=True)
    s_self = jnp.sum(jnp.maximum(d_self, 0.0) * w, axis=0, keepdims=True)

    sc = sc_ref[...]

    def bit_body(bi, ans):
        cand = ans + jnp.left_shift(jnp.int32(1), 31 - bi)
        cand_f = _ordered_to_f32(cand)
        cnt = jnp.sum(jnp.where(sc >= cand_f, 1.0, 0.0), axis=-1, keepdims=True)
        cnt = jnp.sum(cnt, axis=0, keepdims=True) + jnp.where(s_self >= cand_f, 1.0, 0.0)
        return jnp.where(cnt >= n_sel, cand, ans)

    ans = lax.fori_loop(0, 32, bit_body, jnp.full((1, 1), INT_MIN, I32))
    thr = _ordered_to_f32(ans)
    thr = jnp.where(thr != thr, -jnp.inf, thr)

    sce_ref[...] = _dot_exact_rhs(sc, ex_ref[...])

    q8 = q_ref[0]
    own = (lax.broadcasted_iota(I32, (ATT_HEADS, page_rows), 1) % ATT_HEADS
           == lax.broadcasted_iota(I32, (ATT_HEADS, page_rows), 0))

    kn = kn_ref[0].astype(BF16).astype(F32)
    s0 = jnp.sum(q8.astype(F32) * kn, axis=-1, keepdims=True)
    m0 = jnp.where(s_self >= thr, s0, NEG_BIG)
    l0 = jnp.where(m0 > 0.5 * NEG_BIG, 1.0, 0.0)
    acc0 = l0 * vn_ref[0].astype(BF16).astype(F32)

    def att_body(j, carry):
        m_prev, l_prev, acc = carry
        slot = j % 2

        @pl.when(j + 1 < n_pages)
        def _():
            k_copy(j + 1, 1 - slot).start()
            v_copy(j + 1, 1 - slot).start()

        k_copy(j, slot).wait()
        v_copy(j, slot).wait()
        kf = kbuf[slot].reshape(page_rows, ATT_HD).astype(BF16)
        vf = vbuf[slot].reshape(page_rows, ATT_HD).astype(BF16)
        s = _dg(q8, kf, _NT)
        keep = own & (sce_ref[pl.ds(j, 1), :] >= thr)
        s = jnp.where(keep, s, NEG_BIG)
        m_new = jnp.maximum(m_prev, jnp.max(s, axis=-1, keepdims=True))
        alpha = jnp.exp2(m_prev - m_new)
        p = jnp.exp2(s - m_new)
        l_new = alpha * l_prev + jnp.sum(p, axis=-1, keepdims=True)
        return m_new, l_new, alpha * acc + jnp.dot(p.astype(BF16), vf, preferred_element_type=F32)

    m, l, acc = lax.fori_loop(0, n_pages, att_body, (m0, l0, acc0))
    o_ref[0] = acc / l * _silu(g_ref[0])


def _sample_attention(page_table, qb, iqb, wv, ikb, k32, v32, g, cache_idx_k, cache_k, cache_v, n_sel):
    nb, n_pages = page_table.shape
    head3 = pl.BlockSpec((1, ATT_HEADS, ATT_HD), lambda b, pt: (b, 0, 0))
    any_spec = pl.BlockSpec(memory_space=pl.ANY)
    page_rows = PAGE_SIZE * ATT_HEADS
    expand = (jnp.arange(page_rows)[None, :] // ATT_HEADS == jnp.arange(PAGE_SIZE)[:, None]).astype(BF16)
    grid_spec = pltpu.PrefetchScalarGridSpec(
        num_scalar_prefetch=1,
        grid=(nb,),
        in_specs=[head3,
                  pl.BlockSpec((1, IDX_HEADS, IDX_HD), lambda b, pt: (b, 0, 0)),
                  pl.BlockSpec((1, IDX_HEADS, 1), lambda b, pt: (b, 0, 0)),
                  pl.BlockSpec((1, 1, IDX_HD), lambda b, pt: (b, 0, 0)),
                  head3, head3, head3,
                  pl.BlockSpec((PAGE_SIZE, page_rows), lambda b, pt: (0, 0)),
                  any_spec, any_spec, any_spec],
        out_specs=head3,
        scratch_shapes=[pltpu.VMEM((n_pages, PAGE_SIZE, IDX_HD), F32),
                        pltpu.VMEM((2, PAGE_SIZE, ATT_HEADS, ATT_HD), F32),
                        pltpu.VMEM((2, PAGE_SIZE, ATT_HEADS, ATT_HD), F32),
                        pltpu.VMEM((n_pages, PAGE_SIZE), F32),
                        pltpu.VMEM((n_pages, page_rows), F32),
                        pltpu.SemaphoreType.DMA((n_pages,)),
                        pltpu.SemaphoreType.DMA((2,)),
                        pltpu.SemaphoreType.DMA((2,))],
    )
    head = lambda a: a.reshape(nb, ATT_HEADS, ATT_HD)
    out = pl.pallas_call(
        functools.partial(_sample_attn_kernel, n_sel=n_sel),
        grid_spec=grid_spec,
        out_shape=jax.ShapeDtypeStruct((nb, ATT_HEADS, ATT_HD), F32),
        compiler_params=_cparams(("arbitrary",)),
        name="sample_attention",
    )(page_table, head(qb),
      iqb.reshape(nb, IDX_HEADS, IDX_HD),
      wv[:, IDX_HD:IDX_HD + IDX_HEADS].reshape(nb, IDX_HEADS, 1),
      ikb.reshape(nb, 1, IDX_HD),
      head(k32), head(v32), head(g), expand,
      cache_idx_k, cache_k, cache_v)
    return out.reshape(nb, ATT_W).astype(BF16)


def _rwkv_prep_kernel(*refs, shifted):
    n_x = 5
    x_refs = refs[:n_x]
    if shifted:
        halo_refs, first_refs = refs[n_x:2 * n_x], refs[2 * n_x:3 * n_x]
        rest = refs[3 * n_x:]
    else:
        prev_refs = refs[n_x:2 * n_x]
        rest = refs[2 * n_x:]
    (mu_ref, mus_ref, wl_ref, w0_ref, a0_ref, kk_ref, ka_ref, rk_ref, ones_ref,
     ro_ref, kpo_ref, vo_ref, kko_ref, bo_ref, lwo_ref, bonus_ref, sg_ref) = rest

    def lerp(j, mu):
        x = x_refs[j][...]
        if shifted:
            above = jnp.where(pl.program_id(0) == 0, first_refs[j][...],
                              halo_refs[j][SUBLANES - 1:SUBLANES, :])
            row = lax.broadcasted_iota(I32, x.shape, 0)
            prev = jnp.where(row == 0, above, pltpu.roll(x, 1, axis=0))
        else:
            prev = prev_refs[j][...]
        return x + mu * (prev - x)

    w = RWKV_W
    xr = lerp(0, mu_ref[:, 0:w])
    xk = lerp(1, mu_ref[:, w:2 * w])
    xv = lerp(2, mu_ref[:, 2 * w:3 * w])
    xg = lerp(3, mu_ref[:, 3 * w:4 * w])
    xs = lerp(4, mus_ref[...])
    lane = lax.broadcasted_iota(I32, xs.shape, 1)
    lora_in = jnp.where(lane < DECAY_LORA, jnp.tanh(xs), xs)
    lora = _dot3(lora_in, wl_ref[...])
    z = w0_ref[...] + lora[:, :w]
    softplus_neg = jnp.maximum(-z, 0.0) + jnp.log(1.0 + jnp.exp(-jnp.abs(z)))
    w_log = -softplus_neg - 0.5
    lwo_ref[...] = -jnp.exp(w_log)
    a = jax.nn.sigmoid(a0_ref[...] + lora[:, w:])
    ones_bd = ones_ref[...]
    kk = xk * kk_ref[...]
    kk = kk / jnp.maximum(jnp.sqrt(_head_sum(kk * kk, ones_bd)), 1e-12)
    kp = xk * (1.0 + (a - 1.0) * ka_ref[...])
    ro_ref[...] = xr
    kpo_ref[...] = kp
    vo_ref[...] = xv
    kko_ref[...] = kk
    bo_ref[...] = kk * a
    bonus_ref[...] = _head_sum(xr * kp * rk_ref[...], ones_bd) * xv
    sg_ref[...] = _silu(xg)


def _rwkv_prep(p, prev, consts, shifted):
    t = p.shape[0]
    tm = min(256, t)
    mu_big, mu_small, w_lora, w0, a0, k_k, k_a, r_k, ones_bd = consts
    cols = [(RWKV_W, COL_RR // RWKV_W), (RWKV_W, COL_RK // RWKV_W), (RWKV_W, COL_RV // RWKV_W),
            (RWKV_W, COL_RG // RWKV_W), (LANES, COL_SMALL_R // LANES)]
    x_specs = [pl.BlockSpec((tm, wd), lambda i, c=c: (i, c)) for wd, c in cols]
    if shifted:
        per8 = tm // SUBLANES
        halo = [pl.BlockSpec((SUBLANES, wd), lambda i, c=c: (jnp.maximum(i * per8 - 1, 0), c))
                for wd, c in cols]
        first = [pl.BlockSpec((1, wd), lambda i, c=c: (0, c)) for wd, c in cols]
        extra_specs, extra_args = halo + first, [p] * 5 + [prev] * 5
    else:
        extra_specs, extra_args = x_specs, [prev] * 5
    full = lambda a: pl.BlockSpec(a.shape, lambda i: (0, 0))
    row = pl.BlockSpec((tm, RWKV_W), lambda i: (i, 0))
    vec = lambda a: a.reshape(1, -1)
    args = [mu_big, mu_small, w_lora, vec(w0), vec(a0), vec(k_k), vec(k_a), vec(r_k), ones_bd]
    return pl.pallas_call(
        functools.partial(_rwkv_prep_kernel, shifted=shifted),
        grid=(t // tm,),
        in_specs=x_specs + extra_specs + [full(a) for a in args],
        out_specs=[row] * 8,
        out_shape=[jax.ShapeDtypeStruct((t, RWKV_W), F32)] * 8,
        compiler_params=_cparams(("arbitrary",)),
        name="rwkv_prep",
    )(*([p] * 5), *extra_args, *args)


def _scan_kernel(r_ref, kp_ref, v_ref, kk_ref, b_ref, lw_ref, s0_ref, y_ref, st_ref, s_ref):
    @pl.when(pl.program_id(0) == 0)
    def _():
        s_ref[...] = s0_ref[...]

    ch = SCAN_CHUNK
    assert ch == RWKV_HD
    rows_total = r_ref.shape[0]
    n_sub = rows_total // ch
    gw = MXU_DIM
    nh = HEADS_PER_GROUP
    lw = lw_ref[...]
    tr = lax.broadcasted_iota(I32, (rows_total, rows_total), 0)
    tc = lax.broadcasted_iota(I32, (rows_total, rows_total), 1)
    tri = jnp.where((tr >= tc) & (tr // ch == tc // ch), 1.0, 0.0).astype(BF16)
    cum = _dot_exact_lhs(tri, lw)
    cum_end = jnp.concatenate(
        [jnp.broadcast_to(cum[(c + 1) * ch - 1:(c + 1) * ch, :], (ch, RWKV_W)) for c in range(n_sub)], axis=0)
    kk, b, kp, r, v = kk_ref[...], b_ref[...], kp_ref[...], r_ref[...], v_ref[...]
    g_inv = jnp.exp(-cum)
    g_end = jnp.exp(cum_end - cum)
    a_t = (-kk * jnp.exp(cum - lw)).astype(BF16)
    b_t = b * g_inv
    k_t = kp * g_inv
    r_t = (r * jnp.exp(cum)).astype(BF16)
    b_h = (b * g_end).astype(BF16)
    k_h = (kp * g_end).astype(BF16)
    vb = v.astype(BF16)

    srow = lax.broadcasted_iota(I32, (nh * ch, gw), 0)
    scol = lax.broadcasted_iota(I32, (nh * ch, gw), 1)
    own = (srow // ch) == (scol // RWKV_HD)
    step = lax.broadcasted_iota(I32, (ch, gw), 0)
    other = lax.broadcasted_iota(I32, (ch, gw), 1) % ch
    strict, incl = step > other, step >= other
    eye = jnp.where(step == other, 1.0, 0.0)

    def stack(x):
        return jnp.where(own, jnp.concatenate([x] * nh, axis=0), 0.0).astype(BF16)

    groups = range(N_GROUPS)
    units = [(c, g) for c in range(n_sub) for g in groups]
    blk = {(c, g): (slice(c * ch, (c + 1) * ch), slice(g * gw, (g + 1) * gw)) for c, g in units}
    rb = {u: stack(b_t[blk[u]]) for u in units}
    rk = {u: stack(k_t[blk[u]]) for u in units}
    v4 = {u: stack(v[blk[u]]) for u in units}
    n_row = {u: jnp.where(strict, _dg(a_t[blk[u]], rb[u], _NT), 0.0) for u in units}
    bm_row = {u: jnp.where(strict, _dg(a_t[blk[u]], rk[u], _NT), 0.0).astype(BF16) for u in units}
    rb_row = {u: jnp.where(incl, _dg(r_t[blk[u]], rb[u], _NT), 0.0).astype(BF16) for u in units}
    rk_row = {u: jnp.where(incl, _dg(r_t[blk[u]], rk[u], _NT), 0.0).astype(BF16) for u in units}
    minv = {u: eye + n_row[u] for u in units}
    pw = n_row
    pw_bd = {u: stack(pw[u]) for u in units}
    for _ in range(int(math.log2(ch)) - 1):
        pw = {u: _dg(pw[u].astype(BF16), pw_bd[u], _NN) for u in units}
        pw_bd = {u: stack(pw[u]) for u in units}
        minv = {u: minv[u] + _dg(minv[u].astype(BF16), pw_bd[u], _NN) for u in units}
    minv = {u: minv[u].astype(BF16) for u in units}
    y_fixed = {u: _dg(rk_row[u], v4[u], _NN) for u in units}
    x_fixed = {u: _dg(bm_row[u], v4[u], _NN) for u in units}

    s = [s_ref[g] for g in groups]
    for c in range(n_sub):
        sb = [x.astype(BF16) for x in s]
        x_row = [_dg(a_t[blk[c, g]], sb[g], _NT) + x_fixed[c, g] for g in groups]
        x_bd = [stack(x) for x in x_row]
        u_row = [_dg(minv[c, g], x_bd[g], _NN) for g in groups]
        u_bd = [stack(x) for x in u_row]
        for g in groups:
            y_ref[blk[c, g]] = _dg(r_t[blk[c, g]], sb[g], _NT) + _dg(rb_row[c, g], u_bd[g], _NN) + y_fixed[c, g]
        for g in groups:
            rs, ls = blk[c, g]
            upd = _dg(jnp.concatenate([u_row[g].astype(BF16), vb[rs, ls]], axis=0),
                      jnp.concatenate([b_h[rs, ls], k_h[rs, ls]], axis=0), _TN)
            g_c = jnp.exp(cum[(c + 1) * ch - 1:(c + 1) * ch, ls])
            s[g] = s[g] * g_c + jnp.where(own, upd, 0.0)
    for g in groups:
        s_ref[g] = s[g]

    @pl.when(pl.program_id(0) == pl.num_programs(0) - 1)
    def _():
        st_ref[...] = s_ref[...]


def _rwkv_scan(r, kp, v, kk, b, lw, s0):
    t = r.shape[0]
    ch = SCAN_CHUNK * min(SCAN_SUB, t // SCAN_CHUNK)
    row = pl.BlockSpec((ch, RWKV_W), lambda i: (i, 0))
    st = pl.BlockSpec((N_GROUPS, MXU_DIM, MXU_DIM), lambda i: (0, 0, 0))
    return pl.pallas_call(
        _scan_kernel,
        grid=(t // ch,),
        in_specs=[row] * 6 + [st],
        out_specs=[row, st],
        out_shape=[jax.ShapeDtypeStruct((t, RWKV_W), F32),
                   jax.ShapeDtypeStruct((N_GROUPS, MXU_DIM, MXU_DIM), F32)],
        scratch_shapes=[pltpu.VMEM((N_GROUPS, MXU_DIM, MXU_DIM), F32)],
        compiler_params=_cparams(("arbitrary",)),
        name="rwkv_scan",
    )(r, kp, v, kk, b, lw, s0)


def _rwkv_step_kernel(s_ref, r_ref, kp_ref, kk_ref, b_ref, lw_ref, v_ref, so_ref, y_ref):
    s = s_ref[0]
    kk = kk_ref[0]
    sa = -jnp.sum(s * kk, axis=-1, keepdims=True)
    s_new = s * jnp.exp(lw_ref[0]) + sa * b_ref[0] + v_ref[0] * kp_ref[0]
    so_ref[0] = s_new
    y_ref[0] = jnp.sum(s_new * r_ref[0], axis=-1, keepdims=True)


def _rwkv_step(state, r, kp, kk, b, lw, v):
    nb = state.shape[0]
    rowv = lambda a: a.reshape(nb, RWKV_HEADS, 1, RWKV_HD)
    rspec = pl.BlockSpec((1, RWKV_HEADS, 1, RWKV_HD), lambda i: (i, 0, 0, 0))
    cspec = pl.BlockSpec((1, RWKV_HEADS, RWKV_HD, 1), lambda i: (i, 0, 0, 0))
    sspec = pl.BlockSpec((1, RWKV_HEADS, RWKV_HD, RWKV_HD), lambda i: (i, 0, 0, 0))
    s_new, y = pl.pallas_call(
        _rwkv_step_kernel,
        grid=(nb,),
        in_specs=[sspec, rspec, rspec, rspec, rspec, rspec, cspec],
        out_specs=[sspec, cspec],
        out_shape=[jax.ShapeDtypeStruct(state.shape, F32),
                   jax.ShapeDtypeStruct((nb, RWKV_HEADS, RWKV_HD, 1), F32)],
        compiler_params=_cparams(("arbitrary",)),
        name="rwkv_step",
    )(state, rowv(r), rowv(kp), rowv(kk), rowv(b), rowv(lw), v.reshape(nb, RWKV_HEADS, RWKV_HD, 1))
    return s_new, y.reshape(nb, RWKV_W)


def _rwkv_post_kernel(y_ref, bonus_ref, sg_ref, lg_ref, lb_ref, ones_ref, o_ref):
    y = y_ref[...]
    ones_bd = ones_ref[...]
    mean = _head_sum(y, ones_bd) * (1.0 / RWKV_HD)
    yc = y - mean
    var = _head_sum(yc * yc, ones_bd) * (1.0 / RWKV_HD)
    yn = yc * lax.rsqrt(var + GN_EPS) * lg_ref[...] + lb_ref[...]
    o_ref[...] = ((yn + bonus_ref[...]) * sg_ref[...]).astype(BF16)


def _rwkv_post(y, bonus, sg, ln_g, ln_b, ones_bd):
    t = y.shape[0]
    tm = min(512, t)
    row = pl.BlockSpec((tm, RWKV_W), lambda i: (i, 0))
    vec = pl.BlockSpec((1, RWKV_W), lambda i: (0, 0))
    return pl.pallas_call(
        _rwkv_post_kernel,
        grid=(t // tm,),
        in_specs=[row, row, row, vec, vec, pl.BlockSpec(ones_bd.shape, lambda i: (0, 0))],
        out_specs=row,
        out_shape=jax.ShapeDtypeStruct((t, RWKV_W), BF16),
        compiler_params=_cparams(("arbitrary",)),
        name="rwkv_post",
    )(y, bonus, sg, ln_g.reshape(1, -1), ln_b.reshape(1, -1), ones_bd)


def _outproj_kernel(att_ref, rw_ref, wa_ref, wr_ref, x_ref, gate_ref, o_ref):
    out = (jnp.dot(att_ref[...], wa_ref[...], preferred_element_type=F32)
           + jnp.dot(rw_ref[...], wr_ref[...], preferred_element_type=F32))
    o_ref[...] = x_ref[...] + gate_ref[...] * out


def _out_projection(att, rw, w_out_bf16, x, gate):
    t, d = x.shape
    tm = min(512, t)
    per_row = gate.shape[0] != 1
    gspec = (pl.BlockSpec((tm, d), lambda i: (i, 0)) if per_row else pl.BlockSpec((1, d), lambda i: (0, 0)))
    return pl.pallas_call(
        _outproj_kernel,
        grid=(t // tm,),
        in_specs=[pl.BlockSpec((tm, ATT_W), lambda i: (i, 0)),
                  pl.BlockSpec((tm, RWKV_W), lambda i: (i, 0)),
                  pl.BlockSpec((ATT_W, d), lambda i: (0, 0)),
                  pl.BlockSpec((RWKV_W, d), lambda i: (1, 0)),
                  pl.BlockSpec((tm, d), lambda i: (i, 0)),
                  gspec],
        out_specs=pl.BlockSpec((tm, d), lambda i: (i, 0)),
        out_shape=jax.ShapeDtypeStruct((t, d), F32),
        compiler_params=_cparams(("arbitrary",)),
        name="out_projection",
    )(att, rw, w_out_bf16, w_out_bf16, x, gate)


def _reorder_cols(w):
    a = 0
    q, k, v, g = (w[..., a + i * ATT_W:a + (i + 1) * ATT_W] for i in range(4))
    a = 4 * ATT_W
    iq = w[..., a:a + IDX_W]
    iw = w[..., a + IDX_W:a + IDX_W + IDX_HEADS]
    ik = w[..., a + IDX_W + IDX_HEADS:a + IDX_W + IDX_HEADS + IDX_HD]
    o = N_ATT_COLS_ORIG
    rr, rk, rv = (w[..., o + i * RWKV_W:o + (i + 1) * RWKV_W] for i in range(3))
    o2 = o + 3 * RWKV_W
    wl = w[..., o2:o2 + DECAY_LORA]
    al = w[..., o2 + DECAY_LORA:o2 + DECAY_LORA + ICL_LORA]
    rg = w[..., o2 + DECAY_LORA + ICL_LORA:o2 + DECAY_LORA + ICL_LORA + RWKV_W]
    pad_a = jnp.zeros(w.shape[:-1] + (LANES - IDX_HD - IDX_HEADS,), w.dtype)
    pad_end = jnp.zeros(w.shape[:-1] + (N_COLS_PAD - N_COLS,), w.dtype)
    return jnp.concatenate([q, k, v, g, iq, rr, rk, rv, rg, ik, iw, pad_a, wl, al, pad_end], axis=-1)


def _rwkv_cols_original_order(p):
    return jnp.concatenate([p[..., COL_RR:COL_RR + RWKV_W], p[..., COL_RK:COL_RK + RWKV_W],
                            p[..., COL_RV:COL_RV + RWKV_W],
                            p[..., COL_SMALL_R:COL_SMALL_R + DECAY_LORA + ICL_LORA],
                            p[..., COL_RG:COL_RG + RWKV_W]], axis=-1)


def _blockdiag_to_state(sb):
    sb = sb.reshape(N_GROUPS, HEADS_PER_GROUP, RWKV_HD, HEADS_PER_GROUP, RWKV_HD)
    return jnp.stack([sb[:, h, :, h, :] for h in range(HEADS_PER_GROUP)], axis=1).reshape(
        RWKV_HEADS, RWKV_HD, RWKV_HD)


def kernel(x_prompt, x_sample, c_prompt, c_sample, cache_k, cache_v, cache_idx_k, state_wkv, state_shift,
           page_table, w_ada, b_ada, norm_g, w_in, q_norm_g, k_norm_g, shift_mu, w0, w_b, a0, a_b,
           k_k, k_a, r_k, ln_x_g, ln_x_b, w_out):
    depth = w_ada.shape[0]
    assert depth == 1, "single-layer trunk"
    bsz, seq, d = x_prompt.shape
    assert bsz == 1
    nb, dec_seq, _ = x_sample.shape
    assert dec_seq == 1
    n_pages = page_table.shape[1]
    past_len = n_pages * PAGE_SIZE
    l = 0

    w_in_r = _reorder_cols(w_in[l].astype(BF16))
    w_out_b = w_out[l].astype(BF16)
    mu = _reorder_cols(jnp.concatenate([jnp.zeros((N_ATT_COLS_ORIG,), F32), shift_mu[l]]))
    mu_big = mu[COL_RR:COL_RR + 4 * RWKV_W].reshape(1, -1)
    mu_small = mu[COL_SMALL_R:COL_SMALL_R + LANES].reshape(1, -1)
    zl = jnp.zeros((DECAY_LORA, RWKV_W), F32)
    w_lora = jnp.concatenate([jnp.concatenate([w_b[l], zl], axis=1),
                              jnp.concatenate([zl, a_b[l]], axis=1)], axis=0)
    hid = jnp.arange(LANES) // RWKV_HD
    ones_bd = (hid[:, None] == hid[None, :]).astype(BF16)
    rwkv_consts = (mu_big, mu_small, w_lora, w0[l], a0[l], k_k[l], k_a[l], r_k[l].reshape(-1), ones_bd)

    n_mod = 1 + nb
    c_all = jnp.concatenate([c_prompt, c_sample, jnp.zeros((-n_mod % SUBLANES, d), F32)], axis=0)
    mod = _modulation(c_all, w_ada[l], b_ada[l])
    shift_p, scale_p, gate_p = mod[0:1, :d], mod[0:1, d:2 * d], mod[0:1, 2 * d:]
    shift_s, scale_s, gate_s = mod[1:n_mod, :d], mod[1:n_mod, d:2 * d], mod[1:n_mod, 2 * d:]

    xp = x_prompt.reshape(seq, d)
    xs = x_sample.reshape(nb, d)

    p = _in_projection(xp, shift_p, scale_p, norm_g[l], w_in_r)
    qt, k32, kb, vt, iqb, ik32, ikb, wv = _attention_prep(p, _rope_tables(jnp.arange(seq)),
                                                         q_norm_g[l], k_norm_g[l], transposed=True)
    n_sel = min(TOPK_MAX, seq // 4)
    tq = min(MXU_DIM, seq)
    bias = _prompt_indexer(iqb, wv[:, IDX_HD:IDX_HD + IDX_HEADS].T, ikb, n_sel, tq)
    att = _prompt_attention(qt, kb, vt, bias, p, tq)

    r_, kp_, v_, kk_, b_, lw_, bonus, sg = _rwkv_prep(p, jnp.zeros((1, p.shape[1]), F32), rwkv_consts,
                                                      shifted=True)
    y_scan, s_fin = _rwkv_scan(r_, kp_, v_, kk_, b_, lw_,
                               jnp.zeros((N_GROUPS, MXU_DIM, MXU_DIM), F32))
    rw = _rwkv_post(y_scan, bonus, sg, ln_x_g[l], ln_x_b[l], ones_bd)
    y_p = _out_projection(att, rw, w_out_b, xp, gate_p)

    k_prompt = k32.reshape(1, 1, seq, ATT_HEADS, ATT_HD)
    v_prompt = p[:, COL_V:COL_V + ATT_W].reshape(1, 1, seq, ATT_HEADS, ATT_HD)
    idx_k_prompt = ik32.reshape(1, 1, seq, IDX_HD)
    wkv_prompt = _blockdiag_to_state(s_fin).reshape(1, 1, RWKV_HEADS, RWKV_HD, RWKV_HD)
    shift_prompt = _rwkv_cols_original_order(p[seq - 1:seq]).reshape(1, 1, -1)

    ps = _in_projection(xs, shift_s, scale_s, norm_g[l], w_in_r)
    pos_s = jnp.full((nb,), past_len, jnp.int32)
    qb_s, k32_s, _, _, iqb_s, ik32_s, ikb_s, wv_s = _attention_prep(ps, _rope_tables(pos_s),
                                                                   q_norm_g[l], k_norm_g[l],
                                                                   transposed=False)
    v32_s = ps[:, COL_V:COL_V + ATT_W]
    n_sel_s = min(TOPK_MAX, (past_len + 1) // 4)
    att_s = _sample_attention(page_table, qb_s, iqb_s, wv_s, ikb_s, k32_s, v32_s,
                              ps[:, COL_G:COL_G + ATT_W], cache_idx_k[l], cache_k[l], cache_v[l], n_sel_s)

    prev_s = _reorder_cols(jnp.concatenate([jnp.zeros((nb, N_ATT_COLS_ORIG), F32), state_shift[l]], axis=-1))
    r_s, kp_s, v_s, kk_s, b_s, lw_s, bonus_s, sg_s = _rwkv_prep(ps, prev_s, rwkv_consts, shifted=False)
    wkv_new, y_step = _rwkv_step(state_wkv[l], r_s, kp_s, kk_s, b_s, lw_s, v_s)
    rw_s = _rwkv_post(y_step, bonus_s, sg_s, ln_x_g[l], ln_x_b[l], ones_bd)
    y_s = _out_projection(att_s, rw_s, w_out_b, xs, gate_s)

    k_sample = k32_s.reshape(1, nb, 1, ATT_HEADS, ATT_HD)
    v_sample = v32_s.reshape(1, nb, 1, ATT_HEADS, ATT_HD)
    idx_k_sample = ik32_s.reshape(1, nb, 1, IDX_HD)
    wkv_sample = wkv_new.reshape(1, nb, RWKV_HEADS, RWKV_HD, RWKV_HD)
    shift_sample = _rwkv_cols_original_order(ps).reshape(1, nb, -1)

    return (y_p.reshape(1, seq, d), y_s.reshape(nb, 1, d),
            k_prompt, v_prompt, idx_k_prompt, wkv_prompt, shift_prompt,
            k_sample, v_sample, idx_k_sample, wkv_sample, shift_sample)
```

```python
import functools
import math

import jax
import jax.numpy as jnp
from jax import lax
from jax.experimental import pallas as pl
from jax.experimental.pallas import tpu as pltpu

F32 = jnp.float32
BF16 = jnp.bfloat16
I32 = jnp.int32

ATT_HEADS = 8
ATT_HD = 128
ATT_W = ATT_HEADS * ATT_HD
IDX_HEADS = 16
IDX_HD = 64
IDX_W = IDX_HEADS * IDX_HD
TOPK_MAX = 256
PAGE_SIZE = 128
ROPE_THETA = 10000.0
RWKV_HD = 64
RWKV_HEADS = 16
RWKV_W = RWKV_HEADS * RWKV_HD
DECAY_LORA = 64
ICL_LORA = 64
NORM_EPS = 1e-6
GN_EPS = 64e-5

LANES = 128
SUBLANES = 8
MXU_DIM = 256
VMEM_LIMIT = 56 * 1024 * 1024
NEG_BIG = -1e30
INT_MIN = -(2 ** 31)

COL_Q, COL_K, COL_V, COL_G, COL_IQ = 0, 1024, 2048, 3072, 4096
COL_RR, COL_RK, COL_RV, COL_RG = 5120, 6144, 7168, 8192
COL_SMALL_A = 9216
COL_SMALL_R = 9344
N_COLS = 9472
N_COLS_PAD = 9728
N_ATT_COLS_ORIG = 4 * ATT_W + IDX_W + IDX_HEADS + IDX_HD

SCAN_CHUNK = 64
DECODE_PAGES_PER_STEP = 4
SCAN_SUB = 4
HEADS_PER_GROUP = MXU_DIM // RWKV_HD
N_GROUPS = RWKV_HEADS // HEADS_PER_GROUP


def _cparams(sem, vmem=VMEM_LIMIT):
    return pltpu.CompilerParams(dimension_semantics=sem, vmem_limit_bytes=vmem)


def _split2(x):
    hi = x.astype(BF16)
    lo = (x - hi.astype(F32)).astype(BF16)
    return hi, lo


def _split3(x):
    hi = x.astype(BF16)
    r = x - hi.astype(F32)
    mid = r.astype(BF16)
    lo = (r - mid.astype(F32)).astype(BF16)
    return hi, mid, lo


_NN = (((1,), (0,)), ((), ()))
_NT = (((1,), (1,)), ((), ()))
_TN = (((0,), (0,)), ((), ()))


def _dg(a, b, dims):
    return lax.dot_general(a, b, dims, preferred_element_type=F32)


def _dot1(a, b, dims=_NN):
    return _dg(a.astype(BF16), b.astype(BF16), dims)


def _dot3(a, b, dims=_NN):
    ah, al = _split2(a)
    bh, bl = _split2(b)
    return _dg(ah, bh, dims) + _dg(al, bh, dims) + _dg(ah, bl, dims)


def _dot_exact_lhs(a_bf16, b, dims=_NN):
    b0, b1, b2 = _split3(b)
    return _dg(a_bf16, b0, dims) + _dg(a_bf16, b1, dims) + _dg(a_bf16, b2, dims)


def _dot_exact_rhs(a, b_bf16, dims=_NN):
    a0, a1, a2 = _split3(a)
    return _dg(a0, b_bf16, dims) + _dg(a1, b_bf16, dims) + _dg(a2, b_bf16, dims)


def _silu(x):
    return x * jax.nn.sigmoid(x)


def _head_sum(x, ones_bd):
    parts = []
    for j in range(x.shape[1] // LANES):
        parts.append(_dot_exact_rhs(x[:, j * LANES:(j + 1) * LANES], ones_bd))
    return jnp.concatenate(parts, axis=1)


def _ordered_to_f32(key):
    return pltpu.bitcast(key ^ ((key >> 31) & 0x7FFFFFFF), F32)


def _mod_kernel(c_ref, w_ref, b_ref, o_ref):
    o_ref[...] = _dot3(_silu(c_ref[...]), w_ref[...]) + b_ref[...]


def _modulation(c, w_ada, b_ada):
    m, d = c.shape
    n = w_ada.shape[1]
    tn = 768
    return pl.pallas_call(
        _mod_kernel,
        grid=(n // tn,),
        in_specs=[pl.BlockSpec((m, d), lambda j: (0, 0)),
                  pl.BlockSpec((d, tn), lambda j: (0, j)),
                  pl.BlockSpec((1, tn), lambda j: (0, j))],
        out_specs=pl.BlockSpec((m, tn), lambda j: (0, j)),
        out_shape=jax.ShapeDtypeStruct((m, n), F32),
        compiler_params=_cparams(("arbitrary",)),
        name="modulation",
    )(c, w_ada, b_ada.reshape(1, n))


def _inproj_kernel(x_ref, shift_ref, scale_ref, g_ref, w_ref, o_ref, h_ref, *, rows):
    @pl.when(pl.program_id(1) == 0)
    def _():
        def body(r, c):
            sl = pl.ds(pl.multiple_of(r * rows, rows), rows)
            x = x_ref[sl, :]
            ms = jnp.mean(x * x, axis=-1, keepdims=True)
            y = x * lax.rsqrt(ms + NORM_EPS) * g_ref[...]
            if shift_ref.shape[0] == 1:
                h = y * (1.0 + scale_ref[...]) + shift_ref[...]
            else:
                h = y * (1.0 + scale_ref[sl, :]) + shift_ref[sl, :]
            h_ref[sl, :] = h.astype(BF16)
            return c
        lax.fori_loop(0, x_ref.shape[0] // rows, body, 0)

    o_ref[...] = jnp.dot(h_ref[...], w_ref[...], preferred_element_type=F32)


def _in_projection(x, shift, scale, norm_g, w_bf16):
    t, d = x.shape
    n = w_bf16.shape[1]
    tm = min(1024, t)
    tn = 512
    rows = min(128, tm)
    per_row = shift.shape[0] != 1
    mod_spec = (pl.BlockSpec((tm, d), lambda i, j: (i, 0)) if per_row
                else pl.BlockSpec((1, d), lambda i, j: (0, 0)))
    return pl.pallas_call(
        functools.partial(_inproj_kernel, rows=rows),
        grid=(t // tm, n // tn),
        in_specs=[pl.BlockSpec((tm, d), lambda i, j: (i, 0)),
                  mod_spec, mod_spec,
                  pl.BlockSpec((1, d), lambda i, j: (0, 0)),
                  pl.BlockSpec((d, tn), lambda i, j: (0, j))],
        out_specs=pl.BlockSpec((tm, tn), lambda i, j: (i, j)),
        out_shape=jax.ShapeDtypeStruct((t, n), F32),
        scratch_shapes=[pltpu.VMEM((tm, d), BF16)],
        compiler_params=_cparams(("arbitrary", "arbitrary")),
        name="in_projection",
    )(x, shift, scale, norm_g.reshape(1, d), w_bf16)


def _rope128(x, cos, sin_signed):
    return x * cos + pltpu.roll(x, 64, axis=1) * sin_signed


def _rope64(x, cos, sin_signed, first_half):
    rot = jnp.where(first_half, pltpu.roll(x, 96, axis=1), pltpu.roll(x, 32, axis=1))
    return x * cos + rot * sin_signed


def _prep_kernel(q_ref, k_ref, v_ref, iq_ref, sm_ref, c128_ref, s128_ref, c64_ref, s64_ref,
                 qg_ref, kg_ref,
                 qo_ref, k32_ref, kb_ref, vo_ref, v32_ref, iqb_ref, ik32_ref, ikb_ref, w_ref, *, transposed):
    c128, s128 = c128_ref[...], s128_ref[...]
    c64, s64 = c64_ref[...], s64_ref[...]
    tm = q_ref.shape[0]
    lane = lax.broadcasted_iota(I32, (tm, LANES), 1)
    first_half = (lane % IDX_HD) < (IDX_HD // 2)
    q_scale = ATT_HD ** -0.5 * math.log2(math.e)
    for h in range(ATT_HEADS):
        hs = slice(h * ATT_HD, (h + 1) * ATT_HD)
        q = q_ref[:, hs]
        q = q * lax.rsqrt(jnp.mean(q * q, axis=-1, keepdims=True) + NORM_EPS) * qg_ref[...]
        q = _rope128(q, c128, s128) * q_scale
        k = k_ref[:, hs]
        k = k * lax.rsqrt(jnp.mean(k * k, axis=-1, keepdims=True) + NORM_EPS) * kg_ref[...]
        k = _rope128(k, c128, s128)
        k32_ref[:, hs] = k
        kb_ref[:, hs] = k.astype(BF16)
        if transposed:
            qo_ref[hs, :] = q.T.astype(BF16)
            vo_ref[hs, :] = v_ref[:, hs].T.astype(BF16)
        else:
            qo_ref[:, hs] = q.astype(BF16)
            vo_ref[:, hs] = v_ref[:, hs].astype(BF16)
    v32_ref[...] = v_ref[...]
    for j in range(IDX_W // LANES):
        js = slice(j * LANES, (j + 1) * LANES)
        iqb_ref[:, js] = _rope64(iq_ref[:, js], c64, s64, first_half).astype(BF16)
    sm = sm_ref[...]
    ik = _rope64(sm, c64, s64, first_half)[:, :IDX_HD]
    ik32_ref[...] = ik
    ikb_ref[...] = ik.astype(BF16)
    w_ref[...] = sm * ((IDX_HEADS ** -0.5) * (IDX_HD ** -0.5))


def _attention_prep(p, tables, q_norm_g, k_norm_g, transposed):
    t = p.shape[0]
    tm = min(256, t)
    c128, s128, c64, s64 = tables
    wide = lambda col: pl.BlockSpec((tm, 1024), lambda i, c=col // 1024: (i, c))
    tab = pl.BlockSpec((tm, LANES), lambda i: (i, 0))
    gain = pl.BlockSpec((1, ATT_HD), lambda i: (0, 0))
    row1024 = pl.BlockSpec((tm, 1024), lambda i: (i, 0))
    col1024 = pl.BlockSpec((1024, tm), lambda i: (0, i))
    row64 = pl.BlockSpec((tm, IDX_HD), lambda i: (i, 0))
    if transposed:
        qv_spec, qv_shape = col1024, jax.ShapeDtypeStruct((ATT_W, t), BF16)
    else:
        qv_spec, qv_shape = row1024, jax.ShapeDtypeStruct((t, ATT_W), BF16)
    return pl.pallas_call(
        functools.partial(_prep_kernel, transposed=transposed),
        grid=(t // tm,),
        in_specs=[wide(COL_Q), wide(COL_K), wide(COL_V), wide(COL_IQ),
                  pl.BlockSpec((tm, LANES), lambda i: (i, COL_SMALL_A // LANES)),
                  tab, tab, tab, tab, gain, gain],
        out_specs=[qv_spec, row1024, row1024, qv_spec, row1024, row1024, row64, row64, tab],
        out_shape=[qv_shape,
                   jax.ShapeDtypeStruct((t, ATT_W), F32),
                   jax.ShapeDtypeStruct((t, ATT_W), BF16),
                   qv_shape,
                   jax.ShapeDtypeStruct((t, ATT_W), F32),
                   jax.ShapeDtypeStruct((t, IDX_W), BF16),
                   jax.ShapeDtypeStruct((t, IDX_HD), F32),
                   jax.ShapeDtypeStruct((t, IDX_HD), BF16),
                   jax.ShapeDtypeStruct((t, LANES), F32)],
        compiler_params=_cparams(("arbitrary",)),
        name="attention_prep",
    )(p, p, p, p, p, c128, s128, c64, s64, q_norm_g.reshape(1, ATT_HD), k_norm_g.reshape(1, ATT_HD))


def _rope_tables(pos):
    pos = pos.astype(F32)[:, None]

    def tab(dh):
        inv = ROPE_THETA ** (-jnp.arange(0, dh, 2, dtype=F32) / dh)
        ang = pos * inv[None, :]
        cos, sin = jnp.cos(ang), jnp.sin(ang)
        c = jnp.concatenate([cos, cos], axis=-1)
        s = jnp.concatenate([-sin, sin], axis=-1)
        rep = LANES // dh
        return jnp.tile(c, (1, rep)), jnp.tile(s, (1, rep))

    c128, s128 = tab(ATT_HD)
    c64, s64 = tab(IDX_HD)
    return c128, s128, c64, s64


def _indexer_kernel(iq_ref, wt_ref, ik_ref, bias_ref, iqs_ref, *, n_sel):
    i = pl.program_id(0)
    tq = iq_ref.shape[0]
    tk = bias_ref.shape[2]
    nk_total = bias_ref.shape[1]
    nkb = i + 1

    for h in range(IDX_HEADS):
        iqs_ref[h * tq:(h + 1) * tq, :] = iq_ref[:, h * IDX_HD:(h + 1) * IDX_HD]

    key_pos = lax.broadcasted_iota(I32, (tk, tq), 0)
    qry_pos = lax.broadcasted_iota(I32, (tk, tq), 1) + i * tq

    def score_body(kb, c):
        ikt = ik_ref[pl.ds(pl.multiple_of(kb * tk, tk), tk), :]
        d = _dg(ikt, iqs_ref[...], _NT)
        acc = jnp.zeros((tk, tq), F32)
        for h in range(IDX_HEADS):
            acc = acc + jnp.maximum(d[:, h * tq:(h + 1) * tq], 0.0) * wt_ref[h:h + 1, :]
        bias_ref[0, kb] = jnp.where(key_pos + kb * tk <= qry_pos, acc, -jnp.inf)
        return c

    lax.fori_loop(0, nkb, score_body, 0)

    def bit_body(bi, ans):
        cand = ans + jnp.left_shift(jnp.int32(1), 31 - bi)
        cand_f = _ordered_to_f32(cand)

        def cnt_body(kb, cnt):
            x = bias_ref[0, kb].reshape(tk // SUBLANES, SUBLANES, tq)
            return cnt + jnp.sum(jnp.where(x >= cand_f[None], 1.0, 0.0), axis=0)

        cnt = lax.fori_loop(0, nkb, cnt_body, jnp.zeros((SUBLANES, tq), F32))
        tot = jnp.sum(cnt, axis=0, keepdims=True)
        return jnp.where(tot >= n_sel, cand, ans)

    ans = lax.fori_loop(0, 32, bit_body, jnp.full((SUBLANES, tq), INT_MIN, I32))
    thr = _ordered_to_f32(ans)[0:1, :]
    thr = jnp.where(thr != thr, -jnp.inf, thr)

    def bias_body(kb, c):
        keep = (bias_ref[0, kb] >= thr) & (key_pos + kb * tk <= qry_pos)
        bias_ref[0, kb] = jnp.where(keep, 0.0, NEG_BIG)
        return c

    lax.fori_loop(0, nkb, bias_body, 0)

    def fill_body(kb, c):
        bias_ref[0, kb] = jnp.full((tk, tq), NEG_BIG, F32)
        return c

    lax.fori_loop(nkb, nk_total, fill_body, 0)


def _prompt_indexer(iqb, wt, ikb, n_sel, tq):
    t = iqb.shape[0]
    nq = t // tq
    return pl.pallas_call(
        functools.partial(_indexer_kernel, n_sel=n_sel),
        grid=(nq,),
        in_specs=[pl.BlockSpec((tq, IDX_W), lambda i: (i, 0)),
                  pl.BlockSpec((IDX_HEADS, tq), lambda i: (0, i)),
                  pl.BlockSpec((t, IDX_HD), lambda i: (0, 0))],
        out_specs=pl.BlockSpec((1, nq, tq, tq), lambda i: (i, 0, 0, 0)),
        out_shape=jax.ShapeDtypeStruct((nq, nq, tq, tq), F32),
        scratch_shapes=[pltpu.VMEM((IDX_HEADS * tq, IDX_HD), BF16)],
        compiler_params=_cparams(("arbitrary",)),
        name="prompt_indexer",
    )(iqb, wt, ikb)


def _flash_kernel(qi_ref, ki_ref, qt_ref, k_ref, vt_ref, bias_ref, g_ref, o_ref, m_ref, l_ref, acc_ref):
    step = pl.program_id(0)
    qi, ki = qi_ref[step], ki_ref[step]

    @pl.when(ki == 0)
    def _():
        m_ref[...] = jnp.full(m_ref.shape, NEG_BIG, F32)
        l_ref[...] = jnp.zeros(l_ref.shape, F32)
        acc_ref[...] = jnp.zeros(acc_ref.shape, F32)

    heads = [slice(h * ATT_HD, (h + 1) * ATT_HD) for h in range(ATT_HEADS)]
    tq = qt_ref.shape[1]
    for qh in range(tq // LANES):
        qs = slice(qh * LANES, (qh + 1) * LANES)
        bias = bias_ref[0, 0, :, qs]
        m_prev, l_prev = m_ref[:, qs], l_ref[:, qs]
        m_rows, l_rows = [], []
        for h, hs in enumerate(heads):
            s = _dg(k_ref[:, hs], qt_ref[hs, qs], _NN) + bias
            m_old = m_prev[h:h + 1, :]
            m_new = jnp.maximum(m_old, jnp.max(s, axis=0, keepdims=True))
            alpha = jnp.exp2(m_old - m_new)
            p = jnp.exp2(s - m_new)
            l_rows.append(alpha * l_prev[h:h + 1, :] + jnp.sum(p, axis=0, keepdims=True))
            m_rows.append(m_new)
            acc_ref[h, :, qs] = alpha * acc_ref[h, :, qs] + _dg(vt_ref[hs, :], p.astype(BF16), _NN)
        m_ref[:, qs] = jnp.concatenate(m_rows, axis=0)
        l_ref[:, qs] = jnp.concatenate(l_rows, axis=0)

    @pl.when(ki == qi)
    def _():
        for h in range(ATT_HEADS):
            hs = slice(h * ATT_HD, (h + 1) * ATT_HD)
            out = (acc_ref[h] / l_ref[h:h + 1, :]).T
            o_ref[:, hs] = (out * _silu(g_ref[:, hs])).astype(BF16)


def _prompt_attention(qt, kb, vt, bias, p, tq):
    t = kb.shape[0]
    nq = t // tq
    pairs = [(i, j) for i in range(nq) for j in range(i + 1)]
    qi = jnp.asarray([a for a, _ in pairs], I32)
    ki = jnp.asarray([b for _, b in pairs], I32)
    grid_spec = pltpu.PrefetchScalarGridSpec(
        num_scalar_prefetch=2,
        grid=(len(pairs),),
        in_specs=[pl.BlockSpec((ATT_W, tq), lambda s, qi, ki: (0, qi[s])),
                  pl.BlockSpec((tq, ATT_W), lambda s, qi, ki: (ki[s], 0)),
                  pl.BlockSpec((ATT_W, tq), lambda s, qi, ki: (0, ki[s])),
                  pl.BlockSpec((1, 1, tq, tq), lambda s, qi, ki: (qi[s], ki[s], 0, 0)),
                  pl.BlockSpec((tq, ATT_W), lambda s, qi, ki: (qi[s], COL_G // ATT_W))],
        out_specs=pl.BlockSpec((tq, ATT_W), lambda s, qi, ki: (qi[s], 0)),
        scratch_shapes=[pltpu.VMEM((ATT_HEADS, tq), F32),
                        pltpu.VMEM((ATT_HEADS, tq), F32),
                        pltpu.VMEM((ATT_HEADS, ATT_HD, tq), F32)],
    )
    return pl.pallas_call(
        _flash_kernel,
        grid_spec=grid_spec,
        out_shape=jax.ShapeDtypeStruct((t, ATT_W), BF16),
        compiler_params=_cparams(("arbitrary",)),
        name="prompt_attention",
    )(qi, ki, qt, kb, vt, bias, p)


def _sample_attn_kernel(pt_ref, q_ref, iq_ref, w_ref, ikn_ref, kn_ref, vn_ref, g_ref, ex_ref,
                        cik_ref, ck_ref, cv_ref, o_ref,
                        ikbuf, kbuf, vbuf, sce_ref, isem, ksem, vsem, *, n_sel, pps):
    b = pl.program_id(0)
    n_pages = ikbuf.shape[0]
    page_rows = PAGE_SIZE * ATT_HEADS

    def idx_copy(j):
        return pltpu.make_async_copy(cik_ref.at[pt_ref[b, j]], ikbuf.at[j], isem.at[j])

    def k_copy(j, slot):
        return pltpu.make_async_copy(ck_ref.at[pt_ref[b, j]], kbuf.at[slot], ksem.at[slot])

    def v_copy(j, slot):
        return pltpu.make_async_copy(cv_ref.at[pt_ref[b, j]], vbuf.at[slot], vsem.at[slot])

    def start_idx(j, c):
        idx_copy(j).start()
        return c

    lax.fori_loop(0, n_pages, start_idx, 0)

    def start_group(t, grp):
        for i in range(pps):
            k_copy(t * pps + i, grp * pps + i).start()
            v_copy(t * pps + i, grp * pps + i).start()

    start_group(0, 0)

    iq = iq_ref[0]
    w = w_ref[0]

    def wait_idx(j, c):
        idx_copy(j).wait()
        return c

    lax.fori_loop(0, n_pages, wait_idx, 0)
    ik_all = ikbuf[...].reshape(n_pages * PAGE_SIZE, IDX_HD).astype(BF16)
    d = _dg(iq, ik_all, _NT)
    sc_flat = jnp.sum(jnp.maximum(d, 0.0) * w, axis=0, keepdims=True)
    sc = jnp.concatenate([sc_flat[:, j * PAGE_SIZE:(j + 1) * PAGE_SIZE] for j in range(n_pages)], axis=0)
    d_self = jnp.sum(iq.astype(F32) * ikn_ref[0].astype(F32), axis=-1, keepdims=True)
    s_self = jnp.sum(jnp.maximum(d_self, 0.0) * w, axis=0, keepdims=True)

    def count_ge(cand):
        cand_f = _ordered_to_f32(cand)
        cnt = jnp.sum(jnp.where(sc >= cand_f, 1.0, 0.0), axis=-1, keepdims=True)
        return jnp.sum(cnt, axis=0, keepdims=True) + jnp.where(s_self >= cand_f, 1.0, 0.0)

    def digit_body(it, ans):
        step = jnp.left_shift(jnp.int32(1), 30 - 2 * it)
        for k in (1, 2, 3):
            cand = ans + k * step if k == 1 else cand + step
            best = jnp.where(count_ge(cand) >= n_sel, cand, ans if k == 1 else best)
        return best

    ans = lax.fori_loop(0, 16, digit_body, jnp.full((1, 1), INT_MIN, I32))
    thr = _ordered_to_f32(ans)
    thr = jnp.where(thr != thr, -jnp.inf, thr)

    sce_ref[...] = _dot_exact_rhs(sc, ex_ref[...])

    q8 = q_ref[0]
    own = (lax.broadcasted_iota(I32, (ATT_HEADS, page_rows), 1) % ATT_HEADS
           == lax.broadcasted_iota(I32, (ATT_HEADS, page_rows), 0))

    kn = kn_ref[0].astype(BF16).astype(F32)
    s0 = jnp.sum(q8.astype(F32) * kn, axis=-1, keepdims=True)
    m0 = jnp.where(s_self >= thr, s0, NEG_BIG)
    l0 = jnp.where(m0 > 0.5 * NEG_BIG, 1.0, 0.0)
    acc0 = l0 * vn_ref[0].astype(BF16).astype(F32)

    def att_body(t, carry):
        m_prev, l_prev, acc = carry
        grp = t % 2

        @pl.when(t + 1 < n_pages // pps)
        def _():
            start_group(t + 1, 1 - grp)

        for i in range(pps):
            k_copy(t * pps + i, grp * pps + i).wait()
            v_copy(t * pps + i, grp * pps + i).wait()
        s = []
        for i in range(pps):
            kf = kbuf[grp * pps + i].reshape(page_rows, ATT_HD).astype(BF16)
            keep = own & (sce_ref[pl.ds(t * pps + i, 1), :] >= thr)
            s.append(jnp.where(keep, _dg(q8, kf, _NT), NEG_BIG))
        m_new = m_prev
        for x in s:
            m_new = jnp.maximum(m_new, jnp.max(x, axis=-1, keepdims=True))
        alpha = jnp.exp2(m_prev - m_new)
        p = [jnp.exp2(x - m_new) for x in s]
        l_new = alpha * l_prev
        for x in p:
            l_new = l_new + jnp.sum(x, axis=-1, keepdims=True)
        pv = [jnp.dot(p[i].astype(BF16), vbuf[grp * pps + i].reshape(page_rows, ATT_HD).astype(BF16),
                      preferred_element_type=F32) for i in range(pps)]
        acc = alpha * acc
        for x in pv:
            acc = acc + x
        return m_new, l_new, acc

    m, l, acc = lax.fori_loop(0, n_pages // pps, att_body, (m0, l0, acc0))
    o_ref[0] = acc / l * _silu(g_ref[0])


def _sample_attention(page_table, qb, iqb, wv, ikb, k32, v32, g, cache_idx_k, cache_k, cache_v, n_sel):
    nb, n_pages = page_table.shape
    head3 = pl.BlockSpec((1, ATT_HEADS, ATT_HD), lambda b, pt: (b, 0, 0))
    any_spec = pl.BlockSpec(memory_space=pl.ANY)
    page_rows = PAGE_SIZE * ATT_HEADS
    pps = math.gcd(DECODE_PAGES_PER_STEP, n_pages)
    expand = (jnp.arange(page_rows)[None, :] // ATT_HEADS == jnp.arange(PAGE_SIZE)[:, None]).astype(BF16)
    grid_spec = pltpu.PrefetchScalarGridSpec(
        num_scalar_prefetch=1,
        grid=(nb,),
        in_specs=[head3,
                  pl.BlockSpec((1, IDX_HEADS, IDX_HD), lambda b, pt: (b, 0, 0)),
                  pl.BlockSpec((1, IDX_HEADS, 1), lambda b, pt: (b, 0, 0)),
                  pl.BlockSpec((1, 1, IDX_HD), lambda b, pt: (b, 0, 0)),
                  head3, head3, head3,
                  pl.BlockSpec((PAGE_SIZE, page_rows), lambda b, pt: (0, 0)),
                  any_spec, any_spec, any_spec],
        out_specs=head3,
        scratch_shapes=[pltpu.VMEM((n_pages, PAGE_SIZE, IDX_HD), F32),
                        pltpu.VMEM((2 * pps, PAGE_SIZE, ATT_HEADS, ATT_HD), F32),
                        pltpu.VMEM((2 * pps, PAGE_SIZE, ATT_HEADS, ATT_HD), F32),
                        pltpu.VMEM((n_pages, page_rows), F32),
                        pltpu.SemaphoreType.DMA((n_pages,)),
                        pltpu.SemaphoreType.DMA((2 * pps,)),
                        pltpu.SemaphoreType.DMA((2 * pps,))],
    )
    head = lambda a: a.reshape(nb, ATT_HEADS, ATT_HD)
    out = pl.pallas_call(
        functools.partial(_sample_attn_kernel, n_sel=n_sel, pps=pps),
        grid_spec=grid_spec,
        out_shape=jax.ShapeDtypeStruct((nb, ATT_HEADS, ATT_HD), F32),
        compiler_params=_cparams(("arbitrary",)),
        name="sample_attention",
    )(page_table, head(qb),
      iqb.reshape(nb, IDX_HEADS, IDX_HD),
      wv[:, IDX_HD:IDX_HD + IDX_HEADS].reshape(nb, IDX_HEADS, 1),
      ikb.reshape(nb, 1, IDX_HD),
      head(k32), head(v32), head(g), expand,
      cache_idx_k, cache_k, cache_v)
    return out.reshape(nb, ATT_W).astype(BF16)


def _rwkv_prep_kernel(*refs, shifted):
    n_x = 5
    x_refs = refs[:n_x]
    if shifted:
        halo_refs, first_refs = refs[n_x:2 * n_x], refs[2 * n_x:3 * n_x]
        rest = refs[3 * n_x:]
    else:
        prev_refs = refs[n_x:2 * n_x]
        rest = refs[2 * n_x:]
    (mu_ref, mus_ref, wl_ref, w0_ref, a0_ref, kk_ref, ka_ref, rk_ref, ones_ref,
     ro_ref, kpo_ref, vo_ref, kko_ref, bo_ref, lwo_ref, bonus_ref, sg_ref) = rest

    def lerp(j, mu):
        x = x_refs[j][...]
        if shifted:
            above = jnp.where(pl.program_id(0) == 0, first_refs[j][...],
                              halo_refs[j][SUBLANES - 1:SUBLANES, :])
            row = lax.broadcasted_iota(I32, x.shape, 0)
            prev = jnp.where(row == 0, above, pltpu.roll(x, 1, axis=0))
        else:
            prev = prev_refs[j][...]
        return x + mu * (prev - x)

    w = RWKV_W
    xr = lerp(0, mu_ref[:, 0:w])
    xk = lerp(1, mu_ref[:, w:2 * w])
    xv = lerp(2, mu_ref[:, 2 * w:3 * w])
    xg = lerp(3, mu_ref[:, 3 * w:4 * w])
    xs = lerp(4, mus_ref[...])
    lane = lax.broadcasted_iota(I32, xs.shape, 1)
    lora_in = jnp.where(lane < DECAY_LORA, jnp.tanh(xs), xs)
    lora = _dot3(lora_in, wl_ref[...])
    z = w0_ref[...] + lora[:, :w]
    softplus_neg = jnp.maximum(-z, 0.0) + jnp.log(1.0 + jnp.exp(-jnp.abs(z)))
    w_log = -softplus_neg - 0.5
    lwo_ref[...] = -jnp.exp(w_log)
    a = jax.nn.sigmoid(a0_ref[...] + lora[:, w:])
    ones_bd = ones_ref[...]
    kk = xk * kk_ref[...]
    kk = kk / jnp.maximum(jnp.sqrt(_head_sum(kk * kk, ones_bd)), 1e-12)
    kp = xk * (1.0 + (a - 1.0) * ka_ref[...])
    ro_ref[...] = xr
    kpo_ref[...] = kp
    vo_ref[...] = xv
    kko_ref[...] = kk
    bo_ref[...] = kk * a
    bonus_ref[...] = _head_sum(xr * kp * rk_ref[...], ones_bd) * xv
    sg_ref[...] = _silu(xg)


def _rwkv_prep(p, prev, consts, shifted):
    t = p.shape[0]
    tm = min(256, t)
    mu_big, mu_small, w_lora, w0, a0, k_k, k_a, r_k, ones_bd = consts
    cols = [(RWKV_W, COL_RR // RWKV_W), (RWKV_W, COL_RK // RWKV_W), (RWKV_W, COL_RV // RWKV_W),
            (RWKV_W, COL_RG // RWKV_W), (LANES, COL_SMALL_R // LANES)]
    x_specs = [pl.BlockSpec((tm, wd), lambda i, c=c: (i, c)) for wd, c in cols]
    if shifted:
        per8 = tm // SUBLANES
        halo = [pl.BlockSpec((SUBLANES, wd), lambda i, c=c: (jnp.maximum(i * per8 - 1, 0), c))
                for wd, c in cols]
        first = [pl.BlockSpec((1, wd), lambda i, c=c: (0, c)) for wd, c in cols]
        extra_specs, extra_args = halo + first, [p] * 5 + [prev] * 5
    else:
        extra_specs, extra_args = x_specs, [prev] * 5
    full = lambda a: pl.BlockSpec(a.shape, lambda i: (0, 0))
    row = pl.BlockSpec((tm, RWKV_W), lambda i: (i, 0))
    vec = lambda a: a.reshape(1, -1)
    args = [mu_big, mu_small, w_lora, vec(w0), vec(a0), vec(k_k), vec(k_a), vec(r_k), ones_bd]
    return pl.pallas_call(
        functools.partial(_rwkv_prep_kernel, shifted=shifted),
        grid=(t // tm,),
        in_specs=x_specs + extra_specs + [full(a) for a in args],
        out_specs=[row] * 8,
        out_shape=[jax.ShapeDtypeStruct((t, RWKV_W), F32)] * 8,
        compiler_params=_cparams(("arbitrary",)),
        name="rwkv_prep",
    )(*([p] * 5), *extra_args, *args)


def _scan_kernel(r_ref, kp_ref, v_ref, kk_ref, b_ref, lw_ref, s0_ref, y_ref, st_ref, s_ref):
    @pl.when(pl.program_id(0) == 0)
    def _():
        s_ref[...] = s0_ref[...]

    ch = SCAN_CHUNK
    assert ch == RWKV_HD
    rows_total = r_ref.shape[0]
    n_sub = rows_total // ch
    gw = MXU_DIM
    nh = HEADS_PER_GROUP
    lw = lw_ref[...]
    tr = lax.broadcasted_iota(I32, (rows_total, rows_total), 0)
    tc = lax.broadcasted_iota(I32, (rows_total, rows_total), 1)
    tri = jnp.where((tr >= tc) & (tr // ch == tc // ch), 1.0, 0.0).astype(BF16)
    cum = _dot_exact_lhs(tri, lw)
    cum_end = jnp.concatenate(
        [jnp.broadcast_to(cum[(c + 1) * ch - 1:(c + 1) * ch, :], (ch, RWKV_W)) for c in range(n_sub)], axis=0)
    kk, b, kp, r, v = kk_ref[...], b_ref[...], kp_ref[...], r_ref[...], v_ref[...]
    g_inv = jnp.exp(-cum)
    g_end = jnp.exp(cum_end - cum)
    a_t = (-kk * jnp.exp(cum - lw)).astype(BF16)
    b_t = b * g_inv
    k_t = kp * g_inv
    r_t = (r * jnp.exp(cum)).astype(BF16)
    b_h = (b * g_end).astype(BF16)
    k_h = (kp * g_end).astype(BF16)
    vb = v.astype(BF16)

    srow = lax.broadcasted_iota(I32, (nh * ch, gw), 0)
    scol = lax.broadcasted_iota(I32, (nh * ch, gw), 1)
    own = (srow // ch) == (scol // RWKV_HD)
    step = lax.broadcasted_iota(I32, (ch, gw), 0)
    other = lax.broadcasted_iota(I32, (ch, gw), 1) % ch
    strict, incl = step > other, step >= other
    eye = jnp.where(step == other, 1.0, 0.0)

    def stack(x):
        return jnp.where(own, jnp.concatenate([x] * nh, axis=0), 0.0).astype(BF16)

    groups = range(N_GROUPS)
    units = [(c, g) for c in range(n_sub) for g in groups]
    blk = {(c, g): (slice(c * ch, (c + 1) * ch), slice(g * gw, (g + 1) * gw)) for c, g in units}
    rb = {u: stack(b_t[blk[u]]) for u in units}
    rk = {u: stack(k_t[blk[u]]) for u in units}
    v4 = {u: stack(v[blk[u]]) for u in units}
    n_row = {u: jnp.where(strict, _dg(a_t[blk[u]], rb[u], _NT), 0.0) for u in units}
    bm_row = {u: jnp.where(strict, _dg(a_t[blk[u]], rk[u], _NT), 0.0).astype(BF16) for u in units}
    rb_row = {u: jnp.where(incl, _dg(r_t[blk[u]], rb[u], _NT), 0.0).astype(BF16) for u in units}
    rk_row = {u: jnp.where(incl, _dg(r_t[blk[u]], rk[u], _NT), 0.0).astype(BF16) for u in units}
    minv = {u: eye + n_row[u] for u in units}
    pw = n_row
    pw_bd = {u: stack(pw[u]) for u in units}
    for _ in range(int(math.log2(ch)) - 1):
        pw = {u: _dg(pw[u].astype(BF16), pw_bd[u], _NN) for u in units}
        pw_bd = {u: stack(pw[u]) for u in units}
        minv = {u: minv[u] + _dg(minv[u].astype(BF16), pw_bd[u], _NN) for u in units}
    minv = {u: minv[u].astype(BF16) for u in units}
    y_fixed = {u: _dg(rk_row[u], v4[u], _NN) for u in units}
    x_fixed = {u: _dg(bm_row[u], v4[u], _NN) for u in units}

    s = [s_ref[g] for g in groups]
    for c in range(n_sub):
        sb = [x.astype(BF16) for x in s]
        x_row = [_dg(a_t[blk[c, g]], sb[g], _NT) + x_fixed[c, g] for g in groups]
        x_bd = [stack(x) for x in x_row]
        u_row = [_dg(minv[c, g], x_bd[g], _NN) for g in groups]
        u_bd = [stack(x) for x in u_row]
        for g in groups:
            y_ref[blk[c, g]] = _dg(r_t[blk[c, g]], sb[g], _NT) + _dg(rb_row[c, g], u_bd[g], _NN) + y_fixed[c, g]
        for g in groups:
            rs, ls = blk[c, g]
            upd = _dg(jnp.concatenate([u_row[g].astype(BF16), vb[rs, ls]], axis=0),
                      jnp.concatenate([b_h[rs, ls], k_h[rs, ls]], axis=0), _TN)
            g_c = jnp.exp(cum[(c + 1) * ch - 1:(c + 1) * ch, ls])
            s[g] = s[g] * g_c + jnp.where(own, upd, 0.0)
    for g in groups:
        s_ref[g] = s[g]

    @pl.when(pl.program_id(0) == pl.num_programs(0) - 1)
    def _():
        st_ref[...] = s_ref[...]


def _rwkv_scan(r, kp, v, kk, b, lw, s0):
    t = r.shape[0]
    ch = SCAN_CHUNK * min(SCAN_SUB, t // SCAN_CHUNK)
    row = pl.BlockSpec((ch, RWKV_W), lambda i: (i, 0))
    st = pl.BlockSpec((N_GROUPS, MXU_DIM, MXU_DIM), lambda i: (0, 0, 0))
    return pl.pallas_call(
        _scan_kernel,
        grid=(t // ch,),
        in_specs=[row] * 6 + [st],
        out_specs=[row, st],
        out_shape=[jax.ShapeDtypeStruct((t, RWKV_W), F32),
                   jax.ShapeDtypeStruct((N_GROUPS, MXU_DIM, MXU_DIM), F32)],
        scratch_shapes=[pltpu.VMEM((N_GROUPS, MXU_DIM, MXU_DIM), F32)],
        compiler_params=_cparams(("arbitrary",)),
        name="rwkv_scan",
    )(r, kp, v, kk, b, lw, s0)


def _rwkv_step_kernel(s_ref, r_ref, kp_ref, kk_ref, b_ref, lw_ref, v_ref, so_ref, y_ref):
    s = s_ref[0]
    kk = kk_ref[0]
    sa = -jnp.sum(s * kk, axis=-1, keepdims=True)
    s_new = s * jnp.exp(lw_ref[0]) + sa * b_ref[0] + v_ref[0] * kp_ref[0]
    so_ref[0] = s_new
    y_ref[0] = jnp.sum(s_new * r_ref[0], axis=-1, keepdims=True)


def _rwkv_step(state, r, kp, kk, b, lw, v):
    nb = state.shape[0]
    rowv = lambda a: a.reshape(nb, RWKV_HEADS, 1, RWKV_HD)
    rspec = pl.BlockSpec((1, RWKV_HEADS, 1, RWKV_HD), lambda i: (i, 0, 0, 0))
    cspec = pl.BlockSpec((1, RWKV_HEADS, RWKV_HD, 1), lambda i: (i, 0, 0, 0))
    sspec = pl.BlockSpec((1, RWKV_HEADS, RWKV_HD, RWKV_HD), lambda i: (i, 0, 0, 0))
    s_new, y = pl.pallas_call(
        _rwkv_step_kernel,
        grid=(nb,),
        in_specs=[sspec, rspec, rspec, rspec, rspec, rspec, cspec],
        out_specs=[sspec, cspec],
        out_shape=[jax.ShapeDtypeStruct(state.shape, F32),
                   jax.ShapeDtypeStruct((nb, RWKV_HEADS, RWKV_HD, 1), F32)],
        compiler_params=_cparams(("arbitrary",)),
        name="rwkv_step",
    )(state, rowv(r), rowv(kp), rowv(kk), rowv(b), rowv(lw), v.reshape(nb, RWKV_HEADS, RWKV_HD, 1))
    return s_new, y.reshape(nb, RWKV_W)


def _rwkv_post_kernel(y_ref, bonus_ref, sg_ref, lg_ref, lb_ref, ones_ref, o_ref):
    y = y_ref[...]
    ones_bd = ones_ref[...]
    mean = _head_sum(y, ones_bd) * (1.0 / RWKV_HD)
    yc = y - mean
    var = _head_sum(yc * yc, ones_bd) * (1.0 / RWKV_HD)
    yn = yc * lax.rsqrt(var + GN_EPS) * lg_ref[...] + lb_ref[...]
    o_ref[...] = ((yn + bonus_ref[...]) * sg_ref[...]).astype(BF16)


def _rwkv_post(y, bonus, sg, ln_g, ln_b, ones_bd):
    t = y.shape[0]
    tm = min(512, t)
    row = pl.BlockSpec((tm, RWKV_W), lambda i: (i, 0))
    vec = pl.BlockSpec((1, RWKV_W), lambda i: (0, 0))
    return pl.pallas_call(
        _rwkv_post_kernel,
        grid=(t // tm,),
        in_specs=[row, row, row, vec, vec, pl.BlockSpec(ones_bd.shape, lambda i: (0, 0))],
        out_specs=row,
        out_shape=jax.ShapeDtypeStruct((t, RWKV_W), BF16),
        compiler_params=_cparams(("arbitrary",)),
        name="rwkv_post",
    )(y, bonus, sg, ln_g.reshape(1, -1), ln_b.reshape(1, -1), ones_bd)


def _outproj_kernel(att_ref, rw_ref, wa_ref, wr_ref, x_ref, gate_ref, o_ref):
    out = (jnp.dot(att_ref[...], wa_ref[...], preferred_element_type=F32)
           + jnp.dot(rw_ref[...], wr_ref[...], preferred_element_type=F32))
    o_ref[...] = x_ref[...] + gate_ref[...] * out


def _out_projection(att, rw, w_out_bf16, x, gate):
    t, d = x.shape
    tm = min(512, t)
    per_row = gate.shape[0] != 1
    gspec = (pl.BlockSpec((tm, d), lambda i: (i, 0)) if per_row else pl.BlockSpec((1, d), lambda i: (0, 0)))
    return pl.pallas_call(
        _outproj_kernel,
        grid=(t // tm,),
        in_specs=[pl.BlockSpec((tm, ATT_W), lambda i: (i, 0)),
                  pl.BlockSpec((tm, RWKV_W), lambda i: (i, 0)),
                  pl.BlockSpec((ATT_W, d), lambda i: (0, 0)),
                  pl.BlockSpec((RWKV_W, d), lambda i: (1, 0)),
                  pl.BlockSpec((tm, d), lambda i: (i, 0)),
                  gspec],
        out_specs=pl.BlockSpec((tm, d), lambda i: (i, 0)),
        out_shape=jax.ShapeDtypeStruct((t, d), F32),
        compiler_params=_cparams(("arbitrary",)),
        name="out_projection",
    )(att, rw, w_out_bf16, w_out_bf16, x, gate)


def _reorder_cols(w):
    a = 0
    q, k, v, g = (w[..., a + i * ATT_W:a + (i + 1) * ATT_W] for i in range(4))
    a = 4 * ATT_W
    iq = w[..., a:a + IDX_W]
    iw = w[..., a + IDX_W:a + IDX_W + IDX_HEADS]
    ik = w[..., a + IDX_W + IDX_HEADS:a + IDX_W + IDX_HEADS + IDX_HD]
    o = N_ATT_COLS_ORIG
    rr, rk, rv = (w[..., o + i * RWKV_W:o + (i + 1) * RWKV_W] for i in range(3))
    o2 = o + 3 * RWKV_W
    wl = w[..., o2:o2 + DECAY_LORA]
    al = w[..., o2 + DECAY_LORA:o2 + DECAY_LORA + ICL_LORA]
    rg = w[..., o2 + DECAY_LORA + ICL_LORA:o2 + DECAY_LORA + ICL_LORA + RWKV_W]
    pad_a = jnp.zeros(w.shape[:-1] + (LANES - IDX_HD - IDX_HEADS,), w.dtype)
    pad_end = jnp.zeros(w.shape[:-1] + (N_COLS_PAD - N_COLS,), w.dtype)
    return jnp.concatenate([q, k, v, g, iq, rr, rk, rv, rg, ik, iw, pad_a, wl, al, pad_end], axis=-1)


def _reorder_cast_kernel(w_ref, o_ref):
    o_ref[...] = _reorder_cols(w_ref[...]).astype(BF16)


def _reorder_cast(w):
    rows, n = w.shape
    tr = min(256, rows)
    return pl.pallas_call(
        _reorder_cast_kernel,
        grid=(rows // tr,),
        in_specs=[pl.BlockSpec((tr, n), lambda i: (i, 0))],
        out_specs=pl.BlockSpec((tr, N_COLS_PAD), lambda i: (i, 0)),
        out_shape=jax.ShapeDtypeStruct((rows, N_COLS_PAD), BF16),
        compiler_params=_cparams(("arbitrary",)),
        name="reorder_cast",
    )(w)


def _rwkv_cols_original_order(p):
    return jnp.concatenate([p[..., COL_RR:COL_RR + RWKV_W], p[..., COL_RK:COL_RK + RWKV_W],
                            p[..., COL_RV:COL_RV + RWKV_W],
                            p[..., COL_SMALL_R:COL_SMALL_R + DECAY_LORA + ICL_LORA],
                            p[..., COL_RG:COL_RG + RWKV_W]], axis=-1)


def _blockdiag_to_state(sb):
    sb = sb.reshape(N_GROUPS, HEADS_PER_GROUP, RWKV_HD, HEADS_PER_GROUP, RWKV_HD)
    return jnp.stack([sb[:, h, :, h, :] for h in range(HEADS_PER_GROUP)], axis=1).reshape(
        RWKV_HEADS, RWKV_HD, RWKV_HD)


def kernel(x_prompt, x_sample, c_prompt, c_sample, cache_k, cache_v, cache_idx_k, state_wkv, state_shift,
           page_table, w_ada, b_ada, norm_g, w_in, q_norm_g, k_norm_g, shift_mu, w0, w_b, a0, a_b,
           k_k, k_a, r_k, ln_x_g, ln_x_b, w_out):
    depth = w_ada.shape[0]
    assert depth == 1, "single-layer trunk"
    bsz, seq, d = x_prompt.shape
    assert bsz == 1
    nb, dec_seq, _ = x_sample.shape
    assert dec_seq == 1
    n_pages = page_table.shape[1]
    past_len = n_pages * PAGE_SIZE
    l = 0

    w_in_r = _reorder_cast(w_in[l])
    w_out_b = w_out[l].astype(BF16)
    mu = _reorder_cols(jnp.concatenate([jnp.zeros((N_ATT_COLS_ORIG,), F32), shift_mu[l]]))
    mu_big = mu[COL_RR:COL_RR + 4 * RWKV_W].reshape(1, -1)
    mu_small = mu[COL_SMALL_R:COL_SMALL_R + LANES].reshape(1, -1)
    zl = jnp.zeros((DECAY_LORA, RWKV_W), F32)
    w_lora = jnp.concatenate([jnp.concatenate([w_b[l], zl], axis=1),
                              jnp.concatenate([zl, a_b[l]], axis=1)], axis=0)
    hid = jnp.arange(LANES) // RWKV_HD
    ones_bd = (hid[:, None] == hid[None, :]).astype(BF16)
    rwkv_consts = (mu_big, mu_small, w_lora, w0[l], a0[l], k_k[l], k_a[l], r_k[l].reshape(-1), ones_bd)

    n_mod = 1 + nb
    c_all = jnp.concatenate([c_prompt, c_sample, jnp.zeros((-n_mod % SUBLANES, d), F32)], axis=0)
    mod = _modulation(c_all, w_ada[l], b_ada[l])
    shift_p, scale_p, gate_p = mod[0:1, :d], mod[0:1, d:2 * d], mod[0:1, 2 * d:]
    shift_s, scale_s, gate_s = mod[1:n_mod, :d], mod[1:n_mod, d:2 * d], mod[1:n_mod, 2 * d:]

    xp = x_prompt.reshape(seq, d)
    xs = x_sample.reshape(nb, d)

    p = _in_projection(xp, shift_p, scale_p, norm_g[l], w_in_r)
    qt, k32, kb, vt, v32, iqb, ik32, ikb, wv = _attention_prep(p, _rope_tables(jnp.arange(seq)),
                                                         q_norm_g[l], k_norm_g[l], transposed=True)
    n_sel = min(TOPK_MAX, seq // 4)
    tq = min(MXU_DIM, seq)
    bias = _prompt_indexer(iqb, wv[:, IDX_HD:IDX_HD + IDX_HEADS].T, ikb, n_sel, tq)
    att = _prompt_attention(qt, kb, vt, bias, p, tq)

    r_, kp_, v_, kk_, b_, lw_, bonus, sg = _rwkv_prep(p, jnp.zeros((1, p.shape[1]), F32), rwkv_consts,
                                                      shifted=True)
    y_scan, s_fin = _rwkv_scan(r_, kp_, v_, kk_, b_, lw_,
                               jnp.zeros((N_GROUPS, MXU_DIM, MXU_DIM), F32))
    rw = _rwkv_post(y_scan, bonus, sg, ln_x_g[l], ln_x_b[l], ones_bd)
    y_p = _out_projection(att, rw, w_out_b, xp, gate_p)

    k_prompt = k32.reshape(1, 1, seq, ATT_HEADS, ATT_HD)
    v_prompt = v32.reshape(1, 1, seq, ATT_HEADS, ATT_HD)
    idx_k_prompt = ik32.reshape(1, 1, seq, IDX_HD)
    wkv_prompt = _blockdiag_to_state(s_fin).reshape(1, 1, RWKV_HEADS, RWKV_HD, RWKV_HD)
    shift_prompt = _rwkv_cols_original_order(p[seq - 1:seq]).reshape(1, 1, -1)

    ps = _in_projection(xs, shift_s, scale_s, norm_g[l], w_in_r)
    pos_s = jnp.full((nb,), past_len, jnp.int32)
    qb_s, k32_s, _, _, v32_s, iqb_s, ik32_s, ikb_s, wv_s = _attention_prep(ps, _rope_tables(pos_s),
                                                                          q_norm_g[l], k_norm_g[l],
                                                                          transposed=False)
    n_sel_s = min(TOPK_MAX, (past_len + 1) // 4)
    att_s = _sample_attention(page_table, qb_s, iqb_s, wv_s, ikb_s, k32_s, v32_s,
                              ps[:, COL_G:COL_G + ATT_W], cache_idx_k[l], cache_k[l], cache_v[l], n_sel_s)

    prev_s = _reorder_cols(jnp.concatenate([jnp.zeros((nb, N_ATT_COLS_ORIG), F32), state_shift[l]], axis=-1))
    r_s, kp_s, v_s, kk_s, b_s, lw_s, bonus_s, sg_s = _rwkv_prep(ps, prev_s, rwkv_consts, shifted=False)
    wkv_new, y_step = _rwkv_step(state_wkv[l], r_s, kp_s, kk_s, b_s, lw_s, v_s)
    rw_s = _rwkv_post(y_step, bonus_s, sg_s, ln_x_g[l], ln_x_b[l], ones_bd)
    y_s = _out_projection(att_s, rw_s, w_out_b, xs, gate_s)

    k_sample = k32_s.reshape(1, nb, 1, ATT_HEADS, ATT_HD)
    v_sample = v32_s.reshape(1, nb, 1, ATT_HEADS, ATT_HD)
    idx_k_sample = ik32_s.reshape(1, nb, 1, IDX_HD)
    wkv_sample = wkv_new.reshape(1, nb, RWKV_HEADS, RWKV_HD, RWKV_HD)
    shift_sample = _rwkv_cols_original_order(ps).reshape(1, nb, -1)

    return (y_p.reshape(1, seq, d), y_s.reshape(nb, 1, d),
            k_prompt, v_prompt, idx_k_prompt, wkv_prompt, shift_prompt,
            k_sample, v_sample, idx_k_sample, wkv_sample, shift_sample)
```

```python
import functools
import math

import jax
import jax.numpy as jnp
from jax import lax
from jax.experimental import pallas as pl
from jax.experimental.pallas import tpu as pltpu

F32 = jnp.float32
BF16 = jnp.bfloat16
I32 = jnp.int32

ATT_HEADS = 8
ATT_HD = 128
ATT_W = ATT_HEADS * ATT_HD
IDX_HEADS = 16
IDX_HD = 64
IDX_W = IDX_HEADS * IDX_HD
TOPK_MAX = 256
PAGE_SIZE = 128
ROPE_THETA = 10000.0
RWKV_HD = 64
RWKV_HEADS = 16
RWKV_W = RWKV_HEADS * RWKV_HD
DECAY_LORA = 64
ICL_LORA = 64
NORM_EPS = 1e-6
GN_EPS = 64e-5

LANES = 128
SUBLANES = 8
MXU_DIM = 256
VMEM_LIMIT = 56 * 1024 * 1024
NEG_BIG = -1e30
INT_MIN = -(2 ** 31)

COL_Q, COL_K, COL_V, COL_G, COL_IQ = 0, 1024, 2048, 3072, 4096
COL_RR, COL_RK, COL_RV, COL_RG = 5120, 6144, 7168, 8192
COL_SMALL_A = 9216
COL_SMALL_R = 9344
N_COLS = 9472
N_COLS_PAD = 9728
N_ATT_COLS_ORIG = 4 * ATT_W + IDX_W + IDX_HEADS + IDX_HD

SCAN_CHUNK = 64
DECODE_PAGES_PER_STEP = 8
SCAN_SUB = 4
HEADS_PER_GROUP = MXU_DIM // RWKV_HD
N_GROUPS = RWKV_HEADS // HEADS_PER_GROUP


def _cparams(sem, vmem=VMEM_LIMIT):
    return pltpu.CompilerParams(dimension_semantics=sem, vmem_limit_bytes=vmem)


def _split2(x):
    hi = x.astype(BF16)
    lo = (x - hi.astype(F32)).astype(BF16)
    return hi, lo


def _split3(x):
    hi = x.astype(BF16)
    r = x - hi.astype(F32)
    mid = r.astype(BF16)
    lo = (r - mid.astype(F32)).astype(BF16)
    return hi, mid, lo


_NN = (((1,), (0,)), ((), ()))
_NT = (((1,), (1,)), ((), ()))
_TN = (((0,), (0,)), ((), ()))


def _dg(a, b, dims):
    return lax.dot_general(a, b, dims, preferred_element_type=F32)


def _dot1(a, b, dims=_NN):
    return _dg(a.astype(BF16), b.astype(BF16), dims)


def _dot3(a, b, dims=_NN):
    ah, al = _split2(a)
    bh, bl = _split2(b)
    return _dg(ah, bh, dims) + _dg(al, bh, dims) + _dg(ah, bl, dims)


def _dot_exact_lhs(a_bf16, b, dims=_NN):
    b0, b1, b2 = _split3(b)
    return _dg(a_bf16, b0, dims) + _dg(a_bf16, b1, dims) + _dg(a_bf16, b2, dims)


def _dot_exact_rhs(a, b_bf16, dims=_NN):
    a0, a1, a2 = _split3(a)
    return _dg(a0, b_bf16, dims) + _dg(a1, b_bf16, dims) + _dg(a2, b_bf16, dims)


def _silu(x):
    return x * jax.nn.sigmoid(x)


def _head_sum(x, ones_bd):
    parts = []
    for j in range(x.shape[1] // LANES):
        parts.append(_dot_exact_rhs(x[:, j * LANES:(j + 1) * LANES], ones_bd))
    return jnp.concatenate(parts, axis=1)


def _ordered_to_f32(key):
    return pltpu.bitcast(key ^ ((key >> 31) & 0x7FFFFFFF), F32)


def _mod_kernel(c_ref, w_ref, b_ref, o_ref):
    o_ref[...] = _dot3(_silu(c_ref[...]), w_ref[...]) + b_ref[...]


def _modulation(c, w_ada, b_ada):
    m, d = c.shape
    n = w_ada.shape[1]
    tn = 768
    return pl.pallas_call(
        _mod_kernel,
        grid=(n // tn,),
        in_specs=[pl.BlockSpec((m, d), lambda j: (0, 0)),
                  pl.BlockSpec((d, tn), lambda j: (0, j)),
                  pl.BlockSpec((1, tn), lambda j: (0, j))],
        out_specs=pl.BlockSpec((m, tn), lambda j: (0, j)),
        out_shape=jax.ShapeDtypeStruct((m, n), F32),
        compiler_params=_cparams(("arbitrary",)),
        name="modulation",
    )(c, w_ada, b_ada.reshape(1, n))


def _inproj_kernel(x_ref, shift_ref, scale_ref, g_ref, w_ref, o_ref, h_ref, *, rows):
    @pl.when(pl.program_id(1) == 0)
    def _():
        def body(r, c):
            sl = pl.ds(pl.multiple_of(r * rows, rows), rows)
            x = x_ref[sl, :]
            ms = jnp.mean(x * x, axis=-1, keepdims=True)
            y = x * lax.rsqrt(ms + NORM_EPS) * g_ref[...]
            if shift_ref.shape[0] == 1:
                h = y * (1.0 + scale_ref[...]) + shift_ref[...]
            else:
                h = y * (1.0 + scale_ref[sl, :]) + shift_ref[sl, :]
            h_ref[sl, :] = h.astype(BF16)
            return c
        lax.fori_loop(0, x_ref.shape[0] // rows, body, 0)

    o_ref[...] = jnp.dot(h_ref[...], w_ref[...], preferred_element_type=F32)


def _in_projection(x, shift, scale, norm_g, w_bf16):
    t, d = x.shape
    n = w_bf16.shape[1]
    tm = min(1024, t)
    tn = 512
    rows = min(128, tm)
    per_row = shift.shape[0] != 1
    mod_spec = (pl.BlockSpec((tm, d), lambda i, j: (i, 0)) if per_row
                else pl.BlockSpec((1, d), lambda i, j: (0, 0)))
    return pl.pallas_call(
        functools.partial(_inproj_kernel, rows=rows),
        grid=(t // tm, n // tn),
        in_specs=[pl.BlockSpec((tm, d), lambda i, j: (i, 0)),
                  mod_spec, mod_spec,
                  pl.BlockSpec((1, d), lambda i, j: (0, 0)),
                  pl.BlockSpec((d, tn), lambda i, j: (0, j))],
        out_specs=pl.BlockSpec((tm, tn), lambda i, j: (i, j)),
        out_shape=jax.ShapeDtypeStruct((t, n), F32),
        scratch_shapes=[pltpu.VMEM((tm, d), BF16)],
        compiler_params=_cparams(("arbitrary", "arbitrary")),
        name="in_projection",
    )(x, shift, scale, norm_g.reshape(1, d), w_bf16)


def _rope128(x, cos, sin_signed):
    return x * cos + pltpu.roll(x, 64, axis=1) * sin_signed


def _rope64(x, cos, sin_signed, first_half):
    rot = jnp.where(first_half, pltpu.roll(x, 96, axis=1), pltpu.roll(x, 32, axis=1))
    return x * cos + rot * sin_signed


def _prep_kernel(q_ref, k_ref, v_ref, iq_ref, sm_ref, c128_ref, s128_ref, c64_ref, s64_ref,
                 qg_ref, kg_ref,
                 qo_ref, k32_ref, kb_ref, vo_ref, v32_ref, iqb_ref, ik32_ref, ikb_ref, w_ref, *, transposed):
    c128, s128 = c128_ref[...], s128_ref[...]
    c64, s64 = c64_ref[...], s64_ref[...]
    tm = q_ref.shape[0]
    lane = lax.broadcasted_iota(I32, (tm, LANES), 1)
    first_half = (lane % IDX_HD) < (IDX_HD // 2)
    q_scale = ATT_HD ** -0.5 * math.log2(math.e)
    for h in range(ATT_HEADS):
        hs = slice(h * ATT_HD, (h + 1) * ATT_HD)
        q = q_ref[:, hs]
        q = q * lax.rsqrt(jnp.mean(q * q, axis=-1, keepdims=True) + NORM_EPS) * qg_ref[...]
        q = _rope128(q, c128, s128) * q_scale
        k = k_ref[:, hs]
        k = k * lax.rsqrt(jnp.mean(k * k, axis=-1, keepdims=True) + NORM_EPS) * kg_ref[...]
        k = _rope128(k, c128, s128)
        k32_ref[:, h, :] = k
        kb_ref[:, hs] = k.astype(BF16)
        if transposed:
            qo_ref[hs, :] = q.T.astype(BF16)
            vo_ref[hs, :] = v_ref[:, hs].T.astype(BF16)
        else:
            qo_ref[:, hs] = q.astype(BF16)
            vo_ref[:, hs] = v_ref[:, hs].astype(BF16)
    for h in range(ATT_HEADS):
        v32_ref[:, h, :] = v_ref[:, h * ATT_HD:(h + 1) * ATT_HD]
    for j in range(IDX_W // LANES):
        js = slice(j * LANES, (j + 1) * LANES)
        iqb_ref[:, js] = _rope64(iq_ref[:, js], c64, s64, first_half).astype(BF16)
    sm = sm_ref[...]
    ik = _rope64(sm, c64, s64, first_half)[:, :IDX_HD]
    ik32_ref[...] = ik
    ikb_ref[...] = ik.astype(BF16)
    w_ref[...] = sm * ((IDX_HEADS ** -0.5) * (IDX_HD ** -0.5))


def _attention_prep(p, tables, q_norm_g, k_norm_g, transposed):
    t = p.shape[0]
    tm = min(256, t)
    c128, s128, c64, s64 = tables
    wide = lambda col: pl.BlockSpec((tm, 1024), lambda i, c=col // 1024: (i, c))
    tab = pl.BlockSpec((tm, LANES), lambda i: (i, 0))
    gain = pl.BlockSpec((1, ATT_HD), lambda i: (0, 0))
    row1024 = pl.BlockSpec((tm, 1024), lambda i: (i, 0))
    col1024 = pl.BlockSpec((1024, tm), lambda i: (0, i))
    row64 = pl.BlockSpec((tm, IDX_HD), lambda i: (i, 0))
    head3 = pl.BlockSpec((tm, ATT_HEADS, ATT_HD), lambda i: (i, 0, 0))
    if transposed:
        qv_spec, qv_shape = col1024, jax.ShapeDtypeStruct((ATT_W, t), BF16)
    else:
        qv_spec, qv_shape = row1024, jax.ShapeDtypeStruct((t, ATT_W), BF16)
    return pl.pallas_call(
        functools.partial(_prep_kernel, transposed=transposed),
        grid=(t // tm,),
        in_specs=[wide(COL_Q), wide(COL_K), wide(COL_V), wide(COL_IQ),
                  pl.BlockSpec((tm, LANES), lambda i: (i, COL_SMALL_A // LANES)),
                  tab, tab, tab, tab, gain, gain],
        out_specs=[qv_spec, head3, row1024, qv_spec, head3, row1024, row64, row64, tab],
        out_shape=[qv_shape,
                   jax.ShapeDtypeStruct((t, ATT_HEADS, ATT_HD), F32),
                   jax.ShapeDtypeStruct((t, ATT_W), BF16),
                   qv_shape,
                   jax.ShapeDtypeStruct((t, ATT_HEADS, ATT_HD), F32),
                   jax.ShapeDtypeStruct((t, IDX_W), BF16),
                   jax.ShapeDtypeStruct((t, IDX_HD), F32),
                   jax.ShapeDtypeStruct((t, IDX_HD), BF16),
                   jax.ShapeDtypeStruct((t, LANES), F32)],
        compiler_params=_cparams(("arbitrary",)),
        name="attention_prep",
    )(p, p, p, p, p, c128, s128, c64, s64, q_norm_g.reshape(1, ATT_HD), k_norm_g.reshape(1, ATT_HD))


def _rope_tables(pos):
    pos = pos.astype(F32)[:, None]

    def tab(dh):
        inv = ROPE_THETA ** (-jnp.arange(0, dh, 2, dtype=F32) / dh)
        ang = pos * inv[None, :]
        cos, sin = jnp.cos(ang), jnp.sin(ang)
        c = jnp.concatenate([cos, cos], axis=-1)
        s = jnp.concatenate([-sin, sin], axis=-1)
        rep = LANES // dh
        return jnp.tile(c, (1, rep)), jnp.tile(s, (1, rep))

    c128, s128 = tab(ATT_HD)
    c64, s64 = tab(IDX_HD)
    return c128, s128, c64, s64


def _indexer_kernel(iq_ref, wt_ref, ik_ref, bias_ref, iqs_ref, *, n_sel):
    i = pl.program_id(0)
    tq = iq_ref.shape[0]
    tk = bias_ref.shape[2]
    nk_total = bias_ref.shape[1]
    nkb = i + 1

    for h in range(IDX_HEADS):
        iqs_ref[h * tq:(h + 1) * tq, :] = iq_ref[:, h * IDX_HD:(h + 1) * IDX_HD]

    key_pos = lax.broadcasted_iota(I32, (tk, tq), 0)
    qry_pos = lax.broadcasted_iota(I32, (tk, tq), 1) + i * tq

    def score_body(kb, c):
        ikt = ik_ref[pl.ds(pl.multiple_of(kb * tk, tk), tk), :]
        d = _dg(ikt, iqs_ref[...], _NT)
        acc = jnp.zeros((tk, tq), F32)
        for h in range(IDX_HEADS):
            acc = acc + jnp.maximum(d[:, h * tq:(h + 1) * tq], 0.0) * wt_ref[h:h + 1, :]
        bias_ref[0, kb] = jnp.where(key_pos + kb * tk <= qry_pos, acc, -jnp.inf)
        return c

    lax.fori_loop(0, nkb, score_body, 0)

    def bit_body(bi, ans):
        cand = ans + jnp.left_shift(jnp.int32(1), 31 - bi)
        cand_f = _ordered_to_f32(cand)

        def cnt_one(kb, cnt):
            x = bias_ref[0, kb].reshape(tk // SUBLANES, SUBLANES, tq)
            return cnt + jnp.sum(jnp.where(x >= cand_f[None], 1.0, 0.0), axis=0)

        def cnt_four(j, cnt):
            for u in range(4):
                cnt = cnt_one(4 * j + u, cnt)
            return cnt

        cnt = lax.fori_loop(0, nkb // 4, cnt_four, jnp.zeros((SUBLANES, tq), F32))
        cnt = lax.fori_loop(4 * (nkb // 4), nkb, cnt_one, cnt)
        tot = jnp.sum(cnt, axis=0, keepdims=True)
        return jnp.where(tot >= n_sel, cand, ans)

    ans = lax.fori_loop(0, 32, bit_body, jnp.full((SUBLANES, tq), INT_MIN, I32))
    thr = _ordered_to_f32(ans)[0:1, :]
    thr = jnp.where(thr != thr, -jnp.inf, thr)

    def bias_body(kb, c):
        keep = (bias_ref[0, kb] >= thr) & (key_pos + kb * tk <= qry_pos)
        bias_ref[0, kb] = jnp.where(keep, 0.0, NEG_BIG)
        return c

    lax.fori_loop(0, nkb, bias_body, 0)

    def fill_body(kb, c):
        bias_ref[0, kb] = jnp.full((tk, tq), NEG_BIG, F32)
        return c

    lax.fori_loop(nkb, nk_total, fill_body, 0)


def _prompt_indexer(iqb, wt, ikb, n_sel, tq):
    t = iqb.shape[0]
    nq = t // tq
    return pl.pallas_call(
        functools.partial(_indexer_kernel, n_sel=n_sel),
        grid=(nq,),
        in_specs=[pl.BlockSpec((tq, IDX_W), lambda i: (i, 0)),
                  pl.BlockSpec((IDX_HEADS, tq), lambda i: (0, i)),
                  pl.BlockSpec((t, IDX_HD), lambda i: (0, 0))],
        out_specs=pl.BlockSpec((1, nq, tq, tq), lambda i: (i, 0, 0, 0)),
        out_shape=jax.ShapeDtypeStruct((nq, nq, tq, tq), F32),
        scratch_shapes=[pltpu.VMEM((IDX_HEADS * tq, IDX_HD), BF16)],
        compiler_params=_cparams(("arbitrary",)),
        name="prompt_indexer",
    )(iqb, wt, ikb)


def _flash_kernel(qi_ref, ki_ref, qt_ref, k_ref, vt_ref, bias_ref, g_ref, o_ref, m_ref, l_ref, acc_ref):
    step = pl.program_id(0)
    qi, ki = qi_ref[step], ki_ref[step]

    @pl.when(ki == 0)
    def _():
        m_ref[...] = jnp.full(m_ref.shape, NEG_BIG, F32)
        l_ref[...] = jnp.zeros(l_ref.shape, F32)
        acc_ref[...] = jnp.zeros(acc_ref.shape, F32)

    heads = [slice(h * ATT_HD, (h + 1) * ATT_HD) for h in range(ATT_HEADS)]
    tq = qt_ref.shape[1]
    for qh in range(tq // LANES):
        qs = slice(qh * LANES, (qh + 1) * LANES)
        bias = bias_ref[0, 0, :, qs]
        m_prev, l_prev = m_ref[:, qs], l_ref[:, qs]
        m_rows, l_rows = [], []
        for h, hs in enumerate(heads):
            s = _dg(k_ref[:, hs], qt_ref[hs, qs], _NN) + bias
            m_old = m_prev[h:h + 1, :]
            m_new = jnp.maximum(m_old, jnp.max(s, axis=0, keepdims=True))
            alpha = jnp.exp2(m_old - m_new)
            p = jnp.exp2(s - m_new)
            l_rows.append(alpha * l_prev[h:h + 1, :] + jnp.sum(p, axis=0, keepdims=True))
            m_rows.append(m_new)
            acc_ref[h, :, qs] = alpha * acc_ref[h, :, qs] + _dg(vt_ref[hs, :], p.astype(BF16), _NN)
        m_ref[:, qs] = jnp.concatenate(m_rows, axis=0)
        l_ref[:, qs] = jnp.concatenate(l_rows, axis=0)

    @pl.when(ki == qi)
    def _():
        for h in range(ATT_HEADS):
            hs = slice(h * ATT_HD, (h + 1) * ATT_HD)
            out = (acc_ref[h] / l_ref[h:h + 1, :]).T
            o_ref[:, hs] = (out * _silu(g_ref[:, hs])).astype(BF16)


def _prompt_attention(qt, kb, vt, bias, p, tq):
    t = kb.shape[0]
    nq = t // tq
    pairs = [(i, j) for i in range(nq) for j in range(i + 1)]
    qi = jnp.asarray([a for a, _ in pairs], I32)
    ki = jnp.asarray([b for _, b in pairs], I32)
    grid_spec = pltpu.PrefetchScalarGridSpec(
        num_scalar_prefetch=2,
        grid=(len(pairs),),
        in_specs=[pl.BlockSpec((ATT_W, tq), lambda s, qi, ki: (0, qi[s])),
                  pl.BlockSpec((tq, ATT_W), lambda s, qi, ki: (ki[s], 0)),
                  pl.BlockSpec((ATT_W, tq), lambda s, qi, ki: (0, ki[s])),
                  pl.BlockSpec((1, 1, tq, tq), lambda s, qi, ki: (qi[s], ki[s], 0, 0)),
                  pl.BlockSpec((tq, ATT_W), lambda s, qi, ki: (qi[s], COL_G // ATT_W))],
        out_specs=pl.BlockSpec((tq, ATT_W), lambda s, qi, ki: (qi[s], 0)),
        scratch_shapes=[pltpu.VMEM((ATT_HEADS, tq), F32),
                        pltpu.VMEM((ATT_HEADS, tq), F32),
                        pltpu.VMEM((ATT_HEADS, ATT_HD, tq), F32)],
    )
    return pl.pallas_call(
        _flash_kernel,
        grid_spec=grid_spec,
        out_shape=jax.ShapeDtypeStruct((t, ATT_W), BF16),
        compiler_params=_cparams(("arbitrary",)),
        name="prompt_attention",
    )(qi, ki, qt, kb, vt, bias, p)


def _sample_attn_kernel(pt_ref, q_ref, iq_ref, w_ref, ikn_ref, kn_ref, vn_ref, g_ref, ex_ref,
                        cik_ref, ck_ref, cv_ref, o_ref,
                        ikbuf, kbuf, vbuf, sce_ref, isem, ksem, vsem, *, n_sel, pps):
    b = pl.program_id(0)
    n_pages = ikbuf.shape[0]
    page_rows = PAGE_SIZE * ATT_HEADS

    def idx_copy(j):
        return pltpu.make_async_copy(cik_ref.at[pt_ref[b, j]], ikbuf.at[j], isem.at[j])

    def k_copy(j, slot):
        return pltpu.make_async_copy(ck_ref.at[pt_ref[b, j]], kbuf.at[slot], ksem.at[slot])

    def v_copy(j, slot):
        return pltpu.make_async_copy(cv_ref.at[pt_ref[b, j]], vbuf.at[slot], vsem.at[slot])

    def start_idx(j, c):
        idx_copy(j).start()
        return c

    lax.fori_loop(0, n_pages, start_idx, 0)

    def start_group(t, grp):
        for i in range(pps):
            k_copy(t * pps + i, grp * pps + i).start()
            v_copy(t * pps + i, grp * pps + i).start()

    start_group(0, 0)

    iq = iq_ref[0]
    w = w_ref[0]

    def wait_idx(j, c):
        idx_copy(j).wait()
        return c

    lax.fori_loop(0, n_pages, wait_idx, 0)
    ik_all = ikbuf[...].reshape(n_pages * PAGE_SIZE, IDX_HD).astype(BF16)
    d = _dg(iq, ik_all, _NT)
    sc_flat = jnp.sum(jnp.maximum(d, 0.0) * w, axis=0, keepdims=True)
    sc = jnp.concatenate([sc_flat[:, j * PAGE_SIZE:(j + 1) * PAGE_SIZE] for j in range(n_pages)], axis=0)
    d_self = jnp.sum(iq.astype(F32) * ikn_ref[0].astype(F32), axis=-1, keepdims=True)
    s_self = jnp.sum(jnp.maximum(d_self, 0.0) * w, axis=0, keepdims=True)

    def count_ge(cand):
        cand_f = _ordered_to_f32(cand)
        cnt = jnp.sum(jnp.where(sc >= cand_f, 1.0, 0.0), axis=-1, keepdims=True)
        return jnp.sum(cnt, axis=0, keepdims=True) + jnp.where(s_self >= cand_f, 1.0, 0.0)

    def digit_body(it, ans):
        step = jnp.left_shift(jnp.int32(1), 30 - 2 * it)
        for k in (1, 2, 3):
            cand = ans + k * step if k == 1 else cand + step
            best = jnp.where(count_ge(cand) >= n_sel, cand, ans if k == 1 else best)
        return best

    ans = lax.fori_loop(0, 16, digit_body, jnp.full((1, 1), INT_MIN, I32))
    thr = _ordered_to_f32(ans)
    thr = jnp.where(thr != thr, -jnp.inf, thr)

    sce_ref[...] = _dot_exact_rhs(sc, ex_ref[...])

    q8 = q_ref[0]
    own = (lax.broadcasted_iota(I32, (ATT_HEADS, page_rows), 1) % ATT_HEADS
           == lax.broadcasted_iota(I32, (ATT_HEADS, page_rows), 0))

    kn = kn_ref[0].astype(BF16).astype(F32)
    s0 = jnp.sum(q8.astype(F32) * kn, axis=-1, keepdims=True)
    m0 = jnp.where(s_self >= thr, s0, NEG_BIG)
    l0 = jnp.where(m0 > 0.5 * NEG_BIG, 1.0, 0.0)
    acc0 = l0 * vn_ref[0].astype(BF16).astype(F32)

    def att_body(t, carry):
        m_prev, l_prev, acc = carry
        grp = t % 2

        @pl.when(t + 1 < n_pages // pps)
        def _():
            start_group(t + 1, 1 - grp)

        for i in range(pps):
            k_copy(t * pps + i, grp * pps + i).wait()
            v_copy(t * pps + i, grp * pps + i).wait()
        s = []
        for i in range(pps):
            kf = kbuf[grp * pps + i].reshape(page_rows, ATT_HD).astype(BF16)
            keep = own & (sce_ref[pl.ds(t * pps + i, 1), :] >= thr)
            s.append(jnp.where(keep, _dg(q8, kf, _NT), NEG_BIG))
        m_new = m_prev
        for x in s:
            m_new = jnp.maximum(m_new, jnp.max(x, axis=-1, keepdims=True))
        alpha = jnp.exp2(m_prev - m_new)
        p = [jnp.exp2(x - m_new) for x in s]
        l_new = alpha * l_prev
        for x in p:
            l_new = l_new + jnp.sum(x, axis=-1, keepdims=True)
        pv = [jnp.dot(p[i].astype(BF16), vbuf[grp * pps + i].reshape(page_rows, ATT_HD).astype(BF16),
                      preferred_element_type=F32) for i in range(pps)]
        acc = alpha * acc
        for x in pv:
            acc = acc + x
        return m_new, l_new, acc

    m, l, acc = lax.fori_loop(0, n_pages // pps, att_body, (m0, l0, acc0))
    o_ref[0] = acc / l * _silu(g_ref[0])


def _sample_attention(page_table, qb, iqb, wv, ikb, k32, v32, g, cache_idx_k, cache_k, cache_v, n_sel):
    nb, n_pages = page_table.shape
    head3 = pl.BlockSpec((1, ATT_HEADS, ATT_HD), lambda b, pt: (b, 0, 0))
    any_spec = pl.BlockSpec(memory_space=pl.ANY)
    page_rows = PAGE_SIZE * ATT_HEADS
    pps = math.gcd(DECODE_PAGES_PER_STEP, n_pages)
    expand = (jnp.arange(page_rows)[None, :] // ATT_HEADS == jnp.arange(PAGE_SIZE)[:, None]).astype(BF16)
    grid_spec = pltpu.PrefetchScalarGridSpec(
        num_scalar_prefetch=1,
        grid=(nb,),
        in_specs=[head3,
                  pl.BlockSpec((1, IDX_HEADS, IDX_HD), lambda b, pt: (b, 0, 0)),
                  pl.BlockSpec((1, IDX_HEADS, 1), lambda b, pt: (b, 0, 0)),
                  pl.BlockSpec((1, 1, IDX_HD), lambda b, pt: (b, 0, 0)),
                  head3, head3, head3,
                  pl.BlockSpec((PAGE_SIZE, page_rows), lambda b, pt: (0, 0)),
                  any_spec, any_spec, any_spec],
        out_specs=head3,
        scratch_shapes=[pltpu.VMEM((n_pages, PAGE_SIZE, IDX_HD), F32),
                        pltpu.VMEM((2 * pps, PAGE_SIZE, ATT_HEADS, ATT_HD), F32),
                        pltpu.VMEM((2 * pps, PAGE_SIZE, ATT_HEADS, ATT_HD), F32),
                        pltpu.VMEM((n_pages, page_rows), F32),
                        pltpu.SemaphoreType.DMA((n_pages,)),
                        pltpu.SemaphoreType.DMA((2 * pps,)),
                        pltpu.SemaphoreType.DMA((2 * pps,))],
    )
    head = lambda a: a.reshape(nb, ATT_HEADS, ATT_HD)
    out = pl.pallas_call(
        functools.partial(_sample_attn_kernel, n_sel=n_sel, pps=pps),
        grid_spec=grid_spec,
        out_shape=jax.ShapeDtypeStruct((nb, ATT_HEADS, ATT_HD), F32),
        compiler_params=_cparams(("arbitrary",)),
        name="sample_attention",
    )(page_table, head(qb),
      iqb.reshape(nb, IDX_HEADS, IDX_HD),
      wv[:, IDX_HD:IDX_HD + IDX_HEADS].reshape(nb, IDX_HEADS, 1),
      ikb.reshape(nb, 1, IDX_HD),
      head(k32), head(v32), head(g), expand,
      cache_idx_k, cache_k, cache_v)
    return out.reshape(nb, ATT_W).astype(BF16)


def _rwkv_prep_kernel(*refs, shifted):
    n_x = 5
    x_refs = refs[:n_x]
    if shifted:
        halo_refs, first_refs = refs[n_x:2 * n_x], refs[2 * n_x:3 * n_x]
        rest = refs[3 * n_x:]
    else:
        prev_refs = refs[n_x:2 * n_x]
        rest = refs[2 * n_x:]
    (mu_ref, mus_ref, wl_ref, w0_ref, a0_ref, kk_ref, ka_ref, rk_ref, ones_ref,
     ro_ref, kpo_ref, vo_ref, kko_ref, bo_ref, lwo_ref, bonus_ref, sg_ref) = rest

    def lerp(j, mu):
        x = x_refs[j][...]
        if shifted:
            above = jnp.where(pl.program_id(0) == 0, first_refs[j][...],
                              halo_refs[j][SUBLANES - 1:SUBLANES, :])
            row = lax.broadcasted_iota(I32, x.shape, 0)
            prev = jnp.where(row == 0, above, pltpu.roll(x, 1, axis=0))
        else:
            prev = prev_refs[j][...]
        return x + mu * (prev - x)

    w = RWKV_W
    xr = lerp(0, mu_ref[:, 0:w])
    xk = lerp(1, mu_ref[:, w:2 * w])
    xv = lerp(2, mu_ref[:, 2 * w:3 * w])
    xg = lerp(3, mu_ref[:, 3 * w:4 * w])
    xs = lerp(4, mus_ref[...])
    lane = lax.broadcasted_iota(I32, xs.shape, 1)
    lora_in = jnp.where(lane < DECAY_LORA, jnp.tanh(xs), xs)
    lora = _dot3(lora_in, wl_ref[...])
    z = w0_ref[...] + lora[:, :w]
    softplus_neg = jnp.maximum(-z, 0.0) + jnp.log(1.0 + jnp.exp(-jnp.abs(z)))
    w_log = -softplus_neg - 0.5
    lwo_ref[...] = -jnp.exp(w_log)
    a = jax.nn.sigmoid(a0_ref[...] + lora[:, w:])
    ones_bd = ones_ref[...]
    kk = xk * kk_ref[...]
    kk = kk / jnp.maximum(jnp.sqrt(_head_sum(kk * kk, ones_bd)), 1e-12)
    kp = xk * (1.0 + (a - 1.0) * ka_ref[...])
    ro_ref[...] = xr
    kpo_ref[...] = kp
    vo_ref[...] = xv
    kko_ref[...] = kk
    bo_ref[...] = kk * a
    bonus_ref[...] = _head_sum(xr * kp * rk_ref[...], ones_bd) * xv
    sg_ref[...] = _silu(xg)


def _rwkv_prep(p, prev, consts, shifted):
    t = p.shape[0]
    tm = min(256, t)
    mu_big, mu_small, w_lora, w0, a0, k_k, k_a, r_k, ones_bd = consts
    cols = [(RWKV_W, COL_RR // RWKV_W), (RWKV_W, COL_RK // RWKV_W), (RWKV_W, COL_RV // RWKV_W),
            (RWKV_W, COL_RG // RWKV_W), (LANES, COL_SMALL_R // LANES)]
    x_specs = [pl.BlockSpec((tm, wd), lambda i, c=c: (i, c)) for wd, c in cols]
    if shifted:
        per8 = tm // SUBLANES
        halo = [pl.BlockSpec((SUBLANES, wd), lambda i, c=c: (jnp.maximum(i * per8 - 1, 0), c))
                for wd, c in cols]
        first = [pl.BlockSpec((1, wd), lambda i, c=c: (0, c)) for wd, c in cols]
        extra_specs, extra_args = halo + first, [p] * 5 + [prev] * 5
    else:
        extra_specs, extra_args = x_specs, [prev] * 5
    full = lambda a: pl.BlockSpec(a.shape, lambda i: (0, 0))
    row = pl.BlockSpec((tm, RWKV_W), lambda i: (i, 0))
    vec = lambda a: a.reshape(1, -1)
    args = [mu_big, mu_small, w_lora, vec(w0), vec(a0), vec(k_k), vec(k_a), vec(r_k), ones_bd]
    return pl.pallas_call(
        functools.partial(_rwkv_prep_kernel, shifted=shifted),
        grid=(t // tm,),
        in_specs=x_specs + extra_specs + [full(a) for a in args],
        out_specs=[row] * 8,
        out_shape=[jax.ShapeDtypeStruct((t, RWKV_W), F32)] * 8,
        compiler_params=_cparams(("arbitrary",)),
        name="rwkv_prep",
    )(*([p] * 5), *extra_args, *args)


def _scan_kernel(r_ref, kp_ref, v_ref, kk_ref, b_ref, lw_ref, s0_ref, y_ref, st_ref, s_ref):
    @pl.when(pl.program_id(0) == 0)
    def _():
        s_ref[...] = s0_ref[...]

    ch = SCAN_CHUNK
    assert ch == RWKV_HD
    rows_total = r_ref.shape[0]
    n_sub = rows_total // ch
    gw = MXU_DIM
    nh = HEADS_PER_GROUP
    lw = lw_ref[...]
    tr = lax.broadcasted_iota(I32, (rows_total, rows_total), 0)
    tc = lax.broadcasted_iota(I32, (rows_total, rows_total), 1)
    tri = jnp.where((tr >= tc) & (tr // ch == tc // ch), 1.0, 0.0).astype(BF16)
    cum = _dot_exact_lhs(tri, lw)
    cum_end = jnp.concatenate(
        [jnp.broadcast_to(cum[(c + 1) * ch - 1:(c + 1) * ch, :], (ch, RWKV_W)) for c in range(n_sub)], axis=0)
    kk, b, kp, r, v = kk_ref[...], b_ref[...], kp_ref[...], r_ref[...], v_ref[...]
    g_inv = jnp.exp(-cum)
    g_end = jnp.exp(cum_end - cum)
    a_t = (-kk * jnp.exp(cum - lw)).astype(BF16)
    b_t = b * g_inv
    k_t = kp * g_inv
    r_t = (r * jnp.exp(cum)).astype(BF16)
    b_h = (b * g_end).astype(BF16)
    k_h = (kp * g_end).astype(BF16)
    vb = v.astype(BF16)

    srow = lax.broadcasted_iota(I32, (nh * ch, gw), 0)
    scol = lax.broadcasted_iota(I32, (nh * ch, gw), 1)
    own = (srow // ch) == (scol // RWKV_HD)
    step = lax.broadcasted_iota(I32, (ch, gw), 0)
    other = lax.broadcasted_iota(I32, (ch, gw), 1) % ch
    strict, incl = step > other, step >= other
    eye = jnp.where(step == other, 1.0, 0.0)

    def stack(x):
        return jnp.where(own, jnp.concatenate([x] * nh, axis=0), 0.0).astype(BF16)

    groups = range(N_GROUPS)
    units = [(c, g) for c in range(n_sub) for g in groups]
    blk = {(c, g): (slice(c * ch, (c + 1) * ch), slice(g * gw, (g + 1) * gw)) for c, g in units}
    rb = {u: stack(b_t[blk[u]]) for u in units}
    rk = {u: stack(k_t[blk[u]]) for u in units}
    v4 = {u: stack(v[blk[u]]) for u in units}
    n_row = {u: jnp.where(strict, _dg(a_t[blk[u]], rb[u], _NT), 0.0) for u in units}
    bm_row = {u: jnp.where(strict, _dg(a_t[blk[u]], rk[u], _NT), 0.0).astype(BF16) for u in units}
    rb_row = {u: jnp.where(incl, _dg(r_t[blk[u]], rb[u], _NT), 0.0).astype(BF16) for u in units}
    rk_row = {u: jnp.where(incl, _dg(r_t[blk[u]], rk[u], _NT), 0.0).astype(BF16) for u in units}
    minv = {u: eye + n_row[u] for u in units}
    pw = n_row
    pw_bd = {u: stack(pw[u]) for u in units}
    for _ in range(int(math.log2(ch)) - 1):
        pw = {u: _dg(pw[u].astype(BF16), pw_bd[u], _NN) for u in units}
        pw_bd = {u: stack(pw[u]) for u in units}
        minv = {u: minv[u] + _dg(minv[u].astype(BF16), pw_bd[u], _NN) for u in units}
    minv = {u: minv[u].astype(BF16) for u in units}
    y_fixed = {u: _dg(rk_row[u], v4[u], _NN) for u in units}
    x_fixed = {u: _dg(bm_row[u], v4[u], _NN) for u in units}

    s = [s_ref[g] for g in groups]
    for c in range(n_sub):
        sb = [x.astype(BF16) for x in s]
        x_row = [_dg(a_t[blk[c, g]], sb[g], _NT) + x_fixed[c, g] for g in groups]
        x_bd = [stack(x) for x in x_row]
        u_row = [_dg(minv[c, g], x_bd[g], _NN) for g in groups]
        u_bd = [stack(x) for x in u_row]
        for g in groups:
            y_ref[blk[c, g]] = _dg(r_t[blk[c, g]], sb[g], _NT) + _dg(rb_row[c, g], u_bd[g], _NN) + y_fixed[c, g]
        for g in groups:
            rs, ls = blk[c, g]
            upd = _dg(jnp.concatenate([u_row[g].astype(BF16), vb[rs, ls]], axis=0),
                      jnp.concatenate([b_h[rs, ls], k_h[rs, ls]], axis=0), _TN)
            g_c = jnp.exp(cum[(c + 1) * ch - 1:(c + 1) * ch, ls])
            s[g] = s[g] * g_c + jnp.where(own, upd, 0.0)
    for g in groups:
        s_ref[g] = s[g]

    @pl.when(pl.program_id(0) == pl.num_programs(0) - 1)
    def _():
        st_ref[...] = s_ref[...]


def _rwkv_scan(r, kp, v, kk, b, lw, s0):
    t = r.shape[0]
    ch = SCAN_CHUNK * min(SCAN_SUB, t // SCAN_CHUNK)
    row = pl.BlockSpec((ch, RWKV_W), lambda i: (i, 0))
    st = pl.BlockSpec((N_GROUPS, MXU_DIM, MXU_DIM), lambda i: (0, 0, 0))
    return pl.pallas_call(
        _scan_kernel,
        grid=(t // ch,),
        in_specs=[row] * 6 + [st],
        out_specs=[row, st],
        out_shape=[jax.ShapeDtypeStruct((t, RWKV_W), F32),
                   jax.ShapeDtypeStruct((N_GROUPS, MXU_DIM, MXU_DIM), F32)],
        scratch_shapes=[pltpu.VMEM((N_GROUPS, MXU_DIM, MXU_DIM), F32)],
        compiler_params=_cparams(("arbitrary",)),
        name="rwkv_scan",
    )(r, kp, v, kk, b, lw, s0)


def _rwkv_step_kernel(s_ref, r_ref, kp_ref, kk_ref, b_ref, lw_ref, v_ref, so_ref, y_ref):
    s = s_ref[0]
    kk = kk_ref[0]
    sa = -jnp.sum(s * kk, axis=-1, keepdims=True)
    s_new = s * jnp.exp(lw_ref[0]) + sa * b_ref[0] + v_ref[0] * kp_ref[0]
    so_ref[0] = s_new
    y_ref[0] = jnp.sum(s_new * r_ref[0], axis=-1, keepdims=True)


def _rwkv_step(state, r, kp, kk, b, lw, v):
    nb = state.shape[0]
    rowv = lambda a: a.reshape(nb, RWKV_HEADS, 1, RWKV_HD)
    rspec = pl.BlockSpec((1, RWKV_HEADS, 1, RWKV_HD), lambda i: (i, 0, 0, 0))
    cspec = pl.BlockSpec((1, RWKV_HEADS, RWKV_HD, 1), lambda i: (i, 0, 0, 0))
    sspec = pl.BlockSpec((1, RWKV_HEADS, RWKV_HD, RWKV_HD), lambda i: (i, 0, 0, 0))
    s_new, y = pl.pallas_call(
        _rwkv_step_kernel,
        grid=(nb,),
        in_specs=[sspec, rspec, rspec, rspec, rspec, rspec, cspec],
        out_specs=[sspec, cspec],
        out_shape=[jax.ShapeDtypeStruct(state.shape, F32),
                   jax.ShapeDtypeStruct((nb, RWKV_HEADS, RWKV_HD, 1), F32)],
        compiler_params=_cparams(("arbitrary",)),
        name="rwkv_step",
    )(state, rowv(r), rowv(kp), rowv(kk), rowv(b), rowv(lw), v.reshape(nb, RWKV_HEADS, RWKV_HD, 1))
    return s_new, y.reshape(nb, RWKV_W)


def _rwkv_post_kernel(y_ref, bonus_ref, sg_ref, lg_ref, lb_ref, ones_ref, o_ref):
    y = y_ref[...]
    ones_bd = ones_ref[...]
    mean = _head_sum(y, ones_bd) * (1.0 / RWKV_HD)
    yc = y - mean
    var = _head_sum(yc * yc, ones_bd) * (1.0 / RWKV_HD)
    yn = yc * lax.rsqrt(var + GN_EPS) * lg_ref[...] + lb_ref[...]
    o_ref[...] = ((yn + bonus_ref[...]) * sg_ref[...]).astype(BF16)


def _rwkv_post(y, bonus, sg, ln_g, ln_b, ones_bd):
    t = y.shape[0]
    tm = min(512, t)
    row = pl.BlockSpec((tm, RWKV_W), lambda i: (i, 0))
    vec = pl.BlockSpec((1, RWKV_W), lambda i: (0, 0))
    return pl.pallas_call(
        _rwkv_post_kernel,
        grid=(t // tm,),
        in_specs=[row, row, row, vec, vec, pl.BlockSpec(ones_bd.shape, lambda i: (0, 0))],
        out_specs=row,
        out_shape=jax.ShapeDtypeStruct((t, RWKV_W), BF16),
        compiler_params=_cparams(("arbitrary",)),
        name="rwkv_post",
    )(y, bonus, sg, ln_g.reshape(1, -1), ln_b.reshape(1, -1), ones_bd)


def _outproj_kernel(att_ref, rw_ref, wa_ref, wr_ref, x_ref, gate_ref, o_ref):
    out = (jnp.dot(att_ref[...], wa_ref[...], preferred_element_type=F32)
           + jnp.dot(rw_ref[...], wr_ref[...], preferred_element_type=F32))
    o_ref[...] = x_ref[...] + gate_ref[...] * out


def _out_projection(att, rw, w_out_bf16, x, gate):
    t, d = x.shape
    tm = min(512, t)
    per_row = gate.shape[0] != 1
    gspec = (pl.BlockSpec((tm, d), lambda i: (i, 0)) if per_row else pl.BlockSpec((1, d), lambda i: (0, 0)))
    return pl.pallas_call(
        _outproj_kernel,
        grid=(t // tm,),
        in_specs=[pl.BlockSpec((tm, ATT_W), lambda i: (i, 0)),
                  pl.BlockSpec((tm, RWKV_W), lambda i: (i, 0)),
                  pl.BlockSpec((ATT_W, d), lambda i: (0, 0)),
                  pl.BlockSpec((RWKV_W, d), lambda i: (1, 0)),
                  pl.BlockSpec((tm, d), lambda i: (i, 0)),
                  gspec],
        out_specs=pl.BlockSpec((tm, d), lambda i: (i, 0)),
        out_shape=jax.ShapeDtypeStruct((t, d), F32),
        compiler_params=_cparams(("arbitrary",)),
        name="out_projection",
    )(att, rw, w_out_bf16, w_out_bf16, x, gate)


def _reorder_cols(w):
    a = 0
    q, k, v, g = (w[..., a + i * ATT_W:a + (i + 1) * ATT_W] for i in range(4))
    a = 4 * ATT_W
    iq = w[..., a:a + IDX_W]
    iw = w[..., a + IDX_W:a + IDX_W + IDX_HEADS]
    ik = w[..., a + IDX_W + IDX_HEADS:a + IDX_W + IDX_HEADS + IDX_HD]
    o = N_ATT_COLS_ORIG
    rr, rk, rv = (w[..., o + i * RWKV_W:o + (i + 1) * RWKV_W] for i in range(3))
    o2 = o + 3 * RWKV_W
    wl = w[..., o2:o2 + DECAY_LORA]
    al = w[..., o2 + DECAY_LORA:o2 + DECAY_LORA + ICL_LORA]
    rg = w[..., o2 + DECAY_LORA + ICL_LORA:o2 + DECAY_LORA + ICL_LORA + RWKV_W]
    pad_a = jnp.zeros(w.shape[:-1] + (LANES - IDX_HD - IDX_HEADS,), w.dtype)
    pad_end = jnp.zeros(w.shape[:-1] + (N_COLS_PAD - N_COLS,), w.dtype)
    return jnp.concatenate([q, k, v, g, iq, rr, rk, rv, rg, ik, iw, pad_a, wl, al, pad_end], axis=-1)


def _reorder_cast_kernel(w_ref, o_ref):
    o_ref[...] = _reorder_cols(w_ref[...]).astype(BF16)


def _reorder_cast(w):
    rows, n = w.shape
    tr = min(256, rows)
    return pl.pallas_call(
        _reorder_cast_kernel,
        grid=(rows // tr,),
        in_specs=[pl.BlockSpec((tr, n), lambda i: (i, 0))],
        out_specs=pl.BlockSpec((tr, N_COLS_PAD), lambda i: (i, 0)),
        out_shape=jax.ShapeDtypeStruct((rows, N_COLS_PAD), BF16),
        compiler_params=_cparams(("arbitrary",)),
        name="reorder_cast",
    )(w)


def _rwkv_cols_original_order(p):
    return jnp.concatenate([p[..., COL_RR:COL_RR + RWKV_W], p[..., COL_RK:COL_RK + RWKV_W],
                            p[..., COL_RV:COL_RV + RWKV_W],
                            p[..., COL_SMALL_R:COL_SMALL_R + DECAY_LORA + ICL_LORA],
                            p[..., COL_RG:COL_RG + RWKV_W]], axis=-1)


def _blockdiag_to_state(sb):
    sb = sb.reshape(N_GROUPS, HEADS_PER_GROUP, RWKV_HD, HEADS_PER_GROUP, RWKV_HD)
    return jnp.stack([sb[:, h, :, h, :] for h in range(HEADS_PER_GROUP)], axis=1).reshape(
        RWKV_HEADS, RWKV_HD, RWKV_HD)


def kernel(x_prompt, x_sample, c_prompt, c_sample, cache_k, cache_v, cache_idx_k, state_wkv, state_shift,
           page_table, w_ada, b_ada, norm_g, w_in, q_norm_g, k_norm_g, shift_mu, w0, w_b, a0, a_b,
           k_k, k_a, r_k, ln_x_g, ln_x_b, w_out):
    depth = w_ada.shape[0]
    assert depth == 1, "single-layer trunk"
    bsz, seq, d = x_prompt.shape
    assert bsz == 1
    nb, dec_seq, _ = x_sample.shape
    assert dec_seq == 1
    n_pages = page_table.shape[1]
    past_len = n_pages * PAGE_SIZE
    l = 0

    w_in_r = _reorder_cast(w_in[l])
    w_out_b = w_out[l].astype(BF16)
    mu = _reorder_cols(jnp.concatenate([jnp.zeros((N_ATT_COLS_ORIG,), F32), shift_mu[l]]))
    mu_big = mu[COL_RR:COL_RR + 4 * RWKV_W].reshape(1, -1)
    mu_small = mu[COL_SMALL_R:COL_SMALL_R + LANES].reshape(1, -1)
    zl = jnp.zeros((DECAY_LORA, RWKV_W), F32)
    w_lora = jnp.concatenate([jnp.concatenate([w_b[l], zl], axis=1),
                              jnp.concatenate([zl, a_b[l]], axis=1)], axis=0)
    hid = jnp.arange(LANES) // RWKV_HD
    ones_bd = (hid[:, None] == hid[None, :]).astype(BF16)
    rwkv_consts = (mu_big, mu_small, w_lora, w0[l], a0[l], k_k[l], k_a[l], r_k[l].reshape(-1), ones_bd)

    n_mod = 1 + nb
    c_all = jnp.concatenate([c_prompt, c_sample, jnp.zeros((-n_mod % SUBLANES, d), F32)], axis=0)
    mod = _modulation(c_all, w_ada[l], b_ada[l])
    shift_p, scale_p, gate_p = mod[0:1, :d], mod[0:1, d:2 * d], mod[0:1, 2 * d:]
    shift_s, scale_s, gate_s = mod[1:n_mod, :d], mod[1:n_mod, d:2 * d], mod[1:n_mod, 2 * d:]

    xp = x_prompt.reshape(seq, d)
    xs = x_sample.reshape(nb, d)

    p = _in_projection(xp, shift_p, scale_p, norm_g[l], w_in_r)
    qt, k32, kb, vt, v32, iqb, ik32, ikb, wv = _attention_prep(p, _rope_tables(jnp.arange(seq)),
                                                         q_norm_g[l], k_norm_g[l], transposed=True)
    n_sel = min(TOPK_MAX, seq // 4)
    tq = min(MXU_DIM, seq)
    bias = _prompt_indexer(iqb, wv[:, IDX_HD:IDX_HD + IDX_HEADS].T, ikb, n_sel, tq)
    att = _prompt_attention(qt, kb, vt, bias, p, tq)

    r_, kp_, v_, kk_, b_, lw_, bonus, sg = _rwkv_prep(p, jnp.zeros((1, p.shape[1]), F32), rwkv_consts,
                                                      shifted=True)
    y_scan, s_fin = _rwkv_scan(r_, kp_, v_, kk_, b_, lw_,
                               jnp.zeros((N_GROUPS, MXU_DIM, MXU_DIM), F32))
    rw = _rwkv_post(y_scan, bonus, sg, ln_x_g[l], ln_x_b[l], ones_bd)
    y_p = _out_projection(att, rw, w_out_b, xp, gate_p)

    k_prompt = k32.reshape(1, 1, seq, ATT_HEADS, ATT_HD)
    v_prompt = v32.reshape(1, 1, seq, ATT_HEADS, ATT_HD)
    idx_k_prompt = ik32.reshape(1, 1, seq, IDX_HD)
    wkv_prompt = _blockdiag_to_state(s_fin).reshape(1, 1, RWKV_HEADS, RWKV_HD, RWKV_HD)
    shift_prompt = _rwkv_cols_original_order(p[seq - 1:seq]).reshape(1, 1, -1)

    ps = _in_projection(xs, shift_s, scale_s, norm_g[l], w_in_r)
    pos_s = jnp.full((nb,), past_len, jnp.int32)
    qb_s, k32_s, _, _, v32_s, iqb_s, ik32_s, ikb_s, wv_s = _attention_prep(ps, _rope_tables(pos_s),
                                                                          q_norm_g[l], k_norm_g[l],
                                                                          transposed=False)
    n_sel_s = min(TOPK_MAX, (past_len + 1) // 4)
    att_s = _sample_attention(page_table, qb_s, iqb_s, wv_s, ikb_s, k32_s, v32_s,
                              ps[:, COL_G:COL_G + ATT_W], cache_idx_k[l], cache_k[l], cache_v[l], n_sel_s)

    prev_s = _reorder_cols(jnp.concatenate([jnp.zeros((nb, N_ATT_COLS_ORIG), F32), state_shift[l]], axis=-1))
    r_s, kp_s, v_s, kk_s, b_s, lw_s, bonus_s, sg_s = _rwkv_prep(ps, prev_s, rwkv_consts, shifted=False)
    wkv_new, y_step = _rwkv_step(state_wkv[l], r_s, kp_s, kk_s, b_s, lw_s, v_s)
    rw_s = _rwkv_post(y_step, bonus_s, sg_s, ln_x_g[l], ln_x_b[l], ones_bd)
    y_s = _out_projection(att_s, rw_s, w_out_b, xs, gate_s)

    k_sample = k32_s.reshape(1, nb, 1, ATT_HEADS, ATT_HD)
    v_sample = v32_s.reshape(1, nb, 1, ATT_HEADS, ATT_HD)
    idx_k_sample = ik32_s.reshape(1, nb, 1, IDX_HD)
    wkv_sample = wkv_new.reshape(1, nb, RWKV_HEADS, RWKV_HD, RWKV_HD)
    shift_sample = _rwkv_cols_original_order(ps).reshape(1, nb, -1)

    return (y_p.reshape(1, seq, d), y_s.reshape(nb, 1, d),
            k_prompt, v_prompt, idx_k_prompt, wkv_prompt, shift_prompt,
            k_sample, v_sample, idx_k_sample, wkv_sample, shift_sample)
```

```python
import functools
import math

import jax
import jax.numpy as jnp
from jax import lax
from jax.experimental import pallas as pl
from jax.experimental.pallas import tpu as pltpu

F32 = jnp.float32
BF16 = jnp.bfloat16
I32 = jnp.int32

ATT_HEADS = 8
ATT_HD = 128
ATT_W = ATT_HEADS * ATT_HD
IDX_HEADS = 16
IDX_HD = 64
IDX_W = IDX_HEADS * IDX_HD
TOPK_MAX = 256
PAGE_SIZE = 128
ROPE_THETA = 10000.0
RWKV_HD = 64
RWKV_HEADS = 16
RWKV_W = RWKV_HEADS * RWKV_HD
DECAY_LORA = 64
ICL_LORA = 64
NORM_EPS = 1e-6
GN_EPS = 64e-5

LANES = 128
SUBLANES = 8
MXU_DIM = 256
VMEM_LIMIT = 56 * 1024 * 1024
NEG_BIG = -1e30
INT_MIN = -(2 ** 31)

COL_Q, COL_K, COL_V, COL_G, COL_IQ = 0, 1024, 2048, 3072, 4096
COL_RR, COL_RK, COL_RV, COL_RG = 5120, 6144, 7168, 8192
COL_SMALL_A = 9216
COL_SMALL_R = 9344
N_COLS = 9472
N_COLS_PAD = 9728
N_ATT_COLS_ORIG = 4 * ATT_W + IDX_W + IDX_HEADS + IDX_HD

SCAN_CHUNK = 64
DECODE_PAGES_PER_STEP = 8
SCAN_SUB = 4
HEADS_PER_GROUP = MXU_DIM // RWKV_HD
N_GROUPS = RWKV_HEADS // HEADS_PER_GROUP


def _cparams(sem, vmem=VMEM_LIMIT):
    return pltpu.CompilerParams(dimension_semantics=sem, vmem_limit_bytes=vmem)


def _split2(x):
    hi = x.astype(BF16)
    lo = (x - hi.astype(F32)).astype(BF16)
    return hi, lo


def _split3(x):
    hi = x.astype(BF16)
    r = x - hi.astype(F32)
    mid = r.astype(BF16)
    lo = (r - mid.astype(F32)).astype(BF16)
    return hi, mid, lo


_NN = (((1,), (0,)), ((), ()))
_NT = (((1,), (1,)), ((), ()))
_TN = (((0,), (0,)), ((), ()))


def _dg(a, b, dims):
    return lax.dot_general(a, b, dims, preferred_element_type=F32)


def _dot1(a, b, dims=_NN):
    return _dg(a.astype(BF16), b.astype(BF16), dims)


def _dot3(a, b, dims=_NN):
    ah, al = _split2(a)
    bh, bl = _split2(b)
    return _dg(ah, bh, dims) + _dg(al, bh, dims) + _dg(ah, bl, dims)


def _dot_exact_lhs(a_bf16, b, dims=_NN):
    b0, b1, b2 = _split3(b)
    return _dg(a_bf16, b0, dims) + _dg(a_bf16, b1, dims) + _dg(a_bf16, b2, dims)


def _dot_exact_rhs(a, b_bf16, dims=_NN):
    a0, a1, a2 = _split3(a)
    return _dg(a0, b_bf16, dims) + _dg(a1, b_bf16, dims) + _dg(a2, b_bf16, dims)


def _silu(x):
    return x * jax.nn.sigmoid(x)


def _head_sum(x, ones_bd):
    parts = []
    for j in range(x.shape[1] // LANES):
        parts.append(_dot_exact_rhs(x[:, j * LANES:(j + 1) * LANES], ones_bd))
    return jnp.concatenate(parts, axis=1)


def _ordered_to_f32(key):
    return pltpu.bitcast(key ^ ((key >> 31) & 0x7FFFFFFF), F32)


def _mod_kernel(c_ref, w_ref, b_ref, o_ref):
    o_ref[...] = _dot3(_silu(c_ref[...]), w_ref[...]) + b_ref[...]


def _modulation(c, w_ada, b_ada):
    m, d = c.shape
    n = w_ada.shape[1]
    tn = 768
    return pl.pallas_call(
        _mod_kernel,
        grid=(n // tn,),
        in_specs=[pl.BlockSpec((m, d), lambda j: (0, 0)),
                  pl.BlockSpec((d, tn), lambda j: (0, j)),
                  pl.BlockSpec((1, tn), lambda j: (0, j))],
        out_specs=pl.BlockSpec((m, tn), lambda j: (0, j)),
        out_shape=jax.ShapeDtypeStruct((m, n), F32),
        compiler_params=_cparams(("arbitrary",)),
        name="modulation",
    )(c, w_ada, b_ada.reshape(1, n))


def _inproj_kernel(x_ref, shift_ref, scale_ref, g_ref, w_ref, o_ref, h_ref, *, rows):
    @pl.when(pl.program_id(1) == 0)
    def _():
        def body(r, c):
            sl = pl.ds(pl.multiple_of(r * rows, rows), rows)
            x = x_ref[sl, :]
            ms = jnp.mean(x * x, axis=-1, keepdims=True)
            y = x * lax.rsqrt(ms + NORM_EPS) * g_ref[...]
            if shift_ref.shape[0] == 1:
                h = y * (1.0 + scale_ref[...]) + shift_ref[...]
            else:
                h = y * (1.0 + scale_ref[sl, :]) + shift_ref[sl, :]
            h_ref[sl, :] = h.astype(BF16)
            return c
        lax.fori_loop(0, x_ref.shape[0] // rows, body, 0)

    o_ref[...] = jnp.dot(h_ref[...], w_ref[...], preferred_element_type=F32)


def _in_projection(x, shift, scale, norm_g, w_bf16):
    t, d = x.shape
    n = w_bf16.shape[1]
    tm = min(1024, t)
    tn = 512
    rows = min(128, tm)
    per_row = shift.shape[0] != 1
    mod_spec = (pl.BlockSpec((tm, d), lambda i, j: (i, 0)) if per_row
                else pl.BlockSpec((1, d), lambda i, j: (0, 0)))
    return pl.pallas_call(
        functools.partial(_inproj_kernel, rows=rows),
        grid=(t // tm, n // tn),
        in_specs=[pl.BlockSpec((tm, d), lambda i, j: (i, 0)),
                  mod_spec, mod_spec,
                  pl.BlockSpec((1, d), lambda i, j: (0, 0)),
                  pl.BlockSpec((d, tn), lambda i, j: (0, j))],
        out_specs=pl.BlockSpec((tm, tn), lambda i, j: (i, j)),
        out_shape=jax.ShapeDtypeStruct((t, n), F32),
        scratch_shapes=[pltpu.VMEM((tm, d), BF16)],
        compiler_params=_cparams(("arbitrary", "arbitrary")),
        name="in_projection",
    )(x, shift, scale, norm_g.reshape(1, d), w_bf16)


def _rope128(x, cos, sin_signed):
    return x * cos + pltpu.roll(x, 64, axis=1) * sin_signed


def _rope64(x, cos, sin_signed, first_half):
    rot = jnp.where(first_half, pltpu.roll(x, 96, axis=1), pltpu.roll(x, 32, axis=1))
    return x * cos + rot * sin_signed


def _prep_kernel(q_ref, k_ref, v_ref, iq_ref, sm_ref, c128_ref, s128_ref, c64_ref, s64_ref,
                 qg_ref, kg_ref,
                 qo_ref, k32_ref, kb_ref, vo_ref, v32_ref, iqb_ref, ik32_ref, ikb_ref, w_ref, *, transposed):
    c128, s128 = c128_ref[...], s128_ref[...]
    c64, s64 = c64_ref[...], s64_ref[...]
    tm = q_ref.shape[0]
    lane = lax.broadcasted_iota(I32, (tm, LANES), 1)
    first_half = (lane % IDX_HD) < (IDX_HD // 2)
    q_scale = ATT_HD ** -0.5 * math.log2(math.e)
    for h in range(ATT_HEADS):
        hs = slice(h * ATT_HD, (h + 1) * ATT_HD)
        q = q_ref[:, hs]
        q = q * lax.rsqrt(jnp.mean(q * q, axis=-1, keepdims=True) + NORM_EPS) * qg_ref[...]
        q = _rope128(q, c128, s128) * q_scale
        k = k_ref[:, hs]
        k = k * lax.rsqrt(jnp.mean(k * k, axis=-1, keepdims=True) + NORM_EPS) * kg_ref[...]
        k = _rope128(k, c128, s128)
        k32_ref[:, h, :] = k
        kb_ref[:, hs] = k.astype(BF16)
        if transposed:
            qo_ref[hs, :] = q.T.astype(BF16)
            vo_ref[hs, :] = v_ref[:, hs].T.astype(BF16)
        else:
            qo_ref[:, hs] = q.astype(BF16)
            vo_ref[:, hs] = v_ref[:, hs].astype(BF16)
    for h in range(ATT_HEADS):
        v32_ref[:, h, :] = v_ref[:, h * ATT_HD:(h + 1) * ATT_HD]
    for j in range(IDX_W // LANES):
        js = slice(j * LANES, (j + 1) * LANES)
        iqb_ref[:, js] = _rope64(iq_ref[:, js], c64, s64, first_half).astype(BF16)
    sm = sm_ref[...]
    ik = _rope64(sm, c64, s64, first_half)[:, :IDX_HD]
    ik32_ref[...] = ik
    ikb_ref[...] = ik.astype(BF16)
    w_ref[...] = sm * ((IDX_HEADS ** -0.5) * (IDX_HD ** -0.5))


def _attention_prep(p, tables, q_norm_g, k_norm_g, transposed):
    t = p.shape[0]
    tm = min(256, t)
    c128, s128, c64, s64 = tables
    wide = lambda col: pl.BlockSpec((tm, 1024), lambda i, c=col // 1024: (i, c))
    tab = pl.BlockSpec((tm, LANES), lambda i: (i, 0))
    gain = pl.BlockSpec((1, ATT_HD), lambda i: (0, 0))
    row1024 = pl.BlockSpec((tm, 1024), lambda i: (i, 0))
    col1024 = pl.BlockSpec((1024, tm), lambda i: (0, i))
    row64 = pl.BlockSpec((tm, IDX_HD), lambda i: (i, 0))
    head3 = pl.BlockSpec((tm, ATT_HEADS, ATT_HD), lambda i: (i, 0, 0))
    if transposed:
        qv_spec, qv_shape = col1024, jax.ShapeDtypeStruct((ATT_W, t), BF16)
    else:
        qv_spec, qv_shape = row1024, jax.ShapeDtypeStruct((t, ATT_W), BF16)
    return pl.pallas_call(
        functools.partial(_prep_kernel, transposed=transposed),
        grid=(t // tm,),
        in_specs=[wide(COL_Q), wide(COL_K), wide(COL_V), wide(COL_IQ),
                  pl.BlockSpec((tm, LANES), lambda i: (i, COL_SMALL_A // LANES)),
                  tab, tab, tab, tab, gain, gain],
        out_specs=[qv_spec, head3, row1024, qv_spec, head3, row1024, row64, row64, tab],
        out_shape=[qv_shape,
                   jax.ShapeDtypeStruct((t, ATT_HEADS, ATT_HD), F32),
                   jax.ShapeDtypeStruct((t, ATT_W), BF16),
                   qv_shape,
                   jax.ShapeDtypeStruct((t, ATT_HEADS, ATT_HD), F32),
                   jax.ShapeDtypeStruct((t, IDX_W), BF16),
                   jax.ShapeDtypeStruct((t, IDX_HD), F32),
                   jax.ShapeDtypeStruct((t, IDX_HD), BF16),
                   jax.ShapeDtypeStruct((t, LANES), F32)],
        compiler_params=_cparams(("arbitrary",)),
        name="attention_prep",
    )(p, p, p, p, p, c128, s128, c64, s64, q_norm_g.reshape(1, ATT_HD), k_norm_g.reshape(1, ATT_HD))


def _rope_tables(pos):
    pos = pos.astype(F32)[:, None]

    def tab(dh):
        inv = ROPE_THETA ** (-jnp.arange(0, dh, 2, dtype=F32) / dh)
        ang = pos * inv[None, :]
        cos, sin = jnp.cos(ang), jnp.sin(ang)
        c = jnp.concatenate([cos, cos], axis=-1)
        s = jnp.concatenate([-sin, sin], axis=-1)
        rep = LANES // dh
        return jnp.tile(c, (1, rep)), jnp.tile(s, (1, rep))

    c128, s128 = tab(ATT_HD)
    c64, s64 = tab(IDX_HD)
    return c128, s128, c64, s64


def _indexer_kernel(iq_ref, wt_ref, ik_ref, bias_ref, iqs_ref, cut_ref, *, n_sel):
    i = pl.program_id(0)
    tq = iq_ref.shape[0]
    tk = bias_ref.shape[2]
    nk_total = bias_ref.shape[1]
    nkb = i + 1

    for h in range(IDX_HEADS):
        iqs_ref[h * tq:(h + 1) * tq, :] = iq_ref[:, h * IDX_HD:(h + 1) * IDX_HD]

    key_pos = lax.broadcasted_iota(I32, (tk, tq), 0)
    qry_pos = lax.broadcasted_iota(I32, (tk, tq), 1) + i * tq

    def score_body(kb, c):
        ikt = ik_ref[pl.ds(pl.multiple_of(kb * tk, tk), tk), :]
        d = _dg(ikt, iqs_ref[...], _NT)
        acc = jnp.zeros((tk, tq), F32)
        for h in range(IDX_HEADS):
            acc = acc + jnp.maximum(d[:, h * tq:(h + 1) * tq], 0.0) * wt_ref[h:h + 1, :]
        bias_ref[0, kb] = jnp.where(key_pos + kb * tk <= qry_pos, acc, -jnp.inf)
        return c

    lax.fori_loop(0, nkb, score_body, 0)

    def count_keys(pred):
        def one(kb, cnt):
            x = bias_ref[0, kb].reshape(tk // SUBLANES, SUBLANES, tq)
            pos = (key_pos + kb * tk).reshape(tk // SUBLANES, SUBLANES, tq)
            return cnt + jnp.sum(jnp.where(pred(x, pos), 1.0, 0.0), axis=0)

        def four(j, cnt):
            for u in range(4):
                cnt = one(4 * j + u, cnt)
            return cnt

        cnt = lax.fori_loop(0, nkb // 4, four, jnp.zeros((SUBLANES, tq), F32))
        cnt = lax.fori_loop(4 * (nkb // 4), nkb, one, cnt)
        return jnp.sum(cnt, axis=0, keepdims=True)

    def bit_body(bi, carry):
        ans, tot_ans = carry
        cand = ans + jnp.left_shift(jnp.int32(1), 31 - bi)
        cand_f = _ordered_to_f32(cand)
        tot = count_keys(lambda x, pos: x >= cand_f[None])
        ok = tot >= n_sel
        return jnp.where(ok, cand, ans), jnp.where(ok, tot, tot_ans)

    ans, tot_ans = lax.fori_loop(0, 32, bit_body, (jnp.full((SUBLANES, tq), INT_MIN, I32),
                                                   jnp.zeros((1, tq), F32)))
    thr = _ordered_to_f32(ans)[0:1, :]
    thr = jnp.where(thr != thr, -jnp.inf, thr)

    cut_ref[...] = jnp.full(cut_ref.shape, 2 ** 31 - 1, I32)

    @pl.when(jnp.max(tot_ans) > n_sel)
    def _():
        need = n_sel - count_keys(lambda x, pos: x > thr[None])
        n_bits = max(1, (nk_total * tk - 1).bit_length())

        def cut_body(bi, cut):
            cand = cut + jnp.left_shift(jnp.int32(1), n_bits - 1 - bi)
            below = count_keys(lambda x, pos: (x == thr[None]) & (pos < cand[None]))
            return jnp.where(below < need, cand, cut)

        cut = lax.fori_loop(0, n_bits, cut_body, jnp.zeros((1, tq), I32))
        cut_ref[...] = jnp.broadcast_to(jnp.where(tot_ans > n_sel, cut, 2 ** 31 - 1), cut_ref.shape)

    has_surplus_ties = jnp.max(tot_ans) > n_sel
    cut = cut_ref[0:1, :]

    def bias_body(kb, c):
        keep = (bias_ref[0, kb] >= thr) & (key_pos + kb * tk <= qry_pos)
        bias_ref[0, kb] = jnp.where(keep, 0.0, NEG_BIG)
        return c

    def bias_body_ties(kb, c):
        x = bias_ref[0, kb]
        pos = key_pos + kb * tk
        keep = ((x > thr) | ((x == thr) & (pos <= cut))) & (pos <= qry_pos)
        bias_ref[0, kb] = jnp.where(keep, 0.0, NEG_BIG)
        return c

    @pl.when(jnp.logical_not(has_surplus_ties))
    def _():
        lax.fori_loop(0, nkb, bias_body, 0)

    @pl.when(has_surplus_ties)
    def _():
        lax.fori_loop(0, nkb, bias_body_ties, 0)

    def fill_body(kb, c):
        bias_ref[0, kb] = jnp.full((tk, tq), NEG_BIG, F32)
        return c

    lax.fori_loop(nkb, nk_total, fill_body, 0)


def _prompt_indexer(iqb, wt, ikb, n_sel, tq):
    t = iqb.shape[0]
    nq = t // tq
    return pl.pallas_call(
        functools.partial(_indexer_kernel, n_sel=n_sel),
        grid=(nq,),
        in_specs=[pl.BlockSpec((tq, IDX_W), lambda i: (i, 0)),
                  pl.BlockSpec((IDX_HEADS, tq), lambda i: (0, i)),
                  pl.BlockSpec((t, IDX_HD), lambda i: (0, 0))],
        out_specs=pl.BlockSpec((1, nq, tq, tq), lambda i: (i, 0, 0, 0)),
        out_shape=jax.ShapeDtypeStruct((nq, nq, tq, tq), F32),
        scratch_shapes=[pltpu.VMEM((IDX_HEADS * tq, IDX_HD), BF16),
                        pltpu.VMEM((SUBLANES, tq), I32)],
        compiler_params=_cparams(("arbitrary",)),
        name="prompt_indexer",
    )(iqb, wt, ikb)


def _flash_kernel(qi_ref, ki_ref, qt_ref, k_ref, vt_ref, bias_ref, g_ref, o_ref, m_ref, l_ref, acc_ref):
    step = pl.program_id(0)
    qi, ki = qi_ref[step], ki_ref[step]

    @pl.when(ki == 0)
    def _():
        m_ref[...] = jnp.full(m_ref.shape, NEG_BIG, F32)
        l_ref[...] = jnp.zeros(l_ref.shape, F32)
        acc_ref[...] = jnp.zeros(acc_ref.shape, F32)

    heads = [slice(h * ATT_HD, (h + 1) * ATT_HD) for h in range(ATT_HEADS)]
    tq = qt_ref.shape[1]
    for qh in range(tq // LANES):
        qs = slice(qh * LANES, (qh + 1) * LANES)
        bias = bias_ref[0, 0, :, qs]
        m_prev, l_prev = m_ref[:, qs], l_ref[:, qs]
        m_rows, l_rows = [], []
        for h, hs in enumerate(heads):
            s = _dg(k_ref[:, hs], qt_ref[hs, qs], _NN) + bias
            m_old = m_prev[h:h + 1, :]
            m_new = jnp.maximum(m_old, jnp.max(s, axis=0, keepdims=True))
            alpha = jnp.exp2(m_old - m_new)
            p = jnp.exp2(s - m_new)
            l_rows.append(alpha * l_prev[h:h + 1, :] + jnp.sum(p, axis=0, keepdims=True))
            m_rows.append(m_new)
            acc_ref[h, :, qs] = alpha * acc_ref[h, :, qs] + _dg(vt_ref[hs, :], p.astype(BF16), _NN)
        m_ref[:, qs] = jnp.concatenate(m_rows, axis=0)
        l_ref[:, qs] = jnp.concatenate(l_rows, axis=0)

    @pl.when(ki == qi)
    def _():
        for h in range(ATT_HEADS):
            hs = slice(h * ATT_HD, (h + 1) * ATT_HD)
            out = (acc_ref[h] / l_ref[h:h + 1, :]).T
            o_ref[:, hs] = (out * _silu(g_ref[:, hs])).astype(BF16)


def _prompt_attention(qt, kb, vt, bias, p, tq):
    t = kb.shape[0]
    nq = t // tq
    pairs = [(i, j) for i in range(nq) for j in range(i + 1)]
    qi = jnp.asarray([a for a, _ in pairs], I32)
    ki = jnp.asarray([b for _, b in pairs], I32)
    grid_spec = pltpu.PrefetchScalarGridSpec(
        num_scalar_prefetch=2,
        grid=(len(pairs),),
        in_specs=[pl.BlockSpec((ATT_W, tq), lambda s, qi, ki: (0, qi[s])),
                  pl.BlockSpec((tq, ATT_W), lambda s, qi, ki: (ki[s], 0)),
                  pl.BlockSpec((ATT_W, tq), lambda s, qi, ki: (0, ki[s])),
                  pl.BlockSpec((1, 1, tq, tq), lambda s, qi, ki: (qi[s], ki[s], 0, 0)),
                  pl.BlockSpec((tq, ATT_W), lambda s, qi, ki: (qi[s], COL_G // ATT_W))],
        out_specs=pl.BlockSpec((tq, ATT_W), lambda s, qi, ki: (qi[s], 0)),
        scratch_shapes=[pltpu.VMEM((ATT_HEADS, tq), F32),
                        pltpu.VMEM((ATT_HEADS, tq), F32),
                        pltpu.VMEM((ATT_HEADS, ATT_HD, tq), F32)],
    )
    return pl.pallas_call(
        _flash_kernel,
        grid_spec=grid_spec,
        out_shape=jax.ShapeDtypeStruct((t, ATT_W), BF16),
        compiler_params=_cparams(("arbitrary",)),
        name="prompt_attention",
    )(qi, ki, qt, kb, vt, bias, p)


def _sample_attn_kernel(pt_ref, q_ref, iq_ref, w_ref, ikn_ref, kn_ref, vn_ref, g_ref, ex_ref,
                        cik_ref, ck_ref, cv_ref, o_ref,
                        ikbuf, kbuf, vbuf, sce_ref, isem, ksem, vsem, *, n_sel, pps):
    b = pl.program_id(0)
    n_pages = ikbuf.shape[0]
    page_rows = PAGE_SIZE * ATT_HEADS

    def idx_copy(j):
        return pltpu.make_async_copy(cik_ref.at[pt_ref[b, j]], ikbuf.at[j], isem.at[j])

    def k_copy(j, slot):
        return pltpu.make_async_copy(ck_ref.at[pt_ref[b, j]], kbuf.at[slot], ksem.at[slot])

    def v_copy(j, slot):
        return pltpu.make_async_copy(cv_ref.at[pt_ref[b, j]], vbuf.at[slot], vsem.at[slot])

    def start_idx(j, c):
        idx_copy(j).start()
        return c

    lax.fori_loop(0, n_pages, start_idx, 0)

    def start_group(t, grp):
        for i in range(pps):
            k_copy(t * pps + i, grp * pps + i).start()
            v_copy(t * pps + i, grp * pps + i).start()

    start_group(0, 0)

    iq = iq_ref[0]
    w = w_ref[0]

    def wait_idx(j, c):
        idx_copy(j).wait()
        return c

    lax.fori_loop(0, n_pages, wait_idx, 0)
    iq_pages = jnp.broadcast_to(iq[None], (n_pages, IDX_HEADS, IDX_HD))
    d = lax.dot_general(iq_pages, ikbuf[...].astype(BF16), (((2,), (1,)), ((0,), (0,))),
                        preferred_element_type=F32)
    sc = jnp.sum(jnp.maximum(d, 0.0) * w[None], axis=1)
    d_self = jnp.sum(iq.astype(F32) * ikn_ref[0].astype(F32), axis=-1, keepdims=True)
    s_self = jnp.sum(jnp.maximum(d_self, 0.0) * w, axis=0, keepdims=True)

    pos = (lax.broadcasted_iota(I32, sc.shape, 0) * PAGE_SIZE + lax.broadcasted_iota(I32, sc.shape, 1))
    self_pos = n_pages * PAGE_SIZE

    def count_keys(pred):
        cnt = jnp.sum(jnp.where(pred(sc, pos), 1.0, 0.0), axis=-1, keepdims=True)
        return jnp.sum(cnt, axis=0, keepdims=True) + jnp.where(pred(s_self, self_pos), 1.0, 0.0)

    def digit_body(it, carry):
        ans, tot_ans = carry
        step = jnp.left_shift(jnp.int32(1), 30 - 2 * it)
        best, best_tot = ans, tot_ans
        cand = ans
        for _ in range(3):
            cand = cand + step
            cand_f = _ordered_to_f32(cand)
            tot = count_keys(lambda x, p: x >= cand_f)
            best = jnp.where(tot >= n_sel, cand, best)
            best_tot = jnp.where(tot >= n_sel, tot, best_tot)
        return best, best_tot

    ans, tot_ans = lax.fori_loop(0, 16, digit_body, (jnp.full((1, 1), INT_MIN, I32), jnp.zeros((1, 1), F32)))
    thr = _ordered_to_f32(ans)
    thr = jnp.where(thr != thr, -jnp.inf, thr)

    def tie_cut():
        need = n_sel - count_keys(lambda x, p: x > thr)
        n_bits = self_pos.bit_length()

        def cut_body(bi, cut):
            cand = cut + jnp.left_shift(jnp.int32(1), n_bits - 1 - bi)
            below = count_keys(lambda x, p: (x == thr) & (p < cand))
            return jnp.where(below < need, cand, cut)

        return lax.fori_loop(0, n_bits, cut_body, jnp.zeros((1, 1), I32))

    cut = lax.cond(tot_ans[0, 0] > n_sel, tie_cut, lambda: jnp.full((1, 1), 2 ** 31 - 1, I32))

    def selected(x, p):
        return (x > thr) | ((x == thr) & (p <= cut))

    sce_ref[...] = _dot_exact_rhs(sc, ex_ref[...])

    q8 = q_ref[0]
    lane_e = lax.broadcasted_iota(I32, (ATT_HEADS, page_rows), 1)
    own = lane_e % ATT_HEADS == lax.broadcasted_iota(I32, (ATT_HEADS, page_rows), 0)
    tok_lane = lane_e[0:1, :] // ATT_HEADS

    kn = kn_ref[0].astype(BF16).astype(F32)
    s0 = jnp.sum(q8.astype(F32) * kn, axis=-1, keepdims=True)
    m0 = jnp.where(selected(s_self, self_pos), s0, NEG_BIG)
    l0 = jnp.where(m0 > 0.5 * NEG_BIG, 1.0, 0.0)
    acc0 = l0 * vn_ref[0].astype(BF16).astype(F32)

    def att_body(t, carry):
        m_prev, l_prev, acc = carry
        grp = t % 2

        @pl.when(t + 1 < n_pages // pps)
        def _():
            start_group(t + 1, 1 - grp)

        for i in range(pps):
            k_copy(t * pps + i, grp * pps + i).wait()
            v_copy(t * pps + i, grp * pps + i).wait()
        s = []
        for i in range(pps):
            kf = kbuf[grp * pps + i].reshape(page_rows, ATT_HD).astype(BF16)
            keep = own & selected(sce_ref[pl.ds(t * pps + i, 1), :], (t * pps + i) * PAGE_SIZE + tok_lane)
            s.append(jnp.where(keep, _dg(q8, kf, _NT), NEG_BIG))
        m_new = m_prev
        for x in s:
            m_new = jnp.maximum(m_new, jnp.max(x, axis=-1, keepdims=True))
        alpha = jnp.exp2(m_prev - m_new)
        p = [jnp.exp2(x - m_new) for x in s]
        l_new = alpha * l_prev
        for x in p:
            l_new = l_new + jnp.sum(x, axis=-1, keepdims=True)
        pv = [jnp.dot(p[i].astype(BF16), vbuf[grp * pps + i].reshape(page_rows, ATT_HD).astype(BF16),
                      preferred_element_type=F32) for i in range(pps)]
        acc = alpha * acc
        for x in pv:
            acc = acc + x
        return m_new, l_new, acc

    m, l, acc = lax.fori_loop(0, n_pages // pps, att_body, (m0, l0, acc0))
    o_ref[0] = acc / l * _silu(g_ref[0])


def _sample_attention(page_table, qb, iqb, wv, ikb, k32, v32, g, cache_idx_k, cache_k, cache_v, n_sel):
    nb, n_pages = page_table.shape
    head3 = pl.BlockSpec((1, ATT_HEADS, ATT_HD), lambda b, pt: (b, 0, 0))
    any_spec = pl.BlockSpec(memory_space=pl.ANY)
    page_rows = PAGE_SIZE * ATT_HEADS
    pps = math.gcd(DECODE_PAGES_PER_STEP, n_pages)
    expand = (jnp.arange(page_rows)[None, :] // ATT_HEADS == jnp.arange(PAGE_SIZE)[:, None]).astype(BF16)
    grid_spec = pltpu.PrefetchScalarGridSpec(
        num_scalar_prefetch=1,
        grid=(nb,),
        in_specs=[head3,
                  pl.BlockSpec((1, IDX_HEADS, IDX_HD), lambda b, pt: (b, 0, 0)),
                  pl.BlockSpec((1, IDX_HEADS, 1), lambda b, pt: (b, 0, 0)),
                  pl.BlockSpec((1, 1, IDX_HD), lambda b, pt: (b, 0, 0)),
                  head3, head3, head3,
                  pl.BlockSpec((PAGE_SIZE, page_rows), lambda b, pt: (0, 0)),
                  any_spec, any_spec, any_spec],
        out_specs=head3,
        scratch_shapes=[pltpu.VMEM((n_pages, IDX_HD, PAGE_SIZE), F32),
                        pltpu.VMEM((2 * pps, PAGE_SIZE, ATT_HEADS, ATT_HD), F32),
                        pltpu.VMEM((2 * pps, PAGE_SIZE, ATT_HEADS, ATT_HD), F32),
                        pltpu.VMEM((n_pages, page_rows), F32),
                        pltpu.SemaphoreType.DMA((n_pages,)),
                        pltpu.SemaphoreType.DMA((2 * pps,)),
                        pltpu.SemaphoreType.DMA((2 * pps,))],
    )
    head = lambda a: a.reshape(nb, ATT_HEADS, ATT_HD)
    out = pl.pallas_call(
        functools.partial(_sample_attn_kernel, n_sel=n_sel, pps=pps),
        grid_spec=grid_spec,
        out_shape=jax.ShapeDtypeStruct((nb, ATT_HEADS, ATT_HD), F32),
        compiler_params=_cparams(("arbitrary",)),
        name="sample_attention",
    )(page_table, head(qb),
      iqb.reshape(nb, IDX_HEADS, IDX_HD),
      wv[:, IDX_HD:IDX_HD + IDX_HEADS].reshape(nb, IDX_HEADS, 1),
      ikb.reshape(nb, 1, IDX_HD),
      head(k32), head(v32), head(g), expand,
      jnp.swapaxes(cache_idx_k, 1, 2), cache_k, cache_v)
    return out.reshape(nb, ATT_W).astype(BF16)


def _rwkv_prep_kernel(*refs, shifted):
    n_x = 5
    x_refs = refs[:n_x]
    if shifted:
        halo_refs, first_refs = refs[n_x:2 * n_x], refs[2 * n_x:3 * n_x]
        rest = refs[3 * n_x:]
    else:
        prev_refs = refs[n_x:2 * n_x]
        rest = refs[2 * n_x:]
    (mu_ref, mus_ref, wl_ref, w0_ref, a0_ref, kk_ref, ka_ref, rk_ref, ones_ref,
     ro_ref, kpo_ref, vo_ref, kko_ref, bo_ref, lwo_ref, bonus_ref, sg_ref) = rest

    def lerp(j, mu):
        x = x_refs[j][...]
        if shifted:
            above = jnp.where(pl.program_id(0) == 0, first_refs[j][...],
                              halo_refs[j][SUBLANES - 1:SUBLANES, :])
            row = lax.broadcasted_iota(I32, x.shape, 0)
            prev = jnp.where(row == 0, above, pltpu.roll(x, 1, axis=0))
        else:
            prev = prev_refs[j][...]
        return x + mu * (prev - x)

    w = RWKV_W
    xr = lerp(0, mu_ref[:, 0:w])
    xk = lerp(1, mu_ref[:, w:2 * w])
    xv = lerp(2, mu_ref[:, 2 * w:3 * w])
    xg = lerp(3, mu_ref[:, 3 * w:4 * w])
    xs = lerp(4, mus_ref[...])
    lane = lax.broadcasted_iota(I32, xs.shape, 1)
    lora_in = jnp.where(lane < DECAY_LORA, jnp.tanh(xs), xs)
    lora = _dot3(lora_in, wl_ref[...])
    z = w0_ref[...] + lora[:, :w]
    softplus_neg = jnp.maximum(-z, 0.0) + jnp.log(1.0 + jnp.exp(-jnp.abs(z)))
    w_log = -softplus_neg - 0.5
    lwo_ref[...] = -jnp.exp(w_log)
    a = jax.nn.sigmoid(a0_ref[...] + lora[:, w:])
    ones_bd = ones_ref[...]
    kk = xk * kk_ref[...]
    kk = kk / jnp.maximum(jnp.sqrt(_head_sum(kk * kk, ones_bd)), 1e-12)
    kp = xk * (1.0 + (a - 1.0) * ka_ref[...])
    ro_ref[...] = xr
    kpo_ref[...] = kp
    vo_ref[...] = xv
    kko_ref[...] = kk
    bo_ref[...] = kk * a
    bonus_ref[...] = _head_sum(xr * kp * rk_ref[...], ones_bd) * xv
    sg_ref[...] = _silu(xg)


def _rwkv_prep(p, prev, consts, shifted):
    t = p.shape[0]
    tm = min(256, t)
    mu_big, mu_small, w_lora, w0, a0, k_k, k_a, r_k, ones_bd = consts
    cols = [(RWKV_W, COL_RR // RWKV_W), (RWKV_W, COL_RK // RWKV_W), (RWKV_W, COL_RV // RWKV_W),
            (RWKV_W, COL_RG // RWKV_W), (LANES, COL_SMALL_R // LANES)]
    x_specs = [pl.BlockSpec((tm, wd), lambda i, c=c: (i, c)) for wd, c in cols]
    if shifted:
        per8 = tm // SUBLANES
        halo = [pl.BlockSpec((SUBLANES, wd), lambda i, c=c: (jnp.maximum(i * per8 - 1, 0), c))
                for wd, c in cols]
        first = [pl.BlockSpec((1, wd), lambda i, c=c: (0, c)) for wd, c in cols]
        extra_specs, extra_args = halo + first, [p] * 5 + [prev] * 5
    else:
        extra_specs, extra_args = x_specs, [prev] * 5
    full = lambda a: pl.BlockSpec(a.shape, lambda i: (0, 0))
    row = pl.BlockSpec((tm, RWKV_W), lambda i: (i, 0))
    vec = lambda a: a.reshape(1, -1)
    args = [mu_big, mu_small, w_lora, vec(w0), vec(a0), vec(k_k), vec(k_a), vec(r_k), ones_bd]
    return pl.pallas_call(
        functools.partial(_rwkv_prep_kernel, shifted=shifted),
        grid=(t // tm,),
        in_specs=x_specs + extra_specs + [full(a) for a in args],
        out_specs=[row] * 8,
        out_shape=[jax.ShapeDtypeStruct((t, RWKV_W), F32)] * 8,
        compiler_params=_cparams(("arbitrary",)),
        name="rwkv_prep",
    )(*([p] * 5), *extra_args, *args)


def _scan_kernel(r_ref, kp_ref, v_ref, kk_ref, b_ref, lw_ref, s0_ref, y_ref, st_ref, s_ref):
    @pl.when(pl.program_id(0) == 0)
    def _():
        s_ref[...] = s0_ref[...]

    ch = SCAN_CHUNK
    assert ch == RWKV_HD
    rows_total = r_ref.shape[0]
    n_sub = rows_total // ch
    gw = MXU_DIM
    nh = HEADS_PER_GROUP
    lw = lw_ref[...]
    tr = lax.broadcasted_iota(I32, (rows_total, rows_total), 0)
    tc = lax.broadcasted_iota(I32, (rows_total, rows_total), 1)
    tri = jnp.where((tr >= tc) & (tr // ch == tc // ch), 1.0, 0.0).astype(BF16)
    cum = _dot_exact_lhs(tri, lw)
    cum_end = jnp.concatenate(
        [jnp.broadcast_to(cum[(c + 1) * ch - 1:(c + 1) * ch, :], (ch, RWKV_W)) for c in range(n_sub)], axis=0)
    kk, b, kp, r, v = kk_ref[...], b_ref[...], kp_ref[...], r_ref[...], v_ref[...]
    g_inv = jnp.exp(-cum)
    g_end = jnp.exp(cum_end - cum)
    a_t = (-kk * jnp.exp(cum - lw)).astype(BF16)
    b_t = b * g_inv
    k_t = kp * g_inv
    r_t = (r * jnp.exp(cum)).astype(BF16)
    b_h = (b * g_end).astype(BF16)
    k_h = (kp * g_end).astype(BF16)
    vb = v.astype(BF16)

    srow = lax.broadcasted_iota(I32, (nh * ch, gw), 0)
    scol = lax.broadcasted_iota(I32, (nh * ch, gw), 1)
    own = (srow // ch) == (scol // RWKV_HD)
    step = lax.broadcasted_iota(I32, (ch, gw), 0)
    other = lax.broadcasted_iota(I32, (ch, gw), 1) % ch
    strict, incl = step > other, step >= other
    eye = jnp.where(step == other, 1.0, 0.0)

    def stack(x):
        return jnp.where(own, jnp.concatenate([x] * nh, axis=0), 0.0).astype(BF16)

    groups = range(N_GROUPS)
    units = [(c, g) for c in range(n_sub) for g in groups]
    blk = {(c, g): (slice(c * ch, (c + 1) * ch), slice(g * gw, (g + 1) * gw)) for c, g in units}
    rb = {u: stack(b_t[blk[u]]) for u in units}
    rk = {u: stack(k_t[blk[u]]) for u in units}
    v4 = {u: stack(v[blk[u]]) for u in units}
    n_row = {u: jnp.where(strict, _dg(a_t[blk[u]], rb[u], _NT), 0.0) for u in units}
    bm_row = {u: jnp.where(strict, _dg(a_t[blk[u]], rk[u], _NT), 0.0).astype(BF16) for u in units}
    rb_row = {u: jnp.where(incl, _dg(r_t[blk[u]], rb[u], _NT), 0.0).astype(BF16) for u in units}
    rk_row = {u: jnp.where(incl, _dg(r_t[blk[u]], rk[u], _NT), 0.0).astype(BF16) for u in units}
    minv = {u: eye + n_row[u] for u in units}
    pw = n_row
    pw_bd = {u: stack(pw[u]) for u in units}
    for _ in range(int(math.log2(ch)) - 1):
        pw = {u: _dg(pw[u].astype(BF16), pw_bd[u], _NN) for u in units}
        pw_bd = {u: stack(pw[u]) for u in units}
        minv = {u: minv[u] + _dg(minv[u].astype(BF16), pw_bd[u], _NN) for u in units}
    minv = {u: minv[u].astype(BF16) for u in units}
    y_fixed = {u: _dg(rk_row[u], v4[u], _NN) for u in units}
    x_fixed = {u: _dg(bm_row[u], v4[u], _NN) for u in units}

    s = [s_ref[g] for g in groups]
    for c in range(n_sub):
        sb = [x.astype(BF16) for x in s]
        x_row = [_dg(a_t[blk[c, g]], sb[g], _NT) + x_fixed[c, g] for g in groups]
        x_bd = [stack(x) for x in x_row]
        u_row = [_dg(minv[c, g], x_bd[g], _NN) for g in groups]
        u_bd = [stack(x) for x in u_row]
        for g in groups:
            y_ref[blk[c, g]] = _dg(r_t[blk[c, g]], sb[g], _NT) + _dg(rb_row[c, g], u_bd[g], _NN) + y_fixed[c, g]
        for g in groups:
            rs, ls = blk[c, g]
            upd = _dg(jnp.concatenate([u_row[g].astype(BF16), vb[rs, ls]], axis=0),
                      jnp.concatenate([b_h[rs, ls], k_h[rs, ls]], axis=0), _TN)
            g_c = jnp.exp(cum[(c + 1) * ch - 1:(c + 1) * ch, ls])
            s[g] = s[g] * g_c + jnp.where(own, upd, 0.0)
    for g in groups:
        s_ref[g] = s[g]

    @pl.when(pl.program_id(0) == pl.num_programs(0) - 1)
    def _():
        st_ref[...] = s_ref[...]


def _rwkv_scan(r, kp, v, kk, b, lw, s0):
    t = r.shape[0]
    ch = SCAN_CHUNK * min(SCAN_SUB, t // SCAN_CHUNK)
    row = pl.BlockSpec((ch, RWKV_W), lambda i: (i, 0))
    st = pl.BlockSpec((N_GROUPS, MXU_DIM, MXU_DIM), lambda i: (0, 0, 0))
    return pl.pallas_call(
        _scan_kernel,
        grid=(t // ch,),
        in_specs=[row] * 6 + [st],
        out_specs=[row, st],
        out_shape=[jax.ShapeDtypeStruct((t, RWKV_W), F32),
                   jax.ShapeDtypeStruct((N_GROUPS, MXU_DIM, MXU_DIM), F32)],
        scratch_shapes=[pltpu.VMEM((N_GROUPS, MXU_DIM, MXU_DIM), F32)],
        compiler_params=_cparams(("arbitrary",)),
        name="rwkv_scan",
    )(r, kp, v, kk, b, lw, s0)


def _rwkv_step_kernel(s_ref, r_ref, kp_ref, kk_ref, b_ref, lw_ref, v_ref, so_ref, y_ref):
    s = s_ref[0]
    kk = kk_ref[0]
    sa = -jnp.sum(s * kk, axis=-1, keepdims=True)
    s_new = s * jnp.exp(lw_ref[0]) + sa * b_ref[0] + v_ref[0] * kp_ref[0]
    so_ref[0] = s_new
    y_ref[0] = jnp.sum(s_new * r_ref[0], axis=-1, keepdims=True)


def _rwkv_step(state, r, kp, kk, b, lw, v):
    nb = state.shape[0]
    rowv = lambda a: a.reshape(nb, RWKV_HEADS, 1, RWKV_HD)
    rspec = pl.BlockSpec((1, RWKV_HEADS, 1, RWKV_HD), lambda i: (i, 0, 0, 0))
    cspec = pl.BlockSpec((1, RWKV_HEADS, RWKV_HD, 1), lambda i: (i, 0, 0, 0))
    sspec = pl.BlockSpec((1, RWKV_HEADS, RWKV_HD, RWKV_HD), lambda i: (i, 0, 0, 0))
    s_new, y = pl.pallas_call(
        _rwkv_step_kernel,
        grid=(nb,),
        in_specs=[sspec, rspec, rspec, rspec, rspec, rspec, cspec],
        out_specs=[sspec, cspec],
        out_shape=[jax.ShapeDtypeStruct(state.shape, F32),
                   jax.ShapeDtypeStruct((nb, RWKV_HEADS, RWKV_HD, 1), F32)],
        compiler_params=_cparams(("arbitrary",)),
        name="rwkv_step",
    )(state, rowv(r), rowv(kp), rowv(kk), rowv(b), rowv(lw), v.reshape(nb, RWKV_HEADS, RWKV_HD, 1))
    return s_new, y.reshape(nb, RWKV_W)


def _rwkv_post_kernel(y_ref, bonus_ref, sg_ref, lg_ref, lb_ref, ones_ref, o_ref):
    y = y_ref[...]
    ones_bd = ones_ref[...]
    mean = _head_sum(y, ones_bd) * (1.0 / RWKV_HD)
    yc = y - mean
    var = _head_sum(yc * yc, ones_bd) * (1.0 / RWKV_HD)
    yn = yc * lax.rsqrt(var + GN_EPS) * lg_ref[...] + lb_ref[...]
    o_ref[...] = ((yn + bonus_ref[...]) * sg_ref[...]).astype(BF16)


def _rwkv_post(y, bonus, sg, ln_g, ln_b, ones_bd):
    t = y.shape[0]
    tm = min(512, t)
    row = pl.BlockSpec((tm, RWKV_W), lambda i: (i, 0))
    vec = pl.BlockSpec((1, RWKV_W), lambda i: (0, 0))
    return pl.pallas_call(
        _rwkv_post_kernel,
        grid=(t // tm,),
        in_specs=[row, row, row, vec, vec, pl.BlockSpec(ones_bd.shape, lambda i: (0, 0))],
        out_specs=row,
        out_shape=jax.ShapeDtypeStruct((t, RWKV_W), BF16),
        compiler_params=_cparams(("arbitrary",)),
        name="rwkv_post",
    )(y, bonus, sg, ln_g.reshape(1, -1), ln_b.reshape(1, -1), ones_bd)


def _outproj_kernel(att_ref, rw_ref, wa_ref, wr_ref, x_ref, gate_ref, o_ref):
    out = (jnp.dot(att_ref[...], wa_ref[...], preferred_element_type=F32)
           + jnp.dot(rw_ref[...], wr_ref[...], preferred_element_type=F32))
    o_ref[...] = x_ref[...] + gate_ref[...] * out


def _out_projection(att, rw, w_out_bf16, x, gate):
    t, d = x.shape
    tm = min(512, t)
    per_row = gate.shape[0] != 1
    gspec = (pl.BlockSpec((tm, d), lambda i: (i, 0)) if per_row else pl.BlockSpec((1, d), lambda i: (0, 0)))
    return pl.pallas_call(
        _outproj_kernel,
        grid=(t // tm,),
        in_specs=[pl.BlockSpec((tm, ATT_W), lambda i: (i, 0)),
                  pl.BlockSpec((tm, RWKV_W), lambda i: (i, 0)),
                  pl.BlockSpec((ATT_W, d), lambda i: (0, 0)),
                  pl.BlockSpec((RWKV_W, d), lambda i: (1, 0)),
                  pl.BlockSpec((tm, d), lambda i: (i, 0)),
                  gspec],
        out_specs=pl.BlockSpec((tm, d), lambda i: (i, 0)),
        out_shape=jax.ShapeDtypeStruct((t, d), F32),
        compiler_params=_cparams(("arbitrary",)),
        name="out_projection",
    )(att, rw, w_out_bf16, w_out_bf16, x, gate)


def _reorder_cols(w, axis=-1):
    axis = axis % w.ndim

    def cut(start, size):
        return lax.slice_in_dim(w, start, start + size, axis=axis)

    def zeros(size):
        shape = list(w.shape)
        shape[axis] = size
        return jnp.zeros(shape, w.dtype)

    q, k, v, g = (cut(i * ATT_W, ATT_W) for i in range(4))
    a = 4 * ATT_W
    iq, iw, ik = cut(a, IDX_W), cut(a + IDX_W, IDX_HEADS), cut(a + IDX_W + IDX_HEADS, IDX_HD)
    o = N_ATT_COLS_ORIG
    rr, rk, rv = (cut(o + i * RWKV_W, RWKV_W) for i in range(3))
    o2 = o + 3 * RWKV_W
    wl, al = cut(o2, DECAY_LORA), cut(o2 + DECAY_LORA, ICL_LORA)
    rg = cut(o2 + DECAY_LORA + ICL_LORA, RWKV_W)
    return jnp.concatenate([q, k, v, g, iq, rr, rk, rv, rg, ik, iw, zeros(LANES - IDX_HD - IDX_HEADS),
                            wl, al, zeros(N_COLS_PAD - N_COLS)], axis=axis)


def _reorder_cast_kernel(wt_ref, o_ref):
    o_ref[...] = _reorder_cols(wt_ref[...], axis=0).T.astype(BF16)


def _reorder_cast(wt):
    n, d = wt.shape
    td = min(256, d)
    return pl.pallas_call(
        _reorder_cast_kernel,
        grid=(d // td,),
        in_specs=[pl.BlockSpec((n, td), lambda i: (0, i))],
        out_specs=pl.BlockSpec((td, N_COLS_PAD), lambda i: (i, 0)),
        out_shape=jax.ShapeDtypeStruct((d, N_COLS_PAD), BF16),
        compiler_params=_cparams(("arbitrary",)),
        name="reorder_cast",
    )(wt)


def _rwkv_cols_original_order(p):
    return jnp.concatenate([p[..., COL_RR:COL_RR + RWKV_W], p[..., COL_RK:COL_RK + RWKV_W],
                            p[..., COL_RV:COL_RV + RWKV_W],
                            p[..., COL_SMALL_R:COL_SMALL_R + DECAY_LORA + ICL_LORA],
                            p[..., COL_RG:COL_RG + RWKV_W]], axis=-1)


def _blockdiag_to_state(sb):
    sb = sb.reshape(N_GROUPS, HEADS_PER_GROUP, RWKV_HD, HEADS_PER_GROUP, RWKV_HD)
    return jnp.stack([sb[:, h, :, h, :] for h in range(HEADS_PER_GROUP)], axis=1).reshape(
        RWKV_HEADS, RWKV_HD, RWKV_HD)


def kernel(x_prompt, x_sample, c_prompt, c_sample, cache_k, cache_v, cache_idx_k, state_wkv, state_shift,
           page_table, w_ada, b_ada, norm_g, w_in, q_norm_g, k_norm_g, shift_mu, w0, w_b, a0, a_b,
           k_k, k_a, r_k, ln_x_g, ln_x_b, w_out):
    depth = w_ada.shape[0]
    assert depth == 1, "single-layer trunk"
    bsz, seq, d = x_prompt.shape
    assert bsz == 1
    nb, dec_seq, _ = x_sample.shape
    assert dec_seq == 1
    n_pages = page_table.shape[1]
    past_len = n_pages * PAGE_SIZE
    l = 0

    w_in_r = _reorder_cast(w_in[l].T)
    w_out_b = w_out[l].astype(BF16)
    mu = _reorder_cols(jnp.concatenate([jnp.zeros((N_ATT_COLS_ORIG,), F32), shift_mu[l]]))
    mu_big = mu[COL_RR:COL_RR + 4 * RWKV_W].reshape(1, -1)
    mu_small = mu[COL_SMALL_R:COL_SMALL_R + LANES].reshape(1, -1)
    zl = jnp.zeros((DECAY_LORA, RWKV_W), F32)
    w_lora = jnp.concatenate([jnp.concatenate([w_b[l], zl], axis=1),
                              jnp.concatenate([zl, a_b[l]], axis=1)], axis=0)
    hid = jnp.arange(LANES) // RWKV_HD
    ones_bd = (hid[:, None] == hid[None, :]).astype(BF16)
    rwkv_consts = (mu_big, mu_small, w_lora, w0[l], a0[l], k_k[l], k_a[l], r_k[l].reshape(-1), ones_bd)

    n_mod = 1 + nb
    c_all = jnp.concatenate([c_prompt, c_sample, jnp.zeros((-n_mod % SUBLANES, d), F32)], axis=0)
    mod = _modulation(c_all, w_ada[l], b_ada[l])
    shift_p, scale_p, gate_p = mod[0:1, :d], mod[0:1, d:2 * d], mod[0:1, 2 * d:]
    shift_s, scale_s, gate_s = mod[1:n_mod, :d], mod[1:n_mod, d:2 * d], mod[1:n_mod, 2 * d:]

    xp = x_prompt.reshape(seq, d)
    xs = x_sample.reshape(nb, d)

    p = _in_projection(xp, shift_p, scale_p, norm_g[l], w_in_r)
    qt, k32, kb, vt, v32, iqb, ik32, ikb, wv = _attention_prep(p, _rope_tables(jnp.arange(seq)),
                                                         q_norm_g[l], k_norm_g[l], transposed=True)
    n_sel = min(TOPK_MAX, seq // 4)
    tq = min(MXU_DIM, seq)
    bias = _prompt_indexer(iqb, wv[:, IDX_HD:IDX_HD + IDX_HEADS].T, ikb, n_sel, tq)
    att = _prompt_attention(qt, kb, vt, bias, p, tq)

    r_, kp_, v_, kk_, b_, lw_, bonus, sg = _rwkv_prep(p, jnp.zeros((1, p.shape[1]), F32), rwkv_consts,
                                                      shifted=True)
    y_scan, s_fin = _rwkv_scan(r_, kp_, v_, kk_, b_, lw_,
                               jnp.zeros((N_GROUPS, MXU_DIM, MXU_DIM), F32))
    rw = _rwkv_post(y_scan, bonus, sg, ln_x_g[l], ln_x_b[l], ones_bd)
    y_p = _out_projection(att, rw, w_out_b, xp, gate_p)

    k_prompt = k32.reshape(1, 1, seq, ATT_HEADS, ATT_HD)
    v_prompt = v32.reshape(1, 1, seq, ATT_HEADS, ATT_HD)
    idx_k_prompt = ik32.reshape(1, 1, seq, IDX_HD)
    wkv_prompt = _blockdiag_to_state(s_fin).reshape(1, 1, RWKV_HEADS, RWKV_HD, RWKV_HD)
    shift_prompt = _rwkv_cols_original_order(p[seq - 1:seq]).reshape(1, 1, -1)

    ps = _in_projection(xs, shift_s, scale_s, norm_g[l], w_in_r)
    pos_s = jnp.full((nb,), past_len, jnp.int32)
    qb_s, k32_s, _, _, v32_s, iqb_s, ik32_s, ikb_s, wv_s = _attention_prep(ps, _rope_tables(pos_s),
                                                                          q_norm_g[l], k_norm_g[l],
                                                                          transposed=False)
    n_sel_s = min(TOPK_MAX, (past_len + 1) // 4)
    att_s = _sample_attention(page_table, qb_s, iqb_s, wv_s, ikb_s, k32_s, v32_s,
                              ps[:, COL_G:COL_G + ATT_W], cache_idx_k[l], cache_k[l], cache_v[l], n_sel_s)

    prev_s = _reorder_cols(jnp.concatenate([jnp.zeros((nb, N_ATT_COLS_ORIG), F32), state_shift[l]], axis=-1))
    r_s, kp_s, v_s, kk_s, b_s, lw_s, bonus_s, sg_s = _rwkv_prep(ps, prev_s, rwkv_consts, shifted=False)
    wkv_new, y_step = _rwkv_step(state_wkv[l], r_s, kp_s, kk_s, b_s, lw_s, v_s)
    rw_s = _rwkv_post(y_step, bonus_s, sg_s, ln_x_g[l], ln_x_b[l], ones_bd)
    y_s = _out_projection(att_s, rw_s, w_out_b, xs, gate_s)

    k_sample = k32_s.reshape(1, nb, 1, ATT_HEADS, ATT_HD)
    v_sample = v32_s.reshape(1, nb, 1, ATT_HEADS, ATT_HD)
    idx_k_sample = ik32_s.reshape(1, nb, 1, IDX_HD)
    wkv_sample = wkv_new.reshape(1, nb, RWKV_HEADS, RWKV_HD, RWKV_HD)
    shift_sample = _rwkv_cols_original_order(ps).reshape(1, nb, -1)

    return (y_p.reshape(1, seq, d), y_s.reshape(nb, 1, d),
            k_prompt, v_prompt, idx_k_prompt, wkv_prompt, shift_prompt,
            k_sample, v_sample, idx_k_sample, wkv_sample, shift_sample)
```

```python
import functools
import math

import jax
import jax.numpy as jnp
from jax import lax
from jax.experimental import pallas as pl
from jax.experimental.pallas import tpu as pltpu

F32 = jnp.float32
BF16 = jnp.bfloat16
I32 = jnp.int32

ATT_HEADS = 8
ATT_HD = 128
ATT_W = ATT_HEADS * ATT_HD
IDX_HEADS = 16
IDX_HD = 64
IDX_W = IDX_HEADS * IDX_HD
TOPK_MAX = 256
PAGE_SIZE = 128
ROPE_THETA = 10000.0
RWKV_HD = 64
RWKV_HEADS = 16
RWKV_W = RWKV_HEADS * RWKV_HD
DECAY_LORA = 64
ICL_LORA = 64
NORM_EPS = 1e-6
GN_EPS = 64e-5

LANES = 128
SUBLANES = 8
MXU_DIM = 256
VMEM_LIMIT = 56 * 1024 * 1024
NEG_BIG = -1e30
INT_MIN = -(2 ** 31)

COL_Q, COL_K, COL_V, COL_G, COL_IQ = 0, 1024, 2048, 3072, 4096
COL_RR, COL_RK, COL_RV, COL_RG = 5120, 6144, 7168, 8192
COL_SMALL_A = 9216
COL_SMALL_R = 9344
N_COLS = 9472
N_COLS_PAD = 9728
N_ATT_COLS_ORIG = 4 * ATT_W + IDX_W + IDX_HEADS + IDX_HD

FLASH_KEY_BLOCKS = 4
SCAN_CHUNK = 64
DECODE_PAGES_PER_STEP = 8
SCAN_SUB = 4
HEADS_PER_GROUP = MXU_DIM // RWKV_HD
N_GROUPS = RWKV_HEADS // HEADS_PER_GROUP


def _cparams(sem, vmem=VMEM_LIMIT):
    return pltpu.CompilerParams(dimension_semantics=sem, vmem_limit_bytes=vmem)


def _split2(x):
    hi = x.astype(BF16)
    lo = (x - hi.astype(F32)).astype(BF16)
    return hi, lo


def _split3(x):
    hi = x.astype(BF16)
    r = x - hi.astype(F32)
    mid = r.astype(BF16)
    lo = (r - mid.astype(F32)).astype(BF16)
    return hi, mid, lo


_NN = (((1,), (0,)), ((), ()))
_NT = (((1,), (1,)), ((), ()))
_TN = (((0,), (0,)), ((), ()))


def _dg(a, b, dims):
    return lax.dot_general(a, b, dims, preferred_element_type=F32)


def _dot1(a, b, dims=_NN):
    return _dg(a.astype(BF16), b.astype(BF16), dims)


def _dot3(a, b, dims=_NN):
    ah, al = _split2(a)
    bh, bl = _split2(b)
    return _dg(ah, bh, dims) + _dg(al, bh, dims) + _dg(ah, bl, dims)


def _dot_exact_lhs(a_bf16, b, dims=_NN):
    b0, b1, b2 = _split3(b)
    return _dg(a_bf16, b0, dims) + _dg(a_bf16, b1, dims) + _dg(a_bf16, b2, dims)


def _dot_exact_rhs(a, b_bf16, dims=_NN):
    a0, a1, a2 = _split3(a)
    return _dg(a0, b_bf16, dims) + _dg(a1, b_bf16, dims) + _dg(a2, b_bf16, dims)


def _silu(x):
    return x * jax.nn.sigmoid(x)


def _head_sum(x, ones_bd):
    parts = []
    for j in range(x.shape[1] // LANES):
        parts.append(_dot_exact_rhs(x[:, j * LANES:(j + 1) * LANES], ones_bd))
    return jnp.concatenate(parts, axis=1)


def _ordered_to_f32(key):
    return pltpu.bitcast(key ^ ((key >> 31) & 0x7FFFFFFF), F32)


def _mod_kernel(c_ref, w_ref, b_ref, o_ref):
    o_ref[...] = _dot3(_silu(c_ref[...]), w_ref[...]) + b_ref[...]


def _modulation(c, w_ada, b_ada):
    m, d = c.shape
    n = w_ada.shape[1]
    tn = 768
    return pl.pallas_call(
        _mod_kernel,
        grid=(n // tn,),
        in_specs=[pl.BlockSpec((m, d), lambda j: (0, 0)),
                  pl.BlockSpec((d, tn), lambda j: (0, j)),
                  pl.BlockSpec((1, tn), lambda j: (0, j))],
        out_specs=pl.BlockSpec((m, tn), lambda j: (0, j)),
        out_shape=jax.ShapeDtypeStruct((m, n), F32),
        compiler_params=_cparams(("arbitrary",)),
        name="modulation",
    )(c, w_ada, b_ada.reshape(1, n))


def _inproj_kernel(x_ref, shift_ref, scale_ref, g_ref, w_ref, o_ref, h_ref, *, rows):
    @pl.when(pl.program_id(1) == 0)
    def _():
        def body(r, c):
            sl = pl.ds(pl.multiple_of(r * rows, rows), rows)
            x = x_ref[sl, :]
            ms = jnp.mean(x * x, axis=-1, keepdims=True)
            y = x * lax.rsqrt(ms + NORM_EPS) * g_ref[...]
            if shift_ref.shape[0] == 1:
                h = y * (1.0 + scale_ref[...]) + shift_ref[...]
            else:
                h = y * (1.0 + scale_ref[sl, :]) + shift_ref[sl, :]
            h_ref[sl, :] = h.astype(BF16)
            return c
        lax.fori_loop(0, x_ref.shape[0] // rows, body, 0)

    o_ref[...] = jnp.dot(h_ref[...], w_ref[...], preferred_element_type=F32)


def _in_projection(x, shift, scale, norm_g, w_bf16):
    t, d = x.shape
    n = w_bf16.shape[1]
    tm = min(1024, t)
    tn = 512
    rows = min(128, tm)
    per_row = shift.shape[0] != 1
    mod_spec = (pl.BlockSpec((tm, d), lambda i, j: (i, 0)) if per_row
                else pl.BlockSpec((1, d), lambda i, j: (0, 0)))
    return pl.pallas_call(
        functools.partial(_inproj_kernel, rows=rows),
        grid=(t // tm, n // tn),
        in_specs=[pl.BlockSpec((tm, d), lambda i, j: (i, 0)),
                  mod_spec, mod_spec,
                  pl.BlockSpec((1, d), lambda i, j: (0, 0)),
                  pl.BlockSpec((d, tn), lambda i, j: (0, j))],
        out_specs=pl.BlockSpec((tm, tn), lambda i, j: (i, j)),
        out_shape=jax.ShapeDtypeStruct((t, n), F32),
        scratch_shapes=[pltpu.VMEM((tm, d), BF16)],
        compiler_params=_cparams(("arbitrary", "arbitrary")),
        name="in_projection",
    )(x, shift, scale, norm_g.reshape(1, d), w_bf16)


def _rope128(x, cos, sin_signed):
    return x * cos + pltpu.roll(x, 64, axis=1) * sin_signed


def _rope64(x, cos, sin_signed, first_half):
    rot = jnp.where(first_half, pltpu.roll(x, 96, axis=1), pltpu.roll(x, 32, axis=1))
    return x * cos + rot * sin_signed


def _prep_kernel(q_ref, k_ref, v_ref, iq_ref, sm_ref, c128_ref, s128_ref, c64_ref, s64_ref,
                 qg_ref, kg_ref,
                 qo_ref, k32_ref, kb_ref, vo_ref, v32_ref, iqb_ref, ik32_ref, ikb_ref, w_ref, *, transposed):
    c128, s128 = c128_ref[...], s128_ref[...]
    c64, s64 = c64_ref[...], s64_ref[...]
    tm = q_ref.shape[0]
    lane = lax.broadcasted_iota(I32, (tm, LANES), 1)
    first_half = (lane % IDX_HD) < (IDX_HD // 2)
    q_scale = ATT_HD ** -0.5 * math.log2(math.e)
    for h in range(ATT_HEADS):
        hs = slice(h * ATT_HD, (h + 1) * ATT_HD)
        q = q_ref[:, hs]
        q = q * lax.rsqrt(jnp.mean(q * q, axis=-1, keepdims=True) + NORM_EPS) * qg_ref[...]
        q = _rope128(q, c128, s128) * q_scale
        k = k_ref[:, hs]
        k = k * lax.rsqrt(jnp.mean(k * k, axis=-1, keepdims=True) + NORM_EPS) * kg_ref[...]
        k = _rope128(k, c128, s128)
        k32_ref[:, h, :] = k
        kb_ref[:, hs] = k.astype(BF16)
        if transposed:
            qo_ref[hs, :] = q.T.astype(BF16)
            vo_ref[hs, :] = v_ref[:, hs].T.astype(BF16)
        else:
            qo_ref[:, hs] = q.astype(BF16)
            vo_ref[:, hs] = v_ref[:, hs].astype(BF16)
    for h in range(ATT_HEADS):
        v32_ref[:, h, :] = v_ref[:, h * ATT_HD:(h + 1) * ATT_HD]
    for j in range(IDX_W // LANES):
        js = slice(j * LANES, (j + 1) * LANES)
        iqb_ref[:, js] = _rope64(iq_ref[:, js], c64, s64, first_half).astype(BF16)
    sm = sm_ref[...]
    ik = _rope64(sm, c64, s64, first_half)[:, :IDX_HD]
    ik32_ref[...] = ik
    ikb_ref[...] = ik.astype(BF16)
    w_ref[...] = sm * ((IDX_HEADS ** -0.5) * (IDX_HD ** -0.5))


def _attention_prep(p, tables, q_norm_g, k_norm_g, transposed):
    t = p.shape[0]
    tm = min(256, t)
    c128, s128, c64, s64 = tables
    wide = lambda col: pl.BlockSpec((tm, 1024), lambda i, c=col // 1024: (i, c))
    tab = pl.BlockSpec((tm, LANES), lambda i: (i, 0))
    gain = pl.BlockSpec((1, ATT_HD), lambda i: (0, 0))
    row1024 = pl.BlockSpec((tm, 1024), lambda i: (i, 0))
    col1024 = pl.BlockSpec((1024, tm), lambda i: (0, i))
    row64 = pl.BlockSpec((tm, IDX_HD), lambda i: (i, 0))
    head3 = pl.BlockSpec((tm, ATT_HEADS, ATT_HD), lambda i: (i, 0, 0))
    if transposed:
        qv_spec, qv_shape = col1024, jax.ShapeDtypeStruct((ATT_W, t), BF16)
    else:
        qv_spec, qv_shape = row1024, jax.ShapeDtypeStruct((t, ATT_W), BF16)
    return pl.pallas_call(
        functools.partial(_prep_kernel, transposed=transposed),
        grid=(t // tm,),
        in_specs=[wide(COL_Q), wide(COL_K), wide(COL_V), wide(COL_IQ),
                  pl.BlockSpec((tm, LANES), lambda i: (i, COL_SMALL_A // LANES)),
                  tab, tab, tab, tab, gain, gain],
        out_specs=[qv_spec, head3, row1024, qv_spec, head3, row1024, row64, row64, tab],
        out_shape=[qv_shape,
                   jax.ShapeDtypeStruct((t, ATT_HEADS, ATT_HD), F32),
                   jax.ShapeDtypeStruct((t, ATT_W), BF16),
                   qv_shape,
                   jax.ShapeDtypeStruct((t, ATT_HEADS, ATT_HD), F32),
                   jax.ShapeDtypeStruct((t, IDX_W), BF16),
                   jax.ShapeDtypeStruct((t, IDX_HD), F32),
                   jax.ShapeDtypeStruct((t, IDX_HD), BF16),
                   jax.ShapeDtypeStruct((t, LANES), F32)],
        compiler_params=_cparams(("arbitrary",)),
        name="attention_prep",
    )(p, p, p, p, p, c128, s128, c64, s64, q_norm_g.reshape(1, ATT_HD), k_norm_g.reshape(1, ATT_HD))


def _rope_tables(pos):
    pos = pos.astype(F32)[:, None]

    def tab(dh):
        inv = ROPE_THETA ** (-jnp.arange(0, dh, 2, dtype=F32) / dh)
        ang = pos * inv[None, :]
        cos, sin = jnp.cos(ang), jnp.sin(ang)
        c = jnp.concatenate([cos, cos], axis=-1)
        s = jnp.concatenate([-sin, sin], axis=-1)
        rep = LANES // dh
        return jnp.tile(c, (1, rep)), jnp.tile(s, (1, rep))

    c128, s128 = tab(ATT_HD)
    c64, s64 = tab(IDX_HD)
    return c128, s128, c64, s64


def _indexer_kernel(iq_ref, wt_ref, ik_ref, bias_ref, iqs_ref, cut_ref, *, n_sel):
    i = pl.program_id(0)
    tq = iq_ref.shape[0]
    tk = bias_ref.shape[2]
    nk_total = bias_ref.shape[1]
    nkb = i + 1

    for h in range(IDX_HEADS):
        iqs_ref[h * tq:(h + 1) * tq, :] = iq_ref[:, h * IDX_HD:(h + 1) * IDX_HD]

    key_pos = lax.broadcasted_iota(I32, (tk, tq), 0)
    qry_pos = lax.broadcasted_iota(I32, (tk, tq), 1) + i * tq

    def score_body(kb, c):
        ikt = ik_ref[pl.ds(pl.multiple_of(kb * tk, tk), tk), :]
        d = _dg(ikt, iqs_ref[...], _NT)
        acc = jnp.zeros((tk, tq), F32)
        for h in range(IDX_HEADS):
            acc = acc + jnp.maximum(d[:, h * tq:(h + 1) * tq], 0.0) * wt_ref[h:h + 1, :]
        bias_ref[0, kb] = jnp.where(key_pos + kb * tk <= qry_pos, acc, -jnp.inf)
        return c

    lax.fori_loop(0, nkb, score_body, 0)

    def count_keys(pred):
        def one(kb, cnt):
            x = bias_ref[0, kb].reshape(tk // SUBLANES, SUBLANES, tq)
            pos = (key_pos + kb * tk).reshape(tk // SUBLANES, SUBLANES, tq)
            return cnt + jnp.sum(jnp.where(pred(x, pos), 1.0, 0.0), axis=0)

        def four(j, cnt):
            for u in range(4):
                cnt = one(4 * j + u, cnt)
            return cnt

        cnt = lax.fori_loop(0, nkb // 4, four, jnp.zeros((SUBLANES, tq), F32))
        cnt = lax.fori_loop(4 * (nkb // 4), nkb, one, cnt)
        return jnp.sum(cnt, axis=0, keepdims=True)

    def bit_body(bi, carry):
        ans, tot_ans = carry
        cand = ans + jnp.left_shift(jnp.int32(1), 31 - bi)
        cand_f = _ordered_to_f32(cand)
        tot = count_keys(lambda x, pos: x >= cand_f[None])
        ok = tot >= n_sel
        return jnp.where(ok, cand, ans), jnp.where(ok, tot, tot_ans)

    ans, tot_ans = lax.fori_loop(0, 32, bit_body, (jnp.full((SUBLANES, tq), INT_MIN, I32),
                                                   jnp.zeros((1, tq), F32)))
    thr = _ordered_to_f32(ans)[0:1, :]
    thr = jnp.where(thr != thr, -jnp.inf, thr)

    cut_ref[...] = jnp.full(cut_ref.shape, 2 ** 31 - 1, I32)

    @pl.when(jnp.max(tot_ans) > n_sel)
    def _():
        need = n_sel - count_keys(lambda x, pos: x > thr[None])
        n_bits = max(1, (nk_total * tk - 1).bit_length())

        def cut_body(bi, cut):
            cand = cut + jnp.left_shift(jnp.int32(1), n_bits - 1 - bi)
            below = count_keys(lambda x, pos: (x == thr[None]) & (pos < cand[None]))
            return jnp.where(below < need, cand, cut)

        cut = lax.fori_loop(0, n_bits, cut_body, jnp.zeros((1, tq), I32))
        cut_ref[...] = jnp.broadcast_to(jnp.where(tot_ans > n_sel, cut, 2 ** 31 - 1), cut_ref.shape)

    has_surplus_ties = jnp.max(tot_ans) > n_sel
    cut = cut_ref[0:1, :]

    def bias_body(kb, c):
        keep = (bias_ref[0, kb] >= thr) & (key_pos + kb * tk <= qry_pos)
        bias_ref[0, kb] = jnp.where(keep, 0.0, NEG_BIG)
        return c

    def bias_body_ties(kb, c):
        x = bias_ref[0, kb]
        pos = key_pos + kb * tk
        keep = ((x > thr) | ((x == thr) & (pos <= cut))) & (pos <= qry_pos)
        bias_ref[0, kb] = jnp.where(keep, 0.0, NEG_BIG)
        return c

    @pl.when(jnp.logical_not(has_surplus_ties))
    def _():
        lax.fori_loop(0, nkb, bias_body, 0)

    @pl.when(has_surplus_ties)
    def _():
        lax.fori_loop(0, nkb, bias_body_ties, 0)

    def fill_body(kb, c):
        bias_ref[0, kb] = jnp.full((tk, tq), NEG_BIG, F32)
        return c

    lax.fori_loop(nkb, nk_total, fill_body, 0)


def _prompt_indexer(iqb, wt, ikb, n_sel, tq):
    t = iqb.shape[0]
    nq = t // tq
    return pl.pallas_call(
        functools.partial(_indexer_kernel, n_sel=n_sel),
        grid=(nq,),
        in_specs=[pl.BlockSpec((tq, IDX_W), lambda i: (i, 0)),
                  pl.BlockSpec((IDX_HEADS, tq), lambda i: (0, i)),
                  pl.BlockSpec((t, IDX_HD), lambda i: (0, 0))],
        out_specs=pl.BlockSpec((1, nq, tq, tq), lambda i: (i, 0, 0, 0)),
        out_shape=jax.ShapeDtypeStruct((nq, nq, tq, tq), F32),
        scratch_shapes=[pltpu.VMEM((IDX_HEADS * tq, IDX_HD), BF16),
                        pltpu.VMEM((SUBLANES, tq), I32)],
        compiler_params=_cparams(("arbitrary",)),
        name="prompt_indexer",
    )(iqb, wt, ikb)


def _flash_kernel(qi_ref, ki_ref, qt_ref, k_ref, vt_ref, bias_ref, g_ref, o_ref, m_ref, l_ref, acc_ref):
    step = pl.program_id(0)
    qi, ki = qi_ref[step], ki_ref[step]

    @pl.when(ki == 0)
    def _():
        m_ref[...] = jnp.full(m_ref.shape, NEG_BIG, F32)
        l_ref[...] = jnp.zeros(l_ref.shape, F32)
        acc_ref[...] = jnp.zeros(acc_ref.shape, F32)

    heads = [slice(h * ATT_HD, (h + 1) * ATT_HD) for h in range(ATT_HEADS)]
    tq = qt_ref.shape[1]
    tkb = bias_ref.shape[2]
    for kh in range(bias_ref.shape[1]):
        ks = slice(kh * tkb, (kh + 1) * tkb)
        for qh in range(tq // LANES):
            qs = slice(qh * LANES, (qh + 1) * LANES)
            bias = bias_ref[0, kh, :, qs]
            m_prev, l_prev = m_ref[:, qs], l_ref[:, qs]
            m_rows, l_rows = [], []
            for h, hs in enumerate(heads):
                s = _dg(k_ref[ks, hs], qt_ref[hs, qs], _NN) + bias
                m_old = m_prev[h:h + 1, :]
                m_new = jnp.maximum(m_old, jnp.max(s, axis=0, keepdims=True))
                alpha = jnp.exp2(m_old - m_new)
                p = jnp.exp2(s - m_new)
                l_rows.append(alpha * l_prev[h:h + 1, :] + jnp.sum(p, axis=0, keepdims=True))
                m_rows.append(m_new)
                acc_ref[h, :, qs] = alpha * acc_ref[h, :, qs] + _dg(vt_ref[hs, ks], p.astype(BF16), _NN)
            m_ref[:, qs] = jnp.concatenate(m_rows, axis=0)
            l_ref[:, qs] = jnp.concatenate(l_rows, axis=0)

    @pl.when(ki == qi // bias_ref.shape[1])
    def _():
        for h in range(ATT_HEADS):
            hs = slice(h * ATT_HD, (h + 1) * ATT_HD)
            out = (acc_ref[h] / l_ref[h:h + 1, :]).T
            o_ref[:, hs] = (out * _silu(g_ref[:, hs])).astype(BF16)


def _prompt_attention(qt, kb, vt, bias, p, tq):
    t = kb.shape[0]
    nq = t // tq
    kpb = min(FLASH_KEY_BLOCKS, nq)
    tk = kpb * tq
    pairs = [(i, j) for i in range(nq) for j in range(i // kpb + 1)]
    qi = jnp.asarray([a for a, _ in pairs], I32)
    ki = jnp.asarray([b for _, b in pairs], I32)
    grid_spec = pltpu.PrefetchScalarGridSpec(
        num_scalar_prefetch=2,
        grid=(len(pairs),),
        in_specs=[pl.BlockSpec((ATT_W, tq), lambda s, qi, ki: (0, qi[s])),
                  pl.BlockSpec((tk, ATT_W), lambda s, qi, ki: (ki[s], 0)),
                  pl.BlockSpec((ATT_W, tk), lambda s, qi, ki: (0, ki[s])),
                  pl.BlockSpec((1, kpb, tq, tq), lambda s, qi, ki: (qi[s], ki[s], 0, 0)),
                  pl.BlockSpec((tq, ATT_W), lambda s, qi, ki: (qi[s], COL_G // ATT_W))],
        out_specs=pl.BlockSpec((tq, ATT_W), lambda s, qi, ki: (qi[s], 0)),
        scratch_shapes=[pltpu.VMEM((ATT_HEADS, tq), F32),
                        pltpu.VMEM((ATT_HEADS, tq), F32),
                        pltpu.VMEM((ATT_HEADS, ATT_HD, tq), F32)],
    )
    return pl.pallas_call(
        _flash_kernel,
        grid_spec=grid_spec,
        out_shape=jax.ShapeDtypeStruct((t, ATT_W), BF16),
        compiler_params=_cparams(("arbitrary",)),
        name="prompt_attention",
    )(qi, ki, qt, kb, vt, bias, p)


def _sample_attn_kernel(pt_ref, q_ref, iq_ref, w_ref, ikn_ref, kn_ref, vn_ref, g_ref, ex_ref,
                        cik_ref, ck_ref, cv_ref, o_ref,
                        ikbuf, kbuf, vbuf, sce_ref, isem, ksem, vsem, *, n_sel, pps):
    b = pl.program_id(0)
    n_pages = ikbuf.shape[0]
    page_rows = PAGE_SIZE * ATT_HEADS

    n_steps = n_pages // pps
    nb = pl.num_programs(0)
    cross_prefetch = n_steps % 2 == 0

    def idx_copy(seq, j):
        return pltpu.make_async_copy(cik_ref.at[pt_ref[seq, j]], ikbuf.at[j], isem.at[j])

    def k_copy(seq, j, slot):
        return pltpu.make_async_copy(ck_ref.at[pt_ref[seq, j]], kbuf.at[slot], ksem.at[slot])

    def v_copy(seq, j, slot):
        return pltpu.make_async_copy(cv_ref.at[pt_ref[seq, j]], vbuf.at[slot], vsem.at[slot])

    def start_idx(seq):
        def body(j, c):
            idx_copy(seq, j).start()
            return c
        lax.fori_loop(0, n_pages, body, 0)

    def start_group(seq, t, grp):
        for i in range(pps):
            k_copy(seq, t * pps + i, grp * pps + i).start()
            v_copy(seq, t * pps + i, grp * pps + i).start()

    if cross_prefetch:
        @pl.when(b == 0)
        def _():
            start_idx(0)
            start_group(0, 0, 0)
        start_group(b, 1, 1)
    else:
        @pl.when(b == 0)
        def _():
            start_idx(0)
        start_group(b, 0, 0)

    iq = iq_ref[0]
    w = w_ref[0]

    def wait_idx(j, c):
        idx_copy(b, j).wait()
        return c

    lax.fori_loop(0, n_pages, wait_idx, 0)
    iq_pages = jnp.broadcast_to(iq[None], (n_pages, IDX_HEADS, IDX_HD))
    d = lax.dot_general(iq_pages, ikbuf[...].astype(BF16), (((2,), (1,)), ((0,), (0,))),
                        preferred_element_type=F32)
    sc = jnp.sum(jnp.maximum(d, 0.0) * w[None], axis=1)
    d_self = jnp.sum(iq.astype(F32) * ikn_ref[0].astype(F32), axis=-1, keepdims=True)
    s_self = jnp.sum(jnp.maximum(d_self, 0.0) * w, axis=0, keepdims=True)

    pos = (lax.broadcasted_iota(I32, sc.shape, 0) * PAGE_SIZE + lax.broadcasted_iota(I32, sc.shape, 1))
    self_pos = n_pages * PAGE_SIZE

    def count_keys(pred):
        cnt = jnp.sum(jnp.where(pred(sc, pos), 1.0, 0.0), axis=-1, keepdims=True)
        return jnp.sum(cnt, axis=0, keepdims=True) + jnp.where(pred(s_self, self_pos), 1.0, 0.0)

    def digit_body(it, carry):
        ans, tot_ans = carry
        step = jnp.left_shift(jnp.int32(1), 30 - 2 * it)
        best, best_tot = ans, tot_ans
        cand = ans
        for _ in range(3):
            cand = cand + step
            cand_f = _ordered_to_f32(cand)
            tot = count_keys(lambda x, p: x >= cand_f)
            best = jnp.where(tot >= n_sel, cand, best)
            best_tot = jnp.where(tot >= n_sel, tot, best_tot)
        return best, best_tot

    ans, tot_ans = lax.fori_loop(0, 16, digit_body, (jnp.full((1, 1), INT_MIN, I32), jnp.zeros((1, 1), F32)))
    thr = _ordered_to_f32(ans)
    thr = jnp.where(thr != thr, -jnp.inf, thr)

    def tie_cut():
        need = n_sel - count_keys(lambda x, p: x > thr)
        n_bits = self_pos.bit_length()

        def cut_body(bi, cut):
            cand = cut + jnp.left_shift(jnp.int32(1), n_bits - 1 - bi)
            below = count_keys(lambda x, p: (x == thr) & (p < cand))
            return jnp.where(below < need, cand, cut)

        return lax.fori_loop(0, n_bits, cut_body, jnp.zeros((1, 1), I32))

    cut = lax.cond(tot_ans[0, 0] > n_sel, tie_cut, lambda: jnp.full((1, 1), 2 ** 31 - 1, I32))

    def selected(x, p):
        return (x > thr) | ((x == thr) & (p <= cut))

    sce_ref[...] = _dot_exact_rhs(sc, ex_ref[...])

    @pl.when(b + 1 < nb)
    def _():
        start_idx(b + 1)

    q8 = q_ref[0]
    lane_e = lax.broadcasted_iota(I32, (ATT_HEADS, page_rows), 1)
    own = lane_e % ATT_HEADS == lax.broadcasted_iota(I32, (ATT_HEADS, page_rows), 0)
    tok_lane = lane_e[0:1, :] // ATT_HEADS

    kn = kn_ref[0].astype(BF16).astype(F32)
    s0 = jnp.sum(q8.astype(F32) * kn, axis=-1, keepdims=True)
    m0 = jnp.where(selected(s_self, self_pos), s0, NEG_BIG)
    l0 = jnp.where(m0 > 0.5 * NEG_BIG, 1.0, 0.0)
    acc0 = l0 * vn_ref[0].astype(BF16).astype(F32)

    def att_body(t, carry):
        m_prev, l_prev, acc = carry
        grp = t % 2

        if cross_prefetch:
            @pl.when((t >= 1) & (t + 1 < n_steps))
            def _():
                start_group(b, t + 1, 1 - grp)

            @pl.when((t + 1 == n_steps) & (b + 1 < nb))
            def _():
                start_group(b + 1, 0, 0)
        else:
            @pl.when(t + 1 < n_steps)
            def _():
                start_group(b, t + 1, 1 - grp)

        for i in range(pps):
            k_copy(b, t * pps + i, grp * pps + i).wait()
            v_copy(b, t * pps + i, grp * pps + i).wait()
        s = []
        for i in range(pps):
            kf = kbuf[grp * pps + i].reshape(page_rows, ATT_HD).astype(BF16)
            keep = own & selected(sce_ref[pl.ds(t * pps + i, 1), :], (t * pps + i) * PAGE_SIZE + tok_lane)
            s.append(jnp.where(keep, _dg(q8, kf, _NT), NEG_BIG))
        m_new = m_prev
        for x in s:
            m_new = jnp.maximum(m_new, jnp.max(x, axis=-1, keepdims=True))
        alpha = jnp.exp2(m_prev - m_new)
        p = [jnp.exp2(x - m_new) for x in s]
        l_new = alpha * l_prev
        for x in p:
            l_new = l_new + jnp.sum(x, axis=-1, keepdims=True)
        pv = [jnp.dot(p[i].astype(BF16), vbuf[grp * pps + i].reshape(page_rows, ATT_HD).astype(BF16),
                      preferred_element_type=F32) for i in range(pps)]
        acc = alpha * acc
        for x in pv:
            acc = acc + x
        return m_new, l_new, acc

    m, l, acc = lax.fori_loop(0, n_steps, att_body, (m0, l0, acc0))
    o_ref[0] = acc / l * _silu(g_ref[0])


def _sample_attention(page_table, qb, iqb, wv, ikb, k32, v32, g, cache_idx_k, cache_k, cache_v, n_sel):
    nb, n_pages = page_table.shape
    head3 = pl.BlockSpec((1, ATT_HEADS, ATT_HD), lambda b, pt: (b, 0, 0))
    any_spec = pl.BlockSpec(memory_space=pl.ANY)
    page_rows = PAGE_SIZE * ATT_HEADS
    pps = math.gcd(DECODE_PAGES_PER_STEP, n_pages)
    expand = (jnp.arange(page_rows)[None, :] // ATT_HEADS == jnp.arange(PAGE_SIZE)[:, None]).astype(BF16)
    grid_spec = pltpu.PrefetchScalarGridSpec(
        num_scalar_prefetch=1,
        grid=(nb,),
        in_specs=[head3,
                  pl.BlockSpec((1, IDX_HEADS, IDX_HD), lambda b, pt: (b, 0, 0)),
                  pl.BlockSpec((1, IDX_HEADS, 1), lambda b, pt: (b, 0, 0)),
                  pl.BlockSpec((1, 1, IDX_HD), lambda b, pt: (b, 0, 0)),
                  head3, head3, head3,
                  pl.BlockSpec((PAGE_SIZE, page_rows), lambda b, pt: (0, 0)),
                  any_spec, any_spec, any_spec],
        out_specs=head3,
        scratch_shapes=[pltpu.VMEM((n_pages, IDX_HD, PAGE_SIZE), F32),
                        pltpu.VMEM((2 * pps, PAGE_SIZE, ATT_HEADS, ATT_HD), F32),
                        pltpu.VMEM((2 * pps, PAGE_SIZE, ATT_HEADS, ATT_HD), F32),
                        pltpu.VMEM((n_pages, page_rows), F32),
                        pltpu.SemaphoreType.DMA((n_pages,)),
                        pltpu.SemaphoreType.DMA((2 * pps,)),
                        pltpu.SemaphoreType.DMA((2 * pps,))],
    )
    head = lambda a: a.reshape(nb, ATT_HEADS, ATT_HD)
    out = pl.pallas_call(
        functools.partial(_sample_attn_kernel, n_sel=n_sel, pps=pps),
        grid_spec=grid_spec,
        out_shape=jax.ShapeDtypeStruct((nb, ATT_HEADS, ATT_HD), F32),
        compiler_params=_cparams(("arbitrary",)),
        name="sample_attention",
    )(page_table, head(qb),
      iqb.reshape(nb, IDX_HEADS, IDX_HD),
      wv[:, IDX_HD:IDX_HD + IDX_HEADS].reshape(nb, IDX_HEADS, 1),
      ikb.reshape(nb, 1, IDX_HD),
      head(k32), head(v32), head(g), expand,
      jnp.swapaxes(cache_idx_k, 1, 2), cache_k, cache_v)
    return out.reshape(nb, ATT_W).astype(BF16)


def _rwkv_prep_kernel(*refs, shifted):
    n_x = 5
    x_refs = refs[:n_x]
    if shifted:
        halo_refs, first_refs = refs[n_x:2 * n_x], refs[2 * n_x:3 * n_x]
        rest = refs[3 * n_x:]
    else:
        prev_refs = refs[n_x:2 * n_x]
        rest = refs[2 * n_x:]
    (mu_ref, mus_ref, wl_ref, w0_ref, a0_ref, kk_ref, ka_ref, rk_ref, ones_ref,
     ro_ref, kpo_ref, vo_ref, kko_ref, bo_ref, lwo_ref, bonus_ref, sg_ref) = rest

    def lerp(j, mu):
        x = x_refs[j][...]
        if shifted:
            above = jnp.where(pl.program_id(0) == 0, first_refs[j][...],
                              halo_refs[j][SUBLANES - 1:SUBLANES, :])
            row = lax.broadcasted_iota(I32, x.shape, 0)
            prev = jnp.where(row == 0, above, pltpu.roll(x, 1, axis=0))
        else:
            prev = prev_refs[j][...]
        return x + mu * (prev - x)

    w = RWKV_W
    xr = lerp(0, mu_ref[:, 0:w])
    xk = lerp(1, mu_ref[:, w:2 * w])
    xv = lerp(2, mu_ref[:, 2 * w:3 * w])
    xg = lerp(3, mu_ref[:, 3 * w:4 * w])
    xs = lerp(4, mus_ref[...])
    lane = lax.broadcasted_iota(I32, xs.shape, 1)
    lora_in = jnp.where(lane < DECAY_LORA, jnp.tanh(xs), xs)
    lora = _dot3(lora_in, wl_ref[...])
    z = w0_ref[...] + lora[:, :w]
    softplus_neg = jnp.maximum(-z, 0.0) + jnp.log(1.0 + jnp.exp(-jnp.abs(z)))
    w_log = -softplus_neg - 0.5
    lwo_ref[...] = -jnp.exp(w_log)
    a = jax.nn.sigmoid(a0_ref[...] + lora[:, w:])
    ones_bd = ones_ref[...]
    kk = xk * kk_ref[...]
    kk = kk / jnp.maximum(jnp.sqrt(_head_sum(kk * kk, ones_bd)), 1e-12)
    kp = xk * (1.0 + (a - 1.0) * ka_ref[...])
    ro_ref[...] = xr
    kpo_ref[...] = kp
    vo_ref[...] = xv
    kko_ref[...] = kk
    bo_ref[...] = kk * a
    bonus_ref[...] = _head_sum(xr * kp * rk_ref[...], ones_bd) * xv
    sg_ref[...] = _silu(xg)


def _rwkv_prep(p, prev, consts, shifted):
    t = p.shape[0]
    tm = min(256, t)
    mu_big, mu_small, w_lora, w0, a0, k_k, k_a, r_k, ones_bd = consts
    cols = [(RWKV_W, COL_RR // RWKV_W), (RWKV_W, COL_RK // RWKV_W), (RWKV_W, COL_RV // RWKV_W),
            (RWKV_W, COL_RG // RWKV_W), (LANES, COL_SMALL_R // LANES)]
    x_specs = [pl.BlockSpec((tm, wd), lambda i, c=c: (i, c)) for wd, c in cols]
    if shifted:
        per8 = tm // SUBLANES
        halo = [pl.BlockSpec((SUBLANES, wd), lambda i, c=c: (jnp.maximum(i * per8 - 1, 0), c))
                for wd, c in cols]
        first = [pl.BlockSpec((1, wd), lambda i, c=c: (0, c)) for wd, c in cols]
        extra_specs, extra_args = halo + first, [p] * 5 + [prev] * 5
    else:
        extra_specs, extra_args = x_specs, [prev] * 5
    full = lambda a: pl.BlockSpec(a.shape, lambda i: (0, 0))
    row = pl.BlockSpec((tm, RWKV_W), lambda i: (i, 0))
    vec = lambda a: a.reshape(1, -1)
    args = [mu_big, mu_small, w_lora, vec(w0), vec(a0), vec(k_k), vec(k_a), vec(r_k), ones_bd]
    return pl.pallas_call(
        functools.partial(_rwkv_prep_kernel, shifted=shifted),
        grid=(t // tm,),
        in_specs=x_specs + extra_specs + [full(a) for a in args],
        out_specs=[row] * 8,
        out_shape=[jax.ShapeDtypeStruct((t, RWKV_W), F32)] * 8,
        compiler_params=_cparams(("arbitrary",)),
        name="rwkv_prep",
    )(*([p] * 5), *extra_args, *args)


def _scan_kernel(r_ref, kp_ref, v_ref, kk_ref, b_ref, lw_ref, s0_ref, y_ref, st_ref, s_ref):
    @pl.when(pl.program_id(0) == 0)
    def _():
        s_ref[...] = s0_ref[...]

    ch = SCAN_CHUNK
    assert ch == RWKV_HD
    rows_total = r_ref.shape[0]
    n_sub = rows_total // ch
    gw = MXU_DIM
    nh = HEADS_PER_GROUP
    lw = lw_ref[...]
    tr = lax.broadcasted_iota(I32, (rows_total, rows_total), 0)
    tc = lax.broadcasted_iota(I32, (rows_total, rows_total), 1)
    tri = jnp.where((tr >= tc) & (tr // ch == tc // ch), 1.0, 0.0).astype(BF16)
    cum = _dot_exact_lhs(tri, lw)
    cum_end = jnp.concatenate(
        [jnp.broadcast_to(cum[(c + 1) * ch - 1:(c + 1) * ch, :], (ch, RWKV_W)) for c in range(n_sub)], axis=0)
    kk, b, kp, r, v = kk_ref[...], b_ref[...], kp_ref[...], r_ref[...], v_ref[...]
    g_inv = jnp.exp(-cum)
    g_end = jnp.exp(cum_end - cum)
    a_t = (-kk * jnp.exp(cum - lw)).astype(BF16)
    b_t = b * g_inv
    k_t = kp * g_inv
    r_t = (r * jnp.exp(cum)).astype(BF16)
    b_h = (b * g_end).astype(BF16)
    k_h = (kp * g_end).astype(BF16)
    vb = v.astype(BF16)

    srow = lax.broadcasted_iota(I32, (nh * ch, gw), 0)
    scol = lax.broadcasted_iota(I32, (nh * ch, gw), 1)
    own = (srow // ch) == (scol // RWKV_HD)
    step = lax.broadcasted_iota(I32, (ch, gw), 0)
    other = lax.broadcasted_iota(I32, (ch, gw), 1) % ch
    strict, incl = step > other, step >= other
    eye = jnp.where(step == other, 1.0, 0.0)

    def stack(x):
        return jnp.where(own, jnp.concatenate([x] * nh, axis=0), 0.0).astype(BF16)

    groups = range(N_GROUPS)
    units = [(c, g) for c in range(n_sub) for g in groups]
    blk = {(c, g): (slice(c * ch, (c + 1) * ch), slice(g * gw, (g + 1) * gw)) for c, g in units}
    rb = {u: stack(b_t[blk[u]]) for u in units}
    rk = {u: stack(k_t[blk[u]]) for u in units}
    v4 = {u: stack(v[blk[u]]) for u in units}
    n_row = {u: jnp.where(strict, _dg(a_t[blk[u]], rb[u], _NT), 0.0) for u in units}
    bm_row = {u: jnp.where(strict, _dg(a_t[blk[u]], rk[u], _NT), 0.0).astype(BF16) for u in units}
    rb_row = {u: jnp.where(incl, _dg(r_t[blk[u]], rb[u], _NT), 0.0).astype(BF16) for u in units}
    rk_row = {u: jnp.where(incl, _dg(r_t[blk[u]], rk[u], _NT), 0.0).astype(BF16) for u in units}
    minv = {u: eye + n_row[u] for u in units}
    pw = n_row
    pw_bd = {u: stack(pw[u]) for u in units}
    for _ in range(int(math.log2(ch)) - 1):
        pw = {u: _dg(pw[u].astype(BF16), pw_bd[u], _NN) for u in units}
        pw_bd = {u: stack(pw[u]) for u in units}
        minv = {u: minv[u] + _dg(minv[u].astype(BF16), pw_bd[u], _NN) for u in units}
    minv = {u: minv[u].astype(BF16) for u in units}
    y_fixed = {u: _dg(rk_row[u], v4[u], _NN) for u in units}
    x_fixed = {u: _dg(bm_row[u], v4[u], _NN) for u in units}

    s = [s_ref[g] for g in groups]
    for c in range(n_sub):
        sb = [x.astype(BF16) for x in s]
        x_row = [_dg(a_t[blk[c, g]], sb[g], _NT) + x_fixed[c, g] for g in groups]
        x_bd = [stack(x) for x in x_row]
        u_row = [_dg(minv[c, g], x_bd[g], _NN) for g in groups]
        u_bd = [stack(x) for x in u_row]
        for g in groups:
            y_ref[blk[c, g]] = _dg(r_t[blk[c, g]], sb[g], _NT) + _dg(rb_row[c, g], u_bd[g], _NN) + y_fixed[c, g]
        for g in groups:
            rs, ls = blk[c, g]
            upd = _dg(jnp.concatenate([u_row[g].astype(BF16), vb[rs, ls]], axis=0),
                      jnp.concatenate([b_h[rs, ls], k_h[rs, ls]], axis=0), _TN)
            g_c = jnp.exp(cum[(c + 1) * ch - 1:(c + 1) * ch, ls])
            s[g] = s[g] * g_c + jnp.where(own, upd, 0.0)
    for g in groups:
        s_ref[g] = s[g]

    @pl.when(pl.program_id(0) == pl.num_programs(0) - 1)
    def _():
        st_ref[...] = s_ref[...]


def _rwkv_scan(r, kp, v, kk, b, lw, s0):
    t = r.shape[0]
    ch = SCAN_CHUNK * min(SCAN_SUB, t // SCAN_CHUNK)
    row = pl.BlockSpec((ch, RWKV_W), lambda i: (i, 0))
    st = pl.BlockSpec((N_GROUPS, MXU_DIM, MXU_DIM), lambda i: (0, 0, 0))
    return pl.pallas_call(
        _scan_kernel,
        grid=(t // ch,),
        in_specs=[row] * 6 + [st],
        out_specs=[row, st],
        out_shape=[jax.ShapeDtypeStruct((t, RWKV_W), F32),
                   jax.ShapeDtypeStruct((N_GROUPS, MXU_DIM, MXU_DIM), F32)],
        scratch_shapes=[pltpu.VMEM((N_GROUPS, MXU_DIM, MXU_DIM), F32)],
        compiler_params=_cparams(("arbitrary",)),
        name="rwkv_scan",
    )(r, kp, v, kk, b, lw, s0)


def _rwkv_step_kernel(s_ref, r_ref, kp_ref, kk_ref, b_ref, lw_ref, v_ref, so_ref, y_ref):
    s = s_ref[0]
    kk = kk_ref[0]
    sa = -jnp.sum(s * kk, axis=-1, keepdims=True)
    s_new = s * jnp.exp(lw_ref[0]) + sa * b_ref[0] + v_ref[0] * kp_ref[0]
    so_ref[0] = s_new
    y_ref[0] = jnp.sum(s_new * r_ref[0], axis=-1, keepdims=True)


def _rwkv_step(state, r, kp, kk, b, lw, v):
    nb = state.shape[0]
    rowv = lambda a: a.reshape(nb, RWKV_HEADS, 1, RWKV_HD)
    rspec = pl.BlockSpec((1, RWKV_HEADS, 1, RWKV_HD), lambda i: (i, 0, 0, 0))
    cspec = pl.BlockSpec((1, RWKV_HEADS, RWKV_HD, 1), lambda i: (i, 0, 0, 0))
    sspec = pl.BlockSpec((1, RWKV_HEADS, RWKV_HD, RWKV_HD), lambda i: (i, 0, 0, 0))
    s_new, y = pl.pallas_call(
        _rwkv_step_kernel,
        grid=(nb,),
        in_specs=[sspec, rspec, rspec, rspec, rspec, rspec, cspec],
        out_specs=[sspec, cspec],
        out_shape=[jax.ShapeDtypeStruct(state.shape, F32),
                   jax.ShapeDtypeStruct((nb, RWKV_HEADS, RWKV_HD, 1), F32)],
        compiler_params=_cparams(("arbitrary",)),
        name="rwkv_step",
    )(state, rowv(r), rowv(kp), rowv(kk), rowv(b), rowv(lw), v.reshape(nb, RWKV_HEADS, RWKV_HD, 1))
    return s_new, y.reshape(nb, RWKV_W)


def _rwkv_post_kernel(y_ref, bonus_ref, sg_ref, lg_ref, lb_ref, ones_ref, o_ref):
    y = y_ref[...]
    ones_bd = ones_ref[...]
    mean = _head_sum(y, ones_bd) * (1.0 / RWKV_HD)
    yc = y - mean
    var = _head_sum(yc * yc, ones_bd) * (1.0 / RWKV_HD)
    yn = yc * lax.rsqrt(var + GN_EPS) * lg_ref[...] + lb_ref[...]
    o_ref[...] = ((yn + bonus_ref[...]) * sg_ref[...]).astype(BF16)


def _rwkv_post(y, bonus, sg, ln_g, ln_b, ones_bd):
    t = y.shape[0]
    tm = min(512, t)
    row = pl.BlockSpec((tm, RWKV_W), lambda i: (i, 0))
    vec = pl.BlockSpec((1, RWKV_W), lambda i: (0, 0))
    return pl.pallas_call(
        _rwkv_post_kernel,
        grid=(t // tm,),
        in_specs=[row, row, row, vec, vec, pl.BlockSpec(ones_bd.shape, lambda i: (0, 0))],
        out_specs=row,
        out_shape=jax.ShapeDtypeStruct((t, RWKV_W), BF16),
        compiler_params=_cparams(("arbitrary",)),
        name="rwkv_post",
    )(y, bonus, sg, ln_g.reshape(1, -1), ln_b.reshape(1, -1), ones_bd)


def _outproj_kernel(att_ref, rw_ref, wa_ref, wr_ref, x_ref, gate_ref, o_ref):
    out = (jnp.dot(att_ref[...], wa_ref[...], preferred_element_type=F32)
           + jnp.dot(rw_ref[...], wr_ref[...], preferred_element_type=F32))
    o_ref[...] = x_ref[...] + gate_ref[...] * out


def _out_projection(att, rw, w_out_bf16, x, gate):
    t, d = x.shape
    tm = min(512, t)
    per_row = gate.shape[0] != 1
    gspec = (pl.BlockSpec((tm, d), lambda i: (i, 0)) if per_row else pl.BlockSpec((1, d), lambda i: (0, 0)))
    return pl.pallas_call(
        _outproj_kernel,
        grid=(t // tm,),
        in_specs=[pl.BlockSpec((tm, ATT_W), lambda i: (i, 0)),
                  pl.BlockSpec((tm, RWKV_W), lambda i: (i, 0)),
                  pl.BlockSpec((ATT_W, d), lambda i: (0, 0)),
                  pl.BlockSpec((RWKV_W, d), lambda i: (1, 0)),
                  pl.BlockSpec((tm, d), lambda i: (i, 0)),
                  gspec],
        out_specs=pl.BlockSpec((tm, d), lambda i: (i, 0)),
        out_shape=jax.ShapeDtypeStruct((t, d), F32),
        compiler_params=_cparams(("arbitrary",)),
        name="out_projection",
    )(att, rw, w_out_bf16, w_out_bf16, x, gate)


def _reorder_cols(w, axis=-1):
    axis = axis % w.ndim

    def cut(start, size):
        return lax.slice_in_dim(w, start, start + size, axis=axis)

    def zeros(size):
        shape = list(w.shape)
        shape[axis] = size
        return jnp.zeros(shape, w.dtype)

    q, k, v, g = (cut(i * ATT_W, ATT_W) for i in range(4))
    a = 4 * ATT_W
    iq, iw, ik = cut(a, IDX_W), cut(a + IDX_W, IDX_HEADS), cut(a + IDX_W + IDX_HEADS, IDX_HD)
    o = N_ATT_COLS_ORIG
    rr, rk, rv = (cut(o + i * RWKV_W, RWKV_W) for i in range(3))
    o2 = o + 3 * RWKV_W
    wl, al = cut(o2, DECAY_LORA), cut(o2 + DECAY_LORA, ICL_LORA)
    rg = cut(o2 + DECAY_LORA + ICL_LORA, RWKV_W)
    return jnp.concatenate([q, k, v, g, iq, rr, rk, rv, rg, ik, iw, zeros(LANES - IDX_HD - IDX_HEADS),
                            wl, al, zeros(N_COLS_PAD - N_COLS)], axis=axis)


def _reorder_cast_kernel(wt_ref, o_ref):
    o_ref[...] = _reorder_cols(wt_ref[...], axis=0).T.astype(BF16)


def _reorder_cast(wt):
    n, d = wt.shape
    td = min(256, d)
    return pl.pallas_call(
        _reorder_cast_kernel,
        grid=(d // td,),
        in_specs=[pl.BlockSpec((n, td), lambda i: (0, i))],
        out_specs=pl.BlockSpec((td, N_COLS_PAD), lambda i: (i, 0)),
        out_shape=jax.ShapeDtypeStruct((d, N_COLS_PAD), BF16),
        compiler_params=_cparams(("arbitrary",)),
        name="reorder_cast",
    )(wt)


def _rwkv_cols_original_order(p):
    return jnp.concatenate([p[..., COL_RR:COL_RR + RWKV_W], p[..., COL_RK:COL_RK + RWKV_W],
                            p[..., COL_RV:COL_RV + RWKV_W],
                            p[..., COL_SMALL_R:COL_SMALL_R + DECAY_LORA + ICL_LORA],
                            p[..., COL_RG:COL_RG + RWKV_W]], axis=-1)


def _blockdiag_to_state(sb):
    sb = sb.reshape(N_GROUPS, HEADS_PER_GROUP, RWKV_HD, HEADS_PER_GROUP, RWKV_HD)
    return jnp.stack([sb[:, h, :, h, :] for h in range(HEADS_PER_GROUP)], axis=1).reshape(
        RWKV_HEADS, RWKV_HD, RWKV_HD)


def kernel(x_prompt, x_sample, c_prompt, c_sample, cache_k, cache_v, cache_idx_k, state_wkv, state_shift,
           page_table, w_ada, b_ada, norm_g, w_in, q_norm_g, k_norm_g, shift_mu, w0, w_b, a0, a_b,
           k_k, k_a, r_k, ln_x_g, ln_x_b, w_out):
    depth = w_ada.shape[0]
    assert depth == 1, "single-layer trunk"
    bsz, seq, d = x_prompt.shape
    assert bsz == 1
    nb, dec_seq, _ = x_sample.shape
    assert dec_seq == 1
    n_pages = page_table.shape[1]
    past_len = n_pages * PAGE_SIZE
    l = 0

    w_in_r = _reorder_cast(w_in[l].T)
    w_out_b = w_out[l].astype(BF16)
    mu = _reorder_cols(jnp.concatenate([jnp.zeros((N_ATT_COLS_ORIG,), F32), shift_mu[l]]))
    mu_big = mu[COL_RR:COL_RR + 4 * RWKV_W].reshape(1, -1)
    mu_small = mu[COL_SMALL_R:COL_SMALL_R + LANES].reshape(1, -1)
    zl = jnp.zeros((DECAY_LORA, RWKV_W), F32)
    w_lora = jnp.concatenate([jnp.concatenate([w_b[l], zl], axis=1),
                              jnp.concatenate([zl, a_b[l]], axis=1)], axis=0)
    hid = jnp.arange(LANES) // RWKV_HD
    ones_bd = (hid[:, None] == hid[None, :]).astype(BF16)
    rwkv_consts = (mu_big, mu_small, w_lora, w0[l], a0[l], k_k[l], k_a[l], r_k[l].reshape(-1), ones_bd)

    n_mod = 1 + nb
    c_all = jnp.concatenate([c_prompt, c_sample, jnp.zeros((-n_mod % SUBLANES, d), F32)], axis=0)
    mod = _modulation(c_all, w_ada[l], b_ada[l])
    shift_p, scale_p, gate_p = mod[0:1, :d], mod[0:1, d:2 * d], mod[0:1, 2 * d:]
    shift_s, scale_s, gate_s = mod[1:n_mod, :d], mod[1:n_mod, d:2 * d], mod[1:n_mod, 2 * d:]

    xp = x_prompt.reshape(seq, d)
    xs = x_sample.reshape(nb, d)

    p = _in_projection(xp, shift_p, scale_p, norm_g[l], w_in_r)
    qt, k32, kb, vt, v32, iqb, ik32, ikb, wv = _attention_prep(p, _rope_tables(jnp.arange(seq)),
                                                         q_norm_g[l], k_norm_g[l], transposed=True)
    n_sel = min(TOPK_MAX, seq // 4)
    tq = min(MXU_DIM, seq)
    bias = _prompt_indexer(iqb, wv[:, IDX_HD:IDX_HD + IDX_HEADS].T, ikb, n_sel, tq)
    att = _prompt_attention(qt, kb, vt, bias, p, tq)

    r_, kp_, v_, kk_, b_, lw_, bonus, sg = _rwkv_prep(p, jnp.zeros((1, p.shape[1]), F32), rwkv_consts,
                                                      shifted=True)
    y_scan, s_fin = _rwkv_scan(r_, kp_, v_, kk_, b_, lw_,
                               jnp.zeros((N_GROUPS, MXU_DIM, MXU_DIM), F32))
    rw = _rwkv_post(y_scan, bonus, sg, ln_x_g[l], ln_x_b[l], ones_bd)
    y_p = _out_projection(att, rw, w_out_b, xp, gate_p)

    k_prompt = k32.reshape(1, 1, seq, ATT_HEADS, ATT_HD)
    v_prompt = v32.reshape(1, 1, seq, ATT_HEADS, ATT_HD)
    idx_k_prompt = ik32.reshape(1, 1, seq, IDX_HD)
    wkv_prompt = _blockdiag_to_state(s_fin).reshape(1, 1, RWKV_HEADS, RWKV_HD, RWKV_HD)
    shift_prompt = _rwkv_cols_original_order(p[seq - 1:seq]).reshape(1, 1, -1)

    ps = _in_projection(xs, shift_s, scale_s, norm_g[l], w_in_r)
    pos_s = jnp.full((nb,), past_len, jnp.int32)
    qb_s, k32_s, _, _, v32_s, iqb_s, ik32_s, ikb_s, wv_s = _attention_prep(ps, _rope_tables(pos_s),
                                                                          q_norm_g[l], k_norm_g[l],
                                                                          transposed=False)
    n_sel_s = min(TOPK_MAX, (past_len + 1) // 4)
    att_s = _sample_attention(page_table, qb_s, iqb_s, wv_s, ikb_s, k32_s, v32_s,
                              ps[:, COL_G:COL_G + ATT_W], cache_idx_k[l], cache_k[l], cache_v[l], n_sel_s)

    prev_s = _reorder_cols(jnp.concatenate([jnp.zeros((nb, N_ATT_COLS_ORIG), F32), state_shift[l]], axis=-1))
    r_s, kp_s, v_s, kk_s, b_s, lw_s, bonus_s, sg_s = _rwkv_prep(ps, prev_s, rwkv_consts, shifted=False)
    wkv_new, y_step = _rwkv_step(state_wkv[l], r_s, kp_s, kk_s, b_s, lw_s, v_s)
    rw_s = _rwkv_post(y_step, bonus_s, sg_s, ln_x_g[l], ln_x_b[l], ones_bd)
    y_s = _out_projection(att_s, rw_s, w_out_b, xs, gate_s)

    k_sample = k32_s.reshape(1, nb, 1, ATT_HEADS, ATT_HD)
    v_sample = v32_s.reshape(1, nb, 1, ATT_HEADS, ATT_HD)
    idx_k_sample = ik32_s.reshape(1, nb, 1, IDX_HD)
    wkv_sample = wkv_new.reshape(1, nb, RWKV_HEADS, RWKV_HD, RWKV_HD)
    shift_sample = _rwkv_cols_original_order(ps).reshape(1, nb, -1)

    return (y_p.reshape(1, seq, d), y_s.reshape(nb, 1, d),
            k_prompt, v_prompt, idx_k_prompt, wkv_prompt, shift_prompt,
            k_sample, v_sample, idx_k_sample, wkv_sample, shift_sample)
```

```python
import functools
import math

import jax
import jax.numpy as jnp
from jax import lax
from jax.experimental import pallas as pl
from jax.experimental.pallas import tpu as pltpu

F32 = jnp.float32
BF16 = jnp.bfloat16
I32 = jnp.int32

ATT_HEADS = 8
ATT_HD = 128
ATT_W = ATT_HEADS * ATT_HD
IDX_HEADS = 16
IDX_HD = 64
IDX_W = IDX_HEADS * IDX_HD
TOPK_MAX = 256
PAGE_SIZE = 128
ROPE_THETA = 10000.0
RWKV_HD = 64
RWKV_HEADS = 16
RWKV_W = RWKV_HEADS * RWKV_HD
DECAY_LORA = 64
ICL_LORA = 64
NORM_EPS = 1e-6
GN_EPS = 64e-5

LANES = 128
SUBLANES = 8
MXU_DIM = 256
VMEM_LIMIT = 56 * 1024 * 1024
NEG_BIG = -1e30
INT_MIN = -(2 ** 31)

COL_Q, COL_K, COL_V, COL_G, COL_IQ = 0, 1024, 2048, 3072, 4096
COL_RR, COL_RK, COL_RV, COL_RG = 5120, 6144, 7168, 8192
COL_SMALL_A = 9216
COL_SMALL_R = 9344
N_COLS = 9472
N_COLS_PAD = 9728
N_ATT_COLS_ORIG = 4 * ATT_W + IDX_W + IDX_HEADS + IDX_HD

FLASH_KEY_BLOCKS = 4
SCAN_CHUNK = 64
DECODE_PAGES_PER_STEP = 8
SCAN_SUB = 4
HEADS_PER_GROUP = MXU_DIM // RWKV_HD
N_GROUPS = RWKV_HEADS // HEADS_PER_GROUP


def _cparams(sem, vmem=VMEM_LIMIT):
    return pltpu.CompilerParams(dimension_semantics=sem, vmem_limit_bytes=vmem)


def _split2(x):
    hi = x.astype(BF16)
    lo = (x - hi.astype(F32)).astype(BF16)
    return hi, lo


def _split3(x):
    hi = x.astype(BF16)
    r = x - hi.astype(F32)
    mid = r.astype(BF16)
    lo = (r - mid.astype(F32)).astype(BF16)
    return hi, mid, lo


_NN = (((1,), (0,)), ((), ()))
_NT = (((1,), (1,)), ((), ()))
_TN = (((0,), (0,)), ((), ()))


def _dg(a, b, dims):
    return lax.dot_general(a, b, dims, preferred_element_type=F32)


def _dot1(a, b, dims=_NN):
    return _dg(a.astype(BF16), b.astype(BF16), dims)


def _dot3(a, b, dims=_NN):
    ah, al = _split2(a)
    bh, bl = _split2(b)
    return _dg(ah, bh, dims) + _dg(al, bh, dims) + _dg(ah, bl, dims)


def _dot_exact_lhs(a_bf16, b, dims=_NN):
    b0, b1, b2 = _split3(b)
    return _dg(a_bf16, b0, dims) + _dg(a_bf16, b1, dims) + _dg(a_bf16, b2, dims)


def _dot_exact_rhs(a, b_bf16, dims=_NN):
    a0, a1, a2 = _split3(a)
    return _dg(a0, b_bf16, dims) + _dg(a1, b_bf16, dims) + _dg(a2, b_bf16, dims)


def _silu(x):
    return x * jax.nn.sigmoid(x)


def _head_sum(x, ones_bd):
    parts = []
    for j in range(x.shape[1] // LANES):
        parts.append(_dot_exact_rhs(x[:, j * LANES:(j + 1) * LANES], ones_bd))
    return jnp.concatenate(parts, axis=1)


def _ordered_to_f32(key):
    return pltpu.bitcast(key ^ ((key >> 31) & 0x7FFFFFFF), F32)


def _mod_kernel(c_ref, w_ref, b_ref, o_ref):
    o_ref[...] = _dot3(_silu(c_ref[...]), w_ref[...]) + b_ref[...]


def _modulation(c, w_ada, b_ada):
    m, d = c.shape
    n = w_ada.shape[1]
    tn = 768
    return pl.pallas_call(
        _mod_kernel,
        grid=(n // tn,),
        in_specs=[pl.BlockSpec((m, d), lambda j: (0, 0)),
                  pl.BlockSpec((d, tn), lambda j: (0, j)),
                  pl.BlockSpec((1, tn), lambda j: (0, j))],
        out_specs=pl.BlockSpec((m, tn), lambda j: (0, j)),
        out_shape=jax.ShapeDtypeStruct((m, n), F32),
        compiler_params=_cparams(("arbitrary",)),
        name="modulation",
    )(c, w_ada, b_ada.reshape(1, n))


def _inproj_kernel(x_ref, shift_ref, scale_ref, g_ref, w_ref, o_ref, h_ref, *, rows):
    @pl.when(pl.program_id(1) == 0)
    def _():
        def body(r, c):
            sl = pl.ds(pl.multiple_of(r * rows, rows), rows)
            x = x_ref[sl, :]
            ms = jnp.mean(x * x, axis=-1, keepdims=True)
            y = x * lax.rsqrt(ms + NORM_EPS) * g_ref[...]
            if shift_ref.shape[0] == 1:
                h = y * (1.0 + scale_ref[...]) + shift_ref[...]
            else:
                h = y * (1.0 + scale_ref[sl, :]) + shift_ref[sl, :]
            h_ref[sl, :] = h.astype(BF16)
            return c
        lax.fori_loop(0, x_ref.shape[0] // rows, body, 0)

    o_ref[...] = jnp.dot(h_ref[...], w_ref[...], preferred_element_type=F32)


def _in_projection(x, shift, scale, norm_g, w_bf16):
    t, d = x.shape
    n = w_bf16.shape[1]
    tm = min(1024, t)
    tn = 512
    rows = min(128, tm)
    per_row = shift.shape[0] != 1
    mod_spec = (pl.BlockSpec((tm, d), lambda i, j: (i, 0)) if per_row
                else pl.BlockSpec((1, d), lambda i, j: (0, 0)))
    return pl.pallas_call(
        functools.partial(_inproj_kernel, rows=rows),
        grid=(t // tm, n // tn),
        in_specs=[pl.BlockSpec((tm, d), lambda i, j: (i, 0)),
                  mod_spec, mod_spec,
                  pl.BlockSpec((1, d), lambda i, j: (0, 0)),
                  pl.BlockSpec((d, tn), lambda i, j: (0, j))],
        out_specs=pl.BlockSpec((tm, tn), lambda i, j: (i, j)),
        out_shape=jax.ShapeDtypeStruct((t, n), F32),
        scratch_shapes=[pltpu.VMEM((tm, d), BF16)],
        compiler_params=_cparams(("arbitrary", "arbitrary")),
        name="in_projection",
    )(x, shift, scale, norm_g.reshape(1, d), w_bf16)


def _rope128(x, cos, sin_signed):
    return x * cos + pltpu.roll(x, 64, axis=1) * sin_signed


def _rope64(x, cos, sin_signed, first_half):
    rot = jnp.where(first_half, pltpu.roll(x, 96, axis=1), pltpu.roll(x, 32, axis=1))
    return x * cos + rot * sin_signed


def _prep_kernel(q_ref, k_ref, v_ref, iq_ref, sm_ref, c128_ref, s128_ref, c64_ref, s64_ref,
                 qg_ref, kg_ref,
                 qo_ref, k32_ref, kb_ref, vo_ref, v32_ref, iqb_ref, ik32_ref, ikb_ref, w_ref, *, transposed):
    c128, s128 = c128_ref[...], s128_ref[...]
    c64, s64 = c64_ref[...], s64_ref[...]
    tm = q_ref.shape[0]
    lane = lax.broadcasted_iota(I32, (tm, LANES), 1)
    first_half = (lane % IDX_HD) < (IDX_HD // 2)
    q_scale = ATT_HD ** -0.5 * math.log2(math.e)
    for h in range(ATT_HEADS):
        hs = slice(h * ATT_HD, (h + 1) * ATT_HD)
        q = q_ref[:, hs]
        q = q * lax.rsqrt(jnp.mean(q * q, axis=-1, keepdims=True) + NORM_EPS) * qg_ref[...]
        q = _rope128(q, c128, s128) * q_scale
        k = k_ref[:, hs]
        k = k * lax.rsqrt(jnp.mean(k * k, axis=-1, keepdims=True) + NORM_EPS) * kg_ref[...]
        k = _rope128(k, c128, s128)
        k32_ref[:, h, :] = k
        kb_ref[:, hs] = k.astype(BF16)
        if transposed:
            qo_ref[hs, :] = q.T.astype(BF16)
            vo_ref[hs, :] = v_ref[:, hs].T.astype(BF16)
        else:
            qo_ref[:, hs] = q.astype(BF16)
            vo_ref[:, hs] = v_ref[:, hs].astype(BF16)
    for h in range(ATT_HEADS):
        v32_ref[:, h, :] = v_ref[:, h * ATT_HD:(h + 1) * ATT_HD]
    for j in range(IDX_W // LANES):
        js = slice(j * LANES, (j + 1) * LANES)
        iqb_ref[:, js] = _rope64(iq_ref[:, js], c64, s64, first_half).astype(BF16)
    sm = sm_ref[...]
    ik = _rope64(sm, c64, s64, first_half)[:, :IDX_HD]
    ik32_ref[...] = ik
    ikb_ref[...] = ik.astype(BF16)
    w_ref[...] = sm * ((IDX_HEADS ** -0.5) * (IDX_HD ** -0.5))


def _attention_prep(p, tables, q_norm_g, k_norm_g, transposed):
    t = p.shape[0]
    tm = min(256, t)
    c128, s128, c64, s64 = tables
    wide = lambda col: pl.BlockSpec((tm, 1024), lambda i, c=col // 1024: (i, c))
    tab = pl.BlockSpec((tm, LANES), lambda i: (i, 0))
    gain = pl.BlockSpec((1, ATT_HD), lambda i: (0, 0))
    row1024 = pl.BlockSpec((tm, 1024), lambda i: (i, 0))
    col1024 = pl.BlockSpec((1024, tm), lambda i: (0, i))
    row64 = pl.BlockSpec((tm, IDX_HD), lambda i: (i, 0))
    head3 = pl.BlockSpec((tm, ATT_HEADS, ATT_HD), lambda i: (i, 0, 0))
    if transposed:
        qv_spec, qv_shape = col1024, jax.ShapeDtypeStruct((ATT_W, t), BF16)
    else:
        qv_spec, qv_shape = row1024, jax.ShapeDtypeStruct((t, ATT_W), BF16)
    return pl.pallas_call(
        functools.partial(_prep_kernel, transposed=transposed),
        grid=(t // tm,),
        in_specs=[wide(COL_Q), wide(COL_K), wide(COL_V), wide(COL_IQ),
                  pl.BlockSpec((tm, LANES), lambda i: (i, COL_SMALL_A // LANES)),
                  tab, tab, tab, tab, gain, gain],
        out_specs=[qv_spec, head3, row1024, qv_spec, head3, row1024, row64, row64, tab],
        out_shape=[qv_shape,
                   jax.ShapeDtypeStruct((t, ATT_HEADS, ATT_HD), F32),
                   jax.ShapeDtypeStruct((t, ATT_W), BF16),
                   qv_shape,
                   jax.ShapeDtypeStruct((t, ATT_HEADS, ATT_HD), F32),
                   jax.ShapeDtypeStruct((t, IDX_W), BF16),
                   jax.ShapeDtypeStruct((t, IDX_HD), F32),
                   jax.ShapeDtypeStruct((t, IDX_HD), BF16),
                   jax.ShapeDtypeStruct((t, LANES), F32)],
        compiler_params=_cparams(("arbitrary",)),
        name="attention_prep",
    )(p, p, p, p, p, c128, s128, c64, s64, q_norm_g.reshape(1, ATT_HD), k_norm_g.reshape(1, ATT_HD))


def _rope_tables(pos):
    pos = pos.astype(F32)[:, None]

    def tab(dh):
        inv = ROPE_THETA ** (-jnp.arange(0, dh, 2, dtype=F32) / dh)
        ang = pos * inv[None, :]
        cos, sin = jnp.cos(ang), jnp.sin(ang)
        c = jnp.concatenate([cos, cos], axis=-1)
        s = jnp.concatenate([-sin, sin], axis=-1)
        rep = LANES // dh
        return jnp.tile(c, (1, rep)), jnp.tile(s, (1, rep))

    c128, s128 = tab(ATT_HD)
    c64, s64 = tab(IDX_HD)
    return c128, s128, c64, s64


def _indexer_kernel(iq_ref, wt_ref, ik_ref, bias_ref, iqs_ref, cut_ref, *, n_sel):
    i = pl.program_id(0)
    tq = iq_ref.shape[0]
    tk = bias_ref.shape[2]
    nk_total = bias_ref.shape[1]
    nkb = i + 1

    for h in range(IDX_HEADS):
        iqs_ref[h * tq:(h + 1) * tq, :] = iq_ref[:, h * IDX_HD:(h + 1) * IDX_HD]

    key_pos = lax.broadcasted_iota(I32, (tk, tq), 0)
    qry_pos = lax.broadcasted_iota(I32, (tk, tq), 1) + i * tq

    def score_body(kb, c):
        ikt = ik_ref[pl.ds(pl.multiple_of(kb * tk, tk), tk), :]
        d = _dg(ikt, iqs_ref[...], _NT)
        acc = jnp.zeros((tk, tq), F32)
        for h in range(IDX_HEADS):
            acc = acc + jnp.maximum(d[:, h * tq:(h + 1) * tq], 0.0) * wt_ref[h:h + 1, :]
        bias_ref[0, kb] = jnp.where(key_pos + kb * tk <= qry_pos, acc, -jnp.inf)
        return c

    def score_pair(j, c):
        score_body(2 * j, c)
        return score_body(2 * j + 1, c)

    lax.fori_loop(0, nkb // 2, score_pair, 0)
    lax.fori_loop(2 * (nkb // 2), nkb, score_body, 0)

    def count_keys(pred):
        def one(kb, cnt):
            x = bias_ref[0, kb].reshape(tk // SUBLANES, SUBLANES, tq)
            pos = (key_pos + kb * tk).reshape(tk // SUBLANES, SUBLANES, tq)
            return cnt + jnp.sum(jnp.where(pred(x, pos), 1.0, 0.0), axis=0)

        def four(j, cnt):
            for u in range(4):
                cnt = one(4 * j + u, cnt)
            return cnt

        cnt = lax.fori_loop(0, nkb // 4, four, jnp.zeros((SUBLANES, tq), F32))
        cnt = lax.fori_loop(4 * (nkb // 4), nkb, one, cnt)
        return jnp.sum(cnt, axis=0, keepdims=True)

    def bit_body(bi, carry):
        ans, tot_ans = carry
        cand = ans + jnp.left_shift(jnp.int32(1), 31 - bi)
        cand_f = _ordered_to_f32(cand)
        tot = count_keys(lambda x, pos: x >= cand_f[None])
        ok = tot >= n_sel
        return jnp.where(ok, cand, ans), jnp.where(ok, tot, tot_ans)

    ans, tot_ans = lax.fori_loop(0, 32, bit_body, (jnp.full((SUBLANES, tq), INT_MIN, I32),
                                                   jnp.zeros((1, tq), F32)))
    thr = _ordered_to_f32(ans)[0:1, :]
    thr = jnp.where(thr != thr, -jnp.inf, thr)

    cut_ref[...] = jnp.full(cut_ref.shape, 2 ** 31 - 1, I32)

    @pl.when(jnp.max(tot_ans) > n_sel)
    def _():
        need = n_sel - count_keys(lambda x, pos: x > thr[None])
        n_bits = max(1, (nk_total * tk - 1).bit_length())

        def cut_body(bi, cut):
            cand = cut + jnp.left_shift(jnp.int32(1), n_bits - 1 - bi)
            below = count_keys(lambda x, pos: (x == thr[None]) & (pos < cand[None]))
            return jnp.where(below < need, cand, cut)

        cut = lax.fori_loop(0, n_bits, cut_body, jnp.zeros((1, tq), I32))
        cut_ref[...] = jnp.broadcast_to(jnp.where(tot_ans > n_sel, cut, 2 ** 31 - 1), cut_ref.shape)

    has_surplus_ties = jnp.max(tot_ans) > n_sel
    cut = cut_ref[0:1, :]

    def bias_body(kb, c):
        keep = (bias_ref[0, kb] >= thr) & (key_pos + kb * tk <= qry_pos)
        bias_ref[0, kb] = jnp.where(keep, 0.0, NEG_BIG)
        return c

    def bias_body_ties(kb, c):
        x = bias_ref[0, kb]
        pos = key_pos + kb * tk
        keep = ((x > thr) | ((x == thr) & (pos <= cut))) & (pos <= qry_pos)
        bias_ref[0, kb] = jnp.where(keep, 0.0, NEG_BIG)
        return c

    @pl.when(jnp.logical_not(has_surplus_ties))
    def _():
        lax.fori_loop(0, nkb, bias_body, 0)

    @pl.when(has_surplus_ties)
    def _():
        lax.fori_loop(0, nkb, bias_body_ties, 0)

    def fill_body(kb, c):
        bias_ref[0, kb] = jnp.full((tk, tq), NEG_BIG, F32)
        return c

    lax.fori_loop(nkb, nk_total, fill_body, 0)


def _prompt_indexer(iqb, wt, ikb, n_sel, tq):
    t = iqb.shape[0]
    nq = t // tq
    return pl.pallas_call(
        functools.partial(_indexer_kernel, n_sel=n_sel),
        grid=(nq,),
        in_specs=[pl.BlockSpec((tq, IDX_W), lambda i: (i, 0)),
                  pl.BlockSpec((IDX_HEADS, tq), lambda i: (0, i)),
                  pl.BlockSpec((t, IDX_HD), lambda i: (0, 0))],
        out_specs=pl.BlockSpec((1, nq, tq, tq), lambda i: (i, 0, 0, 0)),
        out_shape=jax.ShapeDtypeStruct((nq, nq, tq, tq), F32),
        scratch_shapes=[pltpu.VMEM((IDX_HEADS * tq, IDX_HD), BF16),
                        pltpu.VMEM((SUBLANES, tq), I32)],
        compiler_params=_cparams(("arbitrary",)),
        name="prompt_indexer",
    )(iqb, wt, ikb)


def _flash_kernel(qi_ref, ki_ref, qt_ref, k_ref, vt_ref, bias_ref, g_ref, o_ref, m_ref, l_ref, acc_ref):
    step = pl.program_id(0)
    qi, ki = qi_ref[step], ki_ref[step]

    @pl.when(ki == 0)
    def _():
        m_ref[...] = jnp.full(m_ref.shape, NEG_BIG, F32)
        l_ref[...] = jnp.zeros(l_ref.shape, F32)
        acc_ref[...] = jnp.zeros(acc_ref.shape, F32)

    heads = [slice(h * ATT_HD, (h + 1) * ATT_HD) for h in range(ATT_HEADS)]
    tq = qt_ref.shape[1]
    tkb = bias_ref.shape[2]
    for kh in range(bias_ref.shape[1]):
        ks = slice(kh * tkb, (kh + 1) * tkb)
        for qh in range(tq // LANES):
            qs = slice(qh * LANES, (qh + 1) * LANES)
            bias = bias_ref[0, kh, :, qs]
            m_prev, l_prev = m_ref[:, qs], l_ref[:, qs]
            m_rows, l_rows = [], []
            for h, hs in enumerate(heads):
                s = _dg(k_ref[ks, hs], qt_ref[hs, qs], _NN) + bias
                m_old = m_prev[h:h + 1, :]
                m_new = jnp.maximum(m_old, jnp.max(s, axis=0, keepdims=True))
                alpha = jnp.exp2(m_old - m_new)
                p = jnp.exp2(s - m_new)
                l_rows.append(alpha * l_prev[h:h + 1, :] + jnp.sum(p, axis=0, keepdims=True))
                m_rows.append(m_new)
                acc_ref[h, :, qs] = alpha * acc_ref[h, :, qs] + _dg(vt_ref[hs, ks], p.astype(BF16), _NN)
            m_ref[:, qs] = jnp.concatenate(m_rows, axis=0)
            l_ref[:, qs] = jnp.concatenate(l_rows, axis=0)

    @pl.when(ki == qi // bias_ref.shape[1])
    def _():
        for h in range(ATT_HEADS):
            hs = slice(h * ATT_HD, (h + 1) * ATT_HD)
            out = (acc_ref[h] / l_ref[h:h + 1, :]).T
            o_ref[:, hs] = (out * _silu(g_ref[:, hs])).astype(BF16)


def _prompt_attention(qt, kb, vt, bias, p, tq):
    t = kb.shape[0]
    nq = t // tq
    kpb = min(FLASH_KEY_BLOCKS, nq)
    tk = kpb * tq
    pairs = [(i, j) for i in range(nq) for j in range(i // kpb + 1)]
    qi = jnp.asarray([a for a, _ in pairs], I32)
    ki = jnp.asarray([b for _, b in pairs], I32)
    grid_spec = pltpu.PrefetchScalarGridSpec(
        num_scalar_prefetch=2,
        grid=(len(pairs),),
        in_specs=[pl.BlockSpec((ATT_W, tq), lambda s, qi, ki: (0, qi[s])),
                  pl.BlockSpec((tk, ATT_W), lambda s, qi, ki: (ki[s], 0)),
                  pl.BlockSpec((ATT_W, tk), lambda s, qi, ki: (0, ki[s])),
                  pl.BlockSpec((1, kpb, tq, tq), lambda s, qi, ki: (qi[s], ki[s], 0, 0)),
                  pl.BlockSpec((tq, ATT_W), lambda s, qi, ki: (qi[s], COL_G // ATT_W))],
        out_specs=pl.BlockSpec((tq, ATT_W), lambda s, qi, ki: (qi[s], 0)),
        scratch_shapes=[pltpu.VMEM((ATT_HEADS, tq), F32),
                        pltpu.VMEM((ATT_HEADS, tq), F32),
                        pltpu.VMEM((ATT_HEADS, ATT_HD, tq), F32)],
    )
    return pl.pallas_call(
        _flash_kernel,
        grid_spec=grid_spec,
        out_shape=jax.ShapeDtypeStruct((t, ATT_W), BF16),
        compiler_params=_cparams(("arbitrary",)),
        name="prompt_attention",
    )(qi, ki, qt, kb, vt, bias, p)


def _sample_attn_kernel(pt_ref, q_ref, iq_ref, w_ref, ikn_ref, kn_ref, vn_ref, g_ref, ex_ref,
                        cik_ref, ck_ref, cv_ref, o_ref,
                        ikbuf, kbuf, vbuf, sce_ref, isem, ksem, vsem, *, n_sel, pps):
    b = pl.program_id(0)
    n_pages = ikbuf.shape[0]
    page_rows = PAGE_SIZE * ATT_HEADS

    n_steps = n_pages // pps
    nb = pl.num_programs(0)
    cross_prefetch = n_steps % 2 == 0

    def idx_copy(seq, j):
        return pltpu.make_async_copy(cik_ref.at[pt_ref[seq, j]], ikbuf.at[j], isem.at[j])

    def k_copy(seq, j, slot):
        return pltpu.make_async_copy(ck_ref.at[pt_ref[seq, j]], kbuf.at[slot], ksem.at[slot])

    def v_copy(seq, j, slot):
        return pltpu.make_async_copy(cv_ref.at[pt_ref[seq, j]], vbuf.at[slot], vsem.at[slot])

    def start_idx(seq):
        def body(j, c):
            idx_copy(seq, j).start()
            return c
        lax.fori_loop(0, n_pages, body, 0)

    def start_group(seq, t, grp):
        for i in range(pps):
            k_copy(seq, t * pps + i, grp * pps + i).start()
            v_copy(seq, t * pps + i, grp * pps + i).start()

    if cross_prefetch:
        @pl.when(b == 0)
        def _():
            start_idx(0)
            start_group(0, 0, 0)
        start_group(b, 1, 1)
    else:
        @pl.when(b == 0)
        def _():
            start_idx(0)
        start_group(b, 0, 0)

    iq = iq_ref[0]
    w = w_ref[0]

    def wait_idx(j, c):
        idx_copy(b, j).wait()
        return c

    lax.fori_loop(0, n_pages, wait_idx, 0)
    iq_pages = jnp.broadcast_to(iq[None], (n_pages, IDX_HEADS, IDX_HD))
    d = lax.dot_general(iq_pages, ikbuf[...].astype(BF16), (((2,), (1,)), ((0,), (0,))),
                        preferred_element_type=F32)
    sc = jnp.sum(jnp.maximum(d, 0.0) * w[None], axis=1)
    d_self = jnp.sum(iq.astype(F32) * ikn_ref[0].astype(F32), axis=-1, keepdims=True)
    s_self = jnp.sum(jnp.maximum(d_self, 0.0) * w, axis=0, keepdims=True)

    pos = (lax.broadcasted_iota(I32, sc.shape, 0) * PAGE_SIZE + lax.broadcasted_iota(I32, sc.shape, 1))
    self_pos = n_pages * PAGE_SIZE

    def count_keys(pred):
        cnt = jnp.sum(jnp.where(pred(sc, pos), 1.0, 0.0), axis=-1, keepdims=True)
        return jnp.sum(cnt, axis=0, keepdims=True) + jnp.where(pred(s_self, self_pos), 1.0, 0.0)

    def digit_body(it, carry):
        ans, tot_ans = carry
        step = jnp.left_shift(jnp.int32(1), 30 - 2 * it)
        best, best_tot = ans, tot_ans
        cand = ans
        for _ in range(3):
            cand = cand + step
            cand_f = _ordered_to_f32(cand)
            tot = count_keys(lambda x, p: x >= cand_f)
            best = jnp.where(tot >= n_sel, cand, best)
            best_tot = jnp.where(tot >= n_sel, tot, best_tot)
        return best, best_tot

    ans, tot_ans = lax.fori_loop(0, 16, digit_body, (jnp.full((1, 1), INT_MIN, I32), jnp.zeros((1, 1), F32)))
    thr = _ordered_to_f32(ans)
    thr = jnp.where(thr != thr, -jnp.inf, thr)

    def tie_cut():
        need = n_sel - count_keys(lambda x, p: x > thr)
        n_bits = self_pos.bit_length()

        def cut_body(bi, cut):
            cand = cut + jnp.left_shift(jnp.int32(1), n_bits - 1 - bi)
            below = count_keys(lambda x, p: (x == thr) & (p < cand))
            return jnp.where(below < need, cand, cut)

        return lax.fori_loop(0, n_bits, cut_body, jnp.zeros((1, 1), I32))

    cut = lax.cond(tot_ans[0, 0] > n_sel, tie_cut, lambda: jnp.full((1, 1), 2 ** 31 - 1, I32))

    def selected(x, p):
        return (x > thr) | ((x == thr) & (p <= cut))

    sce_ref[...] = _dot_exact_rhs(sc, ex_ref[...])

    @pl.when(b + 1 < nb)
    def _():
        start_idx(b + 1)

    q8 = q_ref[0]
    lane_e = lax.broadcasted_iota(I32, (ATT_HEADS, page_rows), 1)
    own = lane_e % ATT_HEADS == lax.broadcasted_iota(I32, (ATT_HEADS, page_rows), 0)
    tok_lane = lane_e[0:1, :] // ATT_HEADS

    kn = kn_ref[0].astype(BF16).astype(F32)
    s0 = jnp.sum(q8.astype(F32) * kn, axis=-1, keepdims=True)
    m0 = jnp.where(selected(s_self, self_pos), s0, NEG_BIG)
    l0 = jnp.where(m0 > 0.5 * NEG_BIG, 1.0, 0.0)
    acc0 = l0 * vn_ref[0].astype(BF16).astype(F32)

    def att_body(t, carry):
        m_prev, l_prev, acc = carry
        grp = t % 2

        if cross_prefetch:
            @pl.when((t >= 1) & (t + 1 < n_steps))
            def _():
                start_group(b, t + 1, 1 - grp)

            @pl.when((t + 1 == n_steps) & (b + 1 < nb))
            def _():
                start_group(b + 1, 0, 0)
        else:
            @pl.when(t + 1 < n_steps)
            def _():
                start_group(b, t + 1, 1 - grp)

        for i in range(pps):
            k_copy(b, t * pps + i, grp * pps + i).wait()
            v_copy(b, t * pps + i, grp * pps + i).wait()
        s = []
        for i in range(pps):
            kf = kbuf[grp * pps + i].reshape(page_rows, ATT_HD).astype(BF16)
            keep = own & selected(sce_ref[pl.ds(t * pps + i, 1), :], (t * pps + i) * PAGE_SIZE + tok_lane)
            s.append(jnp.where(keep, _dg(q8, kf, _NT), NEG_BIG))
        m_new = m_prev
        for x in s:
            m_new = jnp.maximum(m_new, jnp.max(x, axis=-1, keepdims=True))
        alpha = jnp.exp2(m_prev - m_new)
        p = [jnp.exp2(x - m_new) for x in s]
        l_new = alpha * l_prev
        for x in p:
            l_new = l_new + jnp.sum(x, axis=-1, keepdims=True)
        pv = [jnp.dot(p[i].astype(BF16), vbuf[grp * pps + i].reshape(page_rows, ATT_HD).astype(BF16),
                      preferred_element_type=F32) for i in range(pps)]
        acc = alpha * acc
        for x in pv:
            acc = acc + x
        return m_new, l_new, acc

    m, l, acc = lax.fori_loop(0, n_steps, att_body, (m0, l0, acc0))
    o_ref[0] = acc / l * _silu(g_ref[0])


def _sample_attention(page_table, qb, iqb, wv, ikb, k32, v32, g, cache_idx_k, cache_k, cache_v, n_sel):
    nb, n_pages = page_table.shape
    head3 = pl.BlockSpec((1, ATT_HEADS, ATT_HD), lambda b, pt: (b, 0, 0))
    any_spec = pl.BlockSpec(memory_space=pl.ANY)
    page_rows = PAGE_SIZE * ATT_HEADS
    pps = math.gcd(DECODE_PAGES_PER_STEP, n_pages)
    expand = (jnp.arange(page_rows)[None, :] // ATT_HEADS == jnp.arange(PAGE_SIZE)[:, None]).astype(BF16)
    grid_spec = pltpu.PrefetchScalarGridSpec(
        num_scalar_prefetch=1,
        grid=(nb,),
        in_specs=[head3,
                  pl.BlockSpec((1, IDX_HEADS, IDX_HD), lambda b, pt: (b, 0, 0)),
                  pl.BlockSpec((1, IDX_HEADS, 1), lambda b, pt: (b, 0, 0)),
                  pl.BlockSpec((1, 1, IDX_HD), lambda b, pt: (b, 0, 0)),
                  head3, head3, head3,
                  pl.BlockSpec((PAGE_SIZE, page_rows), lambda b, pt: (0, 0)),
                  any_spec, any_spec, any_spec],
        out_specs=head3,
        scratch_shapes=[pltpu.VMEM((n_pages, IDX_HD, PAGE_SIZE), F32),
                        pltpu.VMEM((2 * pps, PAGE_SIZE, ATT_HEADS, ATT_HD), F32),
                        pltpu.VMEM((2 * pps, PAGE_SIZE, ATT_HEADS, ATT_HD), F32),
                        pltpu.VMEM((n_pages, page_rows), F32),
                        pltpu.SemaphoreType.DMA((n_pages,)),
                        pltpu.SemaphoreType.DMA((2 * pps,)),
                        pltpu.SemaphoreType.DMA((2 * pps,))],
    )
    head = lambda a: a.reshape(nb, ATT_HEADS, ATT_HD)
    out = pl.pallas_call(
        functools.partial(_sample_attn_kernel, n_sel=n_sel, pps=pps),
        grid_spec=grid_spec,
        out_shape=jax.ShapeDtypeStruct((nb, ATT_HEADS, ATT_HD), F32),
        compiler_params=_cparams(("arbitrary",)),
        name="sample_attention",
    )(page_table, head(qb),
      iqb.reshape(nb, IDX_HEADS, IDX_HD),
      wv[:, IDX_HD:IDX_HD + IDX_HEADS].reshape(nb, IDX_HEADS, 1),
      ikb.reshape(nb, 1, IDX_HD),
      head(k32), head(v32), head(g), expand,
      jnp.swapaxes(cache_idx_k, 1, 2), cache_k, cache_v)
    return out.reshape(nb, ATT_W).astype(BF16)


def _rwkv_prep_kernel(*refs, shifted):
    n_x = 5
    x_refs = refs[:n_x]
    if shifted:
        halo_refs, first_refs = refs[n_x:2 * n_x], refs[2 * n_x:3 * n_x]
        rest = refs[3 * n_x:]
    else:
        prev_refs = refs[n_x:2 * n_x]
        rest = refs[2 * n_x:]
    (mu_ref, mus_ref, wl_ref, w0_ref, a0_ref, kk_ref, ka_ref, rk_ref, ones_ref,
     ro_ref, kpo_ref, vo_ref, kko_ref, bo_ref, lwo_ref, bonus_ref, sg_ref) = rest

    def lerp(j, mu):
        x = x_refs[j][...]
        if shifted:
            above = jnp.where(pl.program_id(0) == 0, first_refs[j][...],
                              halo_refs[j][SUBLANES - 1:SUBLANES, :])
            row = lax.broadcasted_iota(I32, x.shape, 0)
            prev = jnp.where(row == 0, above, pltpu.roll(x, 1, axis=0))
        else:
            prev = prev_refs[j][...]
        return x + mu * (prev - x)

    w = RWKV_W
    xr = lerp(0, mu_ref[:, 0:w])
    xk = lerp(1, mu_ref[:, w:2 * w])
    xv = lerp(2, mu_ref[:, 2 * w:3 * w])
    xg = lerp(3, mu_ref[:, 3 * w:4 * w])
    xs = lerp(4, mus_ref[...])
    lane = lax.broadcasted_iota(I32, xs.shape, 1)
    lora_in = jnp.where(lane < DECAY_LORA, jnp.tanh(xs), xs)
    lora = _dot3(lora_in, wl_ref[...])
    z = w0_ref[...] + lora[:, :w]
    softplus_neg = jnp.maximum(-z, 0.0) + jnp.log(1.0 + jnp.exp(-jnp.abs(z)))
    w_log = -softplus_neg - 0.5
    lwo_ref[...] = -jnp.exp(w_log)
    a = jax.nn.sigmoid(a0_ref[...] + lora[:, w:])
    ones_bd = ones_ref[...]
    kk = xk * kk_ref[...]
    kk = kk / jnp.maximum(jnp.sqrt(_head_sum(kk * kk, ones_bd)), 1e-12)
    kp = xk * (1.0 + (a - 1.0) * ka_ref[...])
    ro_ref[...] = xr
    kpo_ref[...] = kp
    vo_ref[...] = xv
    kko_ref[...] = kk
    bo_ref[...] = kk * a
    bonus_ref[...] = _head_sum(xr * kp * rk_ref[...], ones_bd) * xv
    sg_ref[...] = _silu(xg)


def _rwkv_prep(p, prev, consts, shifted):
    t = p.shape[0]
    tm = min(256, t)
    mu_big, mu_small, w_lora, w0, a0, k_k, k_a, r_k, ones_bd = consts
    cols = [(RWKV_W, COL_RR // RWKV_W), (RWKV_W, COL_RK // RWKV_W), (RWKV_W, COL_RV // RWKV_W),
            (RWKV_W, COL_RG // RWKV_W), (LANES, COL_SMALL_R // LANES)]
    x_specs = [pl.BlockSpec((tm, wd), lambda i, c=c: (i, c)) for wd, c in cols]
    if shifted:
        per8 = tm // SUBLANES
        halo = [pl.BlockSpec((SUBLANES, wd), lambda i, c=c: (jnp.maximum(i * per8 - 1, 0), c))
                for wd, c in cols]
        first = [pl.BlockSpec((1, wd), lambda i, c=c: (0, c)) for wd, c in cols]
        extra_specs, extra_args = halo + first, [p] * 5 + [prev] * 5
    else:
        extra_specs, extra_args = x_specs, [prev] * 5
    full = lambda a: pl.BlockSpec(a.shape, lambda i: (0, 0))
    row = pl.BlockSpec((tm, RWKV_W), lambda i: (i, 0))
    vec = lambda a: a.reshape(1, -1)
    args = [mu_big, mu_small, w_lora, vec(w0), vec(a0), vec(k_k), vec(k_a), vec(r_k), ones_bd]
    return pl.pallas_call(
        functools.partial(_rwkv_prep_kernel, shifted=shifted),
        grid=(t // tm,),
        in_specs=x_specs + extra_specs + [full(a) for a in args],
        out_specs=[row] * 8,
        out_shape=[jax.ShapeDtypeStruct((t, RWKV_W), F32)] * 8,
        compiler_params=_cparams(("arbitrary",)),
        name="rwkv_prep",
    )(*([p] * 5), *extra_args, *args)


def _scan_kernel(r_ref, kp_ref, v_ref, kk_ref, b_ref, lw_ref, s0_ref, y_ref, st_ref, s_ref):
    @pl.when(pl.program_id(0) == 0)
    def _():
        s_ref[...] = s0_ref[...]

    ch = SCAN_CHUNK
    assert ch == RWKV_HD
    rows_total = r_ref.shape[0]
    n_sub = rows_total // ch
    gw = MXU_DIM
    nh = HEADS_PER_GROUP
    lw = lw_ref[...]
    tr = lax.broadcasted_iota(I32, (rows_total, rows_total), 0)
    tc = lax.broadcasted_iota(I32, (rows_total, rows_total), 1)
    tri = jnp.where((tr >= tc) & (tr // ch == tc // ch), 1.0, 0.0).astype(BF16)
    cum = _dot_exact_lhs(tri, lw)
    cum_end = jnp.concatenate(
        [jnp.broadcast_to(cum[(c + 1) * ch - 1:(c + 1) * ch, :], (ch, RWKV_W)) for c in range(n_sub)], axis=0)
    kk, b, kp, r, v = kk_ref[...], b_ref[...], kp_ref[...], r_ref[...], v_ref[...]
    g_inv = jnp.exp(-cum)
    g_end = jnp.exp(cum_end - cum)
    a_t = (-kk * jnp.exp(cum - lw)).astype(BF16)
    b_t = b * g_inv
    k_t = kp * g_inv
    r_t = (r * jnp.exp(cum)).astype(BF16)
    b_h = (b * g_end).astype(BF16)
    k_h = (kp * g_end).astype(BF16)
    vb = v.astype(BF16)

    srow = lax.broadcasted_iota(I32, (nh * ch, gw), 0)
    scol = lax.broadcasted_iota(I32, (nh * ch, gw), 1)
    own = (srow // ch) == (scol // RWKV_HD)
    step = lax.broadcasted_iota(I32, (ch, gw), 0)
    other = lax.broadcasted_iota(I32, (ch, gw), 1) % ch
    strict, incl = step > other, step >= other
    eye = jnp.where(step == other, 1.0, 0.0)

    def stack(x):
        return jnp.where(own, jnp.concatenate([x] * nh, axis=0), 0.0).astype(BF16)

    groups = range(N_GROUPS)
    units = [(c, g) for c in range(n_sub) for g in groups]
    blk = {(c, g): (slice(c * ch, (c + 1) * ch), slice(g * gw, (g + 1) * gw)) for c, g in units}
    rb = {u: stack(b_t[blk[u]]) for u in units}
    rk = {u: stack(k_t[blk[u]]) for u in units}
    v4 = {u: stack(v[blk[u]]) for u in units}
    ar = {u: jnp.concatenate([a_t[blk[u]], r_t[blk[u]]], axis=0) for u in units}
    g_b = {u: _dg(ar[u], rb[u], _NT) for u in units}
    g_k = {u: _dg(ar[u], rk[u], _NT) for u in units}
    n_row = {u: jnp.where(strict, g_b[u][:ch], 0.0) for u in units}
    bm_row = {u: jnp.where(strict, g_k[u][:ch], 0.0).astype(BF16) for u in units}
    rb_row = {u: jnp.where(incl, g_b[u][ch:], 0.0).astype(BF16) for u in units}
    rk_row = {u: jnp.where(incl, g_k[u][ch:], 0.0).astype(BF16) for u in units}
    minv = {u: eye + n_row[u] for u in units}
    pw = {u: _dg(n_row[u].astype(BF16), stack(n_row[u]), _NN) for u in units}
    levels = int(math.log2(ch)) - 1
    for level in range(levels):
        pw_bd = {u: stack(pw[u]) for u in units}
        if level + 1 < levels:
            both = {u: _dg(jnp.concatenate([minv[u], pw[u]], axis=0).astype(BF16), pw_bd[u], _NN) for u in units}
            minv = {u: minv[u] + both[u][:ch] for u in units}
            pw = {u: both[u][ch:] for u in units}
        else:
            minv = {u: minv[u] + _dg(minv[u].astype(BF16), pw_bd[u], _NN) for u in units}
    minv = {u: minv[u].astype(BF16) for u in units}
    xy_fixed = {u: _dg(jnp.concatenate([bm_row[u], rk_row[u]], axis=0), v4[u], _NN) for u in units}
    x_fixed = {u: xy_fixed[u][:ch] for u in units}
    y_fixed = {u: xy_fixed[u][ch:] for u in units}

    s = [s_ref[g] for g in groups]
    for c in range(n_sub):
        sb = [x.astype(BF16) for x in s]
        ar_s = [_dg(ar[c, g], sb[g], _NT) for g in groups]
        x_row = [ar_s[g][:ch] + x_fixed[c, g] for g in groups]
        x_bd = [stack(x) for x in x_row]
        u_row = [_dg(minv[c, g], x_bd[g], _NN) for g in groups]
        u_bd = [stack(x) for x in u_row]
        for g in groups:
            y_ref[blk[c, g]] = ar_s[g][ch:] + _dg(rb_row[c, g], u_bd[g], _NN) + y_fixed[c, g]
        for g in groups:
            rs, ls = blk[c, g]
            upd = _dg(jnp.concatenate([u_row[g].astype(BF16), vb[rs, ls]], axis=0),
                      jnp.concatenate([b_h[rs, ls], k_h[rs, ls]], axis=0), _TN)
            g_c = jnp.exp(cum[(c + 1) * ch - 1:(c + 1) * ch, ls])
            s[g] = s[g] * g_c + jnp.where(own, upd, 0.0)
    for g in groups:
        s_ref[g] = s[g]

    @pl.when(pl.program_id(0) == pl.num_programs(0) - 1)
    def _():
        st_ref[...] = s_ref[...]


def _rwkv_scan(r, kp, v, kk, b, lw, s0):
    t = r.shape[0]
    ch = SCAN_CHUNK * min(SCAN_SUB, t // SCAN_CHUNK)
    row = pl.BlockSpec((ch, RWKV_W), lambda i: (i, 0))
    st = pl.BlockSpec((N_GROUPS, MXU_DIM, MXU_DIM), lambda i: (0, 0, 0))
    return pl.pallas_call(
        _scan_kernel,
        grid=(t // ch,),
        in_specs=[row] * 6 + [st],
        out_specs=[row, st],
        out_shape=[jax.ShapeDtypeStruct((t, RWKV_W), F32),
                   jax.ShapeDtypeStruct((N_GROUPS, MXU_DIM, MXU_DIM), F32)],
        scratch_shapes=[pltpu.VMEM((N_GROUPS, MXU_DIM, MXU_DIM), F32)],
        compiler_params=_cparams(("arbitrary",)),
        name="rwkv_scan",
    )(r, kp, v, kk, b, lw, s0)


def _rwkv_step_kernel(s_ref, r_ref, kp_ref, kk_ref, b_ref, lw_ref, v_ref, so_ref, y_ref):
    s = s_ref[...]
    sa = -jnp.sum(s * kk_ref[...], axis=-1, keepdims=True)
    s_new = s * jnp.exp(lw_ref[...]) + sa * b_ref[...] + v_ref[...] * kp_ref[...]
    so_ref[...] = s_new
    y_ref[...] = jnp.sum(s_new * r_ref[...], axis=-1, keepdims=True)


def _rwkv_step(state, r, kp, kk, b, lw, v):
    nb = state.shape[0]
    ts = math.gcd(4, nb)
    rowv = lambda a: a.reshape(nb, RWKV_HEADS, 1, RWKV_HD)
    rspec = pl.BlockSpec((ts, RWKV_HEADS, 1, RWKV_HD), lambda i: (i, 0, 0, 0))
    cspec = pl.BlockSpec((ts, RWKV_HEADS, RWKV_HD, 1), lambda i: (i, 0, 0, 0))
    sspec = pl.BlockSpec((ts, RWKV_HEADS, RWKV_HD, RWKV_HD), lambda i: (i, 0, 0, 0))
    s_new, y = pl.pallas_call(
        _rwkv_step_kernel,
        grid=(nb // ts,),
        in_specs=[sspec, rspec, rspec, rspec, rspec, rspec, cspec],
        out_specs=[sspec, cspec],
        out_shape=[jax.ShapeDtypeStruct(state.shape, F32),
                   jax.ShapeDtypeStruct((nb, RWKV_HEADS, RWKV_HD, 1), F32)],
        compiler_params=_cparams(("arbitrary",)),
        name="rwkv_step",
    )(state, rowv(r), rowv(kp), rowv(kk), rowv(b), rowv(lw), v.reshape(nb, RWKV_HEADS, RWKV_HD, 1))
    return s_new, y.reshape(nb, RWKV_W)


def _rwkv_post_kernel(y_ref, bonus_ref, sg_ref, lg_ref, lb_ref, ones_ref, o_ref):
    y = y_ref[...]
    ones_bd = ones_ref[...]
    mean = _head_sum(y, ones_bd) * (1.0 / RWKV_HD)
    yc = y - mean
    var = _head_sum(yc * yc, ones_bd) * (1.0 / RWKV_HD)
    yn = yc * lax.rsqrt(var + GN_EPS) * lg_ref[...] + lb_ref[...]
    o_ref[...] = ((yn + bonus_ref[...]) * sg_ref[...]).astype(BF16)


def _rwkv_post(y, bonus, sg, ln_g, ln_b, ones_bd):
    t = y.shape[0]
    tm = min(512, t)
    row = pl.BlockSpec((tm, RWKV_W), lambda i: (i, 0))
    vec = pl.BlockSpec((1, RWKV_W), lambda i: (0, 0))
    return pl.pallas_call(
        _rwkv_post_kernel,
        grid=(t // tm,),
        in_specs=[row, row, row, vec, vec, pl.BlockSpec(ones_bd.shape, lambda i: (0, 0))],
        out_specs=row,
        out_shape=jax.ShapeDtypeStruct((t, RWKV_W), BF16),
        compiler_params=_cparams(("arbitrary",)),
        name="rwkv_post",
    )(y, bonus, sg, ln_g.reshape(1, -1), ln_b.reshape(1, -1), ones_bd)


def _outproj_kernel(att_ref, rw_ref, wa_ref, wr_ref, x_ref, gate_ref, o_ref):
    out = (jnp.dot(att_ref[...], wa_ref[...], preferred_element_type=F32)
           + jnp.dot(rw_ref[...], wr_ref[...], preferred_element_type=F32))
    o_ref[...] = x_ref[...] + gate_ref[...] * out


def _out_projection(att, rw, w_out_bf16, x, gate):
    t, d = x.shape
    tm = min(512, t)
    per_row = gate.shape[0] != 1
    gspec = (pl.BlockSpec((tm, d), lambda i: (i, 0)) if per_row else pl.BlockSpec((1, d), lambda i: (0, 0)))
    return pl.pallas_call(
        _outproj_kernel,
        grid=(t // tm,),
        in_specs=[pl.BlockSpec((tm, ATT_W), lambda i: (i, 0)),
                  pl.BlockSpec((tm, RWKV_W), lambda i: (i, 0)),
                  pl.BlockSpec((ATT_W, d), lambda i: (0, 0)),
                  pl.BlockSpec((RWKV_W, d), lambda i: (1, 0)),
                  pl.BlockSpec((tm, d), lambda i: (i, 0)),
                  gspec],
        out_specs=pl.BlockSpec((tm, d), lambda i: (i, 0)),
        out_shape=jax.ShapeDtypeStruct((t, d), F32),
        compiler_params=_cparams(("arbitrary",)),
        name="out_projection",
    )(att, rw, w_out_bf16, w_out_bf16, x, gate)


def _reorder_cols(w, axis=-1):
    axis = axis % w.ndim

    def cut(start, size):
        return lax.slice_in_dim(w, start, start + size, axis=axis)

    def zeros(size):
        shape = list(w.shape)
        shape[axis] = size
        return jnp.zeros(shape, w.dtype)

    q, k, v, g = (cut(i * ATT_W, ATT_W) for i in range(4))
    a = 4 * ATT_W
    iq, iw, ik = cut(a, IDX_W), cut(a + IDX_W, IDX_HEADS), cut(a + IDX_W + IDX_HEADS, IDX_HD)
    o = N_ATT_COLS_ORIG
    rr, rk, rv = (cut(o + i * RWKV_W, RWKV_W) for i in range(3))
    o2 = o + 3 * RWKV_W
    wl, al = cut(o2, DECAY_LORA), cut(o2 + DECAY_LORA, ICL_LORA)
    rg = cut(o2 + DECAY_LORA + ICL_LORA, RWKV_W)
    return jnp.concatenate([q, k, v, g, iq, rr, rk, rv, rg, ik, iw, zeros(LANES - IDX_HD - IDX_HEADS),
                            wl, al, zeros(N_COLS_PAD - N_COLS)], axis=axis)


def _reorder_cast_kernel(wt_ref, o_ref):
    o_ref[...] = _reorder_cols(wt_ref[...], axis=0).T.astype(BF16)


def _reorder_cast(wt):
    n, d = wt.shape
    td = min(256, d)
    return pl.pallas_call(
        _reorder_cast_kernel,
        grid=(d // td,),
        in_specs=[pl.BlockSpec((n, td), lambda i: (0, i))],
        out_specs=pl.BlockSpec((td, N_COLS_PAD), lambda i: (i, 0)),
        out_shape=jax.ShapeDtypeStruct((d, N_COLS_PAD), BF16),
        compiler_params=_cparams(("arbitrary",)),
        name="reorder_cast",
    )(wt)


def _rwkv_cols_original_order(p):
    return jnp.concatenate([p[..., COL_RR:COL_RR + RWKV_W], p[..., COL_RK:COL_RK + RWKV_W],
                            p[..., COL_RV:COL_RV + RWKV_W],
                            p[..., COL_SMALL_R:COL_SMALL_R + DECAY_LORA + ICL_LORA],
                            p[..., COL_RG:COL_RG + RWKV_W]], axis=-1)


def _blockdiag_to_state(sb):
    sb = sb.reshape(N_GROUPS, HEADS_PER_GROUP, RWKV_HD, HEADS_PER_GROUP, RWKV_HD)
    return jnp.stack([sb[:, h, :, h, :] for h in range(HEADS_PER_GROUP)], axis=1).reshape(
        RWKV_HEADS, RWKV_HD, RWKV_HD)


def kernel(x_prompt, x_sample, c_prompt, c_sample, cache_k, cache_v, cache_idx_k, state_wkv, state_shift,
           page_table, w_ada, b_ada, norm_g, w_in, q_norm_g, k_norm_g, shift_mu, w0, w_b, a0, a_b,
           k_k, k_a, r_k, ln_x_g, ln_x_b, w_out):
    depth = w_ada.shape[0]
    assert depth == 1, "single-layer trunk"
    bsz, seq, d = x_prompt.shape
    assert bsz == 1
    nb, dec_seq, _ = x_sample.shape
    assert dec_seq == 1
    n_pages = page_table.shape[1]
    past_len = n_pages * PAGE_SIZE
    l = 0

    w_in_r = _reorder_cast(w_in[l].T)
    w_out_b = w_out[l].astype(BF16)
    mu = _reorder_cols(jnp.concatenate([jnp.zeros((N_ATT_COLS_ORIG,), F32), shift_mu[l]]))
    mu_big = mu[COL_RR:COL_RR + 4 * RWKV_W].reshape(1, -1)
    mu_small = mu[COL_SMALL_R:COL_SMALL_R + LANES].reshape(1, -1)
    zl = jnp.zeros((DECAY_LORA, RWKV_W), F32)
    w_lora = jnp.concatenate([jnp.concatenate([w_b[l], zl], axis=1),
                              jnp.concatenate([zl, a_b[l]], axis=1)], axis=0)
    hid = jnp.arange(LANES) // RWKV_HD
    ones_bd = (hid[:, None] == hid[None, :]).astype(BF16)
    rwkv_consts = (mu_big, mu_small, w_lora, w0[l], a0[l], k_k[l], k_a[l], r_k[l].reshape(-1), ones_bd)

    n_mod = 1 + nb
    c_all = jnp.concatenate([c_prompt, c_sample, jnp.zeros((-n_mod % SUBLANES, d), F32)], axis=0)
    mod = _modulation(c_all, w_ada[l], b_ada[l])
    shift_p, scale_p, gate_p = mod[0:1, :d], mod[0:1, d:2 * d], mod[0:1, 2 * d:]
    shift_s, scale_s, gate_s = mod[1:n_mod, :d], mod[1:n_mod, d:2 * d], mod[1:n_mod, 2 * d:]

    xp = x_prompt.reshape(seq, d)
    xs = x_sample.reshape(nb, d)

    p = _in_projection(xp, shift_p, scale_p, norm_g[l], w_in_r)
    qt, k32, kb, vt, v32, iqb, ik32, ikb, wv = _attention_prep(p, _rope_tables(jnp.arange(seq)),
                                                         q_norm_g[l], k_norm_g[l], transposed=True)
    n_sel = min(TOPK_MAX, seq // 4)
    tq = min(MXU_DIM, seq)
    bias = _prompt_indexer(iqb, wv[:, IDX_HD:IDX_HD + IDX_HEADS].T, ikb, n_sel, tq)
    att = _prompt_attention(qt, kb, vt, bias, p, tq)

    r_, kp_, v_, kk_, b_, lw_, bonus, sg = _rwkv_prep(p, jnp.zeros((1, p.shape[1]), F32), rwkv_consts,
                                                      shifted=True)
    y_scan, s_fin = _rwkv_scan(r_, kp_, v_, kk_, b_, lw_,
                               jnp.zeros((N_GROUPS, MXU_DIM, MXU_DIM), F32))
    rw = _rwkv_post(y_scan, bonus, sg, ln_x_g[l], ln_x_b[l], ones_bd)
    y_p = _out_projection(att, rw, w_out_b, xp, gate_p)

    k_prompt = k32.reshape(1, 1, seq, ATT_HEADS, ATT_HD)
    v_prompt = v32.reshape(1, 1, seq, ATT_HEADS, ATT_HD)
    idx_k_prompt = ik32.reshape(1, 1, seq, IDX_HD)
    wkv_prompt = _blockdiag_to_state(s_fin).reshape(1, 1, RWKV_HEADS, RWKV_HD, RWKV_HD)
    shift_prompt = _rwkv_cols_original_order(p[seq - 1:seq]).reshape(1, 1, -1)

    ps = _in_projection(xs, shift_s, scale_s, norm_g[l], w_in_r)
    pos_s = jnp.full((nb,), past_len, jnp.int32)
    qb_s, k32_s, _, _, v32_s, iqb_s, ik32_s, ikb_s, wv_s = _attention_prep(ps, _rope_tables(pos_s),
                                                                          q_norm_g[l], k_norm_g[l],
                                                                          transposed=False)
    n_sel_s = min(TOPK_MAX, (past_len + 1) // 4)
    att_s = _sample_attention(page_table, qb_s, iqb_s, wv_s, ikb_s, k32_s, v32_s,
                              ps[:, COL_G:COL_G + ATT_W], cache_idx_k[l], cache_k[l], cache_v[l], n_sel_s)

    prev_s = _reorder_cols(jnp.concatenate([jnp.zeros((nb, N_ATT_COLS_ORIG), F32), state_shift[l]], axis=-1))
    r_s, kp_s, v_s, kk_s, b_s, lw_s, bonus_s, sg_s = _rwkv_prep(ps, prev_s, rwkv_consts, shifted=False)
    wkv_new, y_step = _rwkv_step(state_wkv[l], r_s, kp_s, kk_s, b_s, lw_s, v_s)
    rw_s = _rwkv_post(y_step, bonus_s, sg_s, ln_x_g[l], ln_x_b[l], ones_bd)
    y_s = _out_projection(att_s, rw_s, w_out_b, xs, gate_s)

    k_sample = k32_s.reshape(1, nb, 1, ATT_HEADS, ATT_HD)
    v_sample = v32_s.reshape(1, nb, 1, ATT_HEADS, ATT_HD)
    idx_k_sample = ik32_s.reshape(1, nb, 1, IDX_HD)
    wkv_sample = wkv_new.reshape(1, nb, RWKV_HEADS, RWKV_HD, RWKV_HD)
    shift_sample = _rwkv_cols_original_order(ps).reshape(1, nb, -1)

    return (y_p.reshape(1, seq, d), y_s.reshape(nb, 1, d),
            k_prompt, v_prompt, idx_k_prompt, wkv_prompt, shift_prompt,
            k_sample, v_sample, idx_k_sample, wkv_sample, shift_sample)
```

```python
import functools
import math

import jax
import jax.numpy as jnp
from jax import lax
from jax.experimental import pallas as pl
from jax.experimental.pallas import tpu as pltpu

F32 = jnp.float32
BF16 = jnp.bfloat16
I32 = jnp.int32

ATT_HEADS = 8
ATT_HD = 128
ATT_W = ATT_HEADS * ATT_HD
IDX_HEADS = 16
IDX_HD = 64
IDX_W = IDX_HEADS * IDX_HD
TOPK_MAX = 256
PAGE_SIZE = 128
ROPE_THETA = 10000.0
RWKV_HD = 64
RWKV_HEADS = 16
RWKV_W = RWKV_HEADS * RWKV_HD
DECAY_LORA = 64
ICL_LORA = 64
NORM_EPS = 1e-6
GN_EPS = 64e-5

LANES = 128
SUBLANES = 8
MXU_DIM = 256
VMEM_LIMIT = 56 * 1024 * 1024
NEG_BIG = -1e30
INT_MIN = -(2 ** 31)

COL_Q, COL_K, COL_V, COL_G, COL_IQ = 0, 1024, 2048, 3072, 4096
COL_RR, COL_RK, COL_RV, COL_RG = 5120, 6144, 7168, 8192
COL_SMALL_A = 9216
COL_SMALL_R = 9344
N_COLS = 9472
N_COLS_PAD = 9728
N_ATT_COLS_ORIG = 4 * ATT_W + IDX_W + IDX_HEADS + IDX_HD

EARLY_EXIT_FROM_BIT = 24
FLASH_KEY_BLOCKS = 4
SCAN_CHUNK = 64
DECODE_PAGES_PER_STEP = 8
SCAN_SUB = 4
HEADS_PER_GROUP = MXU_DIM // RWKV_HD
N_GROUPS = RWKV_HEADS // HEADS_PER_GROUP


def _cparams(sem, vmem=VMEM_LIMIT):
    return pltpu.CompilerParams(dimension_semantics=sem, vmem_limit_bytes=vmem)


def _split2(x):
    hi = x.astype(BF16)
    lo = (x - hi.astype(F32)).astype(BF16)
    return hi, lo


def _split3(x):
    hi = x.astype(BF16)
    r = x - hi.astype(F32)
    mid = r.astype(BF16)
    lo = (r - mid.astype(F32)).astype(BF16)
    return hi, mid, lo


_NN = (((1,), (0,)), ((), ()))
_NT = (((1,), (1,)), ((), ()))
_TN = (((0,), (0,)), ((), ()))


def _dg(a, b, dims):
    return lax.dot_general(a, b, dims, preferred_element_type=F32)


def _dot1(a, b, dims=_NN):
    return _dg(a.astype(BF16), b.astype(BF16), dims)


def _dot3(a, b, dims=_NN):
    ah, al = _split2(a)
    bh, bl = _split2(b)
    return _dg(ah, bh, dims) + _dg(al, bh, dims) + _dg(ah, bl, dims)


def _dot_exact_lhs(a_bf16, b, dims=_NN):
    b0, b1, b2 = _split3(b)
    return _dg(a_bf16, b0, dims) + _dg(a_bf16, b1, dims) + _dg(a_bf16, b2, dims)


def _dot_exact_rhs(a, b_bf16, dims=_NN):
    a0, a1, a2 = _split3(a)
    return _dg(a0, b_bf16, dims) + _dg(a1, b_bf16, dims) + _dg(a2, b_bf16, dims)


def _silu(x):
    return x * jax.nn.sigmoid(x)


def _head_sum(x, ones_bd):
    parts = []
    for j in range(x.shape[1] // LANES):
        parts.append(_dot_exact_rhs(x[:, j * LANES:(j + 1) * LANES], ones_bd))
    return jnp.concatenate(parts, axis=1)


def _ordered_to_f32(key):
    return pltpu.bitcast(key ^ ((key >> 31) & 0x7FFFFFFF), F32)


def _mod_kernel(c_ref, w_ref, b_ref, o_ref):
    o_ref[...] = _dot3(_silu(c_ref[...]), w_ref[...]) + b_ref[...]


def _modulation(c, w_ada, b_ada):
    m, d = c.shape
    n = w_ada.shape[1]
    tn = 768
    return pl.pallas_call(
        _mod_kernel,
        grid=(n // tn,),
        in_specs=[pl.BlockSpec((m, d), lambda j: (0, 0)),
                  pl.BlockSpec((d, tn), lambda j: (0, j)),
                  pl.BlockSpec((1, tn), lambda j: (0, j))],
        out_specs=pl.BlockSpec((m, tn), lambda j: (0, j)),
        out_shape=jax.ShapeDtypeStruct((m, n), F32),
        compiler_params=_cparams(("arbitrary",)),
        name="modulation",
    )(c, w_ada, b_ada.reshape(1, n))


def _inproj_kernel(x_ref, shift_ref, scale_ref, g_ref, w_ref, o_ref, h_ref, *, rows):
    @pl.when(pl.program_id(1) == 0)
    def _():
        def body(r, c):
            sl = pl.ds(pl.multiple_of(r * rows, rows), rows)
            x = x_ref[sl, :]
            ms = jnp.mean(x * x, axis=-1, keepdims=True)
            y = x * lax.rsqrt(ms + NORM_EPS) * g_ref[...]
            if shift_ref.shape[0] == 1:
                h = y * (1.0 + scale_ref[...]) + shift_ref[...]
            else:
                h = y * (1.0 + scale_ref[sl, :]) + shift_ref[sl, :]
            h_ref[sl, :] = h.astype(BF16)
            return c
        lax.fori_loop(0, x_ref.shape[0] // rows, body, 0)

    o_ref[...] = jnp.dot(h_ref[...], w_ref[...], preferred_element_type=F32)


def _in_projection(x, shift, scale, norm_g, w_bf16):
    t, d = x.shape
    n = w_bf16.shape[1]
    tm = min(1024, t)
    tn = 512
    rows = min(128, tm)
    per_row = shift.shape[0] != 1
    mod_spec = (pl.BlockSpec((tm, d), lambda i, j: (i, 0)) if per_row
                else pl.BlockSpec((1, d), lambda i, j: (0, 0)))
    return pl.pallas_call(
        functools.partial(_inproj_kernel, rows=rows),
        grid=(t // tm, n // tn),
        in_specs=[pl.BlockSpec((tm, d), lambda i, j: (i, 0)),
                  mod_spec, mod_spec,
                  pl.BlockSpec((1, d), lambda i, j: (0, 0)),
                  pl.BlockSpec((d, tn), lambda i, j: (0, j))],
        out_specs=pl.BlockSpec((tm, tn), lambda i, j: (i, j)),
        out_shape=jax.ShapeDtypeStruct((t, n), F32),
        scratch_shapes=[pltpu.VMEM((tm, d), BF16)],
        compiler_params=_cparams(("arbitrary", "arbitrary")),
        name="in_projection",
    )(x, shift, scale, norm_g.reshape(1, d), w_bf16)


def _rope128(x, cos, sin_signed):
    return x * cos + pltpu.roll(x, 64, axis=1) * sin_signed


def _rope64(x, cos, sin_signed, first_half):
    rot = jnp.where(first_half, pltpu.roll(x, 96, axis=1), pltpu.roll(x, 32, axis=1))
    return x * cos + rot * sin_signed


def _prep_kernel(q_ref, k_ref, v_ref, iq_ref, sm_ref, c128_ref, s128_ref, c64_ref, s64_ref,
                 qg_ref, kg_ref,
                 qo_ref, k32_ref, kb_ref, vo_ref, v32_ref, iqb_ref, ik32_ref, ikb_ref, w_ref, *, transposed):
    c128, s128 = c128_ref[...], s128_ref[...]
    c64, s64 = c64_ref[...], s64_ref[...]
    tm = q_ref.shape[0]
    lane = lax.broadcasted_iota(I32, (tm, LANES), 1)
    first_half = (lane % IDX_HD) < (IDX_HD // 2)
    q_scale = ATT_HD ** -0.5 * math.log2(math.e)
    for h in range(ATT_HEADS):
        hs = slice(h * ATT_HD, (h + 1) * ATT_HD)
        q = q_ref[:, hs]
        q = q * lax.rsqrt(jnp.mean(q * q, axis=-1, keepdims=True) + NORM_EPS) * qg_ref[...]
        q = _rope128(q, c128, s128) * q_scale
        k = k_ref[:, hs]
        k = k * lax.rsqrt(jnp.mean(k * k, axis=-1, keepdims=True) + NORM_EPS) * kg_ref[...]
        k = _rope128(k, c128, s128)
        k32_ref[:, h, :] = k
        kb_ref[:, hs] = k.astype(BF16)
        if transposed:
            qo_ref[hs, :] = q.T.astype(BF16)
            vo_ref[hs, :] = v_ref[:, hs].T.astype(BF16)
        else:
            qo_ref[:, hs] = q.astype(BF16)
            vo_ref[:, hs] = v_ref[:, hs].astype(BF16)
    for h in range(ATT_HEADS):
        v32_ref[:, h, :] = v_ref[:, h * ATT_HD:(h + 1) * ATT_HD]
    for j in range(IDX_W // LANES):
        js = slice(j * LANES, (j + 1) * LANES)
        iqb_ref[:, js] = _rope64(iq_ref[:, js], c64, s64, first_half).astype(BF16)
    sm = sm_ref[...]
    ik = _rope64(sm, c64, s64, first_half)[:, :IDX_HD]
    ik32_ref[...] = ik
    ikb_ref[...] = ik.astype(BF16)
    w_ref[...] = sm * ((IDX_HEADS ** -0.5) * (IDX_HD ** -0.5))


def _attention_prep(p, tables, q_norm_g, k_norm_g, transposed):
    t = p.shape[0]
    tm = min(256, t)
    c128, s128, c64, s64 = tables
    wide = lambda col: pl.BlockSpec((tm, 1024), lambda i, c=col // 1024: (i, c))
    tab = pl.BlockSpec((tm, LANES), lambda i: (i, 0))
    gain = pl.BlockSpec((1, ATT_HD), lambda i: (0, 0))
    row1024 = pl.BlockSpec((tm, 1024), lambda i: (i, 0))
    col1024 = pl.BlockSpec((1024, tm), lambda i: (0, i))
    row64 = pl.BlockSpec((tm, IDX_HD), lambda i: (i, 0))
    head3 = pl.BlockSpec((tm, ATT_HEADS, ATT_HD), lambda i: (i, 0, 0))
    if transposed:
        qv_spec, qv_shape = col1024, jax.ShapeDtypeStruct((ATT_W, t), BF16)
    else:
        qv_spec, qv_shape = row1024, jax.ShapeDtypeStruct((t, ATT_W), BF16)
    return pl.pallas_call(
        functools.partial(_prep_kernel, transposed=transposed),
        grid=(t // tm,),
        in_specs=[wide(COL_Q), wide(COL_K), wide(COL_V), wide(COL_IQ),
                  pl.BlockSpec((tm, LANES), lambda i: (i, COL_SMALL_A // LANES)),
                  tab, tab, tab, tab, gain, gain],
        out_specs=[qv_spec, head3, row1024, qv_spec, head3, row1024, row64, row64, tab],
        out_shape=[qv_shape,
                   jax.ShapeDtypeStruct((t, ATT_HEADS, ATT_HD), F32),
                   jax.ShapeDtypeStruct((t, ATT_W), BF16),
                   qv_shape,
                   jax.ShapeDtypeStruct((t, ATT_HEADS, ATT_HD), F32),
                   jax.ShapeDtypeStruct((t, IDX_W), BF16),
                   jax.ShapeDtypeStruct((t, IDX_HD), F32),
                   jax.ShapeDtypeStruct((t, IDX_HD), BF16),
                   jax.ShapeDtypeStruct((t, LANES), F32)],
        compiler_params=_cparams(("arbitrary",)),
        name="attention_prep",
    )(p, p, p, p, p, c128, s128, c64, s64, q_norm_g.reshape(1, ATT_HD), k_norm_g.reshape(1, ATT_HD))


def _rope_tables(pos):
    pos = pos.astype(F32)[:, None]

    def tab(dh):
        inv = ROPE_THETA ** (-jnp.arange(0, dh, 2, dtype=F32) / dh)
        ang = pos * inv[None, :]
        cos, sin = jnp.cos(ang), jnp.sin(ang)
        c = jnp.concatenate([cos, cos], axis=-1)
        s = jnp.concatenate([-sin, sin], axis=-1)
        rep = LANES // dh
        return jnp.tile(c, (1, rep)), jnp.tile(s, (1, rep))

    c128, s128 = tab(ATT_HD)
    c64, s64 = tab(IDX_HD)
    return c128, s128, c64, s64


def _indexer_kernel(iq_ref, wt_ref, ik_ref, bias_ref, iqs_ref, cut_ref, *, n_sel):
    i = pl.program_id(0)
    tq = iq_ref.shape[0]
    tk = bias_ref.shape[2]
    nk_total = bias_ref.shape[1]
    nkb = i + 1

    for h in range(IDX_HEADS):
        iqs_ref[h * tq:(h + 1) * tq, :] = iq_ref[:, h * IDX_HD:(h + 1) * IDX_HD]

    key_pos = lax.broadcasted_iota(I32, (tk, tq), 0)
    qry_pos = lax.broadcasted_iota(I32, (tk, tq), 1) + i * tq

    def score_body(kb, c):
        ikt = ik_ref[pl.ds(pl.multiple_of(kb * tk, tk), tk), :]
        d = _dg(ikt, iqs_ref[...], _NT)
        acc = jnp.zeros((tk, tq), F32)
        for h in range(IDX_HEADS):
            acc = acc + jnp.maximum(d[:, h * tq:(h + 1) * tq], 0.0) * wt_ref[h:h + 1, :]
        bias_ref[0, kb] = jnp.where(key_pos + kb * tk <= qry_pos, acc, -jnp.inf)
        return c

    def score_pair(j, c):
        score_body(2 * j, c)
        return score_body(2 * j + 1, c)

    lax.fori_loop(0, nkb // 2, score_pair, 0)
    lax.fori_loop(2 * (nkb // 2), nkb, score_body, 0)

    def count_keys(pred):
        def one(kb, cnt):
            x = bias_ref[0, kb].reshape(tk // SUBLANES, SUBLANES, tq)
            pos = (key_pos + kb * tk).reshape(tk // SUBLANES, SUBLANES, tq)
            return cnt + jnp.sum(jnp.where(pred(x, pos), 1.0, 0.0), axis=0)

        def four(j, cnt):
            for u in range(4):
                cnt = one(4 * j + u, cnt)
            return cnt

        cnt = lax.fori_loop(0, nkb // 4, four, jnp.zeros((SUBLANES, tq), F32))
        cnt = lax.fori_loop(4 * (nkb // 4), nkb, one, cnt)
        return jnp.sum(cnt, axis=0, keepdims=True)

    def bit_body(bi, carry):
        ans, tot_ans = carry
        cand = ans + jnp.left_shift(jnp.int32(1), 31 - bi)
        cand_f = _ordered_to_f32(cand)
        tot = count_keys(lambda x, pos: x >= cand_f[None])
        ok = tot >= n_sel
        return jnp.where(ok, cand, ans), jnp.where(ok, tot, tot_ans)

    never_full = qry_pos[0:1, :] + 1 < n_sel

    def pending(tot_ans):
        return jnp.max(jnp.where((tot_ans == n_sel) | never_full, 0.0, 1.0))

    def late_bits(c):
        bi, ans, tot_ans, _ = c
        ans, tot_ans = bit_body(bi + 1, bit_body(bi, (ans, tot_ans)))
        return bi + 2, ans, tot_ans, pending(tot_ans)

    ans, tot_ans = lax.fori_loop(0, EARLY_EXIT_FROM_BIT, bit_body,
                                 (jnp.full((SUBLANES, tq), INT_MIN, I32), jnp.zeros((1, tq), F32)))
    _, ans, tot_ans, _ = lax.while_loop(lambda c: (c[0] < 32) & (c[3] > 0.0), late_bits,
                                        (jnp.int32(EARLY_EXIT_FROM_BIT), ans, tot_ans, pending(tot_ans)))
    thr = _ordered_to_f32(ans)[0:1, :]
    thr = jnp.where(thr != thr, -jnp.inf, thr)

    cut_ref[...] = jnp.full(cut_ref.shape, 2 ** 31 - 1, I32)

    @pl.when(jnp.max(tot_ans) > n_sel)
    def _():
        need = n_sel - count_keys(lambda x, pos: x > thr[None])
        n_bits = max(1, (nk_total * tk - 1).bit_length())

        def cut_body(bi, cut):
            cand = cut + jnp.left_shift(jnp.int32(1), n_bits - 1 - bi)
            below = count_keys(lambda x, pos: (x == thr[None]) & (pos < cand[None]))
            return jnp.where(below < need, cand, cut)

        cut = lax.fori_loop(0, n_bits, cut_body, jnp.zeros((1, tq), I32))
        cut_ref[...] = jnp.broadcast_to(jnp.where(tot_ans > n_sel, cut, 2 ** 31 - 1), cut_ref.shape)

    has_surplus_ties = jnp.max(tot_ans) > n_sel
    cut = cut_ref[0:1, :]

    def bias_body(kb, c):
        keep = (bias_ref[0, kb] >= thr) & (key_pos + kb * tk <= qry_pos)
        bias_ref[0, kb] = jnp.where(keep, 0.0, NEG_BIG)
        return c

    def bias_body_ties(kb, c):
        x = bias_ref[0, kb]
        pos = key_pos + kb * tk
        keep = ((x > thr) | ((x == thr) & (pos <= cut))) & (pos <= qry_pos)
        bias_ref[0, kb] = jnp.where(keep, 0.0, NEG_BIG)
        return c

    @pl.when(jnp.logical_not(has_surplus_ties))
    def _():
        lax.fori_loop(0, nkb, bias_body, 0)

    @pl.when(has_surplus_ties)
    def _():
        lax.fori_loop(0, nkb, bias_body_ties, 0)

    def fill_body(kb, c):
        bias_ref[0, kb] = jnp.full((tk, tq), NEG_BIG, F32)
        return c

    lax.fori_loop(nkb, nk_total, fill_body, 0)


def _prompt_indexer(iqb, wt, ikb, n_sel, tq):
    t = iqb.shape[0]
    nq = t // tq
    return pl.pallas_call(
        functools.partial(_indexer_kernel, n_sel=n_sel),
        grid=(nq,),
        in_specs=[pl.BlockSpec((tq, IDX_W), lambda i: (i, 0)),
                  pl.BlockSpec((IDX_HEADS, tq), lambda i: (0, i)),
                  pl.BlockSpec((t, IDX_HD), lambda i: (0, 0))],
        out_specs=pl.BlockSpec((1, nq, tq, tq), lambda i: (i, 0, 0, 0)),
        out_shape=jax.ShapeDtypeStruct((nq, nq, tq, tq), F32),
        scratch_shapes=[pltpu.VMEM((IDX_HEADS * tq, IDX_HD), BF16),
                        pltpu.VMEM((SUBLANES, tq), I32)],
        compiler_params=_cparams(("arbitrary",)),
        name="prompt_indexer",
    )(iqb, wt, ikb)


def _flash_kernel(qi_ref, ki_ref, qt_ref, k_ref, vt_ref, bias_ref, g_ref, o_ref, m_ref, l_ref, acc_ref):
    step = pl.program_id(0)
    qi, ki = qi_ref[step], ki_ref[step]

    @pl.when(ki == 0)
    def _():
        m_ref[...] = jnp.full(m_ref.shape, NEG_BIG, F32)
        l_ref[...] = jnp.zeros(l_ref.shape, F32)
        acc_ref[...] = jnp.zeros(acc_ref.shape, F32)

    heads = [slice(h * ATT_HD, (h + 1) * ATT_HD) for h in range(ATT_HEADS)]
    tq = qt_ref.shape[1]
    tkb = bias_ref.shape[2]
    for kh in range(bias_ref.shape[1]):
        ks = slice(kh * tkb, (kh + 1) * tkb)
        for qh in range(tq // LANES):
            qs = slice(qh * LANES, (qh + 1) * LANES)
            bias = bias_ref[0, kh, :, qs]
            m_prev, l_prev = m_ref[:, qs], l_ref[:, qs]
            m_rows, l_rows = [], []
            for h, hs in enumerate(heads):
                s = _dg(k_ref[ks, hs], qt_ref[hs, qs], _NN) + bias
                m_old = m_prev[h:h + 1, :]
                m_new = jnp.maximum(m_old, jnp.max(s, axis=0, keepdims=True))
                alpha = jnp.exp2(m_old - m_new)
                p = jnp.exp2(s - m_new)
                l_rows.append(alpha * l_prev[h:h + 1, :] + jnp.sum(p, axis=0, keepdims=True))
                m_rows.append(m_new)
                acc_ref[h, :, qs] = alpha * acc_ref[h, :, qs] + _dg(vt_ref[hs, ks], p.astype(BF16), _NN)
            m_ref[:, qs] = jnp.concatenate(m_rows, axis=0)
            l_ref[:, qs] = jnp.concatenate(l_rows, axis=0)

    @pl.when(ki == qi // bias_ref.shape[1])
    def _():
        for h in range(ATT_HEADS):
            hs = slice(h * ATT_HD, (h + 1) * ATT_HD)
            out = (acc_ref[h] / l_ref[h:h + 1, :]).T
            o_ref[:, hs] = (out * _silu(g_ref[:, hs])).astype(BF16)


def _prompt_attention(qt, kb, vt, bias, p, tq):
    t = kb.shape[0]
    nq = t // tq
    kpb = min(FLASH_KEY_BLOCKS, nq)
    tk = kpb * tq
    pairs = [(i, j) for i in range(nq) for j in range(i // kpb + 1)]
    qi = jnp.asarray([a for a, _ in pairs], I32)
    ki = jnp.asarray([b for _, b in pairs], I32)
    grid_spec = pltpu.PrefetchScalarGridSpec(
        num_scalar_prefetch=2,
        grid=(len(pairs),),
        in_specs=[pl.BlockSpec((ATT_W, tq), lambda s, qi, ki: (0, qi[s])),
                  pl.BlockSpec((tk, ATT_W), lambda s, qi, ki: (ki[s], 0)),
                  pl.BlockSpec((ATT_W, tk), lambda s, qi, ki: (0, ki[s])),
                  pl.BlockSpec((1, kpb, tq, tq), lambda s, qi, ki: (qi[s], ki[s], 0, 0)),
                  pl.BlockSpec((tq, ATT_W), lambda s, qi, ki: (qi[s], COL_G // ATT_W))],
        out_specs=pl.BlockSpec((tq, ATT_W), lambda s, qi, ki: (qi[s], 0)),
        scratch_shapes=[pltpu.VMEM((ATT_HEADS, tq), F32),
                        pltpu.VMEM((ATT_HEADS, tq), F32),
                        pltpu.VMEM((ATT_HEADS, ATT_HD, tq), F32)],
    )
    return pl.pallas_call(
        _flash_kernel,
        grid_spec=grid_spec,
        out_shape=jax.ShapeDtypeStruct((t, ATT_W), BF16),
        compiler_params=_cparams(("arbitrary",)),
        name="prompt_attention",
    )(qi, ki, qt, kb, vt, bias, p)


def _sample_attn_kernel(pt_ref, q_ref, iq_ref, w_ref, ikn_ref, kn_ref, vn_ref, g_ref, ex_ref,
                        cik_ref, ck_ref, cv_ref, o_ref,
                        ikbuf, kbuf, vbuf, sce_ref, isem, ksem, vsem, *, n_sel, pps):
    b = pl.program_id(0)
    n_pages = ikbuf.shape[0]
    page_rows = PAGE_SIZE * ATT_HEADS

    n_steps = n_pages // pps
    nb = pl.num_programs(0)
    cross_prefetch = n_steps % 2 == 0

    def idx_copy(seq, j):
        return pltpu.make_async_copy(cik_ref.at[pt_ref[seq, j]], ikbuf.at[j], isem.at[j])

    def k_copy(seq, j, slot):
        return pltpu.make_async_copy(ck_ref.at[pt_ref[seq, j]], kbuf.at[slot], ksem.at[slot])

    def v_copy(seq, j, slot):
        return pltpu.make_async_copy(cv_ref.at[pt_ref[seq, j]], vbuf.at[slot], vsem.at[slot])

    def start_idx(seq):
        def body(j, c):
            idx_copy(seq, j).start()
            return c
        lax.fori_loop(0, n_pages, body, 0)

    def start_group(seq, t, grp):
        for i in range(pps):
            k_copy(seq, t * pps + i, grp * pps + i).start()
            v_copy(seq, t * pps + i, grp * pps + i).start()

    if cross_prefetch:
        @pl.when(b == 0)
        def _():
            start_idx(0)
            start_group(0, 0, 0)
        start_group(b, 1, 1)
    else:
        @pl.when(b == 0)
        def _():
            start_idx(0)
        start_group(b, 0, 0)

    iq = iq_ref[0]
    w = w_ref[0]

    def wait_idx(j, c):
        idx_copy(b, j).wait()
        return c

    lax.fori_loop(0, n_pages, wait_idx, 0)
    iq_pages = jnp.broadcast_to(iq[None], (n_pages, IDX_HEADS, IDX_HD))
    d = lax.dot_general(iq_pages, ikbuf[...].astype(BF16), (((2,), (1,)), ((0,), (0,))),
                        preferred_element_type=F32)
    sc = jnp.sum(jnp.maximum(d, 0.0) * w[None], axis=1)
    d_self = jnp.sum(iq.astype(F32) * ikn_ref[0].astype(F32), axis=-1, keepdims=True)
    s_self = jnp.sum(jnp.maximum(d_self, 0.0) * w, axis=0, keepdims=True)

    pos = (lax.broadcasted_iota(I32, sc.shape, 0) * PAGE_SIZE + lax.broadcasted_iota(I32, sc.shape, 1))
    self_pos = n_pages * PAGE_SIZE

    def count_keys(pred):
        cnt = jnp.sum(jnp.where(pred(sc, pos), 1.0, 0.0), axis=-1, keepdims=True)
        return jnp.sum(cnt, axis=0, keepdims=True) + jnp.where(pred(s_self, self_pos), 1.0, 0.0)

    def digit_body(it, carry):
        ans, tot_ans = carry
        step = jnp.left_shift(jnp.int32(1), 30 - 2 * it)
        best, best_tot = ans, tot_ans
        cand = ans
        for _ in range(3):
            cand = cand + step
            cand_f = _ordered_to_f32(cand)
            tot = count_keys(lambda x, p: x >= cand_f)
            best = jnp.where(tot >= n_sel, cand, best)
            best_tot = jnp.where(tot >= n_sel, tot, best_tot)
        return best, best_tot

    ans, tot_ans = lax.fori_loop(0, 16, digit_body, (jnp.full((1, 1), INT_MIN, I32), jnp.zeros((1, 1), F32)))
    thr = _ordered_to_f32(ans)
    thr = jnp.where(thr != thr, -jnp.inf, thr)

    def tie_cut():
        need = n_sel - count_keys(lambda x, p: x > thr)
        n_bits = self_pos.bit_length()

        def cut_body(bi, cut):
            cand = cut + jnp.left_shift(jnp.int32(1), n_bits - 1 - bi)
            below = count_keys(lambda x, p: (x == thr) & (p < cand))
            return jnp.where(below < need, cand, cut)

        return lax.fori_loop(0, n_bits, cut_body, jnp.zeros((1, 1), I32))

    cut = lax.cond(tot_ans[0, 0] > n_sel, tie_cut, lambda: jnp.full((1, 1), 2 ** 31 - 1, I32))

    def selected(x, p):
        return (x > thr) | ((x == thr) & (p <= cut))

    sce_ref[...] = _dot_exact_rhs(sc, ex_ref[...])

    @pl.when(b + 1 < nb)
    def _():
        start_idx(b + 1)

    q8 = q_ref[0]
    lane_e = lax.broadcasted_iota(I32, (ATT_HEADS, page_rows), 1)
    own = lane_e % ATT_HEADS == lax.broadcasted_iota(I32, (ATT_HEADS, page_rows), 0)
    tok_lane = lane_e[0:1, :] // ATT_HEADS

    kn = kn_ref[0].astype(BF16).astype(F32)
    s0 = jnp.sum(q8.astype(F32) * kn, axis=-1, keepdims=True)
    m0 = jnp.where(selected(s_self, self_pos), s0, NEG_BIG)
    l0 = jnp.where(m0 > 0.5 * NEG_BIG, 1.0, 0.0)
    acc0 = l0 * vn_ref[0].astype(BF16).astype(F32)

    def att_body(t, carry):
        m_prev, l_prev, acc = carry
        grp = t % 2

        if cross_prefetch:
            @pl.when((t >= 1) & (t + 1 < n_steps))
            def _():
                start_group(b, t + 1, 1 - grp)

            @pl.when((t + 1 == n_steps) & (b + 1 < nb))
            def _():
                start_group(b + 1, 0, 0)
        else:
            @pl.when(t + 1 < n_steps)
            def _():
                start_group(b, t + 1, 1 - grp)

        for i in range(pps):
            k_copy(b, t * pps + i, grp * pps + i).wait()
            v_copy(b, t * pps + i, grp * pps + i).wait()
        s = []
        for i in range(pps):
            kf = kbuf[grp * pps + i].reshape(page_rows, ATT_HD).astype(BF16)
            keep = own & selected(sce_ref[pl.ds(t * pps + i, 1), :], (t * pps + i) * PAGE_SIZE + tok_lane)
            s.append(jnp.where(keep, _dg(q8, kf, _NT), NEG_BIG))
        m_new = m_prev
        for x in s:
            m_new = jnp.maximum(m_new, jnp.max(x, axis=-1, keepdims=True))
        alpha = jnp.exp2(m_prev - m_new)
        p = [jnp.exp2(x - m_new) for x in s]
        l_new = alpha * l_prev
        for x in p:
            l_new = l_new + jnp.sum(x, axis=-1, keepdims=True)
        pv = [jnp.dot(p[i].astype(BF16), vbuf[grp * pps + i].reshape(page_rows, ATT_HD).astype(BF16),
                      preferred_element_type=F32) for i in range(pps)]
        acc = alpha * acc
        for x in pv:
            acc = acc + x
        return m_new, l_new, acc

    m, l, acc = lax.fori_loop(0, n_steps, att_body, (m0, l0, acc0))
    o_ref[0] = acc / l * _silu(g_ref[0])


def _sample_attention(page_table, qb, iqb, wv, ikb, k32, v32, g, cache_idx_k, cache_k, cache_v, n_sel):
    nb, n_pages = page_table.shape
    head3 = pl.BlockSpec((1, ATT_HEADS, ATT_HD), lambda b, pt: (b, 0, 0))
    any_spec = pl.BlockSpec(memory_space=pl.ANY)
    page_rows = PAGE_SIZE * ATT_HEADS
    pps = math.gcd(DECODE_PAGES_PER_STEP, n_pages)
    expand = (jnp.arange(page_rows)[None, :] // ATT_HEADS == jnp.arange(PAGE_SIZE)[:, None]).astype(BF16)
    grid_spec = pltpu.PrefetchScalarGridSpec(
        num_scalar_prefetch=1,
        grid=(nb,),
        in_specs=[head3,
                  pl.BlockSpec((1, IDX_HEADS, IDX_HD), lambda b, pt: (b, 0, 0)),
                  pl.BlockSpec((1, IDX_HEADS, 1), lambda b, pt: (b, 0, 0)),
                  pl.BlockSpec((1, 1, IDX_HD), lambda b, pt: (b, 0, 0)),
                  head3, head3, head3,
                  pl.BlockSpec((PAGE_SIZE, page_rows), lambda b, pt: (0, 0)),
                  any_spec, any_spec, any_spec],
        out_specs=head3,
        scratch_shapes=[pltpu.VMEM((n_pages, IDX_HD, PAGE_SIZE), F32),
                        pltpu.VMEM((2 * pps, PAGE_SIZE, ATT_HEADS, ATT_HD), F32),
                        pltpu.VMEM((2 * pps, PAGE_SIZE, ATT_HEADS, ATT_HD), F32),
                        pltpu.VMEM((n_pages, page_rows), F32),
                        pltpu.SemaphoreType.DMA((n_pages,)),
                        pltpu.SemaphoreType.DMA((2 * pps,)),
                        pltpu.SemaphoreType.DMA((2 * pps,))],
    )
    head = lambda a: a.reshape(nb, ATT_HEADS, ATT_HD)
    out = pl.pallas_call(
        functools.partial(_sample_attn_kernel, n_sel=n_sel, pps=pps),
        grid_spec=grid_spec,
        out_shape=jax.ShapeDtypeStruct((nb, ATT_HEADS, ATT_HD), F32),
        compiler_params=_cparams(("arbitrary",)),
        name="sample_attention",
    )(page_table, head(qb),
      iqb.reshape(nb, IDX_HEADS, IDX_HD),
      wv[:, IDX_HD:IDX_HD + IDX_HEADS].reshape(nb, IDX_HEADS, 1),
      ikb.reshape(nb, 1, IDX_HD),
      head(k32), head(v32), head(g), expand,
      jnp.swapaxes(cache_idx_k, 1, 2), cache_k, cache_v)
    return out.reshape(nb, ATT_W).astype(BF16)


def _rwkv_prep_kernel(*refs, shifted):
    n_x = 5
    x_refs = refs[:n_x]
    if shifted:
        halo_refs, first_refs = refs[n_x:2 * n_x], refs[2 * n_x:3 * n_x]
        rest = refs[3 * n_x:]
    else:
        prev_refs = refs[n_x:2 * n_x]
        rest = refs[2 * n_x:]
    (mu_ref, mus_ref, wl_ref, w0_ref, a0_ref, kk_ref, ka_ref, rk_ref, ones_ref,
     ro_ref, kpo_ref, vo_ref, kko_ref, bo_ref, lwo_ref, bonus_ref, sg_ref) = rest

    def lerp(j, mu):
        x = x_refs[j][...]
        if shifted:
            above = jnp.where(pl.program_id(0) == 0, first_refs[j][...],
                              halo_refs[j][SUBLANES - 1:SUBLANES, :])
            row = lax.broadcasted_iota(I32, x.shape, 0)
            prev = jnp.where(row == 0, above, pltpu.roll(x, 1, axis=0))
        else:
            prev = prev_refs[j][...]
        return x + mu * (prev - x)

    w = RWKV_W
    xr = lerp(0, mu_ref[:, 0:w])
    xk = lerp(1, mu_ref[:, w:2 * w])
    xv = lerp(2, mu_ref[:, 2 * w:3 * w])
    xg = lerp(3, mu_ref[:, 3 * w:4 * w])
    xs = lerp(4, mus_ref[...])
    lane = lax.broadcasted_iota(I32, xs.shape, 1)
    lora_in = jnp.where(lane < DECAY_LORA, jnp.tanh(xs), xs)
    lora = _dot3(lora_in, wl_ref[...])
    z = w0_ref[...] + lora[:, :w]
    softplus_neg = jnp.maximum(-z, 0.0) + jnp.log(1.0 + jnp.exp(-jnp.abs(z)))
    w_log = -softplus_neg - 0.5
    lwo_ref[...] = -jnp.exp(w_log)
    a = jax.nn.sigmoid(a0_ref[...] + lora[:, w:])
    ones_bd = ones_ref[...]
    kk = xk * kk_ref[...]
    kk = kk / jnp.maximum(jnp.sqrt(_head_sum(kk * kk, ones_bd)), 1e-12)
    kp = xk * (1.0 + (a - 1.0) * ka_ref[...])
    ro_ref[...] = xr
    kpo_ref[...] = kp
    vo_ref[...] = xv
    kko_ref[...] = kk
    bo_ref[...] = kk * a
    bonus_ref[...] = _head_sum(xr * kp * rk_ref[...], ones_bd) * xv
    sg_ref[...] = _silu(xg)


def _rwkv_prep(p, prev, consts, shifted):
    t = p.shape[0]
    tm = min(256, t)
    mu_big, mu_small, w_lora, w0, a0, k_k, k_a, r_k, ones_bd = consts
    cols = [(RWKV_W, COL_RR // RWKV_W), (RWKV_W, COL_RK // RWKV_W), (RWKV_W, COL_RV // RWKV_W),
            (RWKV_W, COL_RG // RWKV_W), (LANES, COL_SMALL_R // LANES)]
    x_specs = [pl.BlockSpec((tm, wd), lambda i, c=c: (i, c)) for wd, c in cols]
    if shifted:
        per8 = tm // SUBLANES
        halo = [pl.BlockSpec((SUBLANES, wd), lambda i, c=c: (jnp.maximum(i * per8 - 1, 0), c))
                for wd, c in cols]
        first = [pl.BlockSpec((1, wd), lambda i, c=c: (0, c)) for wd, c in cols]
        extra_specs, extra_args = halo + first, [p] * 5 + [prev] * 5
    else:
        extra_specs, extra_args = x_specs, [prev] * 5
    full = lambda a: pl.BlockSpec(a.shape, lambda i: (0, 0))
    row = pl.BlockSpec((tm, RWKV_W), lambda i: (i, 0))
    vec = lambda a: a.reshape(1, -1)
    args = [mu_big, mu_small, w_lora, vec(w0), vec(a0), vec(k_k), vec(k_a), vec(r_k), ones_bd]
    return pl.pallas_call(
        functools.partial(_rwkv_prep_kernel, shifted=shifted),
        grid=(t // tm,),
        in_specs=x_specs + extra_specs + [full(a) for a in args],
        out_specs=[row] * 8,
        out_shape=[jax.ShapeDtypeStruct((t, RWKV_W), F32)] * 8,
        compiler_params=_cparams(("arbitrary",)),
        name="rwkv_prep",
    )(*([p] * 5), *extra_args, *args)


def _scan_kernel(r_ref, kp_ref, v_ref, kk_ref, b_ref, lw_ref, s0_ref, y_ref, st_ref, s_ref):
    @pl.when(pl.program_id(0) == 0)
    def _():
        s_ref[...] = s0_ref[...]

    ch = SCAN_CHUNK
    assert ch == RWKV_HD
    rows_total = r_ref.shape[0]
    n_sub = rows_total // ch
    gw = MXU_DIM
    nh = HEADS_PER_GROUP
    lw = lw_ref[...]
    tr = lax.broadcasted_iota(I32, (rows_total, rows_total), 0)
    tc = lax.broadcasted_iota(I32, (rows_total, rows_total), 1)
    tri = jnp.where((tr >= tc) & (tr // ch == tc // ch), 1.0, 0.0).astype(BF16)
    cum = _dot_exact_lhs(tri, lw)
    cum_end = jnp.concatenate(
        [jnp.broadcast_to(cum[(c + 1) * ch - 1:(c + 1) * ch, :], (ch, RWKV_W)) for c in range(n_sub)], axis=0)
    kk, b, kp, r, v = kk_ref[...], b_ref[...], kp_ref[...], r_ref[...], v_ref[...]
    g_inv = jnp.exp(-cum)
    g_end = jnp.exp(cum_end - cum)
    a_t = (-kk * jnp.exp(cum - lw)).astype(BF16)
    b_t = b * g_inv
    k_t = kp * g_inv
    r_t = (r * jnp.exp(cum)).astype(BF16)
    b_h = (b * g_end).astype(BF16)
    k_h = (kp * g_end).astype(BF16)
    vb = v.astype(BF16)

    srow = lax.broadcasted_iota(I32, (nh * ch, gw), 0)
    scol = lax.broadcasted_iota(I32, (nh * ch, gw), 1)
    own = (srow // ch) == (scol // RWKV_HD)
    step = lax.broadcasted_iota(I32, (ch, gw), 0)
    other = lax.broadcasted_iota(I32, (ch, gw), 1) % ch
    strict, incl = step > other, step >= other
    eye = jnp.where(step == other, 1.0, 0.0)

    def stack(x):
        return jnp.where(own, jnp.concatenate([x] * nh, axis=0), 0.0).astype(BF16)

    groups = range(N_GROUPS)
    units = [(c, g) for c in range(n_sub) for g in groups]
    blk = {(c, g): (slice(c * ch, (c + 1) * ch), slice(g * gw, (g + 1) * gw)) for c, g in units}
    rb = {u: stack(b_t[blk[u]]) for u in units}
    rk = {u: stack(k_t[blk[u]]) for u in units}
    v4 = {u: stack(v[blk[u]]) for u in units}
    ar = {u: jnp.concatenate([a_t[blk[u]], r_t[blk[u]]], axis=0) for u in units}
    g_b = {u: _dg(ar[u], rb[u], _NT) for u in units}
    g_k = {u: _dg(ar[u], rk[u], _NT) for u in units}
    n_row = {u: jnp.where(strict, g_b[u][:ch], 0.0) for u in units}
    bm_row = {u: jnp.where(strict, g_k[u][:ch], 0.0).astype(BF16) for u in units}
    rb_row = {u: jnp.where(incl, g_b[u][ch:], 0.0).astype(BF16) for u in units}
    rk_row = {u: jnp.where(incl, g_k[u][ch:], 0.0).astype(BF16) for u in units}
    minv = {u: eye + n_row[u] for u in units}
    pw = {u: _dg(n_row[u].astype(BF16), stack(n_row[u]), _NN) for u in units}
    levels = int(math.log2(ch)) - 1
    for level in range(levels):
        pw_bd = {u: stack(pw[u]) for u in units}
        if level + 1 < levels:
            both = {u: _dg(jnp.concatenate([minv[u], pw[u]], axis=0).astype(BF16), pw_bd[u], _NN) for u in units}
            minv = {u: minv[u] + both[u][:ch] for u in units}
            pw = {u: both[u][ch:] for u in units}
        else:
            minv = {u: minv[u] + _dg(minv[u].astype(BF16), pw_bd[u], _NN) for u in units}
    minv = {u: minv[u].astype(BF16) for u in units}
    xy_fixed = {u: _dg(jnp.concatenate([bm_row[u], rk_row[u]], axis=0), v4[u], _NN) for u in units}
    x_fixed = {u: xy_fixed[u][:ch] for u in units}
    y_fixed = {u: xy_fixed[u][ch:] for u in units}

    s = [s_ref[g] for g in groups]
    for c in range(n_sub):
        sb = [x.astype(BF16) for x in s]
        ar_s = [_dg(ar[c, g], sb[g], _NT) for g in groups]
        x_row = [ar_s[g][:ch] + x_fixed[c, g] for g in groups]
        x_bd = [stack(x) for x in x_row]
        u_row = [_dg(minv[c, g], x_bd[g], _NN) for g in groups]
        u_bd = [stack(x) for x in u_row]
        for g in groups:
            y_ref[blk[c, g]] = ar_s[g][ch:] + _dg(rb_row[c, g], u_bd[g], _NN) + y_fixed[c, g]
        for g in groups:
            rs, ls = blk[c, g]
            upd = _dg(jnp.concatenate([u_row[g].astype(BF16), vb[rs, ls]], axis=0),
                      jnp.concatenate([b_h[rs, ls], k_h[rs, ls]], axis=0), _TN)
            g_c = jnp.exp(cum[(c + 1) * ch - 1:(c + 1) * ch, ls])
            s[g] = s[g] * g_c + jnp.where(own, upd, 0.0)
    for g in groups:
        s_ref[g] = s[g]

    @pl.when(pl.program_id(0) == pl.num_programs(0) - 1)
    def _():
        st_ref[...] = s_ref[...]


def _rwkv_scan(r, kp, v, kk, b, lw, s0):
    t = r.shape[0]
    ch = SCAN_CHUNK * min(SCAN_SUB, t // SCAN_CHUNK)
    row = pl.BlockSpec((ch, RWKV_W), lambda i: (i, 0))
    st = pl.BlockSpec((N_GROUPS, MXU_DIM, MXU_DIM), lambda i: (0, 0, 0))
    return pl.pallas_call(
        _scan_kernel,
        grid=(t // ch,),
        in_specs=[row] * 6 + [st],
        out_specs=[row, st],
        out_shape=[jax.ShapeDtypeStruct((t, RWKV_W), F32),
                   jax.ShapeDtypeStruct((N_GROUPS, MXU_DIM, MXU_DIM), F32)],
        scratch_shapes=[pltpu.VMEM((N_GROUPS, MXU_DIM, MXU_DIM), F32)],
        compiler_params=_cparams(("arbitrary",)),
        name="rwkv_scan",
    )(r, kp, v, kk, b, lw, s0)


def _rwkv_step_kernel(s_ref, r_ref, kp_ref, kk_ref, b_ref, lw_ref, v_ref, so_ref, y_ref):
    s = s_ref[...]
    sa = -jnp.sum(s * kk_ref[...], axis=-1, keepdims=True)
    s_new = s * jnp.exp(lw_ref[...]) + sa * b_ref[...] + v_ref[...] * kp_ref[...]
    so_ref[...] = s_new
    y_ref[...] = jnp.sum(s_new * r_ref[...], axis=-1, keepdims=True)


def _rwkv_step(state, r, kp, kk, b, lw, v):
    nb = state.shape[0]
    ts = math.gcd(4, nb)
    rowv = lambda a: a.reshape(nb, RWKV_HEADS, 1, RWKV_HD)
    rspec = pl.BlockSpec((ts, RWKV_HEADS, 1, RWKV_HD), lambda i: (i, 0, 0, 0))
    cspec = pl.BlockSpec((ts, RWKV_HEADS, RWKV_HD, 1), lambda i: (i, 0, 0, 0))
    sspec = pl.BlockSpec((ts, RWKV_HEADS, RWKV_HD, RWKV_HD), lambda i: (i, 0, 0, 0))
    s_new, y = pl.pallas_call(
        _rwkv_step_kernel,
        grid=(nb // ts,),
        in_specs=[sspec, rspec, rspec, rspec, rspec, rspec, cspec],
        out_specs=[sspec, cspec],
        out_shape=[jax.ShapeDtypeStruct(state.shape, F32),
                   jax.ShapeDtypeStruct((nb, RWKV_HEADS, RWKV_HD, 1), F32)],
        compiler_params=_cparams(("arbitrary",)),
        name="rwkv_step",
    )(state, rowv(r), rowv(kp), rowv(kk), rowv(b), rowv(lw), v.reshape(nb, RWKV_HEADS, RWKV_HD, 1))
    return s_new, y.reshape(nb, RWKV_W)


def _rwkv_post_kernel(y_ref, bonus_ref, sg_ref, lg_ref, lb_ref, ones_ref, o_ref):
    y = y_ref[...]
    ones_bd = ones_ref[...]
    mean = _head_sum(y, ones_bd) * (1.0 / RWKV_HD)
    yc = y - mean
    var = _head_sum(yc * yc, ones_bd) * (1.0 / RWKV_HD)
    yn = yc * lax.rsqrt(var + GN_EPS) * lg_ref[...] + lb_ref[...]
    o_ref[...] = ((yn + bonus_ref[...]) * sg_ref[...]).astype(BF16)


def _rwkv_post(y, bonus, sg, ln_g, ln_b, ones_bd):
    t = y.shape[0]
    tm = min(512, t)
    row = pl.BlockSpec((tm, RWKV_W), lambda i: (i, 0))
    vec = pl.BlockSpec((1, RWKV_W), lambda i: (0, 0))
    return pl.pallas_call(
        _rwkv_post_kernel,
        grid=(t // tm,),
        in_specs=[row, row, row, vec, vec, pl.BlockSpec(ones_bd.shape, lambda i: (0, 0))],
        out_specs=row,
        out_shape=jax.ShapeDtypeStruct((t, RWKV_W), BF16),
        compiler_params=_cparams(("arbitrary",)),
        name="rwkv_post",
    )(y, bonus, sg, ln_g.reshape(1, -1), ln_b.reshape(1, -1), ones_bd)


def _outproj_kernel(att_ref, rw_ref, wa_ref, wr_ref, x_ref, gate_ref, o_ref):
    out = (jnp.dot(att_ref[...], wa_ref[...], preferred_element_type=F32)
           + jnp.dot(rw_ref[...], wr_ref[...], preferred_element_type=F32))
    o_ref[...] = x_ref[...] + gate_ref[...] * out


def _out_projection(att, rw, w_out_bf16, x, gate):
    t, d = x.shape
    tm = min(512, t)
    per_row = gate.shape[0] != 1
    gspec = (pl.BlockSpec((tm, d), lambda i: (i, 0)) if per_row else pl.BlockSpec((1, d), lambda i: (0, 0)))
    return pl.pallas_call(
        _outproj_kernel,
        grid=(t // tm,),
        in_specs=[pl.BlockSpec((tm, ATT_W), lambda i: (i, 0)),
                  pl.BlockSpec((tm, RWKV_W), lambda i: (i, 0)),
                  pl.BlockSpec((ATT_W, d), lambda i: (0, 0)),
                  pl.BlockSpec((RWKV_W, d), lambda i: (1, 0)),
                  pl.BlockSpec((tm, d), lambda i: (i, 0)),
                  gspec],
        out_specs=pl.BlockSpec((tm, d), lambda i: (i, 0)),
        out_shape=jax.ShapeDtypeStruct((t, d), F32),
        compiler_params=_cparams(("arbitrary",)),
        name="out_projection",
    )(att, rw, w_out_bf16, w_out_bf16, x, gate)


def _reorder_cols(w, axis=-1):
    axis = axis % w.ndim

    def cut(start, size):
        return lax.slice_in_dim(w, start, start + size, axis=axis)

    def zeros(size):
        shape = list(w.shape)
        shape[axis] = size
        return jnp.zeros(shape, w.dtype)

    q, k, v, g = (cut(i * ATT_W, ATT_W) for i in range(4))
    a = 4 * ATT_W
    iq, iw, ik = cut(a, IDX_W), cut(a + IDX_W, IDX_HEADS), cut(a + IDX_W + IDX_HEADS, IDX_HD)
    o = N_ATT_COLS_ORIG
    rr, rk, rv = (cut(o + i * RWKV_W, RWKV_W) for i in range(3))
    o2 = o + 3 * RWKV_W
    wl, al = cut(o2, DECAY_LORA), cut(o2 + DECAY_LORA, ICL_LORA)
    rg = cut(o2 + DECAY_LORA + ICL_LORA, RWKV_W)
    return jnp.concatenate([q, k, v, g, iq, rr, rk, rv, rg, ik, iw, zeros(LANES - IDX_HD - IDX_HEADS),
                            wl, al, zeros(N_COLS_PAD - N_COLS)], axis=axis)


def _reorder_cast_kernel(wt_ref, o_ref):
    o_ref[...] = _reorder_cols(wt_ref[...], axis=0).T.astype(BF16)


def _reorder_cast(wt):
    n, d = wt.shape
    td = min(256, d)
    return pl.pallas_call(
        _reorder_cast_kernel,
        grid=(d // td,),
        in_specs=[pl.BlockSpec((n, td), lambda i: (0, i))],
        out_specs=pl.BlockSpec((td, N_COLS_PAD), lambda i: (i, 0)),
        out_shape=jax.ShapeDtypeStruct((d, N_COLS_PAD), BF16),
        compiler_params=_cparams(("arbitrary",)),
        name="reorder_cast",
    )(wt)


def _rwkv_cols_original_order(p):
    return jnp.concatenate([p[..., COL_RR:COL_RR + RWKV_W], p[..., COL_RK:COL_RK + RWKV_W],
                            p[..., COL_RV:COL_RV + RWKV_W],
                            p[..., COL_SMALL_R:COL_SMALL_R + DECAY_LORA + ICL_LORA],
                            p[..., COL_RG:COL_RG + RWKV_W]], axis=-1)


def _blockdiag_to_state(sb):
    sb = sb.reshape(N_GROUPS, HEADS_PER_GROUP, RWKV_HD, HEADS_PER_GROUP, RWKV_HD)
    return jnp.stack([sb[:, h, :, h, :] for h in range(HEADS_PER_GROUP)], axis=1).reshape(
        RWKV_HEADS, RWKV_HD, RWKV_HD)


def kernel(x_prompt, x_sample, c_prompt, c_sample, cache_k, cache_v, cache_idx_k, state_wkv, state_shift,
           page_table, w_ada, b_ada, norm_g, w_in, q_norm_g, k_norm_g, shift_mu, w0, w_b, a0, a_b,
           k_k, k_a, r_k, ln_x_g, ln_x_b, w_out):
    depth = w_ada.shape[0]
    assert depth == 1, "single-layer trunk"
    bsz, seq, d = x_prompt.shape
    assert bsz == 1
    nb, dec_seq, _ = x_sample.shape
    assert dec_seq == 1
    n_pages = page_table.shape[1]
    past_len = n_pages * PAGE_SIZE
    l = 0

    w_in_r = _reorder_cast(w_in[l].T)
    w_out_b = w_out[l].astype(BF16)
    mu = _reorder_cols(jnp.concatenate([jnp.zeros((N_ATT_COLS_ORIG,), F32), shift_mu[l]]))
    mu_big = mu[COL_RR:COL_RR + 4 * RWKV_W].reshape(1, -1)
    mu_small = mu[COL_SMALL_R:COL_SMALL_R + LANES].reshape(1, -1)
    zl = jnp.zeros((DECAY_LORA, RWKV_W), F32)
    w_lora = jnp.concatenate([jnp.concatenate([w_b[l], zl], axis=1),
                              jnp.concatenate([zl, a_b[l]], axis=1)], axis=0)
    hid = jnp.arange(LANES) // RWKV_HD
    ones_bd = (hid[:, None] == hid[None, :]).astype(BF16)
    rwkv_consts = (mu_big, mu_small, w_lora, w0[l], a0[l], k_k[l], k_a[l], r_k[l].reshape(-1), ones_bd)

    n_mod = 1 + nb
    c_all = jnp.concatenate([c_prompt, c_sample, jnp.zeros((-n_mod % SUBLANES, d), F32)], axis=0)
    mod = _modulation(c_all, w_ada[l], b_ada[l])
    shift_p, scale_p, gate_p = mod[0:1, :d], mod[0:1, d:2 * d], mod[0:1, 2 * d:]
    shift_s, scale_s, gate_s = mod[1:n_mod, :d], mod[1:n_mod, d:2 * d], mod[1:n_mod, 2 * d:]

    xp = x_prompt.reshape(seq, d)
    xs = x_sample.reshape(nb, d)

    p = _in_projection(xp, shift_p, scale_p, norm_g[l], w_in_r)
    qt, k32, kb, vt, v32, iqb, ik32, ikb, wv = _attention_prep(p, _rope_tables(jnp.arange(seq)),
                                                         q_norm_g[l], k_norm_g[l], transposed=True)
    n_sel = min(TOPK_MAX, seq // 4)
    tq = min(MXU_DIM, seq)
    bias = _prompt_indexer(iqb, wv[:, IDX_HD:IDX_HD + IDX_HEADS].T, ikb, n_sel, tq)
    att = _prompt_attention(qt, kb, vt, bias, p, tq)

    r_, kp_, v_, kk_, b_, lw_, bonus, sg = _rwkv_prep(p, jnp.zeros((1, p.shape[1]), F32), rwkv_consts,
                                                      shifted=True)
    y_scan, s_fin = _rwkv_scan(r_, kp_, v_, kk_, b_, lw_,
                               jnp.zeros((N_GROUPS, MXU_DIM, MXU_DIM), F32))
    rw = _rwkv_post(y_scan, bonus, sg, ln_x_g[l], ln_x_b[l], ones_bd)
    y_p = _out_projection(att, rw, w_out_b, xp, gate_p)

    k_prompt = k32.reshape(1, 1, seq, ATT_HEADS, ATT_HD)
    v_prompt = v32.reshape(1, 1, seq, ATT_HEADS, ATT_HD)
    idx_k_prompt = ik32.reshape(1, 1, seq, IDX_HD)
    wkv_prompt = _blockdiag_to_state(s_fin).reshape(1, 1, RWKV_HEADS, RWKV_HD, RWKV_HD)
    shift_prompt = _rwkv_cols_original_order(p[seq - 1:seq]).reshape(1, 1, -1)

    ps = _in_projection(xs, shift_s, scale_s, norm_g[l], w_in_r)
    pos_s = jnp.full((nb,), past_len, jnp.int32)
    qb_s, k32_s, _, _, v32_s, iqb_s, ik32_s, ikb_s, wv_s = _attention_prep(ps, _rope_tables(pos_s),
                                                                          q_norm_g[l], k_norm_g[l],
                                                                          transposed=False)
    n_sel_s = min(TOPK_MAX, (past_len + 1) // 4)
    att_s = _sample_attention(page_table, qb_s, iqb_s, wv_s, ikb_s, k32_s, v32_s,
                              ps[:, COL_G:COL_G + ATT_W], cache_idx_k[l], cache_k[l], cache_v[l], n_sel_s)

    prev_s = _reorder_cols(jnp.concatenate([jnp.zeros((nb, N_ATT_COLS_ORIG), F32), state_shift[l]], axis=-1))
    r_s, kp_s, v_s, kk_s, b_s, lw_s, bonus_s, sg_s = _rwkv_prep(ps, prev_s, rwkv_consts, shifted=False)
    wkv_new, y_step = _rwkv_step(state_wkv[l], r_s, kp_s, kk_s, b_s, lw_s, v_s)
    rw_s = _rwkv_post(y_step, bonus_s, sg_s, ln_x_g[l], ln_x_b[l], ones_bd)
    y_s = _out_projection(att_s, rw_s, w_out_b, xs, gate_s)

    k_sample = k32_s.reshape(1, nb, 1, ATT_HEADS, ATT_HD)
    v_sample = v32_s.reshape(1, nb, 1, ATT_HEADS, ATT_HD)
    idx_k_sample = ik32_s.reshape(1, nb, 1, IDX_HD)
    wkv_sample = wkv_new.reshape(1, nb, RWKV_HEADS, RWKV_HD, RWKV_HD)
    shift_sample = _rwkv_cols_original_order(ps).reshape(1, nb, -1)

    return (y_p.reshape(1, seq, d), y_s.reshape(nb, 1, d),
            k_prompt, v_prompt, idx_k_prompt, wkv_prompt, shift_prompt,
            k_sample, v_sample, idx_k_sample, wkv_sample, shift_sample)
```

```python
import functools
import math

import jax
import jax.numpy as jnp
from jax import lax
from jax.experimental import pallas as pl
from jax.experimental.pallas import tpu as pltpu

F32 = jnp.float32
BF16 = jnp.bfloat16
I32 = jnp.int32

ATT_HEADS = 8
ATT_HD = 128
ATT_W = ATT_HEADS * ATT_HD
IDX_HEADS = 16
IDX_HD = 64
IDX_W = IDX_HEADS * IDX_HD
TOPK_MAX = 256
PAGE_SIZE = 128
ROPE_THETA = 10000.0
RWKV_HD = 64
RWKV_HEADS = 16
RWKV_W = RWKV_HEADS * RWKV_HD
DECAY_LORA = 64
ICL_LORA = 64
NORM_EPS = 1e-6
GN_EPS = 64e-5

LANES = 128
SUBLANES = 8
MXU_DIM = 256
VMEM_LIMIT = 56 * 1024 * 1024
NEG_BIG = -1e30
INT_MIN = -(2 ** 31)

COL_Q, COL_K, COL_V, COL_G, COL_IQ = 0, 1024, 2048, 3072, 4096
COL_RR, COL_RK, COL_RV, COL_RG = 5120, 6144, 7168, 8192
COL_SMALL_A = 9216
COL_SMALL_R = 9344
N_COLS = 9472
N_COLS_PAD = 9728
N_ATT_COLS_ORIG = 4 * ATT_W + IDX_W + IDX_HEADS + IDX_HD

EARLY_EXIT_FROM_BIT = 24
FLASH_KEY_BLOCKS = 4
SCAN_CHUNK = 64
MOD_TN = 768
INPROJ_TM, INPROJ_TN = 1024, 512
PREP_TM = 256
POST_TM = 512
REORDER_TD = 256
STEP_SEQS = 4
DECODE_SLOT_GROUPS = 3
DECODE_PAGES_PER_STEP = 8
SCAN_SUB = 4
HEADS_PER_GROUP = MXU_DIM // RWKV_HD
N_GROUPS = RWKV_HEADS // HEADS_PER_GROUP


def _cparams(sem, vmem=VMEM_LIMIT):
    return pltpu.CompilerParams(dimension_semantics=sem, vmem_limit_bytes=vmem)


def _split2(x):
    hi = x.astype(BF16)
    lo = (x - hi.astype(F32)).astype(BF16)
    return hi, lo


def _split3(x):
    hi = x.astype(BF16)
    r = x - hi.astype(F32)
    mid = r.astype(BF16)
    lo = (r - mid.astype(F32)).astype(BF16)
    return hi, mid, lo


_NN = (((1,), (0,)), ((), ()))
_NT = (((1,), (1,)), ((), ()))
_TN = (((0,), (0,)), ((), ()))


def _dg(a, b, dims):
    return lax.dot_general(a, b, dims, preferred_element_type=F32)


def _dot3(a, b, dims=_NN):
    ah, al = _split2(a)
    bh, bl = _split2(b)
    return _dg(ah, bh, dims) + _dg(al, bh, dims) + _dg(ah, bl, dims)


def _dot_exact_lhs(a_bf16, b, dims=_NN):
    b0, b1, b2 = _split3(b)
    return _dg(a_bf16, b0, dims) + _dg(a_bf16, b1, dims) + _dg(a_bf16, b2, dims)


def _dot_exact_rhs(a, b_bf16, dims=_NN):
    a0, a1, a2 = _split3(a)
    return _dg(a0, b_bf16, dims) + _dg(a1, b_bf16, dims) + _dg(a2, b_bf16, dims)


def _silu(x):
    return x * jax.nn.sigmoid(x)


def _head_sum(x, ones_bd):
    parts = []
    for j in range(x.shape[1] // LANES):
        parts.append(_dot_exact_rhs(x[:, j * LANES:(j + 1) * LANES], ones_bd))
    return jnp.concatenate(parts, axis=1)


def _ordered_to_f32(key):
    return pltpu.bitcast(key ^ ((key >> 31) & 0x7FFFFFFF), F32)


def _mod_kernel(c_ref, w_ref, b_ref, o_ref):
    o_ref[...] = _dot3(_silu(c_ref[...]), w_ref[...]) + b_ref[...]


def _modulation(c, w_ada, b_ada):
    m, d = c.shape
    n = w_ada.shape[1]
    tn = MOD_TN
    return pl.pallas_call(
        _mod_kernel,
        grid=(n // tn,),
        in_specs=[pl.BlockSpec((m, d), lambda j: (0, 0)),
                  pl.BlockSpec((d, tn), lambda j: (0, j)),
                  pl.BlockSpec((1, tn), lambda j: (0, j))],
        out_specs=pl.BlockSpec((m, tn), lambda j: (0, j)),
        out_shape=jax.ShapeDtypeStruct((m, n), F32),
        compiler_params=_cparams(("arbitrary",)),
        name="modulation",
    )(c, w_ada, b_ada.reshape(1, n))


def _inproj_kernel(x_ref, shift_ref, scale_ref, g_ref, w_ref, o_ref, h_ref, *, rows):
    @pl.when(pl.program_id(1) == 0)
    def _():
        def body(r, c):
            sl = pl.ds(pl.multiple_of(r * rows, rows), rows)
            x = x_ref[sl, :]
            ms = jnp.mean(x * x, axis=-1, keepdims=True)
            y = x * lax.rsqrt(ms + NORM_EPS) * g_ref[...]
            if shift_ref.shape[0] == 1:
                h = y * (1.0 + scale_ref[...]) + shift_ref[...]
            else:
                h = y * (1.0 + scale_ref[sl, :]) + shift_ref[sl, :]
            h_ref[sl, :] = h.astype(BF16)
            return c
        lax.fori_loop(0, x_ref.shape[0] // rows, body, 0)

    o_ref[...] = jnp.dot(h_ref[...], w_ref[...], preferred_element_type=F32)


def _in_projection(x, shift, scale, norm_g, w_bf16):
    t, d = x.shape
    n = w_bf16.shape[1]
    tm = min(INPROJ_TM, t)
    tn = INPROJ_TN
    rows = min(128, tm)
    per_row = shift.shape[0] != 1
    mod_spec = (pl.BlockSpec((tm, d), lambda i, j: (i, 0)) if per_row
                else pl.BlockSpec((1, d), lambda i, j: (0, 0)))
    return pl.pallas_call(
        functools.partial(_inproj_kernel, rows=rows),
        grid=(t // tm, n // tn),
        in_specs=[pl.BlockSpec((tm, d), lambda i, j: (i, 0)),
                  mod_spec, mod_spec,
                  pl.BlockSpec((1, d), lambda i, j: (0, 0)),
                  pl.BlockSpec((d, tn), lambda i, j: (0, j))],
        out_specs=pl.BlockSpec((tm, tn), lambda i, j: (i, j)),
        out_shape=jax.ShapeDtypeStruct((t, n), F32),
        scratch_shapes=[pltpu.VMEM((tm, d), BF16)],
        compiler_params=_cparams(("arbitrary", "arbitrary")),
        name="in_projection",
    )(x, shift, scale, norm_g.reshape(1, d), w_bf16)


def _rope128(x, cos, sin_signed):
    return x * cos + pltpu.roll(x, 64, axis=1) * sin_signed


def _rope64(x, cos, sin_signed, first_half):
    rot = jnp.where(first_half, pltpu.roll(x, 96, axis=1), pltpu.roll(x, 32, axis=1))
    return x * cos + rot * sin_signed


def _prep_kernel(q_ref, k_ref, v_ref, iq_ref, sm_ref, c128_ref, s128_ref, c64_ref, s64_ref,
                 qg_ref, kg_ref,
                 qo_ref, k32_ref, kb_ref, vo_ref, v32_ref, iqb_ref, ik32_ref, ikb_ref, w_ref, *, transposed):
    c128, s128 = c128_ref[...], s128_ref[...]
    c64, s64 = c64_ref[...], s64_ref[...]
    tm = q_ref.shape[0]
    lane = lax.broadcasted_iota(I32, (tm, LANES), 1)
    first_half = (lane % IDX_HD) < (IDX_HD // 2)
    q_scale = ATT_HD ** -0.5 * math.log2(math.e)
    for h in range(ATT_HEADS):
        hs = slice(h * ATT_HD, (h + 1) * ATT_HD)
        q = q_ref[:, hs]
        q = q * lax.rsqrt(jnp.mean(q * q, axis=-1, keepdims=True) + NORM_EPS) * qg_ref[...]
        q = _rope128(q, c128, s128) * q_scale
        k = k_ref[:, hs]
        k = k * lax.rsqrt(jnp.mean(k * k, axis=-1, keepdims=True) + NORM_EPS) * kg_ref[...]
        k = _rope128(k, c128, s128)
        k32_ref[:, h, :] = k
        kb_ref[:, hs] = k.astype(BF16)
        if transposed:
            qo_ref[hs, :] = q.T.astype(BF16)
            vo_ref[hs, :] = v_ref[:, hs].T.astype(BF16)
        else:
            qo_ref[:, hs] = q.astype(BF16)
            vo_ref[:, hs] = v_ref[:, hs].astype(BF16)
    for h in range(ATT_HEADS):
        v32_ref[:, h, :] = v_ref[:, h * ATT_HD:(h + 1) * ATT_HD]
    for j in range(IDX_W // LANES):
        js = slice(j * LANES, (j + 1) * LANES)
        iqb_ref[:, js] = _rope64(iq_ref[:, js], c64, s64, first_half).astype(BF16)
    sm = sm_ref[...]
    ik = _rope64(sm, c64, s64, first_half)[:, :IDX_HD]
    ik32_ref[...] = ik
    ikb_ref[...] = ik.astype(BF16)
    w_ref[...] = sm * ((IDX_HEADS ** -0.5) * (IDX_HD ** -0.5))


def _attention_prep(p, tables, q_norm_g, k_norm_g, transposed):
    t = p.shape[0]
    tm = min(PREP_TM, t)
    c128, s128, c64, s64 = tables
    wide = lambda col: pl.BlockSpec((tm, ATT_W), lambda i, c=col // ATT_W: (i, c))
    tab = pl.BlockSpec((tm, LANES), lambda i: (i, 0))
    gain = pl.BlockSpec((1, ATT_HD), lambda i: (0, 0))
    row_w = pl.BlockSpec((tm, ATT_W), lambda i: (i, 0))
    col_w = pl.BlockSpec((ATT_W, tm), lambda i: (0, i))
    row64 = pl.BlockSpec((tm, IDX_HD), lambda i: (i, 0))
    head3 = pl.BlockSpec((tm, ATT_HEADS, ATT_HD), lambda i: (i, 0, 0))
    if transposed:
        qv_spec, qv_shape = col_w, jax.ShapeDtypeStruct((ATT_W, t), BF16)
    else:
        qv_spec, qv_shape = row_w, jax.ShapeDtypeStruct((t, ATT_W), BF16)
    return pl.pallas_call(
        functools.partial(_prep_kernel, transposed=transposed),
        grid=(t // tm,),
        in_specs=[wide(COL_Q), wide(COL_K), wide(COL_V), wide(COL_IQ),
                  pl.BlockSpec((tm, LANES), lambda i: (i, COL_SMALL_A // LANES)),
                  tab, tab, tab, tab, gain, gain],
        out_specs=[qv_spec, head3, row_w, qv_spec, head3, row_w, row64, row64, tab],
        out_shape=[qv_shape,
                   jax.ShapeDtypeStruct((t, ATT_HEADS, ATT_HD), F32),
                   jax.ShapeDtypeStruct((t, ATT_W), BF16),
                   qv_shape,
                   jax.ShapeDtypeStruct((t, ATT_HEADS, ATT_HD), F32),
                   jax.ShapeDtypeStruct((t, IDX_W), BF16),
                   jax.ShapeDtypeStruct((t, IDX_HD), F32),
                   jax.ShapeDtypeStruct((t, IDX_HD), BF16),
                   jax.ShapeDtypeStruct((t, LANES), F32)],
        compiler_params=_cparams(("arbitrary",)),
        name="attention_prep",
    )(p, p, p, p, p, c128, s128, c64, s64, q_norm_g.reshape(1, ATT_HD), k_norm_g.reshape(1, ATT_HD))


def _rope_tables(pos):
    pos = pos.astype(F32)[:, None]

    def tab(dh):
        inv = ROPE_THETA ** (-jnp.arange(0, dh, 2, dtype=F32) / dh)
        ang = pos * inv[None, :]
        cos, sin = jnp.cos(ang), jnp.sin(ang)
        c = jnp.concatenate([cos, cos], axis=-1)
        s = jnp.concatenate([-sin, sin], axis=-1)
        rep = LANES // dh
        return jnp.tile(c, (1, rep)), jnp.tile(s, (1, rep))

    c128, s128 = tab(ATT_HD)
    c64, s64 = tab(IDX_HD)
    return c128, s128, c64, s64


def _indexer_kernel(iq_ref, wt_ref, ik_ref, bias_ref, iqs_ref, cut_ref, *, n_sel):
    i = pl.program_id(0)
    tq = iq_ref.shape[0]
    tk = bias_ref.shape[2]
    nk_total = bias_ref.shape[1]
    nkb = i + 1

    for h in range(IDX_HEADS):
        iqs_ref[h * tq:(h + 1) * tq, :] = iq_ref[:, h * IDX_HD:(h + 1) * IDX_HD]

    key_pos = lax.broadcasted_iota(I32, (tk, tq), 0)
    qry_pos = lax.broadcasted_iota(I32, (tk, tq), 1) + i * tq

    def score_body(kb, c):
        ikt = ik_ref[pl.ds(pl.multiple_of(kb * tk, tk), tk), :]
        d = _dg(ikt, iqs_ref[...], _NT)
        acc = jnp.zeros((tk, tq), F32)
        for h in range(IDX_HEADS):
            acc = acc + jnp.maximum(d[:, h * tq:(h + 1) * tq], 0.0) * wt_ref[h:h + 1, :]
        bias_ref[0, kb] = jnp.where(key_pos + kb * tk <= qry_pos, acc, -jnp.inf)
        return c

    def score_pair(j, c):
        score_body(2 * j, c)
        return score_body(2 * j + 1, c)

    lax.fori_loop(0, nkb // 2, score_pair, 0)
    lax.fori_loop(2 * (nkb // 2), nkb, score_body, 0)

    def count_keys(pred):
        def one(kb, cnt):
            x = bias_ref[0, kb].reshape(tk // SUBLANES, SUBLANES, tq)
            pos = (key_pos + kb * tk).reshape(tk // SUBLANES, SUBLANES, tq)
            return cnt + jnp.sum(jnp.where(pred(x, pos), 1.0, 0.0), axis=0)

        def four(j, cnt):
            for u in range(4):
                cnt = one(4 * j + u, cnt)
            return cnt

        cnt = lax.fori_loop(0, nkb // 4, four, jnp.zeros((SUBLANES, tq), F32))
        cnt = lax.fori_loop(4 * (nkb // 4), nkb, one, cnt)
        return jnp.sum(cnt, axis=0, keepdims=True)

    def bit_body(bi, carry):
        ans, tot_ans = carry
        cand = ans + jnp.left_shift(jnp.int32(1), 31 - bi)
        cand_f = _ordered_to_f32(cand)
        tot = count_keys(lambda x, pos: x >= cand_f[None])
        ok = tot >= n_sel
        return jnp.where(ok, cand, ans), jnp.where(ok, tot, tot_ans)

    never_full = qry_pos[0:1, :] + 1 < n_sel

    def pending(tot_ans):
        return jnp.max(jnp.where((tot_ans == n_sel) | never_full, 0.0, 1.0))

    def late_bits(c):
        bi, ans, tot_ans, _ = c
        ans, tot_ans = bit_body(bi + 1, bit_body(bi, (ans, tot_ans)))
        return bi + 2, ans, tot_ans, pending(tot_ans)

    ans, tot_ans = lax.fori_loop(0, EARLY_EXIT_FROM_BIT, bit_body,
                                 (jnp.full((SUBLANES, tq), INT_MIN, I32), jnp.zeros((1, tq), F32)))
    _, ans, tot_ans, _ = lax.while_loop(lambda c: (c[0] < 32) & (c[3] > 0.0), late_bits,
                                        (jnp.int32(EARLY_EXIT_FROM_BIT), ans, tot_ans, pending(tot_ans)))
    thr = _ordered_to_f32(ans)[0:1, :]
    thr = jnp.where(thr != thr, -jnp.inf, thr)

    cut_ref[...] = jnp.full(cut_ref.shape, 2 ** 31 - 1, I32)

    @pl.when(jnp.max(tot_ans) > n_sel)
    def _():
        need = n_sel - count_keys(lambda x, pos: x > thr[None])
        n_bits = max(1, (nk_total * tk - 1).bit_length())

        def cut_body(bi, cut):
            cand = cut + jnp.left_shift(jnp.int32(1), n_bits - 1 - bi)
            below = count_keys(lambda x, pos: (x == thr[None]) & (pos < cand[None]))
            return jnp.where(below < need, cand, cut)

        cut = lax.fori_loop(0, n_bits, cut_body, jnp.zeros((1, tq), I32))
        cut_ref[...] = jnp.broadcast_to(jnp.where(tot_ans > n_sel, cut, 2 ** 31 - 1), cut_ref.shape)

    has_surplus_ties = jnp.max(tot_ans) > n_sel
    cut = cut_ref[0:1, :]

    def bias_body(kb, c):
        keep = (bias_ref[0, kb] >= thr) & (key_pos + kb * tk <= qry_pos)
        bias_ref[0, kb] = jnp.where(keep, 0.0, NEG_BIG)
        return c

    def bias_body_ties(kb, c):
        x = bias_ref[0, kb]
        pos = key_pos + kb * tk
        keep = ((x > thr) | ((x == thr) & (pos <= cut))) & (pos <= qry_pos)
        bias_ref[0, kb] = jnp.where(keep, 0.0, NEG_BIG)
        return c

    @pl.when(jnp.logical_not(has_surplus_ties))
    def _():
        lax.fori_loop(0, nkb, bias_body, 0)

    @pl.when(has_surplus_ties)
    def _():
        lax.fori_loop(0, nkb, bias_body_ties, 0)

    def fill_body(kb, c):
        bias_ref[0, kb] = jnp.full((tk, tq), NEG_BIG, F32)
        return c

    lax.fori_loop(nkb, nk_total, fill_body, 0)


def _prompt_indexer(iqb, wt, ikb, n_sel, tq):
    t = iqb.shape[0]
    nq = t // tq
    return pl.pallas_call(
        functools.partial(_indexer_kernel, n_sel=n_sel),
        grid=(nq,),
        in_specs=[pl.BlockSpec((tq, IDX_W), lambda i: (i, 0)),
                  pl.BlockSpec((IDX_HEADS, tq), lambda i: (0, i)),
                  pl.BlockSpec((t, IDX_HD), lambda i: (0, 0))],
        out_specs=pl.BlockSpec((1, nq, tq, tq), lambda i: (i, 0, 0, 0)),
        out_shape=jax.ShapeDtypeStruct((nq, nq, tq, tq), F32),
        scratch_shapes=[pltpu.VMEM((IDX_HEADS * tq, IDX_HD), BF16),
                        pltpu.VMEM((SUBLANES, tq), I32)],
        compiler_params=_cparams(("arbitrary",)),
        name="prompt_indexer",
    )(iqb, wt, ikb)


def _flash_kernel(qi_ref, ki_ref, qt_ref, k_ref, vt_ref, bias_ref, g_ref, o_ref, m_ref, l_ref, acc_ref):
    step = pl.program_id(0)
    qi, ki = qi_ref[step], ki_ref[step]

    @pl.when(ki == 0)
    def _():
        m_ref[...] = jnp.full(m_ref.shape, NEG_BIG, F32)
        l_ref[...] = jnp.zeros(l_ref.shape, F32)
        acc_ref[...] = jnp.zeros(acc_ref.shape, F32)

    heads = [slice(h * ATT_HD, (h + 1) * ATT_HD) for h in range(ATT_HEADS)]
    tq = qt_ref.shape[1]
    tkb = bias_ref.shape[2]
    for kh in range(bias_ref.shape[1]):
        ks = slice(kh * tkb, (kh + 1) * tkb)
        for qh in range(tq // LANES):
            qs = slice(qh * LANES, (qh + 1) * LANES)
            bias = bias_ref[0, kh, :, qs]
            m_prev, l_prev = m_ref[:, qs], l_ref[:, qs]
            m_rows, l_rows = [], []
            for h, hs in enumerate(heads):
                s = _dg(k_ref[ks, hs], qt_ref[hs, qs], _NN) + bias
                m_old = m_prev[h:h + 1, :]
                m_new = jnp.maximum(m_old, jnp.max(s, axis=0, keepdims=True))
                alpha = jnp.exp2(m_old - m_new)
                p = jnp.exp2(s - m_new)
                l_rows.append(alpha * l_prev[h:h + 1, :] + jnp.sum(p, axis=0, keepdims=True))
                m_rows.append(m_new)
                acc_ref[h, :, qs] = alpha * acc_ref[h, :, qs] + _dg(vt_ref[hs, ks], p.astype(BF16), _NN)
            m_ref[:, qs] = jnp.concatenate(m_rows, axis=0)
            l_ref[:, qs] = jnp.concatenate(l_rows, axis=0)

    @pl.when(ki == qi // bias_ref.shape[1])
    def _():
        for h in range(ATT_HEADS):
            hs = slice(h * ATT_HD, (h + 1) * ATT_HD)
            out = (acc_ref[h] / l_ref[h:h + 1, :]).T
            o_ref[:, hs] = (out * _silu(g_ref[:, hs])).astype(BF16)


def _prompt_attention(qt, kb, vt, bias, p, tq):
    t = kb.shape[0]
    nq = t // tq
    kpb = min(FLASH_KEY_BLOCKS, nq)
    tk = kpb * tq
    pairs = [(i, j) for i in range(nq) for j in range(i // kpb + 1)]
    qi = jnp.asarray([a for a, _ in pairs], I32)
    ki = jnp.asarray([b for _, b in pairs], I32)
    grid_spec = pltpu.PrefetchScalarGridSpec(
        num_scalar_prefetch=2,
        grid=(len(pairs),),
        in_specs=[pl.BlockSpec((ATT_W, tq), lambda s, qi, ki: (0, qi[s])),
                  pl.BlockSpec((tk, ATT_W), lambda s, qi, ki: (ki[s], 0)),
                  pl.BlockSpec((ATT_W, tk), lambda s, qi, ki: (0, ki[s])),
                  pl.BlockSpec((1, kpb, tq, tq), lambda s, qi, ki: (qi[s], ki[s], 0, 0)),
                  pl.BlockSpec((tq, ATT_W), lambda s, qi, ki: (qi[s], COL_G // ATT_W))],
        out_specs=pl.BlockSpec((tq, ATT_W), lambda s, qi, ki: (qi[s], 0)),
        scratch_shapes=[pltpu.VMEM((ATT_HEADS, tq), F32),
                        pltpu.VMEM((ATT_HEADS, tq), F32),
                        pltpu.VMEM((ATT_HEADS, ATT_HD, tq), F32)],
    )
    return pl.pallas_call(
        _flash_kernel,
        grid_spec=grid_spec,
        out_shape=jax.ShapeDtypeStruct((t, ATT_W), BF16),
        compiler_params=_cparams(("arbitrary",)),
        name="prompt_attention",
    )(qi, ki, qt, kb, vt, bias, p)


def _sample_attn_kernel(pt_ref, q_ref, iq_ref, w_ref, ikn_ref, kn_ref, vn_ref, g_ref, ex_ref,
                        cik_ref, ck_ref, cv_ref, o_ref,
                        ikbuf, kbuf, vbuf, sce_ref, isem, ksem, vsem, *, n_sel, pps):
    b = pl.program_id(0)
    n_pages = ikbuf.shape[0]
    page_rows = PAGE_SIZE * ATT_HEADS

    n_steps = n_pages // pps
    nb = pt_ref.shape[0]
    n_groups = kbuf.shape[0] // pps
    total_steps = nb * n_steps

    def idx_copy(seq, j):
        return pltpu.make_async_copy(cik_ref.at[pt_ref[seq, j]], ikbuf.at[j], isem.at[j])

    def k_copy(seq, j, slot):
        return pltpu.make_async_copy(ck_ref.at[pt_ref[seq, j]], kbuf.at[slot], ksem.at[slot])

    def v_copy(seq, j, slot):
        return pltpu.make_async_copy(cv_ref.at[pt_ref[seq, j]], vbuf.at[slot], vsem.at[slot])

    def start_idx(seq):
        def body(j, c):
            idx_copy(seq, j).start()
            return c
        lax.fori_loop(0, n_pages, body, 0)

    def start_group(gs):
        seq, t, grp = gs // n_steps, gs % n_steps, gs % n_groups
        for i in range(pps):
            k_copy(seq, t * pps + i, grp * pps + i).start()
            v_copy(seq, t * pps + i, grp * pps + i).start()

    @pl.when(b == 0)
    def _():
        start_idx(0)
        for gs in range(min(n_groups - 1, total_steps)):
            start_group(gs)

    iq = iq_ref[0]
    w = w_ref[0]

    def wait_idx(j, c):
        idx_copy(b, j).wait()
        return c

    lax.fori_loop(0, n_pages, wait_idx, 0)
    iq_pages = jnp.broadcast_to(iq[None], (n_pages, IDX_HEADS, IDX_HD))
    d = lax.dot_general(iq_pages, ikbuf[...].astype(BF16), (((2,), (1,)), ((0,), (0,))),
                        preferred_element_type=F32)
    sc = jnp.sum(jnp.maximum(d, 0.0) * w[None], axis=1)
    d_self = jnp.sum(iq.astype(F32) * ikn_ref[0].astype(F32), axis=-1, keepdims=True)
    s_self = jnp.sum(jnp.maximum(d_self, 0.0) * w, axis=0, keepdims=True)

    pos = (lax.broadcasted_iota(I32, sc.shape, 0) * PAGE_SIZE + lax.broadcasted_iota(I32, sc.shape, 1))
    self_pos = n_pages * PAGE_SIZE

    def count_keys(pred):
        cnt = jnp.sum(jnp.where(pred(sc, pos), 1.0, 0.0), axis=-1, keepdims=True)
        return jnp.sum(cnt, axis=0, keepdims=True) + jnp.where(pred(s_self, self_pos), 1.0, 0.0)

    def digit_body(it, carry):
        ans, tot_ans = carry
        step = jnp.left_shift(jnp.int32(1), 30 - 2 * it)
        best, best_tot = ans, tot_ans
        cand = ans
        for _ in range(3):
            cand = cand + step
            cand_f = _ordered_to_f32(cand)
            tot = count_keys(lambda x, p: x >= cand_f)
            best = jnp.where(tot >= n_sel, cand, best)
            best_tot = jnp.where(tot >= n_sel, tot, best_tot)
        return best, best_tot

    ans, tot_ans = lax.fori_loop(0, 16, digit_body, (jnp.full((1, 1), INT_MIN, I32), jnp.zeros((1, 1), F32)))
    thr = _ordered_to_f32(ans)
    thr = jnp.where(thr != thr, -jnp.inf, thr)

    def tie_cut():
        need = n_sel - count_keys(lambda x, p: x > thr)
        n_bits = self_pos.bit_length()

        def cut_body(bi, cut):
            cand = cut + jnp.left_shift(jnp.int32(1), n_bits - 1 - bi)
            below = count_keys(lambda x, p: (x == thr) & (p < cand))
            return jnp.where(below < need, cand, cut)

        return lax.fori_loop(0, n_bits, cut_body, jnp.zeros((1, 1), I32))

    cut = lax.cond(tot_ans[0, 0] > n_sel, tie_cut, lambda: jnp.full((1, 1), 2 ** 31 - 1, I32))

    def selected(x, p):
        return (x > thr) | ((x == thr) & (p <= cut))

    sce_ref[...] = _dot_exact_rhs(sc, ex_ref[...])

    @pl.when(b + 1 < nb)
    def _():
        start_idx(b + 1)

    q8 = q_ref[0]
    lane_e = lax.broadcasted_iota(I32, (ATT_HEADS, page_rows), 1)
    own = lane_e % ATT_HEADS == lax.broadcasted_iota(I32, (ATT_HEADS, page_rows), 0)
    tok_lane = lane_e[0:1, :] // ATT_HEADS

    kn = kn_ref[0].astype(BF16).astype(F32)
    s0 = jnp.sum(q8.astype(F32) * kn, axis=-1, keepdims=True)
    m0 = jnp.where(selected(s_self, self_pos), s0, NEG_BIG)
    l0 = jnp.where(m0 > 0.5 * NEG_BIG, 1.0, 0.0)
    acc0 = l0 * vn_ref[0].astype(BF16).astype(F32)

    def att_body(t, carry):
        m_prev, l_prev, acc = carry
        gs = b * n_steps + t
        grp = gs % n_groups

        @pl.when(gs + n_groups - 1 < total_steps)
        def _():
            start_group(gs + n_groups - 1)

        for i in range(pps):
            k_copy(b, t * pps + i, grp * pps + i).wait()
            v_copy(b, t * pps + i, grp * pps + i).wait()
        s = []
        for i in range(pps):
            kf = kbuf[grp * pps + i].reshape(page_rows, ATT_HD).astype(BF16)
            keep = own & selected(sce_ref[pl.ds(t * pps + i, 1), :], (t * pps + i) * PAGE_SIZE + tok_lane)
            s.append(jnp.where(keep, _dg(q8, kf, _NT), NEG_BIG))
        m_new = m_prev
        for x in s:
            m_new = jnp.maximum(m_new, jnp.max(x, axis=-1, keepdims=True))
        alpha = jnp.exp2(m_prev - m_new)
        p = [jnp.exp2(x - m_new) for x in s]
        l_new = alpha * l_prev
        for x in p:
            l_new = l_new + jnp.sum(x, axis=-1, keepdims=True)
        pv = [jnp.dot(p[i].astype(BF16), vbuf[grp * pps + i].reshape(page_rows, ATT_HD).astype(BF16),
                      preferred_element_type=F32) for i in range(pps)]
        acc = alpha * acc
        for x in pv:
            acc = acc + x
        return m_new, l_new, acc

    m, l, acc = lax.fori_loop(0, n_steps, att_body, (m0, l0, acc0))
    o_ref[0] = acc / l * _silu(g_ref[0])


def _sample_attention(page_table, qb, iqb, wv, ikb, k32, v32, g, cache_idx_k, cache_k, cache_v, n_sel):
    nb, n_pages = page_table.shape
    head3 = pl.BlockSpec((1, ATT_HEADS, ATT_HD), lambda b, pt: (b, 0, 0))
    any_spec = pl.BlockSpec(memory_space=pl.ANY)
    page_rows = PAGE_SIZE * ATT_HEADS
    pps = math.gcd(DECODE_PAGES_PER_STEP, n_pages)
    expand = (jnp.arange(page_rows)[None, :] // ATT_HEADS == jnp.arange(PAGE_SIZE)[:, None]).astype(BF16)
    grid_spec = pltpu.PrefetchScalarGridSpec(
        num_scalar_prefetch=1,
        grid=(nb,),
        in_specs=[head3,
                  pl.BlockSpec((1, IDX_HEADS, IDX_HD), lambda b, pt: (b, 0, 0)),
                  pl.BlockSpec((1, IDX_HEADS, 1), lambda b, pt: (b, 0, 0)),
                  pl.BlockSpec((1, 1, IDX_HD), lambda b, pt: (b, 0, 0)),
                  head3, head3, head3,
                  pl.BlockSpec((PAGE_SIZE, page_rows), lambda b, pt: (0, 0)),
                  any_spec, any_spec, any_spec],
        out_specs=head3,
        scratch_shapes=[pltpu.VMEM((n_pages, IDX_HD, PAGE_SIZE), F32),
                        pltpu.VMEM((DECODE_SLOT_GROUPS * pps, PAGE_SIZE, ATT_HEADS, ATT_HD), F32),
                        pltpu.VMEM((DECODE_SLOT_GROUPS * pps, PAGE_SIZE, ATT_HEADS, ATT_HD), F32),
                        pltpu.VMEM((n_pages, page_rows), F32),
                        pltpu.SemaphoreType.DMA((n_pages,)),
                        pltpu.SemaphoreType.DMA((DECODE_SLOT_GROUPS * pps,)),
                        pltpu.SemaphoreType.DMA((DECODE_SLOT_GROUPS * pps,))],
    )
    head = lambda a: a.reshape(nb, ATT_HEADS, ATT_HD)
    out = pl.pallas_call(
        functools.partial(_sample_attn_kernel, n_sel=n_sel, pps=pps),
        grid_spec=grid_spec,
        out_shape=jax.ShapeDtypeStruct((nb, ATT_HEADS, ATT_HD), F32),
        compiler_params=_cparams(("arbitrary",)),
        name="sample_attention",
    )(page_table, head(qb),
      iqb.reshape(nb, IDX_HEADS, IDX_HD),
      wv[:, IDX_HD:IDX_HD + IDX_HEADS].reshape(nb, IDX_HEADS, 1),
      ikb.reshape(nb, 1, IDX_HD),
      head(k32), head(v32), head(g), expand,
      jnp.swapaxes(cache_idx_k, 1, 2), cache_k, cache_v)
    return out.reshape(nb, ATT_W).astype(BF16)


def _rwkv_prep_kernel(*refs, shifted):
    n_x = 5
    x_refs = refs[:n_x]
    if shifted:
        halo_refs, first_refs = refs[n_x:2 * n_x], refs[2 * n_x:3 * n_x]
        rest = refs[3 * n_x:]
    else:
        prev_refs = refs[n_x:2 * n_x]
        rest = refs[2 * n_x:]
    (mu_ref, mus_ref, wl_ref, w0_ref, a0_ref, kk_ref, ka_ref, rk_ref, ones_ref,
     ro_ref, kpo_ref, vo_ref, kko_ref, bo_ref, lwo_ref, bonus_ref, sg_ref) = rest

    def lerp(j, mu):
        x = x_refs[j][...]
        if shifted:
            above = jnp.where(pl.program_id(0) == 0, first_refs[j][...],
                              halo_refs[j][SUBLANES - 1:SUBLANES, :])
            row = lax.broadcasted_iota(I32, x.shape, 0)
            prev = jnp.where(row == 0, above, pltpu.roll(x, 1, axis=0))
        else:
            prev = prev_refs[j][...]
        return x + mu * (prev - x)

    w = RWKV_W
    xr = lerp(0, mu_ref[:, 0:w])
    xk = lerp(1, mu_ref[:, w:2 * w])
    xv = lerp(2, mu_ref[:, 2 * w:3 * w])
    xg = lerp(3, mu_ref[:, 3 * w:4 * w])
    xs = lerp(4, mus_ref[...])
    lane = lax.broadcasted_iota(I32, xs.shape, 1)
    lora_in = jnp.where(lane < DECAY_LORA, jnp.tanh(xs), xs)
    lora = _dot3(lora_in, wl_ref[...])
    z = w0_ref[...] + lora[:, :w]
    softplus_neg = jnp.maximum(-z, 0.0) + jnp.log(1.0 + jnp.exp(-jnp.abs(z)))
    w_log = -softplus_neg - 0.5
    lwo_ref[...] = -jnp.exp(w_log)
    a = jax.nn.sigmoid(a0_ref[...] + lora[:, w:])
    ones_bd = ones_ref[...]
    kk = xk * kk_ref[...]
    kk = kk / jnp.maximum(jnp.sqrt(_head_sum(kk * kk, ones_bd)), 1e-12)
    kp = xk * (1.0 + (a - 1.0) * ka_ref[...])
    ro_ref[...] = xr
    kpo_ref[...] = kp
    vo_ref[...] = xv
    kko_ref[...] = kk
    bo_ref[...] = kk * a
    bonus_ref[...] = _head_sum(xr * kp * rk_ref[...], ones_bd) * xv
    sg_ref[...] = _silu(xg)


def _rwkv_prep(p, prev, consts, shifted):
    t = p.shape[0]
    tm = min(PREP_TM, t)
    mu_big, mu_small, w_lora, w0, a0, k_k, k_a, r_k, ones_bd = consts
    cols = [(RWKV_W, COL_RR // RWKV_W), (RWKV_W, COL_RK // RWKV_W), (RWKV_W, COL_RV // RWKV_W),
            (RWKV_W, COL_RG // RWKV_W), (LANES, COL_SMALL_R // LANES)]
    x_specs = [pl.BlockSpec((tm, wd), lambda i, c=c: (i, c)) for wd, c in cols]
    if shifted:
        per8 = tm // SUBLANES
        halo = [pl.BlockSpec((SUBLANES, wd), lambda i, c=c: (jnp.maximum(i * per8 - 1, 0), c))
                for wd, c in cols]
        first = [pl.BlockSpec((1, wd), lambda i, c=c: (0, c)) for wd, c in cols]
        extra_specs, extra_args = halo + first, [p] * 5 + [prev] * 5
    else:
        extra_specs, extra_args = x_specs, [prev] * 5
    full = lambda a: pl.BlockSpec(a.shape, lambda i: (0, 0))
    row = pl.BlockSpec((tm, RWKV_W), lambda i: (i, 0))
    vec = lambda a: a.reshape(1, -1)
    args = [mu_big, mu_small, w_lora, vec(w0), vec(a0), vec(k_k), vec(k_a), vec(r_k), ones_bd]
    return pl.pallas_call(
        functools.partial(_rwkv_prep_kernel, shifted=shifted),
        grid=(t // tm,),
        in_specs=x_specs + extra_specs + [full(a) for a in args],
        out_specs=[row] * 8,
        out_shape=[jax.ShapeDtypeStruct((t, RWKV_W), F32)] * 8,
        compiler_params=_cparams(("arbitrary",)),
        name="rwkv_prep",
    )(*([p] * 5), *extra_args, *args)


def _scan_kernel(r_ref, kp_ref, v_ref, kk_ref, b_ref, lw_ref, s0_ref, y_ref, st_ref, s_ref):
    @pl.when(pl.program_id(0) == 0)
    def _():
        s_ref[...] = s0_ref[...]

    ch = SCAN_CHUNK
    assert ch == RWKV_HD
    rows_total = r_ref.shape[0]
    n_sub = rows_total // ch
    gw = MXU_DIM
    nh = HEADS_PER_GROUP
    lw = lw_ref[...]
    tr = lax.broadcasted_iota(I32, (rows_total, rows_total), 0)
    tc = lax.broadcasted_iota(I32, (rows_total, rows_total), 1)
    tri = jnp.where((tr >= tc) & (tr // ch == tc // ch), 1.0, 0.0).astype(BF16)
    cum = _dot_exact_lhs(tri, lw)
    cum_end = jnp.concatenate(
        [jnp.broadcast_to(cum[(c + 1) * ch - 1:(c + 1) * ch, :], (ch, RWKV_W)) for c in range(n_sub)], axis=0)
    kk, b, kp, r, v = kk_ref[...], b_ref[...], kp_ref[...], r_ref[...], v_ref[...]
    g_inv = jnp.exp(-cum)
    g_end = jnp.exp(cum_end - cum)
    a_t = (-kk * jnp.exp(cum - lw)).astype(BF16)
    b_t = b * g_inv
    k_t = kp * g_inv
    r_t = (r * jnp.exp(cum)).astype(BF16)
    b_h = (b * g_end).astype(BF16)
    k_h = (kp * g_end).astype(BF16)
    vb = v.astype(BF16)

    srow = lax.broadcasted_iota(I32, (nh * ch, gw), 0)
    scol = lax.broadcasted_iota(I32, (nh * ch, gw), 1)
    own = (srow // ch) == (scol // RWKV_HD)
    step = lax.broadcasted_iota(I32, (ch, gw), 0)
    other = lax.broadcasted_iota(I32, (ch, gw), 1) % ch
    strict, incl = step > other, step >= other
    eye = jnp.where(step == other, 1.0, 0.0)

    def stack(x):
        return jnp.where(own, jnp.concatenate([x] * nh, axis=0), 0.0).astype(BF16)

    groups = range(N_GROUPS)
    units = [(c, g) for c in range(n_sub) for g in groups]
    blk = {(c, g): (slice(c * ch, (c + 1) * ch), slice(g * gw, (g + 1) * gw)) for c, g in units}
    rb = {u: stack(b_t[blk[u]]) for u in units}
    rk = {u: stack(k_t[blk[u]]) for u in units}
    v4 = {u: stack(v[blk[u]]) for u in units}
    ar = {u: jnp.concatenate([a_t[blk[u]], r_t[blk[u]]], axis=0) for u in units}
    g_b = {u: _dg(ar[u], rb[u], _NT) for u in units}
    g_k = {u: _dg(ar[u], rk[u], _NT) for u in units}
    n_row = {u: jnp.where(strict, g_b[u][:ch], 0.0) for u in units}
    bm_row = {u: jnp.where(strict, g_k[u][:ch], 0.0).astype(BF16) for u in units}
    rb_row = {u: jnp.where(incl, g_b[u][ch:], 0.0).astype(BF16) for u in units}
    rk_row = {u: jnp.where(incl, g_k[u][ch:], 0.0).astype(BF16) for u in units}
    minv = {u: eye + n_row[u] for u in units}
    pw = {u: _dg(n_row[u].astype(BF16), stack(n_row[u]), _NN) for u in units}
    levels = int(math.log2(ch)) - 1
    for level in range(levels):
        pw_bd = {u: stack(pw[u]) for u in units}
        if level + 1 < levels:
            both = {u: _dg(jnp.concatenate([minv[u], pw[u]], axis=0).astype(BF16), pw_bd[u], _NN) for u in units}
            minv = {u: minv[u] + both[u][:ch] for u in units}
            pw = {u: both[u][ch:] for u in units}
        else:
            minv = {u: minv[u] + _dg(minv[u].astype(BF16), pw_bd[u], _NN) for u in units}
    minv = {u: minv[u].astype(BF16) for u in units}
    xy_fixed = {u: _dg(jnp.concatenate([bm_row[u], rk_row[u]], axis=0), v4[u], _NN) for u in units}
    x_fixed = {u: xy_fixed[u][:ch] for u in units}
    y_fixed = {u: xy_fixed[u][ch:] for u in units}

    s = [s_ref[g] for g in groups]
    for c in range(n_sub):
        sb = [x.astype(BF16) for x in s]
        ar_s = [_dg(ar[c, g], sb[g], _NT) for g in groups]
        x_row = [ar_s[g][:ch] + x_fixed[c, g] for g in groups]
        x_bd = [stack(x) for x in x_row]
        u_row = [_dg(minv[c, g], x_bd[g], _NN) for g in groups]
        u_bd = [stack(x) for x in u_row]
        for g in groups:
            y_ref[blk[c, g]] = ar_s[g][ch:] + _dg(rb_row[c, g], u_bd[g], _NN) + y_fixed[c, g]
        for g in groups:
            rs, ls = blk[c, g]
            upd = _dg(jnp.concatenate([u_row[g].astype(BF16), vb[rs, ls]], axis=0),
                      jnp.concatenate([b_h[rs, ls], k_h[rs, ls]], axis=0), _TN)
            g_c = jnp.exp(cum[(c + 1) * ch - 1:(c + 1) * ch, ls])
            s[g] = s[g] * g_c + jnp.where(own, upd, 0.0)
    for g in groups:
        s_ref[g] = s[g]

    @pl.when(pl.program_id(0) == pl.num_programs(0) - 1)
    def _():
        st_ref[...] = s_ref[...]


def _rwkv_scan(r, kp, v, kk, b, lw, s0):
    t = r.shape[0]
    ch = SCAN_CHUNK * min(SCAN_SUB, t // SCAN_CHUNK)
    row = pl.BlockSpec((ch, RWKV_W), lambda i: (i, 0))
    st = pl.BlockSpec((N_GROUPS, MXU_DIM, MXU_DIM), lambda i: (0, 0, 0))
    return pl.pallas_call(
        _scan_kernel,
        grid=(t // ch,),
        in_specs=[row] * 6 + [st],
        out_specs=[row, st],
        out_shape=[jax.ShapeDtypeStruct((t, RWKV_W), F32),
                   jax.ShapeDtypeStruct((N_GROUPS, MXU_DIM, MXU_DIM), F32)],
        scratch_shapes=[pltpu.VMEM((N_GROUPS, MXU_DIM, MXU_DIM), F32)],
        compiler_params=_cparams(("arbitrary",)),
        name="rwkv_scan",
    )(r, kp, v, kk, b, lw, s0)


def _rwkv_step_kernel(s_ref, r_ref, kp_ref, kk_ref, b_ref, lw_ref, v_ref, so_ref, y_ref):
    s = s_ref[...]
    sa = -jnp.sum(s * kk_ref[...], axis=-1, keepdims=True)
    s_new = s * jnp.exp(lw_ref[...]) + sa * b_ref[...] + v_ref[...] * kp_ref[...]
    so_ref[...] = s_new
    y_ref[...] = jnp.sum(s_new * r_ref[...], axis=-1, keepdims=True)


def _rwkv_step(state, r, kp, kk, b, lw, v):
    nb = state.shape[0]
    ts = math.gcd(STEP_SEQS, nb)
    rowv = lambda a: a.reshape(nb, RWKV_HEADS, 1, RWKV_HD)
    rspec = pl.BlockSpec((ts, RWKV_HEADS, 1, RWKV_HD), lambda i: (i, 0, 0, 0))
    cspec = pl.BlockSpec((ts, RWKV_HEADS, RWKV_HD, 1), lambda i: (i, 0, 0, 0))
    sspec = pl.BlockSpec((ts, RWKV_HEADS, RWKV_HD, RWKV_HD), lambda i: (i, 0, 0, 0))
    s_new, y = pl.pallas_call(
        _rwkv_step_kernel,
        grid=(nb // ts,),
        in_specs=[sspec, rspec, rspec, rspec, rspec, rspec, cspec],
        out_specs=[sspec, cspec],
        out_shape=[jax.ShapeDtypeStruct(state.shape, F32),
                   jax.ShapeDtypeStruct((nb, RWKV_HEADS, RWKV_HD, 1), F32)],
        compiler_params=_cparams(("arbitrary",)),
        name="rwkv_step",
    )(state, rowv(r), rowv(kp), rowv(kk), rowv(b), rowv(lw), v.reshape(nb, RWKV_HEADS, RWKV_HD, 1))
    return s_new, y.reshape(nb, RWKV_W)


def _rwkv_post_kernel(y_ref, bonus_ref, sg_ref, lg_ref, lb_ref, ones_ref, o_ref):
    y = y_ref[...]
    ones_bd = ones_ref[...]
    mean = _head_sum(y, ones_bd) * (1.0 / RWKV_HD)
    yc = y - mean
    var = _head_sum(yc * yc, ones_bd) * (1.0 / RWKV_HD)
    yn = yc * lax.rsqrt(var + GN_EPS) * lg_ref[...] + lb_ref[...]
    o_ref[...] = ((yn + bonus_ref[...]) * sg_ref[...]).astype(BF16)


def _rwkv_post(y, bonus, sg, ln_g, ln_b, ones_bd):
    t = y.shape[0]
    tm = min(POST_TM, t)
    row = pl.BlockSpec((tm, RWKV_W), lambda i: (i, 0))
    vec = pl.BlockSpec((1, RWKV_W), lambda i: (0, 0))
    return pl.pallas_call(
        _rwkv_post_kernel,
        grid=(t // tm,),
        in_specs=[row, row, row, vec, vec, pl.BlockSpec(ones_bd.shape, lambda i: (0, 0))],
        out_specs=row,
        out_shape=jax.ShapeDtypeStruct((t, RWKV_W), BF16),
        compiler_params=_cparams(("arbitrary",)),
        name="rwkv_post",
    )(y, bonus, sg, ln_g.reshape(1, -1), ln_b.reshape(1, -1), ones_bd)


def _outproj_kernel(att_ref, rw_ref, wa_ref, wr_ref, x_ref, gate_ref, o_ref):
    out = (jnp.dot(att_ref[...], wa_ref[...], preferred_element_type=F32)
           + jnp.dot(rw_ref[...], wr_ref[...], preferred_element_type=F32))
    o_ref[...] = x_ref[...] + gate_ref[...] * out


def _out_projection(att, rw, w_out_bf16, x, gate):
    t, d = x.shape
    tm = min(POST_TM, t)
    per_row = gate.shape[0] != 1
    gspec = (pl.BlockSpec((tm, d), lambda i: (i, 0)) if per_row else pl.BlockSpec((1, d), lambda i: (0, 0)))
    return pl.pallas_call(
        _outproj_kernel,
        grid=(t // tm,),
        in_specs=[pl.BlockSpec((tm, ATT_W), lambda i: (i, 0)),
                  pl.BlockSpec((tm, RWKV_W), lambda i: (i, 0)),
                  pl.BlockSpec((ATT_W, d), lambda i: (0, 0)),
                  pl.BlockSpec((RWKV_W, d), lambda i: (1, 0)),
                  pl.BlockSpec((tm, d), lambda i: (i, 0)),
                  gspec],
        out_specs=pl.BlockSpec((tm, d), lambda i: (i, 0)),
        out_shape=jax.ShapeDtypeStruct((t, d), F32),
        compiler_params=_cparams(("arbitrary",)),
        name="out_projection",
    )(att, rw, w_out_bf16, w_out_bf16, x, gate)


def _reorder_cols(w, axis=-1):
    axis = axis % w.ndim

    def cut(start, size):
        return lax.slice_in_dim(w, start, start + size, axis=axis)

    def zeros(size):
        shape = list(w.shape)
        shape[axis] = size
        return jnp.zeros(shape, w.dtype)

    q, k, v, g = (cut(i * ATT_W, ATT_W) for i in range(4))
    a = 4 * ATT_W
    iq, iw, ik = cut(a, IDX_W), cut(a + IDX_W, IDX_HEADS), cut(a + IDX_W + IDX_HEADS, IDX_HD)
    o = N_ATT_COLS_ORIG
    rr, rk, rv = (cut(o + i * RWKV_W, RWKV_W) for i in range(3))
    o2 = o + 3 * RWKV_W
    wl, al = cut(o2, DECAY_LORA), cut(o2 + DECAY_LORA, ICL_LORA)
    rg = cut(o2 + DECAY_LORA + ICL_LORA, RWKV_W)
    return jnp.concatenate([q, k, v, g, iq, rr, rk, rv, rg, ik, iw, zeros(LANES - IDX_HD - IDX_HEADS),
                            wl, al, zeros(N_COLS_PAD - N_COLS)], axis=axis)


def _reorder_cast_kernel(wt_ref, o_ref):
    o_ref[...] = _reorder_cols(wt_ref[...], axis=0).T.astype(BF16)


def _reorder_cast(wt):
    n, d = wt.shape
    td = min(REORDER_TD, d)
    return pl.pallas_call(
        _reorder_cast_kernel,
        grid=(d // td,),
        in_specs=[pl.BlockSpec((n, td), lambda i: (0, i))],
        out_specs=pl.BlockSpec((td, N_COLS_PAD), lambda i: (i, 0)),
        out_shape=jax.ShapeDtypeStruct((d, N_COLS_PAD), BF16),
        compiler_params=_cparams(("arbitrary",)),
        name="reorder_cast",
    )(wt)


def _rwkv_cols_original_order(p):
    return jnp.concatenate([p[..., COL_RR:COL_RR + RWKV_W], p[..., COL_RK:COL_RK + RWKV_W],
                            p[..., COL_RV:COL_RV + RWKV_W],
                            p[..., COL_SMALL_R:COL_SMALL_R + DECAY_LORA + ICL_LORA],
                            p[..., COL_RG:COL_RG + RWKV_W]], axis=-1)


def _blockdiag_to_state(sb):
    sb = sb.reshape(N_GROUPS, HEADS_PER_GROUP, RWKV_HD, HEADS_PER_GROUP, RWKV_HD)
    return jnp.stack([sb[:, h, :, h, :] for h in range(HEADS_PER_GROUP)], axis=1).reshape(
        RWKV_HEADS, RWKV_HD, RWKV_HD)


def kernel(x_prompt, x_sample, c_prompt, c_sample, cache_k, cache_v, cache_idx_k, state_wkv, state_shift,
           page_table, w_ada, b_ada, norm_g, w_in, q_norm_g, k_norm_g, shift_mu, w0, w_b, a0, a_b,
           k_k, k_a, r_k, ln_x_g, ln_x_b, w_out):
    depth = w_ada.shape[0]
    assert depth == 1, "single-layer trunk"
    bsz, seq, d = x_prompt.shape
    assert bsz == 1
    nb, dec_seq, _ = x_sample.shape
    assert dec_seq == 1
    n_pages = page_table.shape[1]
    past_len = n_pages * PAGE_SIZE
    l = 0

    w_in_r = _reorder_cast(w_in[l].T)
    w_out_b = w_out[l].astype(BF16)
    mu = _reorder_cols(jnp.concatenate([jnp.zeros((N_ATT_COLS_ORIG,), F32), shift_mu[l]]))
    mu_big = mu[COL_RR:COL_RR + 4 * RWKV_W].reshape(1, -1)
    mu_small = mu[COL_SMALL_R:COL_SMALL_R + LANES].reshape(1, -1)
    zl = jnp.zeros((DECAY_LORA, RWKV_W), F32)
    w_lora = jnp.concatenate([jnp.concatenate([w_b[l], zl], axis=1),
                              jnp.concatenate([zl, a_b[l]], axis=1)], axis=0)
    hid = jnp.arange(LANES) // RWKV_HD
    ones_bd = (hid[:, None] == hid[None, :]).astype(BF16)
    rwkv_consts = (mu_big, mu_small, w_lora, w0[l], a0[l], k_k[l], k_a[l], r_k[l].reshape(-1), ones_bd)

    n_mod = 1 + nb
    c_all = jnp.concatenate([c_prompt, c_sample, jnp.zeros((-n_mod % SUBLANES, d), F32)], axis=0)
    mod = _modulation(c_all, w_ada[l], b_ada[l])
    shift_p, scale_p, gate_p = mod[0:1, :d], mod[0:1, d:2 * d], mod[0:1, 2 * d:]
    shift_s, scale_s, gate_s = mod[1:n_mod, :d], mod[1:n_mod, d:2 * d], mod[1:n_mod, 2 * d:]

    xp = x_prompt.reshape(seq, d)
    xs = x_sample.reshape(nb, d)

    p = _in_projection(xp, shift_p, scale_p, norm_g[l], w_in_r)
    qt, k32, kb, vt, v32, iqb, ik32, ikb, wv = _attention_prep(p, _rope_tables(jnp.arange(seq)),
                                                         q_norm_g[l], k_norm_g[l], transposed=True)
    n_sel = min(TOPK_MAX, seq // 4)
    tq = min(MXU_DIM, seq)
    bias = _prompt_indexer(iqb, wv[:, IDX_HD:IDX_HD + IDX_HEADS].T, ikb, n_sel, tq)
    att = _prompt_attention(qt, kb, vt, bias, p, tq)

    r_, kp_, v_, kk_, b_, lw_, bonus, sg = _rwkv_prep(p, jnp.zeros((1, p.shape[1]), F32), rwkv_consts,
                                                      shifted=True)
    y_scan, s_fin = _rwkv_scan(r_, kp_, v_, kk_, b_, lw_,
                               jnp.zeros((N_GROUPS, MXU_DIM, MXU_DIM), F32))
    rw = _rwkv_post(y_scan, bonus, sg, ln_x_g[l], ln_x_b[l], ones_bd)
    y_p = _out_projection(att, rw, w_out_b, xp, gate_p)

    k_prompt = k32.reshape(1, 1, seq, ATT_HEADS, ATT_HD)
    v_prompt = v32.reshape(1, 1, seq, ATT_HEADS, ATT_HD)
    idx_k_prompt = ik32.reshape(1, 1, seq, IDX_HD)
    wkv_prompt = _blockdiag_to_state(s_fin).reshape(1, 1, RWKV_HEADS, RWKV_HD, RWKV_HD)
    shift_prompt = _rwkv_cols_original_order(p[seq - 1:seq]).reshape(1, 1, -1)

    ps = _in_projection(xs, shift_s, scale_s, norm_g[l], w_in_r)
    pos_s = jnp.full((nb,), past_len, jnp.int32)
    qb_s, k32_s, _, _, v32_s, iqb_s, ik32_s, ikb_s, wv_s = _attention_prep(ps, _rope_tables(pos_s),
                                                                          q_norm_g[l], k_norm_g[l],
                                                                          transposed=False)
    n_sel_s = min(TOPK_MAX, (past_len + 1) // 4)
    att_s = _sample_attention(page_table, qb_s, iqb_s, wv_s, ikb_s, k32_s, v32_s,
                              ps[:, COL_G:COL_G + ATT_W], cache_idx_k[l], cache_k[l], cache_v[l], n_sel_s)

    prev_s = _reorder_cols(jnp.concatenate([jnp.zeros((nb, N_ATT_COLS_ORIG), F32), state_shift[l]], axis=-1))
    r_s, kp_s, v_s, kk_s, b_s, lw_s, bonus_s, sg_s = _rwkv_prep(ps, prev_s, rwkv_consts, shifted=False)
    wkv_new, y_step = _rwkv_step(state_wkv[l], r_s, kp_s, kk_s, b_s, lw_s, v_s)
    rw_s = _rwkv_post(y_step, bonus_s, sg_s, ln_x_g[l], ln_x_b[l], ones_bd)
    y_s = _out_projection(att_s, rw_s, w_out_b, xs, gate_s)

    k_sample = k32_s.reshape(1, nb, 1, ATT_HEADS, ATT_HD)
    v_sample = v32_s.reshape(1, nb, 1, ATT_HEADS, ATT_HD)
    idx_k_sample = ik32_s.reshape(1, nb, 1, IDX_HD)
    wkv_sample = wkv_new.reshape(1, nb, RWKV_HEADS, RWKV_HD, RWKV_HD)
    shift_sample = _rwkv_cols_original_order(ps).reshape(1, nb, -1)

    return (y_p.reshape(1, seq, d), y_s.reshape(nb, 1, d),
            k_prompt, v_prompt, idx_k_prompt, wkv_prompt, shift_prompt,
            k_sample, v_sample, idx_k_sample, wkv_sample, shift_sample)
```

```python
import functools
import math

import jax
import jax.numpy as jnp
from jax import lax
from jax.experimental import pallas as pl
from jax.experimental.pallas import tpu as pltpu

F32 = jnp.float32
BF16 = jnp.bfloat16
I32 = jnp.int32

ATT_HEADS = 8
ATT_HD = 128
ATT_W = ATT_HEADS * ATT_HD
IDX_HEADS = 16
IDX_HD = 64
IDX_W = IDX_HEADS * IDX_HD
TOPK_MAX = 256
PAGE_SIZE = 128
ROPE_THETA = 10000.0
RWKV_HD = 64
RWKV_HEADS = 16
RWKV_W = RWKV_HEADS * RWKV_HD
DECAY_LORA = 64
ICL_LORA = 64
NORM_EPS = 1e-6
GN_EPS = 64e-5

LANES = 128
SUBLANES = 8
MXU_DIM = 256
VMEM_LIMIT = 56 * 1024 * 1024
NEG_BIG = -1e30
INT_MIN = -(2 ** 31)

COL_Q, COL_K, COL_V, COL_G, COL_IQ = 0, 1024, 2048, 3072, 4096
COL_RR, COL_RK, COL_RV, COL_RG = 5120, 6144, 7168, 8192
COL_SMALL_A = 9216
COL_SMALL_R = 9344
N_COLS = 9472
N_COLS_PAD = 9728
N_ATT_COLS_ORIG = 4 * ATT_W + IDX_W + IDX_HEADS + IDX_HD

EARLY_EXIT_FROM_BIT = 24
FLASH_KEY_BLOCKS = 4
SCAN_CHUNK = 64
MOD_TN = 768
INPROJ_TM, INPROJ_TN = 1024, 512
PREP_TM = 256
POST_TM = 512
REORDER_TD = 256
STEP_SEQS = 4
DECODE_SLOT_GROUPS = 4
DECODE_PAGES_PER_STEP = 8
SCAN_SUB = 4
HEADS_PER_GROUP = MXU_DIM // RWKV_HD
N_GROUPS = RWKV_HEADS // HEADS_PER_GROUP


def _cparams(sem, vmem=VMEM_LIMIT):
    return pltpu.CompilerParams(dimension_semantics=sem, vmem_limit_bytes=vmem)


def _split2(x):
    hi = x.astype(BF16)
    lo = (x - hi.astype(F32)).astype(BF16)
    return hi, lo


def _split3(x):
    hi = x.astype(BF16)
    r = x - hi.astype(F32)
    mid = r.astype(BF16)
    lo = (r - mid.astype(F32)).astype(BF16)
    return hi, mid, lo


_NN = (((1,), (0,)), ((), ()))
_NT = (((1,), (1,)), ((), ()))
_TN = (((0,), (0,)), ((), ()))


def _dg(a, b, dims):
    return lax.dot_general(a, b, dims, preferred_element_type=F32)


def _dot3(a, b, dims=_NN):
    ah, al = _split2(a)
    bh, bl = _split2(b)
    return _dg(ah, bh, dims) + _dg(al, bh, dims) + _dg(ah, bl, dims)


def _dot_exact_lhs(a_bf16, b, dims=_NN):
    b0, b1, b2 = _split3(b)
    return _dg(a_bf16, b0, dims) + _dg(a_bf16, b1, dims) + _dg(a_bf16, b2, dims)


def _dot_exact_rhs(a, b_bf16, dims=_NN):
    a0, a1, a2 = _split3(a)
    return _dg(a0, b_bf16, dims) + _dg(a1, b_bf16, dims) + _dg(a2, b_bf16, dims)


def _silu(x):
    return x * jax.nn.sigmoid(x)


def _head_sum(x, ones_bd):
    parts = []
    for j in range(x.shape[1] // LANES):
        parts.append(_dot_exact_rhs(x[:, j * LANES:(j + 1) * LANES], ones_bd))
    return jnp.concatenate(parts, axis=1)


def _ordered_to_f32(key):
    return pltpu.bitcast(key ^ ((key >> 31) & 0x7FFFFFFF), F32)


def _mod_kernel(c_ref, w_ref, b_ref, o_ref):
    o_ref[...] = _dot3(_silu(c_ref[...]), w_ref[...]) + b_ref[...]


def _modulation(c, w_ada, b_ada):
    m, d = c.shape
    n = w_ada.shape[1]
    tn = MOD_TN
    return pl.pallas_call(
        _mod_kernel,
        grid=(n // tn,),
        in_specs=[pl.BlockSpec((m, d), lambda j: (0, 0)),
                  pl.BlockSpec((d, tn), lambda j: (0, j)),
                  pl.BlockSpec((1, tn), lambda j: (0, j))],
        out_specs=pl.BlockSpec((m, tn), lambda j: (0, j)),
        out_shape=jax.ShapeDtypeStruct((m, n), F32),
        compiler_params=_cparams(("arbitrary",)),
        name="modulation",
    )(c, w_ada, b_ada.reshape(1, n))


def _inproj_kernel(x_ref, shift_ref, scale_ref, g_ref, w_ref, o_ref, h_ref, *, rows):
    @pl.when(pl.program_id(1) == 0)
    def _():
        def body(r, c):
            sl = pl.ds(pl.multiple_of(r * rows, rows), rows)
            x = x_ref[sl, :]
            ms = jnp.mean(x * x, axis=-1, keepdims=True)
            y = x * lax.rsqrt(ms + NORM_EPS) * g_ref[...]
            if shift_ref.shape[0] == 1:
                h = y * (1.0 + scale_ref[...]) + shift_ref[...]
            else:
                h = y * (1.0 + scale_ref[sl, :]) + shift_ref[sl, :]
            h_ref[sl, :] = h.astype(BF16)
            return c
        lax.fori_loop(0, x_ref.shape[0] // rows, body, 0)

    o_ref[...] = jnp.dot(h_ref[...], w_ref[...], preferred_element_type=F32)


def _in_projection(x, shift, scale, norm_g, w_bf16):
    t, d = x.shape
    n = w_bf16.shape[1]
    tm = min(INPROJ_TM, t)
    tn = INPROJ_TN
    rows = min(128, tm)
    per_row = shift.shape[0] != 1
    mod_spec = (pl.BlockSpec((tm, d), lambda i, j: (i, 0)) if per_row
                else pl.BlockSpec((1, d), lambda i, j: (0, 0)))
    return pl.pallas_call(
        functools.partial(_inproj_kernel, rows=rows),
        grid=(t // tm, n // tn),
        in_specs=[pl.BlockSpec((tm, d), lambda i, j: (i, 0)),
                  mod_spec, mod_spec,
                  pl.BlockSpec((1, d), lambda i, j: (0, 0)),
                  pl.BlockSpec((d, tn), lambda i, j: (0, j))],
        out_specs=pl.BlockSpec((tm, tn), lambda i, j: (i, j)),
        out_shape=jax.ShapeDtypeStruct((t, n), F32),
        scratch_shapes=[pltpu.VMEM((tm, d), BF16)],
        compiler_params=_cparams(("arbitrary", "arbitrary")),
        name="in_projection",
    )(x, shift, scale, norm_g.reshape(1, d), w_bf16)


def _rope128(x, cos, sin_signed):
    return x * cos + pltpu.roll(x, 64, axis=1) * sin_signed


def _rope64(x, cos, sin_signed, first_half):
    rot = jnp.where(first_half, pltpu.roll(x, 96, axis=1), pltpu.roll(x, 32, axis=1))
    return x * cos + rot * sin_signed


def _prep_kernel(q_ref, k_ref, v_ref, iq_ref, sm_ref, c128_ref, s128_ref, c64_ref, s64_ref,
                 qg_ref, kg_ref,
                 qo_ref, k32_ref, kb_ref, vo_ref, v32_ref, iqb_ref, ik32_ref, ikb_ref, w_ref, *, transposed):
    c128, s128 = c128_ref[...], s128_ref[...]
    c64, s64 = c64_ref[...], s64_ref[...]
    tm = q_ref.shape[0]
    lane = lax.broadcasted_iota(I32, (tm, LANES), 1)
    first_half = (lane % IDX_HD) < (IDX_HD // 2)
    q_scale = ATT_HD ** -0.5 * math.log2(math.e)
    k_heads = []
    for h in range(ATT_HEADS):
        hs = slice(h * ATT_HD, (h + 1) * ATT_HD)
        q = q_ref[:, hs]
        q = q * lax.rsqrt(jnp.mean(q * q, axis=-1, keepdims=True) + NORM_EPS) * qg_ref[...]
        q = _rope128(q, c128, s128) * q_scale
        k = k_ref[:, hs]
        k = k * lax.rsqrt(jnp.mean(k * k, axis=-1, keepdims=True) + NORM_EPS) * kg_ref[...]
        k = _rope128(k, c128, s128)
        k_heads.append(k)
        kb_ref[:, hs] = k.astype(BF16)
        if transposed:
            qo_ref[hs, :] = q.T.astype(BF16)
            vo_ref[hs, :] = v_ref[:, hs].T.astype(BF16)
        else:
            qo_ref[:, hs] = q.astype(BF16)
            vo_ref[:, hs] = v_ref[:, hs].astype(BF16)
    k32_ref[...] = pltpu.einshape("hmd->mhd", jnp.stack(k_heads, axis=0))
    v32_ref[...] = pltpu.einshape("hmd->mhd", jnp.stack(
        [v_ref[:, h * ATT_HD:(h + 1) * ATT_HD] for h in range(ATT_HEADS)], axis=0))
    for j in range(IDX_W // LANES):
        js = slice(j * LANES, (j + 1) * LANES)
        iqb_ref[:, js] = _rope64(iq_ref[:, js], c64, s64, first_half).astype(BF16)
    sm = sm_ref[...]
    ik = _rope64(sm, c64, s64, first_half)[:, :IDX_HD]
    ik32_ref[...] = ik
    ikb_ref[...] = ik.astype(BF16)
    w_ref[...] = sm * ((IDX_HEADS ** -0.5) * (IDX_HD ** -0.5))


def _attention_prep(p, tables, q_norm_g, k_norm_g, transposed):
    t = p.shape[0]
    tm = min(PREP_TM, t)
    c128, s128, c64, s64 = tables
    wide = lambda col: pl.BlockSpec((tm, ATT_W), lambda i, c=col // ATT_W: (i, c))
    tab = pl.BlockSpec((tm, LANES), lambda i: (i, 0))
    gain = pl.BlockSpec((1, ATT_HD), lambda i: (0, 0))
    row_w = pl.BlockSpec((tm, ATT_W), lambda i: (i, 0))
    col_w = pl.BlockSpec((ATT_W, tm), lambda i: (0, i))
    row64 = pl.BlockSpec((tm, IDX_HD), lambda i: (i, 0))
    head3 = pl.BlockSpec((tm, ATT_HEADS, ATT_HD), lambda i: (i, 0, 0))
    if transposed:
        qv_spec, qv_shape = col_w, jax.ShapeDtypeStruct((ATT_W, t), BF16)
    else:
        qv_spec, qv_shape = row_w, jax.ShapeDtypeStruct((t, ATT_W), BF16)
    return pl.pallas_call(
        functools.partial(_prep_kernel, transposed=transposed),
        grid=(t // tm,),
        in_specs=[wide(COL_Q), wide(COL_K), wide(COL_V), wide(COL_IQ),
                  pl.BlockSpec((tm, LANES), lambda i: (i, COL_SMALL_A // LANES)),
                  tab, tab, tab, tab, gain, gain],
        out_specs=[qv_spec, head3, row_w, qv_spec, head3, row_w, row64, row64, tab],
        out_shape=[qv_shape,
                   jax.ShapeDtypeStruct((t, ATT_HEADS, ATT_HD), F32),
                   jax.ShapeDtypeStruct((t, ATT_W), BF16),
                   qv_shape,
                   jax.ShapeDtypeStruct((t, ATT_HEADS, ATT_HD), F32),
                   jax.ShapeDtypeStruct((t, IDX_W), BF16),
                   jax.ShapeDtypeStruct((t, IDX_HD), F32),
                   jax.ShapeDtypeStruct((t, IDX_HD), BF16),
                   jax.ShapeDtypeStruct((t, LANES), F32)],
        compiler_params=_cparams(("arbitrary",)),
        name="attention_prep",
    )(p, p, p, p, p, c128, s128, c64, s64, q_norm_g.reshape(1, ATT_HD), k_norm_g.reshape(1, ATT_HD))


def _rope_tables(pos):
    pos = pos.astype(F32)[:, None]

    def tab(dh):
        inv = ROPE_THETA ** (-jnp.arange(0, dh, 2, dtype=F32) / dh)
        ang = pos * inv[None, :]
        cos, sin = jnp.cos(ang), jnp.sin(ang)
        c = jnp.concatenate([cos, cos], axis=-1)
        s = jnp.concatenate([-sin, sin], axis=-1)
        rep = LANES // dh
        return jnp.tile(c, (1, rep)), jnp.tile(s, (1, rep))

    c128, s128 = tab(ATT_HD)
    c64, s64 = tab(IDX_HD)
    return c128, s128, c64, s64


def _indexer_kernel(iq_ref, wt_ref, ik_ref, bias_ref, iqs_ref, cut_ref, *, n_sel):
    i = pl.program_id(0)
    tq = iq_ref.shape[0]
    tk = bias_ref.shape[2]
    nk_total = bias_ref.shape[1]
    nkb = i + 1

    for h in range(IDX_HEADS):
        iqs_ref[h * tq:(h + 1) * tq, :] = iq_ref[:, h * IDX_HD:(h + 1) * IDX_HD]

    key_pos = lax.broadcasted_iota(I32, (tk, tq), 0)
    qry_pos = lax.broadcasted_iota(I32, (tk, tq), 1) + i * tq

    def score_body(kb, c):
        ikt = ik_ref[pl.ds(pl.multiple_of(kb * tk, tk), tk), :]
        d = _dg(ikt, iqs_ref[...], _NT)
        acc = jnp.zeros((tk, tq), F32)
        for h in range(IDX_HEADS):
            acc = acc + jnp.maximum(d[:, h * tq:(h + 1) * tq], 0.0) * wt_ref[h:h + 1, :]
        bias_ref[0, kb] = jnp.where(key_pos + kb * tk <= qry_pos, acc, -jnp.inf)
        return c

    def score_pair(j, c):
        score_body(2 * j, c)
        return score_body(2 * j + 1, c)

    lax.fori_loop(0, nkb // 2, score_pair, 0)
    lax.fori_loop(2 * (nkb // 2), nkb, score_body, 0)

    def count_keys(pred):
        def one(kb, cnt):
            x = bias_ref[0, kb].reshape(tk // SUBLANES, SUBLANES, tq)
            pos = (key_pos + kb * tk).reshape(tk // SUBLANES, SUBLANES, tq)
            return cnt + jnp.sum(jnp.where(pred(x, pos), 1.0, 0.0), axis=0)

        def four(j, cnt):
            for u in range(4):
                cnt = one(4 * j + u, cnt)
            return cnt

        cnt = lax.fori_loop(0, nkb // 4, four, jnp.zeros((SUBLANES, tq), F32))
        cnt = lax.fori_loop(4 * (nkb // 4), nkb, one, cnt)
        return jnp.sum(cnt, axis=0, keepdims=True)

    def bit_body(bi, carry):
        ans, tot_ans = carry
        cand = ans + jnp.left_shift(jnp.int32(1), 31 - bi)
        cand_f = _ordered_to_f32(cand)
        tot = count_keys(lambda x, pos: x >= cand_f[None])
        ok = tot >= n_sel
        return jnp.where(ok, cand, ans), jnp.where(ok, tot, tot_ans)

    never_full = qry_pos[0:1, :] + 1 < n_sel

    def pending(tot_ans):
        return jnp.max(jnp.where((tot_ans == n_sel) | never_full, 0.0, 1.0))

    def late_bits(c):
        bi, ans, tot_ans, _ = c
        ans, tot_ans = bit_body(bi + 1, bit_body(bi, (ans, tot_ans)))
        return bi + 2, ans, tot_ans, pending(tot_ans)

    ans, tot_ans = lax.fori_loop(0, EARLY_EXIT_FROM_BIT, bit_body,
                                 (jnp.full((SUBLANES, tq), INT_MIN, I32), jnp.zeros((1, tq), F32)))
    _, ans, tot_ans, _ = lax.while_loop(lambda c: (c[0] < 32) & (c[3] > 0.0), late_bits,
                                        (jnp.int32(EARLY_EXIT_FROM_BIT), ans, tot_ans, pending(tot_ans)))
    thr = _ordered_to_f32(ans)[0:1, :]
    thr = jnp.where(thr != thr, -jnp.inf, thr)

    cut_ref[...] = jnp.full(cut_ref.shape, 2 ** 31 - 1, I32)

    @pl.when(jnp.max(tot_ans) > n_sel)
    def _():
        need = n_sel - count_keys(lambda x, pos: x > thr[None])
        n_bits = max(1, (nk_total * tk - 1).bit_length())

        def cut_body(bi, cut):
            cand = cut + jnp.left_shift(jnp.int32(1), n_bits - 1 - bi)
            below = count_keys(lambda x, pos: (x == thr[None]) & (pos < cand[None]))
            return jnp.where(below < need, cand, cut)

        cut = lax.fori_loop(0, n_bits, cut_body, jnp.zeros((1, tq), I32))
        cut_ref[...] = jnp.broadcast_to(jnp.where(tot_ans > n_sel, cut, 2 ** 31 - 1), cut_ref.shape)

    has_surplus_ties = jnp.max(tot_ans) > n_sel
    cut = cut_ref[0:1, :]

    def bias_body(kb, c):
        keep = (bias_ref[0, kb] >= thr) & (key_pos + kb * tk <= qry_pos)
        bias_ref[0, kb] = jnp.where(keep, 0.0, NEG_BIG)
        return c

    def bias_body_ties(kb, c):
        x = bias_ref[0, kb]
        pos = key_pos + kb * tk
        keep = ((x > thr) | ((x == thr) & (pos <= cut))) & (pos <= qry_pos)
        bias_ref[0, kb] = jnp.where(keep, 0.0, NEG_BIG)
        return c

    @pl.when(jnp.logical_not(has_surplus_ties))
    def _():
        lax.fori_loop(0, nkb, bias_body, 0)

    @pl.when(has_surplus_ties)
    def _():
        lax.fori_loop(0, nkb, bias_body_ties, 0)

    def fill_body(kb, c):
        bias_ref[0, kb] = jnp.full((tk, tq), NEG_BIG, F32)
        return c

    lax.fori_loop(nkb, nk_total, fill_body, 0)


def _prompt_indexer(iqb, wt, ikb, n_sel, tq):
    t = iqb.shape[0]
    nq = t // tq
    return pl.pallas_call(
        functools.partial(_indexer_kernel, n_sel=n_sel),
        grid=(nq,),
        in_specs=[pl.BlockSpec((tq, IDX_W), lambda i: (i, 0)),
                  pl.BlockSpec((IDX_HEADS, tq), lambda i: (0, i)),
                  pl.BlockSpec((t, IDX_HD), lambda i: (0, 0))],
        out_specs=pl.BlockSpec((1, nq, tq, tq), lambda i: (i, 0, 0, 0)),
        out_shape=jax.ShapeDtypeStruct((nq, nq, tq, tq), F32),
        scratch_shapes=[pltpu.VMEM((IDX_HEADS * tq, IDX_HD), BF16),
                        pltpu.VMEM((SUBLANES, tq), I32)],
        compiler_params=_cparams(("arbitrary",)),
        name="prompt_indexer",
    )(iqb, wt, ikb)


def _flash_kernel(qi_ref, ki_ref, qt_ref, k_ref, vt_ref, bias_ref, g_ref, o_ref, m_ref, l_ref, acc_ref):
    step = pl.program_id(0)
    qi, ki = qi_ref[step], ki_ref[step]

    @pl.when(ki == 0)
    def _():
        m_ref[...] = jnp.full(m_ref.shape, NEG_BIG, F32)
        l_ref[...] = jnp.zeros(l_ref.shape, F32)
        acc_ref[...] = jnp.zeros(acc_ref.shape, F32)

    heads = [slice(h * ATT_HD, (h + 1) * ATT_HD) for h in range(ATT_HEADS)]
    tq = qt_ref.shape[1]
    tkb = bias_ref.shape[2]
    for kh in range(bias_ref.shape[1]):
        ks = slice(kh * tkb, (kh + 1) * tkb)
        for qh in range(tq // LANES):
            qs = slice(qh * LANES, (qh + 1) * LANES)
            bias = bias_ref[0, kh, :, qs]
            m_prev, l_prev = m_ref[:, qs], l_ref[:, qs]
            m_rows, l_rows = [], []
            for h, hs in enumerate(heads):
                s = _dg(k_ref[ks, hs], qt_ref[hs, qs], _NN) + bias
                m_old = m_prev[h:h + 1, :]
                m_new = jnp.maximum(m_old, jnp.max(s, axis=0, keepdims=True))
                alpha = jnp.exp2(m_old - m_new)
                p = jnp.exp2(s - m_new)
                l_rows.append(alpha * l_prev[h:h + 1, :] + jnp.sum(p, axis=0, keepdims=True))
                m_rows.append(m_new)
                acc_ref[h, :, qs] = alpha * acc_ref[h, :, qs] + _dg(vt_ref[hs, ks], p.astype(BF16), _NN)
            m_ref[:, qs] = jnp.concatenate(m_rows, axis=0)
            l_ref[:, qs] = jnp.concatenate(l_rows, axis=0)

    @pl.when(ki == qi // bias_ref.shape[1])
    def _():
        for h in range(ATT_HEADS):
            hs = slice(h * ATT_HD, (h + 1) * ATT_HD)
            out = (acc_ref[h] / l_ref[h:h + 1, :]).T
            o_ref[:, hs] = (out * _silu(g_ref[:, hs])).astype(BF16)


def _prompt_attention(qt, kb, vt, bias, p, tq):
    t = kb.shape[0]
    nq = t // tq
    kpb = min(FLASH_KEY_BLOCKS, nq)
    tk = kpb * tq
    pairs = [(i, j) for i in range(nq) for j in range(i // kpb + 1)]
    qi = jnp.asarray([a for a, _ in pairs], I32)
    ki = jnp.asarray([b for _, b in pairs], I32)
    grid_spec = pltpu.PrefetchScalarGridSpec(
        num_scalar_prefetch=2,
        grid=(len(pairs),),
        in_specs=[pl.BlockSpec((ATT_W, tq), lambda s, qi, ki: (0, qi[s])),
                  pl.BlockSpec((tk, ATT_W), lambda s, qi, ki: (ki[s], 0)),
                  pl.BlockSpec((ATT_W, tk), lambda s, qi, ki: (0, ki[s])),
                  pl.BlockSpec((1, kpb, tq, tq), lambda s, qi, ki: (qi[s], ki[s], 0, 0)),
                  pl.BlockSpec((tq, ATT_W), lambda s, qi, ki: (qi[s], COL_G // ATT_W))],
        out_specs=pl.BlockSpec((tq, ATT_W), lambda s, qi, ki: (qi[s], 0)),
        scratch_shapes=[pltpu.VMEM((ATT_HEADS, tq), F32),
                        pltpu.VMEM((ATT_HEADS, tq), F32),
                        pltpu.VMEM((ATT_HEADS, ATT_HD, tq), F32)],
    )
    return pl.pallas_call(
        _flash_kernel,
        grid_spec=grid_spec,
        out_shape=jax.ShapeDtypeStruct((t, ATT_W), BF16),
        compiler_params=_cparams(("arbitrary",)),
        name="prompt_attention",
    )(qi, ki, qt, kb, vt, bias, p)


def _sample_attn_kernel(pt_ref, q_ref, iq_ref, w_ref, ikn_ref, kn_ref, vn_ref, g_ref, ex_ref,
                        cik_ref, ck_ref, cv_ref, o_ref,
                        ikbuf, kbuf, vbuf, sce_ref, isem, ksem, vsem, *, n_sel, pps):
    b = pl.program_id(0)
    n_pages = ikbuf.shape[0]
    page_rows = PAGE_SIZE * ATT_HEADS

    n_steps = n_pages // pps
    nb = pt_ref.shape[0]
    n_groups = kbuf.shape[0] // pps
    total_steps = nb * n_steps

    def idx_copy(seq, j):
        return pltpu.make_async_copy(cik_ref.at[pt_ref[seq, j]], ikbuf.at[j], isem.at[j])

    def k_copy(seq, j, slot):
        return pltpu.make_async_copy(ck_ref.at[pt_ref[seq, j]], kbuf.at[slot], ksem.at[slot])

    def v_copy(seq, j, slot):
        return pltpu.make_async_copy(cv_ref.at[pt_ref[seq, j]], vbuf.at[slot], vsem.at[slot])

    def start_idx(seq):
        def body(j, c):
            idx_copy(seq, j).start()
            return c
        lax.fori_loop(0, n_pages, body, 0)

    def start_group(gs):
        seq, t, grp = gs // n_steps, gs % n_steps, gs % n_groups
        for i in range(pps):
            k_copy(seq, t * pps + i, grp * pps + i).start()
            v_copy(seq, t * pps + i, grp * pps + i).start()

    @pl.when(b == 0)
    def _():
        start_idx(0)
        for gs in range(min(n_groups - 1, total_steps)):
            start_group(gs)

    iq = iq_ref[0]
    w = w_ref[0]

    def wait_idx(j, c):
        idx_copy(b, j).wait()
        return c

    lax.fori_loop(0, n_pages, wait_idx, 0)
    iq_pages = jnp.broadcast_to(iq[None], (n_pages, IDX_HEADS, IDX_HD))
    d = lax.dot_general(iq_pages, ikbuf[...].astype(BF16), (((2,), (1,)), ((0,), (0,))),
                        preferred_element_type=F32)
    sc = jnp.sum(jnp.maximum(d, 0.0) * w[None], axis=1)
    d_self = jnp.sum(iq.astype(F32) * ikn_ref[0].astype(F32), axis=-1, keepdims=True)
    s_self = jnp.sum(jnp.maximum(d_self, 0.0) * w, axis=0, keepdims=True)

    pos = (lax.broadcasted_iota(I32, sc.shape, 0) * PAGE_SIZE + lax.broadcasted_iota(I32, sc.shape, 1))
    self_pos = n_pages * PAGE_SIZE

    def count_keys(pred):
        cnt = jnp.sum(jnp.where(pred(sc, pos), 1.0, 0.0), axis=-1, keepdims=True)
        return jnp.sum(cnt, axis=0, keepdims=True) + jnp.where(pred(s_self, self_pos), 1.0, 0.0)

    def digit_body(it, carry):
        ans, tot_ans = carry
        step = jnp.left_shift(jnp.int32(1), 30 - 2 * it)
        best, best_tot = ans, tot_ans
        cand = ans
        for _ in range(3):
            cand = cand + step
            cand_f = _ordered_to_f32(cand)
            tot = count_keys(lambda x, p: x >= cand_f)
            best = jnp.where(tot >= n_sel, cand, best)
            best_tot = jnp.where(tot >= n_sel, tot, best_tot)
        return best, best_tot

    ans, tot_ans = lax.fori_loop(0, 16, digit_body, (jnp.full((1, 1), INT_MIN, I32), jnp.zeros((1, 1), F32)))
    thr = _ordered_to_f32(ans)
    thr = jnp.where(thr != thr, -jnp.inf, thr)

    def tie_cut():
        need = n_sel - count_keys(lambda x, p: x > thr)
        n_bits = self_pos.bit_length()

        def cut_body(bi, cut):
            cand = cut + jnp.left_shift(jnp.int32(1), n_bits - 1 - bi)
            below = count_keys(lambda x, p: (x == thr) & (p < cand))
            return jnp.where(below < need, cand, cut)

        return lax.fori_loop(0, n_bits, cut_body, jnp.zeros((1, 1), I32))

    cut = lax.cond(tot_ans[0, 0] > n_sel, tie_cut, lambda: jnp.full((1, 1), 2 ** 31 - 1, I32))

    def selected(x, p):
        return (x > thr) | ((x == thr) & (p <= cut))

    sce_ref[...] = _dot_exact_rhs(sc, ex_ref[...])

    @pl.when(b + 1 < nb)
    def _():
        start_idx(b + 1)

    q8 = q_ref[0]
    lane_e = lax.broadcasted_iota(I32, (ATT_HEADS, page_rows), 1)
    own = lane_e % ATT_HEADS == lax.broadcasted_iota(I32, (ATT_HEADS, page_rows), 0)
    tok_lane = lane_e[0:1, :] // ATT_HEADS

    kn = kn_ref[0].astype(BF16).astype(F32)
    s0 = jnp.sum(q8.astype(F32) * kn, axis=-1, keepdims=True)
    m0 = jnp.where(selected(s_self, self_pos), s0, NEG_BIG)
    l0 = jnp.where(m0 > 0.5 * NEG_BIG, 1.0, 0.0)
    acc0 = l0 * vn_ref[0].astype(BF16).astype(F32)

    def att_body(t, carry):
        m_prev, l_prev, acc = carry
        gs = b * n_steps + t
        grp = gs % n_groups

        @pl.when(gs + n_groups - 1 < total_steps)
        def _():
            start_group(gs + n_groups - 1)

        for i in range(pps):
            k_copy(b, t * pps + i, grp * pps + i).wait()
            v_copy(b, t * pps + i, grp * pps + i).wait()
        s = []
        for i in range(pps):
            kf = kbuf[grp * pps + i].reshape(page_rows, ATT_HD).astype(BF16)
            keep = own & selected(sce_ref[pl.ds(t * pps + i, 1), :], (t * pps + i) * PAGE_SIZE + tok_lane)
            s.append(jnp.where(keep, _dg(q8, kf, _NT), NEG_BIG))
        m_new = m_prev
        for x in s:
            m_new = jnp.maximum(m_new, jnp.max(x, axis=-1, keepdims=True))
        alpha = jnp.exp2(m_prev - m_new)
        p = [jnp.exp2(x - m_new) for x in s]
        l_new = alpha * l_prev
        for x in p:
            l_new = l_new + jnp.sum(x, axis=-1, keepdims=True)
        pv = [jnp.dot(p[i].astype(BF16), vbuf[grp * pps + i].reshape(page_rows, ATT_HD).astype(BF16),
                      preferred_element_type=F32) for i in range(pps)]
        acc = alpha * acc
        for x in pv:
            acc = acc + x
        return m_new, l_new, acc

    m, l, acc = lax.fori_loop(0, n_steps, att_body, (m0, l0, acc0))
    o_ref[0] = acc / l * _silu(g_ref[0])


def _sample_attention(page_table, qb, iqb, wv, ikb, k32, v32, g, cache_idx_k, cache_k, cache_v, n_sel):
    nb, n_pages = page_table.shape
    head3 = pl.BlockSpec((1, ATT_HEADS, ATT_HD), lambda b, pt: (b, 0, 0))
    any_spec = pl.BlockSpec(memory_space=pl.ANY)
    page_rows = PAGE_SIZE * ATT_HEADS
    pps = math.gcd(DECODE_PAGES_PER_STEP, n_pages)
    expand = (jnp.arange(page_rows)[None, :] // ATT_HEADS == jnp.arange(PAGE_SIZE)[:, None]).astype(BF16)
    grid_spec = pltpu.PrefetchScalarGridSpec(
        num_scalar_prefetch=1,
        grid=(nb,),
        in_specs=[head3,
                  pl.BlockSpec((1, IDX_HEADS, IDX_HD), lambda b, pt: (b, 0, 0)),
                  pl.BlockSpec((1, IDX_HEADS, 1), lambda b, pt: (b, 0, 0)),
                  pl.BlockSpec((1, 1, IDX_HD), lambda b, pt: (b, 0, 0)),
                  head3, head3, head3,
                  pl.BlockSpec((PAGE_SIZE, page_rows), lambda b, pt: (0, 0)),
                  any_spec, any_spec, any_spec],
        out_specs=head3,
        scratch_shapes=[pltpu.VMEM((n_pages, IDX_HD, PAGE_SIZE), F32),
                        pltpu.VMEM((DECODE_SLOT_GROUPS * pps, PAGE_SIZE, ATT_HEADS, ATT_HD), F32),
                        pltpu.VMEM((DECODE_SLOT_GROUPS * pps, PAGE_SIZE, ATT_HEADS, ATT_HD), F32),
                        pltpu.VMEM((n_pages, page_rows), F32),
                        pltpu.SemaphoreType.DMA((n_pages,)),
                        pltpu.SemaphoreType.DMA((DECODE_SLOT_GROUPS * pps,)),
                        pltpu.SemaphoreType.DMA((DECODE_SLOT_GROUPS * pps,))],
    )
    head = lambda a: a.reshape(nb, ATT_HEADS, ATT_HD)
    out = pl.pallas_call(
        functools.partial(_sample_attn_kernel, n_sel=n_sel, pps=pps),
        grid_spec=grid_spec,
        out_shape=jax.ShapeDtypeStruct((nb, ATT_HEADS, ATT_HD), F32),
        compiler_params=_cparams(("arbitrary",)),
        name="sample_attention",
    )(page_table, head(qb),
      iqb.reshape(nb, IDX_HEADS, IDX_HD),
      wv[:, IDX_HD:IDX_HD + IDX_HEADS].reshape(nb, IDX_HEADS, 1),
      ikb.reshape(nb, 1, IDX_HD),
      head(k32), head(v32), head(g), expand,
      jnp.swapaxes(cache_idx_k, 1, 2), cache_k, cache_v)
    return out.reshape(nb, ATT_W).astype(BF16)


def _rwkv_prep_kernel(*refs, shifted):
    n_x = 5
    x_refs = refs[:n_x]
    if shifted:
        halo_refs, first_refs = refs[n_x:2 * n_x], refs[2 * n_x:3 * n_x]
        rest = refs[3 * n_x:]
    else:
        prev_refs = refs[n_x:2 * n_x]
        rest = refs[2 * n_x:]
    (mu_ref, mus_ref, wl_ref, w0_ref, a0_ref, kk_ref, ka_ref, rk_ref, ones_ref,
     ro_ref, kpo_ref, vo_ref, kko_ref, bo_ref, lwo_ref, bonus_ref, sg_ref) = rest

    def lerp(j, mu):
        x = x_refs[j][...]
        if shifted:
            above = jnp.where(pl.program_id(0) == 0, first_refs[j][...],
                              halo_refs[j][SUBLANES - 1:SUBLANES, :])
            row = lax.broadcasted_iota(I32, x.shape, 0)
            prev = jnp.where(row == 0, above, pltpu.roll(x, 1, axis=0))
        else:
            prev = prev_refs[j][...]
        return x + mu * (prev - x)

    w = RWKV_W
    xr = lerp(0, mu_ref[:, 0:w])
    xk = lerp(1, mu_ref[:, w:2 * w])
    xv = lerp(2, mu_ref[:, 2 * w:3 * w])
    xg = lerp(3, mu_ref[:, 3 * w:4 * w])
    xs = lerp(4, mus_ref[...])
    lane = lax.broadcasted_iota(I32, xs.shape, 1)
    lora_in = jnp.where(lane < DECAY_LORA, jnp.tanh(xs), xs)
    lora = _dot3(lora_in, wl_ref[...])
    z = w0_ref[...] + lora[:, :w]
    softplus_neg = jnp.maximum(-z, 0.0) + jnp.log(1.0 + jnp.exp(-jnp.abs(z)))
    w_log = -softplus_neg - 0.5
    lwo_ref[...] = -jnp.exp(w_log)
    a = jax.nn.sigmoid(a0_ref[...] + lora[:, w:])
    ones_bd = ones_ref[...]
    kk = xk * kk_ref[...]
    kk = kk / jnp.maximum(jnp.sqrt(_head_sum(kk * kk, ones_bd)), 1e-12)
    kp = xk * (1.0 + (a - 1.0) * ka_ref[...])
    ro_ref[...] = xr
    kpo_ref[...] = kp
    vo_ref[...] = xv
    kko_ref[...] = kk
    bo_ref[...] = kk * a
    bonus_ref[...] = _head_sum(xr * kp * rk_ref[...], ones_bd) * xv
    sg_ref[...] = _silu(xg)


def _rwkv_prep(p, prev, consts, shifted):
    t = p.shape[0]
    tm = min(PREP_TM, t)
    mu_big, mu_small, w_lora, w0, a0, k_k, k_a, r_k, ones_bd = consts
    cols = [(RWKV_W, COL_RR // RWKV_W), (RWKV_W, COL_RK // RWKV_W), (RWKV_W, COL_RV // RWKV_W),
            (RWKV_W, COL_RG // RWKV_W), (LANES, COL_SMALL_R // LANES)]
    x_specs = [pl.BlockSpec((tm, wd), lambda i, c=c: (i, c)) for wd, c in cols]
    if shifted:
        per8 = tm // SUBLANES
        halo = [pl.BlockSpec((SUBLANES, wd), lambda i, c=c: (jnp.maximum(i * per8 - 1, 0), c))
                for wd, c in cols]
        first = [pl.BlockSpec((1, wd), lambda i, c=c: (0, c)) for wd, c in cols]
        extra_specs, extra_args = halo + first, [p] * 5 + [prev] * 5
    else:
        extra_specs, extra_args = x_specs, [prev] * 5
    full = lambda a: pl.BlockSpec(a.shape, lambda i: (0, 0))
    row = pl.BlockSpec((tm, RWKV_W), lambda i: (i, 0))
    vec = lambda a: a.reshape(1, -1)
    args = [mu_big, mu_small, w_lora, vec(w0), vec(a0), vec(k_k), vec(k_a), vec(r_k), ones_bd]
    return pl.pallas_call(
        functools.partial(_rwkv_prep_kernel, shifted=shifted),
        grid=(t // tm,),
        in_specs=x_specs + extra_specs + [full(a) for a in args],
        out_specs=[row] * 8,
        out_shape=[jax.ShapeDtypeStruct((t, RWKV_W), F32)] * 8,
        compiler_params=_cparams(("arbitrary",)),
        name="rwkv_prep",
    )(*([p] * 5), *extra_args, *args)


def _scan_kernel(r_ref, kp_ref, v_ref, kk_ref, b_ref, lw_ref, s0_ref, y_ref, st_ref, s_ref):
    @pl.when(pl.program_id(0) == 0)
    def _():
        s_ref[...] = s0_ref[...]

    ch = SCAN_CHUNK
    assert ch == RWKV_HD
    rows_total = r_ref.shape[0]
    n_sub = rows_total // ch
    gw = MXU_DIM
    nh = HEADS_PER_GROUP
    lw = lw_ref[...]
    tr = lax.broadcasted_iota(I32, (rows_total, rows_total), 0)
    tc = lax.broadcasted_iota(I32, (rows_total, rows_total), 1)
    tri = jnp.where((tr >= tc) & (tr // ch == tc // ch), 1.0, 0.0).astype(BF16)
    cum = _dot_exact_lhs(tri, lw)
    cum_end = jnp.concatenate(
        [jnp.broadcast_to(cum[(c + 1) * ch - 1:(c + 1) * ch, :], (ch, RWKV_W)) for c in range(n_sub)], axis=0)
    kk, b, kp, r, v = kk_ref[...], b_ref[...], kp_ref[...], r_ref[...], v_ref[...]
    g_inv = jnp.exp(-cum)
    g_end = jnp.exp(cum_end - cum)
    a_t = (-kk * jnp.exp(cum - lw)).astype(BF16)
    b_t = b * g_inv
    k_t = kp * g_inv
    r_t = (r * jnp.exp(cum)).astype(BF16)
    b_h = (b * g_end).astype(BF16)
    k_h = (kp * g_end).astype(BF16)
    vb = v.astype(BF16)

    srow = lax.broadcasted_iota(I32, (nh * ch, gw), 0)
    scol = lax.broadcasted_iota(I32, (nh * ch, gw), 1)
    own = (srow // ch) == (scol // RWKV_HD)
    step = lax.broadcasted_iota(I32, (ch, gw), 0)
    other = lax.broadcasted_iota(I32, (ch, gw), 1) % ch
    strict, incl = step > other, step >= other
    eye = jnp.where(step == other, 1.0, 0.0)

    def stack(x):
        return jnp.where(own, jnp.concatenate([x] * nh, axis=0), 0.0).astype(BF16)

    groups = range(N_GROUPS)
    units = [(c, g) for c in range(n_sub) for g in groups]
    blk = {(c, g): (slice(c * ch, (c + 1) * ch), slice(g * gw, (g + 1) * gw)) for c, g in units}
    rb = {u: stack(b_t[blk[u]]) for u in units}
    rk = {u: stack(k_t[blk[u]]) for u in units}
    v4 = {u: stack(v[blk[u]]) for u in units}
    ar = {u: jnp.concatenate([a_t[blk[u]], r_t[blk[u]]], axis=0) for u in units}
    g_b = {u: _dg(ar[u], rb[u], _NT) for u in units}
    g_k = {u: _dg(ar[u], rk[u], _NT) for u in units}
    n_row = {u: jnp.where(strict, g_b[u][:ch], 0.0) for u in units}
    bm_row = {u: jnp.where(strict, g_k[u][:ch], 0.0).astype(BF16) for u in units}
    rb_row = {u: jnp.where(incl, g_b[u][ch:], 0.0).astype(BF16) for u in units}
    rk_row = {u: jnp.where(incl, g_k[u][ch:], 0.0).astype(BF16) for u in units}
    minv = {u: eye + n_row[u] for u in units}
    pw = {u: _dg(n_row[u].astype(BF16), stack(n_row[u]), _NN) for u in units}
    levels = int(math.log2(ch)) - 1
    for level in range(levels):
        pw_bd = {u: stack(pw[u]) for u in units}
        if level + 1 < levels:
            both = {u: _dg(jnp.concatenate([minv[u], pw[u]], axis=0).astype(BF16), pw_bd[u], _NN) for u in units}
            minv = {u: minv[u] + both[u][:ch] for u in units}
            pw = {u: both[u][ch:] for u in units}
        else:
            minv = {u: minv[u] + _dg(minv[u].astype(BF16), pw_bd[u], _NN) for u in units}
    minv = {u: minv[u].astype(BF16) for u in units}
    xy_fixed = {u: _dg(jnp.concatenate([bm_row[u], rk_row[u]], axis=0), v4[u], _NN) for u in units}
    x_fixed = {u: xy_fixed[u][:ch] for u in units}
    y_fixed = {u: xy_fixed[u][ch:] for u in units}

    s = [s_ref[g] for g in groups]
    for c in range(n_sub):
        sb = [x.astype(BF16) for x in s]
        ar_s = [_dg(ar[c, g], sb[g], _NT) for g in groups]
        x_row = [ar_s[g][:ch] + x_fixed[c, g] for g in groups]
        x_bd = [stack(x) for x in x_row]
        u_row = [_dg(minv[c, g], x_bd[g], _NN) for g in groups]
        u_bd = [stack(x) for x in u_row]
        for g in groups:
            y_ref[blk[c, g]] = ar_s[g][ch:] + _dg(rb_row[c, g], u_bd[g], _NN) + y_fixed[c, g]
        for g in groups:
            rs, ls = blk[c, g]
            upd = _dg(jnp.concatenate([u_row[g].astype(BF16), vb[rs, ls]], axis=0),
                      jnp.concatenate([b_h[rs, ls], k_h[rs, ls]], axis=0), _TN)
            g_c = jnp.exp(cum[(c + 1) * ch - 1:(c + 1) * ch, ls])
            s[g] = s[g] * g_c + jnp.where(own, upd, 0.0)
    for g in groups:
        s_ref[g] = s[g]

    @pl.when(pl.program_id(0) == pl.num_programs(0) - 1)
    def _():
        st_ref[...] = s_ref[...]


def _rwkv_scan(r, kp, v, kk, b, lw, s0):
    t = r.shape[0]
    ch = SCAN_CHUNK * min(SCAN_SUB, t // SCAN_CHUNK)
    row = pl.BlockSpec((ch, RWKV_W), lambda i: (i, 0))
    st = pl.BlockSpec((N_GROUPS, MXU_DIM, MXU_DIM), lambda i: (0, 0, 0))
    return pl.pallas_call(
        _scan_kernel,
        grid=(t // ch,),
        in_specs=[row] * 6 + [st],
        out_specs=[row, st],
        out_shape=[jax.ShapeDtypeStruct((t, RWKV_W), F32),
                   jax.ShapeDtypeStruct((N_GROUPS, MXU_DIM, MXU_DIM), F32)],
        scratch_shapes=[pltpu.VMEM((N_GROUPS, MXU_DIM, MXU_DIM), F32)],
        compiler_params=_cparams(("arbitrary",)),
        name="rwkv_scan",
    )(r, kp, v, kk, b, lw, s0)


def _rwkv_step_kernel(s_ref, r_ref, kp_ref, kk_ref, b_ref, lw_ref, v_ref, so_ref, y_ref):
    s = s_ref[...]
    sa = -jnp.sum(s * kk_ref[...], axis=-1, keepdims=True)
    s_new = s * jnp.exp(lw_ref[...]) + sa * b_ref[...] + v_ref[...] * kp_ref[...]
    so_ref[...] = s_new
    y_ref[...] = jnp.sum(s_new * r_ref[...], axis=-1, keepdims=True)


def _rwkv_step(state, r, kp, kk, b, lw, v):
    nb = state.shape[0]
    ts = math.gcd(STEP_SEQS, nb)
    rowv = lambda a: a.reshape(nb, RWKV_HEADS, 1, RWKV_HD)
    rspec = pl.BlockSpec((ts, RWKV_HEADS, 1, RWKV_HD), lambda i: (i, 0, 0, 0))
    cspec = pl.BlockSpec((ts, RWKV_HEADS, RWKV_HD, 1), lambda i: (i, 0, 0, 0))
    sspec = pl.BlockSpec((ts, RWKV_HEADS, RWKV_HD, RWKV_HD), lambda i: (i, 0, 0, 0))
    s_new, y = pl.pallas_call(
        _rwkv_step_kernel,
        grid=(nb // ts,),
        in_specs=[sspec, rspec, rspec, rspec, rspec, rspec, cspec],
        out_specs=[sspec, cspec],
        out_shape=[jax.ShapeDtypeStruct(state.shape, F32),
                   jax.ShapeDtypeStruct((nb, RWKV_HEADS, RWKV_HD, 1), F32)],
        compiler_params=_cparams(("arbitrary",)),
        name="rwkv_step",
    )(state, rowv(r), rowv(kp), rowv(kk), rowv(b), rowv(lw), v.reshape(nb, RWKV_HEADS, RWKV_HD, 1))
    return s_new, y.reshape(nb, RWKV_W)


def _rwkv_post_kernel(y_ref, bonus_ref, sg_ref, lg_ref, lb_ref, ones_ref, o_ref):
    y = y_ref[...]
    ones_bd = ones_ref[...]
    mean = _head_sum(y, ones_bd) * (1.0 / RWKV_HD)
    yc = y - mean
    var = _head_sum(yc * yc, ones_bd) * (1.0 / RWKV_HD)
    yn = yc * lax.rsqrt(var + GN_EPS) * lg_ref[...] + lb_ref[...]
    o_ref[...] = ((yn + bonus_ref[...]) * sg_ref[...]).astype(BF16)


def _rwkv_post(y, bonus, sg, ln_g, ln_b, ones_bd):
    t = y.shape[0]
    tm = min(POST_TM, t)
    row = pl.BlockSpec((tm, RWKV_W), lambda i: (i, 0))
    vec = pl.BlockSpec((1, RWKV_W), lambda i: (0, 0))
    return pl.pallas_call(
        _rwkv_post_kernel,
        grid=(t // tm,),
        in_specs=[row, row, row, vec, vec, pl.BlockSpec(ones_bd.shape, lambda i: (0, 0))],
        out_specs=row,
        out_shape=jax.ShapeDtypeStruct((t, RWKV_W), BF16),
        compiler_params=_cparams(("arbitrary",)),
        name="rwkv_post",
    )(y, bonus, sg, ln_g.reshape(1, -1), ln_b.reshape(1, -1), ones_bd)


def _outproj_kernel(att_ref, rw_ref, wa_ref, wr_ref, x_ref, gate_ref, o_ref):
    out = (jnp.dot(att_ref[...], wa_ref[...], preferred_element_type=F32)
           + jnp.dot(rw_ref[...], wr_ref[...], preferred_element_type=F32))
    o_ref[...] = x_ref[...] + gate_ref[...] * out


def _out_projection(att, rw, w_out_bf16, x, gate):
    t, d = x.shape
    tm = min(POST_TM, t)
    per_row = gate.shape[0] != 1
    gspec = (pl.BlockSpec((tm, d), lambda i: (i, 0)) if per_row else pl.BlockSpec((1, d), lambda i: (0, 0)))
    return pl.pallas_call(
        _outproj_kernel,
        grid=(t // tm,),
        in_specs=[pl.BlockSpec((tm, ATT_W), lambda i: (i, 0)),
                  pl.BlockSpec((tm, RWKV_W), lambda i: (i, 0)),
                  pl.BlockSpec((ATT_W, d), lambda i: (0, 0)),
                  pl.BlockSpec((RWKV_W, d), lambda i: (1, 0)),
                  pl.BlockSpec((tm, d), lambda i: (i, 0)),
                  gspec],
        out_specs=pl.BlockSpec((tm, d), lambda i: (i, 0)),
        out_shape=jax.ShapeDtypeStruct((t, d), F32),
        compiler_params=_cparams(("arbitrary",)),
        name="out_projection",
    )(att, rw, w_out_bf16, w_out_bf16, x, gate)


def _reorder_cols(w, axis=-1):
    axis = axis % w.ndim

    def cut(start, size):
        return lax.slice_in_dim(w, start, start + size, axis=axis)

    def zeros(size):
        shape = list(w.shape)
        shape[axis] = size
        return jnp.zeros(shape, w.dtype)

    q, k, v, g = (cut(i * ATT_W, ATT_W) for i in range(4))
    a = 4 * ATT_W
    iq, iw, ik = cut(a, IDX_W), cut(a + IDX_W, IDX_HEADS), cut(a + IDX_W + IDX_HEADS, IDX_HD)
    o = N_ATT_COLS_ORIG
    rr, rk, rv = (cut(o + i * RWKV_W, RWKV_W) for i in range(3))
    o2 = o + 3 * RWKV_W
    wl, al = cut(o2, DECAY_LORA), cut(o2 + DECAY_LORA, ICL_LORA)
    rg = cut(o2 + DECAY_LORA + ICL_LORA, RWKV_W)
    return jnp.concatenate([q, k, v, g, iq, rr, rk, rv, rg, ik, iw, zeros(LANES - IDX_HD - IDX_HEADS),
                            wl, al, zeros(N_COLS_PAD - N_COLS)], axis=axis)


def _reorder_cast_kernel(wt_ref, o_ref):
    o_ref[...] = _reorder_cols(wt_ref[...], axis=0).T.astype(BF16)


def _reorder_cast(wt):
    n, d = wt.shape
    td = min(REORDER_TD, d)
    return pl.pallas_call(
        _reorder_cast_kernel,
        grid=(d // td,),
        in_specs=[pl.BlockSpec((n, td), lambda i: (0, i))],
        out_specs=pl.BlockSpec((td, N_COLS_PAD), lambda i: (i, 0)),
        out_shape=jax.ShapeDtypeStruct((d, N_COLS_PAD), BF16),
        compiler_params=_cparams(("arbitrary",)),
        name="reorder_cast",
    )(wt)


def _rwkv_cols_original_order(p):
    return jnp.concatenate([p[..., COL_RR:COL_RR + RWKV_W], p[..., COL_RK:COL_RK + RWKV_W],
                            p[..., COL_RV:COL_RV + RWKV_W],
                            p[..., COL_SMALL_R:COL_SMALL_R + DECAY_LORA + ICL_LORA],
                            p[..., COL_RG:COL_RG + RWKV_W]], axis=-1)


def _blockdiag_to_state(sb):
    sb = sb.reshape(N_GROUPS, HEADS_PER_GROUP, RWKV_HD, HEADS_PER_GROUP, RWKV_HD)
    return jnp.stack([sb[:, h, :, h, :] for h in range(HEADS_PER_GROUP)], axis=1).reshape(
        RWKV_HEADS, RWKV_HD, RWKV_HD)


def kernel(x_prompt, x_sample, c_prompt, c_sample, cache_k, cache_v, cache_idx_k, state_wkv, state_shift,
           page_table, w_ada, b_ada, norm_g, w_in, q_norm_g, k_norm_g, shift_mu, w0, w_b, a0, a_b,
           k_k, k_a, r_k, ln_x_g, ln_x_b, w_out):
    depth = w_ada.shape[0]
    assert depth == 1, "single-layer trunk"
    bsz, seq, d = x_prompt.shape
    assert bsz == 1
    nb, dec_seq, _ = x_sample.shape
    assert dec_seq == 1
    n_pages = page_table.shape[1]
    past_len = n_pages * PAGE_SIZE
    l = 0

    w_in_r = _reorder_cast(w_in[l].T)
    w_out_b = w_out[l].astype(BF16)
    mu = _reorder_cols(jnp.concatenate([jnp.zeros((N_ATT_COLS_ORIG,), F32), shift_mu[l]]))
    mu_big = mu[COL_RR:COL_RR + 4 * RWKV_W].reshape(1, -1)
    mu_small = mu[COL_SMALL_R:COL_SMALL_R + LANES].reshape(1, -1)
    zl = jnp.zeros((DECAY_LORA, RWKV_W), F32)
    w_lora = jnp.concatenate([jnp.concatenate([w_b[l], zl], axis=1),
                              jnp.concatenate([zl, a_b[l]], axis=1)], axis=0)
    hid = jnp.arange(LANES) // RWKV_HD
    ones_bd = (hid[:, None] == hid[None, :]).astype(BF16)
    rwkv_consts = (mu_big, mu_small, w_lora, w0[l], a0[l], k_k[l], k_a[l], r_k[l].reshape(-1), ones_bd)

    n_mod = 1 + nb
    c_all = jnp.concatenate([c_prompt, c_sample, jnp.zeros((-n_mod % SUBLANES, d), F32)], axis=0)
    mod = _modulation(c_all, w_ada[l], b_ada[l])
    shift_p, scale_p, gate_p = mod[0:1, :d], mod[0:1, d:2 * d], mod[0:1, 2 * d:]
    shift_s, scale_s, gate_s = mod[1:n_mod, :d], mod[1:n_mod, d:2 * d], mod[1:n_mod, 2 * d:]

    xp = x_prompt.reshape(seq, d)
    xs = x_sample.reshape(nb, d)

    p = _in_projection(xp, shift_p, scale_p, norm_g[l], w_in_r)
    qt, k32, kb, vt, v32, iqb, ik32, ikb, wv = _attention_prep(p, _rope_tables(jnp.arange(seq)),
                                                         q_norm_g[l], k_norm_g[l], transposed=True)
    n_sel = min(TOPK_MAX, seq // 4)
    tq = min(MXU_DIM, seq)
    bias = _prompt_indexer(iqb, wv[:, IDX_HD:IDX_HD + IDX_HEADS].T, ikb, n_sel, tq)
    att = _prompt_attention(qt, kb, vt, bias, p, tq)

    r_, kp_, v_, kk_, b_, lw_, bonus, sg = _rwkv_prep(p, jnp.zeros((1, p.shape[1]), F32), rwkv_consts,
                                                      shifted=True)
    y_scan, s_fin = _rwkv_scan(r_, kp_, v_, kk_, b_, lw_,
                               jnp.zeros((N_GROUPS, MXU_DIM, MXU_DIM), F32))
    rw = _rwkv_post(y_scan, bonus, sg, ln_x_g[l], ln_x_b[l], ones_bd)
    y_p = _out_projection(att, rw, w_out_b, xp, gate_p)

    k_prompt = k32.reshape(1, 1, seq, ATT_HEADS, ATT_HD)
    v_prompt = v32.reshape(1, 1, seq, ATT_HEADS, ATT_HD)
    idx_k_prompt = ik32.reshape(1, 1, seq, IDX_HD)
    wkv_prompt = _blockdiag_to_state(s_fin).reshape(1, 1, RWKV_HEADS, RWKV_HD, RWKV_HD)
    shift_prompt = _rwkv_cols_original_order(p[seq - 1:seq]).reshape(1, 1, -1)

    ps = _in_projection(xs, shift_s, scale_s, norm_g[l], w_in_r)
    pos_s = jnp.full((nb,), past_len, jnp.int32)
    qb_s, k32_s, _, _, v32_s, iqb_s, ik32_s, ikb_s, wv_s = _attention_prep(ps, _rope_tables(pos_s),
                                                                          q_norm_g[l], k_norm_g[l],
                                                                          transposed=False)
    n_sel_s = min(TOPK_MAX, (past_len + 1) // 4)
    att_s = _sample_attention(page_table, qb_s, iqb_s, wv_s, ikb_s, k32_s, v32_s,
                              ps[:, COL_G:COL_G + ATT_W], cache_idx_k[l], cache_k[l], cache_v[l], n_sel_s)

    prev_s = _reorder_cols(jnp.concatenate([jnp.zeros((nb, N_ATT_COLS_ORIG), F32), state_shift[l]], axis=-1))
    r_s, kp_s, v_s, kk_s, b_s, lw_s, bonus_s, sg_s = _rwkv_prep(ps, prev_s, rwkv_consts, shifted=False)
    wkv_new, y_step = _rwkv_step(state_wkv[l], r_s, kp_s, kk_s, b_s, lw_s, v_s)
    rw_s = _rwkv_post(y_step, bonus_s, sg_s, ln_x_g[l], ln_x_b[l], ones_bd)
    y_s = _out_projection(att_s, rw_s, w_out_b, xs, gate_s)

    k_sample = k32_s.reshape(1, nb, 1, ATT_HEADS, ATT_HD)
    v_sample = v32_s.reshape(1, nb, 1, ATT_HEADS, ATT_HD)
    idx_k_sample = ik32_s.reshape(1, nb, 1, IDX_HD)
    wkv_sample = wkv_new.reshape(1, nb, RWKV_HEADS, RWKV_HD, RWKV_HD)
    shift_sample = _rwkv_cols_original_order(ps).reshape(1, nb, -1)

    return (y_p.reshape(1, seq, d), y_s.reshape(nb, 1, d),
            k_prompt, v_prompt, idx_k_prompt, wkv_prompt, shift_prompt,
            k_sample, v_sample, idx_k_sample, wkv_sample, shift_sample)
```

```python
import functools
import math

import jax
import jax.numpy as jnp
from jax import lax
from jax.experimental import pallas as pl
from jax.experimental.pallas import tpu as pltpu

F32 = jnp.float32
BF16 = jnp.bfloat16
I32 = jnp.int32

ATT_HEADS = 8
ATT_HD = 128
ATT_W = ATT_HEADS * ATT_HD
IDX_HEADS = 16
IDX_HD = 64
IDX_W = IDX_HEADS * IDX_HD
TOPK_MAX = 256
PAGE_SIZE = 128
ROPE_THETA = 10000.0
RWKV_HD = 64
RWKV_HEADS = 16
RWKV_W = RWKV_HEADS * RWKV_HD
DECAY_LORA = 64
ICL_LORA = 64
NORM_EPS = 1e-6
GN_EPS = 64e-5

LANES = 128
SUBLANES = 8
MXU_DIM = 256
VMEM_LIMIT = 56 * 1024 * 1024
NEG_BIG = -1e30
INT_MIN = -(2 ** 31)

COL_Q, COL_K, COL_V, COL_G, COL_IQ = 0, 1024, 2048, 3072, 4096
COL_RR, COL_RK, COL_RV, COL_RG = 5120, 6144, 7168, 8192
COL_SMALL_A = 9216
COL_SMALL_R = 9344
N_COLS = 9472
N_COLS_PAD = 9728
N_ATT_COLS_ORIG = 4 * ATT_W + IDX_W + IDX_HEADS + IDX_HD

EARLY_EXIT_FROM_BIT = 24
FLASH_KEY_BLOCKS = 4
SCAN_CHUNK = 64
MOD_TN = 768
INPROJ_TM, INPROJ_TN = 1024, 512
PREP_TM = 256
POST_TM = 512
REORDER_TD = 256
STEP_SEQS = 4
DECODE_SLOT_GROUPS = 4
DECODE_PAGES_PER_STEP = 8
SCAN_SUB = 4
HEADS_PER_GROUP = MXU_DIM // RWKV_HD
N_GROUPS = RWKV_HEADS // HEADS_PER_GROUP


def _cparams(sem, vmem=VMEM_LIMIT):
    return pltpu.CompilerParams(dimension_semantics=sem, vmem_limit_bytes=vmem)


def _split2(x):
    hi = x.astype(BF16)
    lo = (x - hi.astype(F32)).astype(BF16)
    return hi, lo


def _split3(x):
    hi = x.astype(BF16)
    r = x - hi.astype(F32)
    mid = r.astype(BF16)
    lo = (r - mid.astype(F32)).astype(BF16)
    return hi, mid, lo


_NN = (((1,), (0,)), ((), ()))
_NT = (((1,), (1,)), ((), ()))
_TN = (((0,), (0,)), ((), ()))


def _dg(a, b, dims):
    return lax.dot_general(a, b, dims, preferred_element_type=F32)


def _dot3(a, b, dims=_NN):
    ah, al = _split2(a)
    bh, bl = _split2(b)
    return _dg(ah, bh, dims) + _dg(al, bh, dims) + _dg(ah, bl, dims)


def _dot_exact_lhs(a_bf16, b, dims=_NN):
    b0, b1, b2 = _split3(b)
    return _dg(a_bf16, b0, dims) + _dg(a_bf16, b1, dims) + _dg(a_bf16, b2, dims)


def _dot_exact_rhs(a, b_bf16, dims=_NN):
    a0, a1, a2 = _split3(a)
    return _dg(a0, b_bf16, dims) + _dg(a1, b_bf16, dims) + _dg(a2, b_bf16, dims)


def _silu(x):
    return x * jax.nn.sigmoid(x)


def _head_sum(x, ones_bd):
    parts = []
    for j in range(x.shape[1] // LANES):
        parts.append(_dot_exact_rhs(x[:, j * LANES:(j + 1) * LANES], ones_bd))
    return jnp.concatenate(parts, axis=1)


def _ordered_to_f32(key):
    return pltpu.bitcast(key ^ ((key >> 31) & 0x7FFFFFFF), F32)


def _mod_kernel(c_ref, w_ref, b_ref, o_ref):
    o_ref[...] = _dot3(_silu(c_ref[...]), w_ref[...]) + b_ref[...]


def _modulation(c, w_ada, b_ada):
    m, d = c.shape
    n = w_ada.shape[1]
    tn = MOD_TN
    return pl.pallas_call(
        _mod_kernel,
        grid=(n // tn,),
        in_specs=[pl.BlockSpec((m, d), lambda j: (0, 0)),
                  pl.BlockSpec((d, tn), lambda j: (0, j)),
                  pl.BlockSpec((1, tn), lambda j: (0, j))],
        out_specs=pl.BlockSpec((m, tn), lambda j: (0, j)),
        out_shape=jax.ShapeDtypeStruct((m, n), F32),
        compiler_params=_cparams(("arbitrary",)),
        name="modulation",
    )(c, w_ada, b_ada.reshape(1, n))


def _inproj_kernel(x_ref, shift_ref, scale_ref, g_ref, w_ref, o_ref, h_ref, *, rows):
    @pl.when(pl.program_id(1) == 0)
    def _():
        def body(r, c):
            sl = pl.ds(pl.multiple_of(r * rows, rows), rows)
            x = x_ref[sl, :]
            ms = jnp.mean(x * x, axis=-1, keepdims=True)
            y = x * lax.rsqrt(ms + NORM_EPS) * g_ref[...]
            if shift_ref.shape[0] == 1:
                h = y * (1.0 + scale_ref[...]) + shift_ref[...]
            else:
                h = y * (1.0 + scale_ref[sl, :]) + shift_ref[sl, :]
            h_ref[sl, :] = h.astype(BF16)
            return c
        lax.fori_loop(0, x_ref.shape[0] // rows, body, 0)

    o_ref[...] = jnp.dot(h_ref[...], w_ref[...], preferred_element_type=F32)


def _in_projection(x, shift, scale, norm_g, w_bf16):
    t, d = x.shape
    n = w_bf16.shape[1]
    tm = min(INPROJ_TM, t)
    tn = INPROJ_TN
    rows = min(128, tm)
    per_row = shift.shape[0] != 1
    mod_spec = (pl.BlockSpec((tm, d), lambda i, j: (i, 0)) if per_row
                else pl.BlockSpec((1, d), lambda i, j: (0, 0)))
    return pl.pallas_call(
        functools.partial(_inproj_kernel, rows=rows),
        grid=(t // tm, n // tn),
        in_specs=[pl.BlockSpec((tm, d), lambda i, j: (i, 0)),
                  mod_spec, mod_spec,
                  pl.BlockSpec((1, d), lambda i, j: (0, 0)),
                  pl.BlockSpec((d, tn), lambda i, j: (0, j))],
        out_specs=pl.BlockSpec((tm, tn), lambda i, j: (i, j)),
        out_shape=jax.ShapeDtypeStruct((t, n), F32),
        scratch_shapes=[pltpu.VMEM((tm, d), BF16)],
        compiler_params=_cparams(("arbitrary", "arbitrary")),
        name="in_projection",
    )(x, shift, scale, norm_g.reshape(1, d), w_bf16)


def _rope128(x, cos, sin_signed):
    return x * cos + pltpu.roll(x, 64, axis=1) * sin_signed


def _rope64(x, cos, sin_signed, first_half):
    rot = jnp.where(first_half, pltpu.roll(x, 96, axis=1), pltpu.roll(x, 32, axis=1))
    return x * cos + rot * sin_signed


def _prep_kernel(q_ref, k_ref, v_ref, iq_ref, sm_ref, c128_ref, s128_ref, c64_ref, s64_ref,
                 qg_ref, kg_ref,
                 qo_ref, k32_ref, kb_ref, vo_ref, v32_ref, iqb_ref, ik32_ref, ikb_ref, w_ref, *, transposed):
    c128, s128 = c128_ref[...], s128_ref[...]
    c64, s64 = c64_ref[...], s64_ref[...]
    tm = q_ref.shape[0]
    lane = lax.broadcasted_iota(I32, (tm, LANES), 1)
    first_half = (lane % IDX_HD) < (IDX_HD // 2)
    q_scale = ATT_HD ** -0.5 * math.log2(math.e)
    for h in range(ATT_HEADS):
        hs = slice(h * ATT_HD, (h + 1) * ATT_HD)
        q = q_ref[:, hs]
        q = q * lax.rsqrt(jnp.mean(q * q, axis=-1, keepdims=True) + NORM_EPS) * qg_ref[...]
        q = _rope128(q, c128, s128) * q_scale
        k = k_ref[:, hs]
        k = k * lax.rsqrt(jnp.mean(k * k, axis=-1, keepdims=True) + NORM_EPS) * kg_ref[...]
        k = _rope128(k, c128, s128)
        k32_ref[:, h, :] = k
        kb_ref[:, hs] = k.astype(BF16)
        if transposed:
            qo_ref[hs, :] = q.T.astype(BF16)
            vo_ref[hs, :] = v_ref[:, hs].T.astype(BF16)
        else:
            qo_ref[:, hs] = q.astype(BF16)
            vo_ref[:, hs] = v_ref[:, hs].astype(BF16)
    for h in range(ATT_HEADS):
        v32_ref[:, h, :] = v_ref[:, h * ATT_HD:(h + 1) * ATT_HD]
    for j in range(IDX_W // LANES):
        js = slice(j * LANES, (j + 1) * LANES)
        iqb_ref[:, js] = _rope64(iq_ref[:, js], c64, s64, first_half).astype(BF16)
    sm = sm_ref[...]
    ik = _rope64(sm, c64, s64, first_half)[:, :IDX_HD]
    ik32_ref[...] = ik
    ikb_ref[...] = ik.astype(BF16)
    w_ref[...] = sm * ((IDX_HEADS ** -0.5) * (IDX_HD ** -0.5))


def _attention_prep(p, tables, q_norm_g, k_norm_g, transposed):
    t = p.shape[0]
    tm = min(PREP_TM, t)
    c128, s128, c64, s64 = tables
    wide = lambda col: pl.BlockSpec((tm, ATT_W), lambda i, c=col // ATT_W: (i, c))
    tab = pl.BlockSpec((tm, LANES), lambda i: (i, 0))
    gain = pl.BlockSpec((1, ATT_HD), lambda i: (0, 0))
    row_w = pl.BlockSpec((tm, ATT_W), lambda i: (i, 0))
    col_w = pl.BlockSpec((ATT_W, tm), lambda i: (0, i))
    row64 = pl.BlockSpec((tm, IDX_HD), lambda i: (i, 0))
    head3 = pl.BlockSpec((tm, ATT_HEADS, ATT_HD), lambda i: (i, 0, 0))
    if transposed:
        qv_spec, qv_shape = col_w, jax.ShapeDtypeStruct((ATT_W, t), BF16)
    else:
        qv_spec, qv_shape = row_w, jax.ShapeDtypeStruct((t, ATT_W), BF16)
    return pl.pallas_call(
        functools.partial(_prep_kernel, transposed=transposed),
        grid=(t // tm,),
        in_specs=[wide(COL_Q), wide(COL_K), wide(COL_V), wide(COL_IQ),
                  pl.BlockSpec((tm, LANES), lambda i: (i, COL_SMALL_A // LANES)),
                  tab, tab, tab, tab, gain, gain],
        out_specs=[qv_spec, head3, row_w, qv_spec, head3, row_w, row64, row64, tab],
        out_shape=[qv_shape,
                   jax.ShapeDtypeStruct((t, ATT_HEADS, ATT_HD), F32),
                   jax.ShapeDtypeStruct((t, ATT_W), BF16),
                   qv_shape,
                   jax.ShapeDtypeStruct((t, ATT_HEADS, ATT_HD), F32),
                   jax.ShapeDtypeStruct((t, IDX_W), BF16),
                   jax.ShapeDtypeStruct((t, IDX_HD), F32),
                   jax.ShapeDtypeStruct((t, IDX_HD), BF16),
                   jax.ShapeDtypeStruct((t, LANES), F32)],
        compiler_params=_cparams(("arbitrary",)),
        name="attention_prep",
    )(p, p, p, p, p, c128, s128, c64, s64, q_norm_g.reshape(1, ATT_HD), k_norm_g.reshape(1, ATT_HD))


def _rope_tables(pos):
    pos = pos.astype(F32)[:, None]

    def tab(dh):
        inv = ROPE_THETA ** (-jnp.arange(0, dh, 2, dtype=F32) / dh)
        ang = pos * inv[None, :]
        cos, sin = jnp.cos(ang), jnp.sin(ang)
        c = jnp.concatenate([cos, cos], axis=-1)
        s = jnp.concatenate([-sin, sin], axis=-1)
        rep = LANES // dh
        return jnp.tile(c, (1, rep)), jnp.tile(s, (1, rep))

    c128, s128 = tab(ATT_HD)
    c64, s64 = tab(IDX_HD)
    return c128, s128, c64, s64


def _indexer_kernel(iq_ref, wt_ref, ik_ref, bias_ref, iqs_ref, cut_ref, *, n_sel):
    i = pl.program_id(0)
    tq = iq_ref.shape[0]
    tk = bias_ref.shape[2]
    nk_total = bias_ref.shape[1]
    nkb = i + 1

    for h in range(IDX_HEADS):
        iqs_ref[h * tq:(h + 1) * tq, :] = iq_ref[:, h * IDX_HD:(h + 1) * IDX_HD]

    key_pos = lax.broadcasted_iota(I32, (tk, tq), 0)
    qry_pos = lax.broadcasted_iota(I32, (tk, tq), 1) + i * tq

    def score_body(kb, c):
        ikt = ik_ref[pl.ds(pl.multiple_of(kb * tk, tk), tk), :]
        d = _dg(ikt, iqs_ref[...], _NT)
        acc = jnp.zeros((tk, tq), F32)
        for h in range(IDX_HEADS):
            acc = acc + jnp.maximum(d[:, h * tq:(h + 1) * tq], 0.0) * wt_ref[h:h + 1, :]
        bias_ref[0, kb] = jnp.where(key_pos + kb * tk <= qry_pos, acc, -jnp.inf)
        return c

    def score_pair(j, c):
        score_body(2 * j, c)
        return score_body(2 * j + 1, c)

    lax.fori_loop(0, nkb // 2, score_pair, 0)
    lax.fori_loop(2 * (nkb // 2), nkb, score_body, 0)

    def count_keys(pred):
        def one(kb, cnt):
            x = bias_ref[0, kb].reshape(tk // SUBLANES, SUBLANES, tq)
            pos = (key_pos + kb * tk).reshape(tk // SUBLANES, SUBLANES, tq)
            return cnt + jnp.sum(jnp.where(pred(x, pos), 1.0, 0.0), axis=0)

        def four(j, cnt):
            for u in range(4):
                cnt = one(4 * j + u, cnt)
            return cnt

        cnt = lax.fori_loop(0, nkb // 4, four, jnp.zeros((SUBLANES, tq), F32))
        cnt = lax.fori_loop(4 * (nkb // 4), nkb, one, cnt)
        return jnp.sum(cnt, axis=0, keepdims=True)

    def bit_body(bi, carry):
        ans, tot_ans = carry
        cand = ans + jnp.left_shift(jnp.int32(1), 31 - bi)
        cand_f = _ordered_to_f32(cand)
        tot = count_keys(lambda x, pos: x >= cand_f[None])
        ok = tot >= n_sel
        return jnp.where(ok, cand, ans), jnp.where(ok, tot, tot_ans)

    never_full = qry_pos[0:1, :] + 1 < n_sel

    def pending(tot_ans):
        return jnp.max(jnp.where((tot_ans == n_sel) | never_full, 0.0, 1.0))

    def late_bits(c):
        bi, ans, tot_ans, _ = c
        ans, tot_ans = bit_body(bi + 1, bit_body(bi, (ans, tot_ans)))
        return bi + 2, ans, tot_ans, pending(tot_ans)

    ans, tot_ans = lax.fori_loop(0, EARLY_EXIT_FROM_BIT, bit_body,
                                 (jnp.full((SUBLANES, tq), INT_MIN, I32), jnp.zeros((1, tq), F32)))
    _, ans, tot_ans, _ = lax.while_loop(lambda c: (c[0] < 32) & (c[3] > 0.0), late_bits,
                                        (jnp.int32(EARLY_EXIT_FROM_BIT), ans, tot_ans, pending(tot_ans)))
    thr = _ordered_to_f32(ans)[0:1, :]
    thr = jnp.where(thr != thr, -jnp.inf, thr)

    cut_ref[...] = jnp.full(cut_ref.shape, 2 ** 31 - 1, I32)

    @pl.when(jnp.max(tot_ans) > n_sel)
    def _():
        need = n_sel - count_keys(lambda x, pos: x > thr[None])
        n_bits = max(1, (nk_total * tk - 1).bit_length())

        def cut_body(bi, cut):
            cand = cut + jnp.left_shift(jnp.int32(1), n_bits - 1 - bi)
            below = count_keys(lambda x, pos: (x == thr[None]) & (pos < cand[None]))
            return jnp.where(below < need, cand, cut)

        cut = lax.fori_loop(0, n_bits, cut_body, jnp.zeros((1, tq), I32))
        cut_ref[...] = jnp.broadcast_to(jnp.where(tot_ans > n_sel, cut, 2 ** 31 - 1), cut_ref.shape)

    has_surplus_ties = jnp.max(tot_ans) > n_sel
    cut = cut_ref[0:1, :]

    def bias_body(kb, c):
        keep = (bias_ref[0, kb] >= thr) & (key_pos + kb * tk <= qry_pos)
        bias_ref[0, kb] = jnp.where(keep, 0.0, NEG_BIG)
        return c

    def bias_body_ties(kb, c):
        x = bias_ref[0, kb]
        pos = key_pos + kb * tk
        keep = ((x > thr) | ((x == thr) & (pos <= cut))) & (pos <= qry_pos)
        bias_ref[0, kb] = jnp.where(keep, 0.0, NEG_BIG)
        return c

    @pl.when(jnp.logical_not(has_surplus_ties))
    def _():
        lax.fori_loop(0, nkb, bias_body, 0)

    @pl.when(has_surplus_ties)
    def _():
        lax.fori_loop(0, nkb, bias_body_ties, 0)

    def fill_body(kb, c):
        bias_ref[0, kb] = jnp.full((tk, tq), NEG_BIG, F32)
        return c

    lax.fori_loop(nkb, nk_total, fill_body, 0)


def _prompt_indexer(iqb, wt, ikb, n_sel, tq):
    t = iqb.shape[0]
    nq = t // tq
    return pl.pallas_call(
        functools.partial(_indexer_kernel, n_sel=n_sel),
        grid=(nq,),
        in_specs=[pl.BlockSpec((tq, IDX_W), lambda i: (i, 0)),
                  pl.BlockSpec((IDX_HEADS, tq), lambda i: (0, i)),
                  pl.BlockSpec((t, IDX_HD), lambda i: (0, 0))],
        out_specs=pl.BlockSpec((1, nq, tq, tq), lambda i: (i, 0, 0, 0)),
        out_shape=jax.ShapeDtypeStruct((nq, nq, tq, tq), F32),
        scratch_shapes=[pltpu.VMEM((IDX_HEADS * tq, IDX_HD), BF16),
                        pltpu.VMEM((SUBLANES, tq), I32)],
        compiler_params=_cparams(("arbitrary",)),
        name="prompt_indexer",
    )(iqb, wt, ikb)


def _flash_kernel(qi_ref, ki_ref, qt_ref, k_ref, vt_ref, bias_ref, g_ref, o_ref, m_ref, l_ref, acc_ref):
    step = pl.program_id(0)
    qi, ki = qi_ref[step], ki_ref[step]

    @pl.when(ki == 0)
    def _():
        m_ref[...] = jnp.full(m_ref.shape, NEG_BIG, F32)
        l_ref[...] = jnp.zeros(l_ref.shape, F32)
        acc_ref[...] = jnp.zeros(acc_ref.shape, F32)

    heads = [slice(h * ATT_HD, (h + 1) * ATT_HD) for h in range(ATT_HEADS)]
    tq = qt_ref.shape[1]
    tkb = bias_ref.shape[2]
    for kh in range(bias_ref.shape[1]):
        ks = slice(kh * tkb, (kh + 1) * tkb)
        for qh in range(tq // LANES):
            qs = slice(qh * LANES, (qh + 1) * LANES)
            bias = bias_ref[0, kh, :, qs]
            m_prev, l_prev = m_ref[:, qs], l_ref[:, qs]
            m_rows, l_rows = [], []
            for h, hs in enumerate(heads):
                s = _dg(k_ref[ks, hs], qt_ref[hs, qs], _NN) + bias
                m_old = m_prev[h:h + 1, :]
                m_new = jnp.maximum(m_old, jnp.max(s, axis=0, keepdims=True))
                alpha = jnp.exp2(m_old - m_new)
                p = jnp.exp2(s - m_new)
                l_rows.append(alpha * l_prev[h:h + 1, :] + jnp.sum(p, axis=0, keepdims=True))
                m_rows.append(m_new)
                acc_ref[h, :, qs] = alpha * acc_ref[h, :, qs] + _dg(vt_ref[hs, ks], p.astype(BF16), _NN)
            m_ref[:, qs] = jnp.concatenate(m_rows, axis=0)
            l_ref[:, qs] = jnp.concatenate(l_rows, axis=0)

    @pl.when(ki == qi // bias_ref.shape[1])
    def _():
        for h in range(ATT_HEADS):
            hs = slice(h * ATT_HD, (h + 1) * ATT_HD)
            out = (acc_ref[h] / l_ref[h:h + 1, :]).T
            o_ref[:, hs] = (out * _silu(g_ref[:, hs])).astype(BF16)


def _prompt_attention(qt, kb, vt, bias, p, tq):
    t = kb.shape[0]
    nq = t // tq
    kpb = min(FLASH_KEY_BLOCKS, nq)
    tk = kpb * tq
    pairs = [(i, j) for i in range(nq) for j in range(i // kpb + 1)]
    qi = jnp.asarray([a for a, _ in pairs], I32)
    ki = jnp.asarray([b for _, b in pairs], I32)
    grid_spec = pltpu.PrefetchScalarGridSpec(
        num_scalar_prefetch=2,
        grid=(len(pairs),),
        in_specs=[pl.BlockSpec((ATT_W, tq), lambda s, qi, ki: (0, qi[s])),
                  pl.BlockSpec((tk, ATT_W), lambda s, qi, ki: (ki[s], 0)),
                  pl.BlockSpec((ATT_W, tk), lambda s, qi, ki: (0, ki[s])),
                  pl.BlockSpec((1, kpb, tq, tq), lambda s, qi, ki: (qi[s], ki[s], 0, 0)),
                  pl.BlockSpec((tq, ATT_W), lambda s, qi, ki: (qi[s], COL_G // ATT_W))],
        out_specs=pl.BlockSpec((tq, ATT_W), lambda s, qi, ki: (qi[s], 0)),
        scratch_shapes=[pltpu.VMEM((ATT_HEADS, tq), F32),
                        pltpu.VMEM((ATT_HEADS, tq), F32),
                        pltpu.VMEM((ATT_HEADS, ATT_HD, tq), F32)],
    )
    return pl.pallas_call(
        _flash_kernel,
        grid_spec=grid_spec,
        out_shape=jax.ShapeDtypeStruct((t, ATT_W), BF16),
        compiler_params=_cparams(("arbitrary",)),
        name="prompt_attention",
    )(qi, ki, qt, kb, vt, bias, p)


def _sample_attn_kernel(pt_ref, q_ref, iq_ref, w_ref, ikn_ref, kn_ref, vn_ref, g_ref, ex_ref,
                        cik_ref, ck_ref, cv_ref, o_ref,
                        ikbuf, kbuf, vbuf, sce_ref, isem, ksem, vsem, *, n_sel, pps):
    b = pl.program_id(0)
    n_pages = ikbuf.shape[0]
    page_rows = PAGE_SIZE * ATT_HEADS

    n_steps = n_pages // pps
    nb = pt_ref.shape[0]
    n_groups = kbuf.shape[0] // pps
    total_steps = nb * n_steps

    def idx_copy(seq, j):
        return pltpu.make_async_copy(cik_ref.at[pt_ref[seq, j]], ikbuf.at[j], isem.at[j])

    def k_copy(seq, j, slot):
        return pltpu.make_async_copy(ck_ref.at[pt_ref[seq, j]], kbuf.at[slot], ksem.at[slot])

    def v_copy(seq, j, slot):
        return pltpu.make_async_copy(cv_ref.at[pt_ref[seq, j]], vbuf.at[slot], vsem.at[slot])

    def start_idx(seq):
        def body(j, c):
            idx_copy(seq, j).start()
            return c
        lax.fori_loop(0, n_pages, body, 0)

    def start_group(gs):
        seq, t, grp = gs // n_steps, gs % n_steps, gs % n_groups
        for i in range(pps):
            k_copy(seq, t * pps + i, grp * pps + i).start()
            v_copy(seq, t * pps + i, grp * pps + i).start()

    @pl.when(b == 0)
    def _():
        start_idx(0)
        for gs in range(min(n_groups - 1, total_steps)):
            start_group(gs)

    iq = iq_ref[0]
    w = w_ref[0]

    def wait_idx(j, c):
        idx_copy(b, j).wait()
        return c

    lax.fori_loop(0, n_pages, wait_idx, 0)
    iq_pages = jnp.broadcast_to(iq[None], (n_pages, IDX_HEADS, IDX_HD))
    d = lax.dot_general(iq_pages, ikbuf[...].astype(BF16), (((2,), (1,)), ((0,), (0,))),
                        preferred_element_type=F32)
    sc = jnp.sum(jnp.maximum(d, 0.0) * w[None], axis=1)
    d_self = jnp.sum(iq.astype(F32) * ikn_ref[0].astype(F32), axis=-1, keepdims=True)
    s_self = jnp.sum(jnp.maximum(d_self, 0.0) * w, axis=0, keepdims=True)

    pos = (lax.broadcasted_iota(I32, sc.shape, 0) * PAGE_SIZE + lax.broadcasted_iota(I32, sc.shape, 1))
    self_pos = n_pages * PAGE_SIZE

    def count_keys(pred):
        cnt = jnp.sum(jnp.where(pred(sc, pos), 1.0, 0.0), axis=-1, keepdims=True)
        return jnp.sum(cnt, axis=0, keepdims=True) + jnp.where(pred(s_self, self_pos), 1.0, 0.0)

    def digit_body(it, carry):
        ans, tot_ans = carry
        step = jnp.left_shift(jnp.int32(1), 30 - 2 * it)
        best, best_tot = ans, tot_ans
        cand = ans
        for _ in range(3):
            cand = cand + step
            cand_f = _ordered_to_f32(cand)
            tot = count_keys(lambda x, p: x >= cand_f)
            best = jnp.where(tot >= n_sel, cand, best)
            best_tot = jnp.where(tot >= n_sel, tot, best_tot)
        return best, best_tot

    ans, tot_ans = lax.fori_loop(0, 16, digit_body, (jnp.full((1, 1), INT_MIN, I32), jnp.zeros((1, 1), F32)))
    thr = _ordered_to_f32(ans)
    thr = jnp.where(thr != thr, -jnp.inf, thr)

    def tie_cut():
        need = n_sel - count_keys(lambda x, p: x > thr)
        n_bits = self_pos.bit_length()

        def cut_body(bi, cut):
            cand = cut + jnp.left_shift(jnp.int32(1), n_bits - 1 - bi)
            below = count_keys(lambda x, p: (x == thr) & (p < cand))
            return jnp.where(below < need, cand, cut)

        return lax.fori_loop(0, n_bits, cut_body, jnp.zeros((1, 1), I32))

    cut = lax.cond(tot_ans[0, 0] > n_sel, tie_cut, lambda: jnp.full((1, 1), 2 ** 31 - 1, I32))

    def selected(x, p):
        return (x > thr) | ((x == thr) & (p <= cut))

    sce_ref[...] = _dot_exact_rhs(sc, ex_ref[...])

    @pl.when(b + 1 < nb)
    def _():
        start_idx(b + 1)

    q8 = q_ref[0]
    lane_e = lax.broadcasted_iota(I32, (ATT_HEADS, page_rows), 1)
    own = lane_e % ATT_HEADS == lax.broadcasted_iota(I32, (ATT_HEADS, page_rows), 0)
    tok_lane = lane_e[0:1, :] // ATT_HEADS

    kn = kn_ref[0].astype(BF16).astype(F32)
    s0 = jnp.sum(q8.astype(F32) * kn, axis=-1, keepdims=True)
    m0 = jnp.where(selected(s_self, self_pos), s0, NEG_BIG)
    l0 = jnp.where(m0 > 0.5 * NEG_BIG, 1.0, 0.0)
    acc0 = l0 * vn_ref[0].astype(BF16).astype(F32)

    def att_body(t, carry):
        m_prev, l_prev, acc = carry
        gs = b * n_steps + t
        grp = gs % n_groups

        @pl.when(gs + n_groups - 1 < total_steps)
        def _():
            start_group(gs + n_groups - 1)

        for i in range(pps):
            k_copy(b, t * pps + i, grp * pps + i).wait()
            v_copy(b, t * pps + i, grp * pps + i).wait()
        s = []
        for i in range(pps):
            kf = kbuf[grp * pps + i].reshape(page_rows, ATT_HD).astype(BF16)
            keep = own & selected(sce_ref[pl.ds(t * pps + i, 1), :], (t * pps + i) * PAGE_SIZE + tok_lane)
            s.append(jnp.where(keep, _dg(q8, kf, _NT), NEG_BIG))
        m_new = m_prev
        for x in s:
            m_new = jnp.maximum(m_new, jnp.max(x, axis=-1, keepdims=True))
        alpha = jnp.exp2(m_prev - m_new)
        p = [jnp.exp2(x - m_new) for x in s]
        l_new = alpha * l_prev
        for x in p:
            l_new = l_new + jnp.sum(x, axis=-1, keepdims=True)
        pv = [jnp.dot(p[i].astype(BF16), vbuf[grp * pps + i].reshape(page_rows, ATT_HD).astype(BF16),
                      preferred_element_type=F32) for i in range(pps)]
        acc = alpha * acc
        for x in pv:
            acc = acc + x
        return m_new, l_new, acc

    m, l, acc = lax.fori_loop(0, n_steps, att_body, (m0, l0, acc0))
    o_ref[0] = acc / l * _silu(g_ref[0])


def _sample_attention(page_table, qb, iqb, wv, ikb, k32, v32, g, cache_idx_k, cache_k, cache_v, n_sel):
    nb, n_pages = page_table.shape
    head3 = pl.BlockSpec((1, ATT_HEADS, ATT_HD), lambda b, pt: (b, 0, 0))
    any_spec = pl.BlockSpec(memory_space=pl.ANY)
    page_rows = PAGE_SIZE * ATT_HEADS
    pps = math.gcd(DECODE_PAGES_PER_STEP, n_pages)
    expand = (jnp.arange(page_rows)[None, :] // ATT_HEADS == jnp.arange(PAGE_SIZE)[:, None]).astype(BF16)
    grid_spec = pltpu.PrefetchScalarGridSpec(
        num_scalar_prefetch=1,
        grid=(nb,),
        in_specs=[head3,
                  pl.BlockSpec((1, IDX_HEADS, IDX_HD), lambda b, pt: (b, 0, 0)),
                  pl.BlockSpec((1, IDX_HEADS, 1), lambda b, pt: (b, 0, 0)),
                  pl.BlockSpec((1, 1, IDX_HD), lambda b, pt: (b, 0, 0)),
                  head3, head3, head3,
                  pl.BlockSpec((PAGE_SIZE, page_rows), lambda b, pt: (0, 0)),
                  any_spec, any_spec, any_spec],
        out_specs=head3,
        scratch_shapes=[pltpu.VMEM((n_pages, IDX_HD, PAGE_SIZE), F32),
                        pltpu.VMEM((DECODE_SLOT_GROUPS * pps, PAGE_SIZE, ATT_HEADS, ATT_HD), F32),
                        pltpu.VMEM((DECODE_SLOT_GROUPS * pps, PAGE_SIZE, ATT_HEADS, ATT_HD), F32),
                        pltpu.VMEM((n_pages, page_rows), F32),
                        pltpu.SemaphoreType.DMA((n_pages,)),
                        pltpu.SemaphoreType.DMA((DECODE_SLOT_GROUPS * pps,)),
                        pltpu.SemaphoreType.DMA((DECODE_SLOT_GROUPS * pps,))],
    )
    head = lambda a: a.reshape(nb, ATT_HEADS, ATT_HD)
    out = pl.pallas_call(
        functools.partial(_sample_attn_kernel, n_sel=n_sel, pps=pps),
        grid_spec=grid_spec,
        out_shape=jax.ShapeDtypeStruct((nb, ATT_HEADS, ATT_HD), F32),
        compiler_params=_cparams(("arbitrary",)),
        name="sample_attention",
    )(page_table, head(qb),
      iqb.reshape(nb, IDX_HEADS, IDX_HD),
      wv[:, IDX_HD:IDX_HD + IDX_HEADS].reshape(nb, IDX_HEADS, 1),
      ikb.reshape(nb, 1, IDX_HD),
      head(k32), head(v32), head(g), expand,
      jnp.swapaxes(cache_idx_k, 1, 2), cache_k, cache_v)
    return out.reshape(nb, ATT_W).astype(BF16)


def _rwkv_prep_kernel(*refs, shifted):
    n_x = 5
    x_refs = refs[:n_x]
    if shifted:
        halo_refs, first_refs = refs[n_x:2 * n_x], refs[2 * n_x:3 * n_x]
        rest = refs[3 * n_x:]
    else:
        prev_refs = refs[n_x:2 * n_x]
        rest = refs[2 * n_x:]
    (mu_ref, mus_ref, wl_ref, w0_ref, a0_ref, kk_ref, ka_ref, rk_ref, ones_ref,
     ro_ref, kpo_ref, vo_ref, kko_ref, bo_ref, lwo_ref, bonus_ref, sg_ref) = rest

    def lerp(j, mu):
        x = x_refs[j][...]
        if shifted:
            above = jnp.where(pl.program_id(0) == 0, first_refs[j][...],
                              halo_refs[j][SUBLANES - 1:SUBLANES, :])
            row = lax.broadcasted_iota(I32, x.shape, 0)
            prev = jnp.where(row == 0, above, pltpu.roll(x, 1, axis=0))
        else:
            prev = prev_refs[j][...]
        return x + mu * (prev - x)

    w = RWKV_W
    xr = lerp(0, mu_ref[:, 0:w])
    xk = lerp(1, mu_ref[:, w:2 * w])
    xv = lerp(2, mu_ref[:, 2 * w:3 * w])
    xg = lerp(3, mu_ref[:, 3 * w:4 * w])
    xs = lerp(4, mus_ref[...])
    lane = lax.broadcasted_iota(I32, xs.shape, 1)
    lora_in = jnp.where(lane < DECAY_LORA, jnp.tanh(xs), xs)
    lora = _dot3(lora_in, wl_ref[...])
    z = w0_ref[...] + lora[:, :w]
    softplus_neg = jnp.maximum(-z, 0.0) + jnp.log(1.0 + jnp.exp(-jnp.abs(z)))
    w_log = -softplus_neg - 0.5
    lwo_ref[...] = -jnp.exp(w_log)
    a = jax.nn.sigmoid(a0_ref[...] + lora[:, w:])
    ones_bd = ones_ref[...]
    kk = xk * kk_ref[...]
    kk = kk / jnp.maximum(jnp.sqrt(_head_sum(kk * kk, ones_bd)), 1e-12)
    kp = xk * (1.0 + (a - 1.0) * ka_ref[...])
    ro_ref[...] = xr
    kpo_ref[...] = kp
    vo_ref[...] = xv
    kko_ref[...] = kk
    bo_ref[...] = kk * a
    bonus_ref[...] = _head_sum(xr * kp * rk_ref[...], ones_bd) * xv
    sg_ref[...] = _silu(xg)


def _rwkv_prep(p, prev, consts, shifted):
    t = p.shape[0]
    tm = min(PREP_TM, t)
    mu_big, mu_small, w_lora, w0, a0, k_k, k_a, r_k, ones_bd = consts
    cols = [(RWKV_W, COL_RR // RWKV_W), (RWKV_W, COL_RK // RWKV_W), (RWKV_W, COL_RV // RWKV_W),
            (RWKV_W, COL_RG // RWKV_W), (LANES, COL_SMALL_R // LANES)]
    x_specs = [pl.BlockSpec((tm, wd), lambda i, c=c: (i, c)) for wd, c in cols]
    if shifted:
        per8 = tm // SUBLANES
        halo = [pl.BlockSpec((SUBLANES, wd), lambda i, c=c: (jnp.maximum(i * per8 - 1, 0), c))
                for wd, c in cols]
        first = [pl.BlockSpec((1, wd), lambda i, c=c: (0, c)) for wd, c in cols]
        extra_specs, extra_args = halo + first, [p] * 5 + [prev] * 5
    else:
        extra_specs, extra_args = x_specs, [prev] * 5
    full = lambda a: pl.BlockSpec(a.shape, lambda i: (0, 0))
    row = pl.BlockSpec((tm, RWKV_W), lambda i: (i, 0))
    vec = lambda a: a.reshape(1, -1)
    args = [mu_big, mu_small, w_lora, vec(w0), vec(a0), vec(k_k), vec(k_a), vec(r_k), ones_bd]
    return pl.pallas_call(
        functools.partial(_rwkv_prep_kernel, shifted=shifted),
        grid=(t // tm,),
        in_specs=x_specs + extra_specs + [full(a) for a in args],
        out_specs=[row] * 8,
        out_shape=[jax.ShapeDtypeStruct((t, RWKV_W), F32)] * 8,
        compiler_params=_cparams(("arbitrary",)),
        name="rwkv_prep",
    )(*([p] * 5), *extra_args, *args)


def _scan_kernel(r_ref, kp_ref, v_ref, kk_ref, b_ref, lw_ref, s0_ref, y_ref, st_ref, s_ref):
    @pl.when(pl.program_id(0) == 0)
    def _():
        s_ref[...] = s0_ref[...]

    ch = SCAN_CHUNK
    assert ch == RWKV_HD
    rows_total = r_ref.shape[0]
    n_sub = rows_total // ch
    gw = MXU_DIM
    nh = HEADS_PER_GROUP
    lw = lw_ref[...]
    tr = lax.broadcasted_iota(I32, (rows_total, rows_total), 0)
    tc = lax.broadcasted_iota(I32, (rows_total, rows_total), 1)
    tri = jnp.where((tr >= tc) & (tr // ch == tc // ch), 1.0, 0.0).astype(BF16)
    cum = _dot_exact_lhs(tri, lw)
    cum_end = jnp.concatenate(
        [jnp.broadcast_to(cum[(c + 1) * ch - 1:(c + 1) * ch, :], (ch, RWKV_W)) for c in range(n_sub)], axis=0)
    kk, b, kp, r, v = kk_ref[...], b_ref[...], kp_ref[...], r_ref[...], v_ref[...]
    g_inv = jnp.exp(-cum)
    g_end = jnp.exp(cum_end - cum)
    a_t = (-kk * jnp.exp(cum - lw)).astype(BF16)
    b_t = b * g_inv
    k_t = kp * g_inv
    r_t = (r * jnp.exp(cum)).astype(BF16)
    b_h = (b * g_end).astype(BF16)
    k_h = (kp * g_end).astype(BF16)
    vb = v.astype(BF16)

    srow = lax.broadcasted_iota(I32, (nh * ch, gw), 0)
    scol = lax.broadcasted_iota(I32, (nh * ch, gw), 1)
    own = (srow // ch) == (scol // RWKV_HD)
    step = lax.broadcasted_iota(I32, (ch, gw), 0)
    other = lax.broadcasted_iota(I32, (ch, gw), 1) % ch
    strict, incl = step > other, step >= other
    eye = jnp.where(step == other, 1.0, 0.0)

    def stack(x):
        return jnp.where(own, jnp.concatenate([x] * nh, axis=0), 0.0).astype(BF16)

    groups = range(N_GROUPS)
    units = [(c, g) for c in range(n_sub) for g in groups]
    blk = {(c, g): (slice(c * ch, (c + 1) * ch), slice(g * gw, (g + 1) * gw)) for c, g in units}
    rb = {u: stack(b_t[blk[u]]) for u in units}
    rk = {u: stack(k_t[blk[u]]) for u in units}
    v4 = {u: stack(v[blk[u]]) for u in units}
    ar = {u: jnp.concatenate([a_t[blk[u]], r_t[blk[u]]], axis=0) for u in units}
    g_b = {u: _dg(ar[u], rb[u], _NT) for u in units}
    g_k = {u: _dg(ar[u], rk[u], _NT) for u in units}
    n_row = {u: jnp.where(strict, g_b[u][:ch], 0.0) for u in units}
    bm_row = {u: jnp.where(strict, g_k[u][:ch], 0.0).astype(BF16) for u in units}
    rb_row = {u: jnp.where(incl, g_b[u][ch:], 0.0).astype(BF16) for u in units}
    rk_row = {u: jnp.where(incl, g_k[u][ch:], 0.0).astype(BF16) for u in units}
    minv = {u: eye + n_row[u] for u in units}
    pw = {u: _dg(n_row[u].astype(BF16), stack(n_row[u]), _NN) for u in units}
    levels = int(math.log2(ch)) - 1
    for level in range(levels):
        pw_bd = {u: stack(pw[u]) for u in units}
        if level + 1 < levels:
            both = {u: _dg(jnp.concatenate([minv[u], pw[u]], axis=0).astype(BF16), pw_bd[u], _NN) for u in units}
            minv = {u: minv[u] + both[u][:ch] for u in units}
            pw = {u: both[u][ch:] for u in units}
        else:
            minv = {u: minv[u] + _dg(minv[u].astype(BF16), pw_bd[u], _NN) for u in units}
    minv = {u: minv[u].astype(BF16) for u in units}
    xy_fixed = {u: _dg(jnp.concatenate([bm_row[u], rk_row[u]], axis=0), v4[u], _NN) for u in units}
    x_fixed = {u: xy_fixed[u][:ch] for u in units}
    y_fixed = {u: xy_fixed[u][ch:] for u in units}

    s = [s_ref[g] for g in groups]
    for c in range(n_sub):
        sb = [x.astype(BF16) for x in s]
        ar_s = [_dg(ar[c, g], sb[g], _NT) for g in groups]
        x_row = [ar_s[g][:ch] + x_fixed[c, g] for g in groups]
        x_bd = [stack(x) for x in x_row]
        u_row = [_dg(minv[c, g], x_bd[g], _NN) for g in groups]
        u_bd = [stack(x) for x in u_row]
        for g in groups:
            y_ref[blk[c, g]] = ar_s[g][ch:] + _dg(rb_row[c, g], u_bd[g], _NN) + y_fixed[c, g]
        for g in groups:
            rs, ls = blk[c, g]
            upd = _dg(jnp.concatenate([u_row[g].astype(BF16), vb[rs, ls]], axis=0),
                      jnp.concatenate([b_h[rs, ls], k_h[rs, ls]], axis=0), _TN)
            g_c = jnp.exp(cum[(c + 1) * ch - 1:(c + 1) * ch, ls])
            s[g] = s[g] * g_c + jnp.where(own, upd, 0.0)
    for g in groups:
        s_ref[g] = s[g]

    @pl.when(pl.program_id(0) == pl.num_programs(0) - 1)
    def _():
        st_ref[...] = s_ref[...]


def _rwkv_scan(r, kp, v, kk, b, lw, s0):
    t = r.shape[0]
    ch = SCAN_CHUNK * min(SCAN_SUB, t // SCAN_CHUNK)
    row = pl.BlockSpec((ch, RWKV_W), lambda i: (i, 0))
    st = pl.BlockSpec((N_GROUPS, MXU_DIM, MXU_DIM), lambda i: (0, 0, 0))
    return pl.pallas_call(
        _scan_kernel,
        grid=(t // ch,),
        in_specs=[row] * 6 + [st],
        out_specs=[row, st],
        out_shape=[jax.ShapeDtypeStruct((t, RWKV_W), F32),
                   jax.ShapeDtypeStruct((N_GROUPS, MXU_DIM, MXU_DIM), F32)],
        scratch_shapes=[pltpu.VMEM((N_GROUPS, MXU_DIM, MXU_DIM), F32)],
        compiler_params=_cparams(("arbitrary",)),
        name="rwkv_scan",
    )(r, kp, v, kk, b, lw, s0)


def _rwkv_step_kernel(s_ref, r_ref, kp_ref, kk_ref, b_ref, lw_ref, v_ref, so_ref, y_ref):
    s = s_ref[...]
    sa = -jnp.sum(s * kk_ref[...], axis=-1, keepdims=True)
    s_new = s * jnp.exp(lw_ref[...]) + sa * b_ref[...] + v_ref[...] * kp_ref[...]
    so_ref[...] = s_new
    y_ref[...] = jnp.sum(s_new * r_ref[...], axis=-1, keepdims=True)


def _rwkv_step(state, r, kp, kk, b, lw, v):
    nb = state.shape[0]
    ts = math.gcd(STEP_SEQS, nb)
    rowv = lambda a: a.reshape(nb, RWKV_HEADS, 1, RWKV_HD)
    rspec = pl.BlockSpec((ts, RWKV_HEADS, 1, RWKV_HD), lambda i: (i, 0, 0, 0))
    cspec = pl.BlockSpec((ts, RWKV_HEADS, RWKV_HD, 1), lambda i: (i, 0, 0, 0))
    sspec = pl.BlockSpec((ts, RWKV_HEADS, RWKV_HD, RWKV_HD), lambda i: (i, 0, 0, 0))
    s_new, y = pl.pallas_call(
        _rwkv_step_kernel,
        grid=(nb // ts,),
        in_specs=[sspec, rspec, rspec, rspec, rspec, rspec, cspec],
        out_specs=[sspec, cspec],
        out_shape=[jax.ShapeDtypeStruct(state.shape, F32),
                   jax.ShapeDtypeStruct((nb, RWKV_HEADS, RWKV_HD, 1), F32)],
        compiler_params=_cparams(("arbitrary",)),
        name="rwkv_step",
    )(state, rowv(r), rowv(kp), rowv(kk), rowv(b), rowv(lw), v.reshape(nb, RWKV_HEADS, RWKV_HD, 1))
    return s_new, y.reshape(nb, RWKV_W)


def _rwkv_post_kernel(y_ref, bonus_ref, sg_ref, lg_ref, lb_ref, ones_ref, o_ref):
    y = y_ref[...]
    ones_bd = ones_ref[...]
    mean = _head_sum(y, ones_bd) * (1.0 / RWKV_HD)
    yc = y - mean
    var = _head_sum(yc * yc, ones_bd) * (1.0 / RWKV_HD)
    yn = yc * lax.rsqrt(var + GN_EPS) * lg_ref[...] + lb_ref[...]
    o_ref[...] = ((yn + bonus_ref[...]) * sg_ref[...]).astype(BF16)


def _rwkv_post(y, bonus, sg, ln_g, ln_b, ones_bd):
    t = y.shape[0]
    tm = min(POST_TM, t)
    row = pl.BlockSpec((tm, RWKV_W), lambda i: (i, 0))
    vec = pl.BlockSpec((1, RWKV_W), lambda i: (0, 0))
    return pl.pallas_call(
        _rwkv_post_kernel,
        grid=(t // tm,),
        in_specs=[row, row, row, vec, vec, pl.BlockSpec(ones_bd.shape, lambda i: (0, 0))],
        out_specs=row,
        out_shape=jax.ShapeDtypeStruct((t, RWKV_W), BF16),
        compiler_params=_cparams(("arbitrary",)),
        name="rwkv_post",
    )(y, bonus, sg, ln_g.reshape(1, -1), ln_b.reshape(1, -1), ones_bd)


def _outproj_kernel(att_ref, rw_ref, wa_ref, wr_ref, x_ref, gate_ref, o_ref):
    out = (jnp.dot(att_ref[...], wa_ref[...], preferred_element_type=F32)
           + jnp.dot(rw_ref[...], wr_ref[...], preferred_element_type=F32))
    o_ref[...] = x_ref[...] + gate_ref[...] * out


def _out_projection(att, rw, w_out_bf16, x, gate):
    t, d = x.shape
    tm = min(POST_TM, t)
    per_row = gate.shape[0] != 1
    gspec = (pl.BlockSpec((tm, d), lambda i: (i, 0)) if per_row else pl.BlockSpec((1, d), lambda i: (0, 0)))
    return pl.pallas_call(
        _outproj_kernel,
        grid=(t // tm,),
        in_specs=[pl.BlockSpec((tm, ATT_W), lambda i: (i, 0)),
                  pl.BlockSpec((tm, RWKV_W), lambda i: (i, 0)),
                  pl.BlockSpec((ATT_W, d), lambda i: (0, 0)),
                  pl.BlockSpec((RWKV_W, d), lambda i: (1, 0)),
                  pl.BlockSpec((tm, d), lambda i: (i, 0)),
                  gspec],
        out_specs=pl.BlockSpec((tm, d), lambda i: (i, 0)),
        out_shape=jax.ShapeDtypeStruct((t, d), F32),
        compiler_params=_cparams(("arbitrary",)),
        name="out_projection",
    )(att, rw, w_out_bf16, w_out_bf16, x, gate)


def _reorder_cols(w, axis=-1):
    axis = axis % w.ndim

    def cut(start, size):
        return lax.slice_in_dim(w, start, start + size, axis=axis)

    def zeros(size):
        shape = list(w.shape)
        shape[axis] = size
        return jnp.zeros(shape, w.dtype)

    q, k, v, g = (cut(i * ATT_W, ATT_W) for i in range(4))
    a = 4 * ATT_W
    iq, iw, ik = cut(a, IDX_W), cut(a + IDX_W, IDX_HEADS), cut(a + IDX_W + IDX_HEADS, IDX_HD)
    o = N_ATT_COLS_ORIG
    rr, rk, rv = (cut(o + i * RWKV_W, RWKV_W) for i in range(3))
    o2 = o + 3 * RWKV_W
    wl, al = cut(o2, DECAY_LORA), cut(o2 + DECAY_LORA, ICL_LORA)
    rg = cut(o2 + DECAY_LORA + ICL_LORA, RWKV_W)
    return jnp.concatenate([q, k, v, g, iq, rr, rk, rv, rg, ik, iw, zeros(LANES - IDX_HD - IDX_HEADS),
                            wl, al, zeros(N_COLS_PAD - N_COLS)], axis=axis)


def _reorder_cast_kernel(wt_ref, o_ref):
    o_ref[...] = _reorder_cols(wt_ref[...], axis=0).T.astype(BF16)


def _reorder_cast(wt):
    n, d = wt.shape
    td = min(REORDER_TD, d)
    return pl.pallas_call(
        _reorder_cast_kernel,
        grid=(d // td,),
        in_specs=[pl.BlockSpec((n, td), lambda i: (0, i))],
        out_specs=pl.BlockSpec((td, N_COLS_PAD), lambda i: (i, 0)),
        out_shape=jax.ShapeDtypeStruct((d, N_COLS_PAD), BF16),
        compiler_params=_cparams(("arbitrary",)),
        name="reorder_cast",
    )(wt)


def _rwkv_cols_original_order(p):
    return jnp.concatenate([p[..., COL_RR:COL_RR + RWKV_W], p[..., COL_RK:COL_RK + RWKV_W],
                            p[..., COL_RV:COL_RV + RWKV_W],
                            p[..., COL_SMALL_R:COL_SMALL_R + DECAY_LORA + ICL_LORA],
                            p[..., COL_RG:COL_RG + RWKV_W]], axis=-1)


def _blockdiag_to_state(sb):
    sb = sb.reshape(N_GROUPS, HEADS_PER_GROUP, RWKV_HD, HEADS_PER_GROUP, RWKV_HD)
    return jnp.stack([sb[:, h, :, h, :] for h in range(HEADS_PER_GROUP)], axis=1).reshape(
        RWKV_HEADS, RWKV_HD, RWKV_HD)


def kernel(x_prompt, x_sample, c_prompt, c_sample, cache_k, cache_v, cache_idx_k, state_wkv, state_shift,
           page_table, w_ada, b_ada, norm_g, w_in, q_norm_g, k_norm_g, shift_mu, w0, w_b, a0, a_b,
           k_k, k_a, r_k, ln_x_g, ln_x_b, w_out):
    depth = w_ada.shape[0]
    assert depth == 1, "single-layer trunk"
    bsz, seq, d = x_prompt.shape
    assert bsz == 1
    nb, dec_seq, _ = x_sample.shape
    assert dec_seq == 1
    n_pages = page_table.shape[1]
    past_len = n_pages * PAGE_SIZE
    l = 0

    w_in_r = _reorder_cast(w_in[l].T)
    w_out_b = w_out[l].astype(BF16)
    mu = _reorder_cols(jnp.concatenate([jnp.zeros((N_ATT_COLS_ORIG,), F32), shift_mu[l]]))
    mu_big = mu[COL_RR:COL_RR + 4 * RWKV_W].reshape(1, -1)
    mu_small = mu[COL_SMALL_R:COL_SMALL_R + LANES].reshape(1, -1)
    zl = jnp.zeros((DECAY_LORA, RWKV_W), F32)
    w_lora = jnp.concatenate([jnp.concatenate([w_b[l], zl], axis=1),
                              jnp.concatenate([zl, a_b[l]], axis=1)], axis=0)
    hid = jnp.arange(LANES) // RWKV_HD
    ones_bd = (hid[:, None] == hid[None, :]).astype(BF16)
    rwkv_consts = (mu_big, mu_small, w_lora, w0[l], a0[l], k_k[l], k_a[l], r_k[l].reshape(-1), ones_bd)

    n_mod = 1 + nb
    c_all = jnp.concatenate([c_prompt, c_sample, jnp.zeros((-n_mod % SUBLANES, d), F32)], axis=0)
    mod = _modulation(c_all, w_ada[l], b_ada[l])
    shift_p, scale_p, gate_p = mod[0:1, :d], mod[0:1, d:2 * d], mod[0:1, 2 * d:]
    shift_s, scale_s, gate_s = mod[1:n_mod, :d], mod[1:n_mod, d:2 * d], mod[1:n_mod, 2 * d:]

    xp = x_prompt.reshape(seq, d)
    xs = x_sample.reshape(nb, d)

    p = _in_projection(xp, shift_p, scale_p, norm_g[l], w_in_r)
    qt, k32, kb, vt, v32, iqb, ik32, ikb, wv = _attention_prep(p, _rope_tables(jnp.arange(seq)),
                                                         q_norm_g[l], k_norm_g[l], transposed=True)
    n_sel = min(TOPK_MAX, seq // 4)
    tq = min(MXU_DIM, seq)
    bias = _prompt_indexer(iqb, wv[:, IDX_HD:IDX_HD + IDX_HEADS].T, ikb, n_sel, tq)
    att = _prompt_attention(qt, kb, vt, bias, p, tq)

    r_, kp_, v_, kk_, b_, lw_, bonus, sg = _rwkv_prep(p, jnp.zeros((1, p.shape[1]), F32), rwkv_consts,
                                                      shifted=True)
    y_scan, s_fin = _rwkv_scan(r_, kp_, v_, kk_, b_, lw_,
                               jnp.zeros((N_GROUPS, MXU_DIM, MXU_DIM), F32))
    rw = _rwkv_post(y_scan, bonus, sg, ln_x_g[l], ln_x_b[l], ones_bd)
    y_p = _out_projection(att, rw, w_out_b, xp, gate_p)

    k_prompt = k32.reshape(1, 1, seq, ATT_HEADS, ATT_HD)
    v_prompt = v32.reshape(1, 1, seq, ATT_HEADS, ATT_HD)
    idx_k_prompt = ik32.reshape(1, 1, seq, IDX_HD)
    wkv_prompt = _blockdiag_to_state(s_fin).reshape(1, 1, RWKV_HEADS, RWKV_HD, RWKV_HD)
    shift_prompt = _rwkv_cols_original_order(p[seq - 1:seq]).reshape(1, 1, -1)

    ps = _in_projection(xs, shift_s, scale_s, norm_g[l], w_in_r)
    pos_s = jnp.full((nb,), past_len, jnp.int32)
    qb_s, k32_s, _, _, v32_s, iqb_s, ik32_s, ikb_s, wv_s = _attention_prep(ps, _rope_tables(pos_s),
                                                                          q_norm_g[l], k_norm_g[l],
                                                                          transposed=False)
    n_sel_s = min(TOPK_MAX, (past_len + 1) // 4)
    att_s = _sample_attention(page_table, qb_s, iqb_s, wv_s, ikb_s, k32_s, v32_s,
                              ps[:, COL_G:COL_G + ATT_W], cache_idx_k[l], cache_k[l], cache_v[l], n_sel_s)

    prev_s = _reorder_cols(jnp.concatenate([jnp.zeros((nb, N_ATT_COLS_ORIG), F32), state_shift[l]], axis=-1))
    r_s, kp_s, v_s, kk_s, b_s, lw_s, bonus_s, sg_s = _rwkv_prep(ps, prev_s, rwkv_consts, shifted=False)
    wkv_new, y_step = _rwkv_step(state_wkv[l], r_s, kp_s, kk_s, b_s, lw_s, v_s)
    rw_s = _rwkv_post(y_step, bonus_s, sg_s, ln_x_g[l], ln_x_b[l], ones_bd)
    y_s = _out_projection(att_s, rw_s, w_out_b, xs, gate_s)

    k_sample = k32_s.reshape(1, nb, 1, ATT_HEADS, ATT_HD)
    v_sample = v32_s.reshape(1, nb, 1, ATT_HEADS, ATT_HD)
    idx_k_sample = ik32_s.reshape(1, nb, 1, IDX_HD)
    wkv_sample = wkv_new.reshape(1, nb, RWKV_HEADS, RWKV_HD, RWKV_HD)
    shift_sample = _rwkv_cols_original_order(ps).reshape(1, nb, -1)

    return (y_p.reshape(1, seq, d), y_s.reshape(nb, 1, d),
            k_prompt, v_prompt, idx_k_prompt, wkv_prompt, shift_prompt,
            k_sample, v_sample, idx_k_sample, wkv_sample, shift_sample)
```
